```python
import jax, jax.numpy as jnp
from jax import lax
import numpy as np

D_MODEL = 1024
BATCH = 16
SEQ = 256
DEPTH = 2
DEC_BATCH = 4
DEC_SEQ = 4096
PAST_LEN = 256

GRID_W = 64
MIX_WIDTH = D_MODEL
ATTN_HEADS = 4
QK_NOPE = 128
QK_ROPE = 64
V_DIM = 128
Q_LORA = 384
KV_LORA = 256
ATTN_SCALE = (QK_NOPE + QK_ROPE) ** -0.5
Q_BLOCK = 128
ROPE_BASE = 10000.0
CHUNK = 128
GM_WIDTH = MIX_WIDTH - ATTN_HEADS * V_DIM
GM_GROUPS = 4
GM_GROUP_DIM = GM_WIDTH // GM_GROUPS
IN_WIDTH = Q_LORA + KV_LORA + QK_ROPE + 2 * GM_WIDTH
IN_SPLITS = [Q_LORA, Q_LORA + KV_LORA, Q_LORA + KV_LORA + QK_ROPE,
             Q_LORA + KV_LORA + QK_ROPE + GM_WIDTH]
N_EXPERTS = 16
N_GROUPS = 4
EXPERTS_PER_GROUP = N_EXPERTS // N_GROUPS
TOP_K = 2
EXPERT_FF = 256
SHARED_FF = 256
EPS = 1e-6

kernel_name = 'hybrid_mla_chunkgmlp_groupmoe_diffusion_step'


def rms_norm(x, g):
    xf = x.astype(jnp.float32)
    y = xf * lax.rsqrt(jnp.mean(xf * xf, axis=-1, keepdims=True) + EPS)
    return (y * g.astype(jnp.float32)).astype(x.dtype)


def modulation(cond, w_ada, b_ada):
    return (jax.nn.silu(cond) @ w_ada + b_ada)[:, None, :]


def axial_rope(n, dtype):
    rows = n // GRID_W
    row = jnp.repeat(jnp.arange(rows, dtype=jnp.float32), GRID_W)
    col = jnp.tile(jnp.arange(GRID_W, dtype=jnp.float32), rows)
    half = QK_ROPE // 2
    freqs = 1.0 / (ROPE_BASE ** (jnp.arange(0, half, 2, dtype=jnp.float32) / half))
    ang = jnp.concatenate([row[:, None] * freqs, col[:, None] * freqs], axis=-1)
    return jnp.cos(ang).astype(dtype), jnp.sin(ang).astype(dtype)


def apply_rope(x, cos, sin):
    x1, x2 = x[..., 0::2], x[..., 1::2]
    return jnp.stack([x1 * cos - x2 * sin, x1 * sin + x2 * cos], axis=-1).reshape(x.shape)


def attend(q_nope, q_rope, k_nope, k_rope, v):
    B, S, H, _ = q_nope.shape
    nb = S // Q_BLOCK
    qn = q_nope.reshape(B, nb, Q_BLOCK, H, QK_NOPE).transpose(1, 0, 2, 3, 4)
    qr = q_rope.reshape(B, nb, Q_BLOCK, H, QK_ROPE).transpose(1, 0, 2, 3, 4)

    def block(args):
        qn_b, qr_b = args
        s = (jnp.einsum('bqhd,bthd->bhqt', qn_b, k_nope)
             + jnp.einsum('bqhr,btr->bhqt', qr_b, k_rope))
        p = jax.nn.softmax(s.astype(jnp.float32) * ATTN_SCALE, axis=-1).astype(v.dtype)
        return jnp.einsum('bhqt,bthd->bqhd', p, v)

    out = lax.map(block, (qn, qr))
    return out.transpose(1, 0, 2, 3, 4).reshape(B, S, H * V_DIM)


def chunk_mlp(u, v, gm_norm_g, w_spatial, b_spatial):
    B, S, _ = u.shape
    n = S // CHUNK
    u = jax.nn.gelu(u).reshape(B, n, CHUNK, GM_GROUPS, GM_GROUP_DIM)
    v = rms_norm(jax.nn.gelu(v).reshape(B, n, CHUNK, GM_GROUPS, GM_GROUP_DIM),
                 gm_norm_g.reshape(GM_GROUPS, GM_GROUP_DIM))
    sv = jnp.einsum('gpq,bnqgc->bnpgc', w_spatial, v) + b_spatial.T[:, :, None]
    return (u * sv).reshape(B, S, GM_WIDTH)


def route(t, w_router, router_bias):
    T = t.shape[0]
    s = jax.nn.sigmoid((t @ w_router).astype(jnp.float32))
    sb = s + router_bias.astype(jnp.float32)
    gscore = lax.top_k(sb.reshape(T, N_GROUPS, EXPERTS_PER_GROUP), 2)[0].sum(-1)
    gsel = jnp.argmax(gscore, axis=-1)
    in_group = (jnp.arange(N_EXPERTS) // EXPERTS_PER_GROUP)[None, :] == gsel[:, None]
    _, idx = lax.top_k(jnp.where(in_group, sb, -jnp.inf), TOP_K)
    w = jnp.take_along_axis(s, idx, axis=-1)
    w = w / jnp.sum(w, axis=-1, keepdims=True)
    return jnp.sum(jax.nn.one_hot(idx, N_EXPERTS, dtype=jnp.float32) * w[..., None], axis=1)


def moe(h, w_router, router_bias, w_gate, w_up, w_down, ws_gate, ws_up, ws_down):
    B, S, D = h.shape
    t = h.reshape(B * S, D)
    combine = route(t, w_router, router_bias).astype(h.dtype)
    act = jax.nn.silu(jnp.einsum('td,edf->tef', t, w_gate)) * jnp.einsum('td,edf->tef', t, w_up)
    routed = jnp.einsum('tef,efd->td', act * combine[..., None], w_down)
    shared = (jax.nn.silu(t @ ws_gate) * (t @ ws_up)) @ ws_down
    return (routed + shared).reshape(B, S, D)


def trunk_layer(x, mod, p, w_router, router_bias, rope, ctx_kv):
    shift1, scale1, gate1, shift2, scale2, gate2 = jnp.split(mod, 6, axis=-1)
    B, S, _ = x.shape
    h = rms_norm(x, p['norm1_g']) * (1 + scale1) + shift1
    cq, ckv, krope, u, v = jnp.split(h @ p['w_in'], IN_SPLITS, axis=-1)
    q = (rms_norm(cq, p['q_norm_g']) @ p['w_qb']).reshape(B, S, ATTN_HEADS, QK_NOPE + QK_ROPE)
    q_nope, q_rope = q[..., :QK_NOPE], q[..., QK_NOPE:]
    ckv = rms_norm(ckv, p['kv_norm_g'])
    own_kv = (ckv, krope)
    if rope is None:
        keys_ckv, keys_krope = ckv, krope
    else:
        cos, sin = rope
        q_rope = apply_rope(q_rope, cos[:, None], sin[:, None])
        keys_ckv = jnp.concatenate([ctx_kv[0], ckv], axis=1)
        keys_krope = jnp.concatenate([ctx_kv[1], apply_rope(krope, cos, sin)], axis=1)
    T = keys_ckv.shape[1]
    kv = (keys_ckv @ p['w_kvb']).reshape(B, T, ATTN_HEADS, QK_NOPE + V_DIM)
    attn = attend(q_nope, q_rope, kv[..., :QK_NOPE], keys_krope, kv[..., QK_NOPE:])
    gm = chunk_mlp(u, v, p['gm_norm_g'], p['w_spatial'], p['b_spatial'])
    mixed = jnp.concatenate([rms_norm(attn, p['onorm_attn_g']),
                             rms_norm(gm, p['onorm_gm_g'])], axis=-1) @ p['w_out']
    x = x + gate1 * mixed
    h2 = rms_norm(x, p['norm2_g']) * (1 + scale2) + shift2
    x = x + gate2 * moe(h2, w_router, router_bias, p['w_gate'], p['w_up'], p['w_down'],
                        p['ws_gate'], p['ws_up'], p['ws_down'])
    return x, own_kv


def setup_inputs(seed: int = 0) -> dict:
    key = jax.random.key(seed)
    ks = jax.random.split(key, 32)
    f32 = jnp.float32
    D = D_MODEL

    def nrm(k, shape, scale):
        return jax.random.normal(k, shape, f32) * scale

    def gain(k, shape):
        return 1.0 + 0.02 * jax.random.normal(k, shape, f32)

    return {
        'x_prompt': nrm(ks[0], (BATCH, SEQ, D), 1.0),
        'x_sample': nrm(ks[1], (DEC_BATCH, DEC_SEQ, D), 1.0),
        'cache_ckv': nrm(ks[2], (DEC_BATCH, DEPTH, PAST_LEN, KV_LORA), 1.0),
        'cache_krope': nrm(ks[3], (DEC_BATCH, DEPTH, PAST_LEN, QK_ROPE), 1.0),
        'c': nrm(ks[4], (DEC_BATCH, D), 1.0),
        'c_ctx': nrm(ks[5], (D,), 1.0),
        'norm1_g': gain(ks[6], (DEPTH, D)),
        'w_ada': nrm(ks[7], (DEPTH, D, 6 * D), 0.5 * D ** -0.5),
        'b_ada': nrm(ks[8], (DEPTH, 6 * D), 0.02),
        'w_in': nrm(ks[9], (DEPTH, D, IN_WIDTH), D ** -0.5),
        'q_norm_g': gain(ks[10], (DEPTH, Q_LORA)),
        'w_qb': nrm(ks[11], (DEPTH, Q_LORA, ATTN_HEADS * (QK_NOPE + QK_ROPE)), Q_LORA ** -0.5),
        'kv_norm_g': gain(ks[12], (DEPTH, KV_LORA)),
        'w_kvb': nrm(ks[13], (DEPTH, KV_LORA, ATTN_HEADS * (QK_NOPE + V_DIM)), KV_LORA ** -0.5),
        'gm_norm_g': gain(ks[14], (DEPTH, GM_WIDTH)),
        'w_spatial': nrm(ks[15], (DEPTH, GM_GROUPS, CHUNK, CHUNK), CHUNK ** -0.5),
        'b_spatial': gain(ks[16], (DEPTH, GM_GROUPS, CHUNK)),
        'onorm_attn_g': gain(ks[17], (DEPTH, ATTN_HEADS * V_DIM)),
        'onorm_gm_g': gain(ks[18], (DEPTH, GM_WIDTH)),
        'w_out': nrm(ks[19], (DEPTH, MIX_WIDTH, D), MIX_WIDTH ** -0.5),
        'norm2_g': gain(ks[20], (DEPTH, D)),
        'w_router': nrm(ks[21], (D, N_EXPERTS), D ** -0.5),
        'router_bias': nrm(ks[22], (N_EXPERTS,), 0.01),
        'w_gate': nrm(ks[23], (DEPTH, N_EXPERTS, D, EXPERT_FF), D ** -0.5),
        'w_up': nrm(ks[24], (DEPTH, N_EXPERTS, D, EXPERT_FF), D ** -0.5),
        'w_down': nrm(ks[25], (DEPTH, N_EXPERTS, EXPERT_FF, D), EXPERT_FF ** -0.5),
        'ws_gate': nrm(ks[26], (DEPTH, D, SHARED_FF), D ** -0.5),
        'ws_up': nrm(ks[27], (DEPTH, D, SHARED_FF), D ** -0.5),
        'ws_down': nrm(ks[28], (DEPTH, SHARED_FF, D), SHARED_FF ** -0.5),
        'final_norm_g': gain(ks[29], (D,)),
    }


def reference(x_prompt, x_sample, cache_ckv, cache_krope, c, c_ctx, norm1_g, w_ada, b_ada,
              w_in, q_norm_g, w_qb, kv_norm_g, w_kvb, gm_norm_g, w_spatial, b_spatial,
              onorm_attn_g, onorm_gm_g, w_out, norm2_g, w_router, router_bias, w_gate, w_up,
              w_down, ws_gate, ws_up, ws_down, final_norm_g):
    rope = axial_rope(x_sample.shape[1], x_sample.dtype)
    xp, xs = x_prompt, x_sample
    ckv_list, krope_list = [], []
    for l in range(DEPTH):
        p = dict(norm1_g=norm1_g[l], w_in=w_in[l], q_norm_g=q_norm_g[l], w_qb=w_qb[l],
                 kv_norm_g=kv_norm_g[l], w_kvb=w_kvb[l], gm_norm_g=gm_norm_g[l],
                 w_spatial=w_spatial[l], b_spatial=b_spatial[l], onorm_attn_g=onorm_attn_g[l],
                 onorm_gm_g=onorm_gm_g[l], w_out=w_out[l], norm2_g=norm2_g[l],
                 w_gate=w_gate[l], w_up=w_up[l], w_down=w_down[l],
                 ws_gate=ws_gate[l], ws_up=ws_up[l], ws_down=ws_down[l])
        xp, (ckv_l, krope_l) = trunk_layer(xp, modulation(c_ctx[None, :], w_ada[l], b_ada[l]),
                                           p, w_router, router_bias, None, None)
        ckv_list.append(ckv_l)
        krope_list.append(krope_l)
        xs, _ = trunk_layer(xs, modulation(c, w_ada[l], b_ada[l]), p, w_router, router_bias,
                            rope, (cache_ckv[:, l], cache_krope[:, l]))
    y_prompt = rms_norm(xp, final_norm_g)
    y_sample = rms_norm(xs, final_norm_g)
    new_ckv = jnp.stack(ckv_list, axis=1)
    new_krope = jnp.stack(krope_list, axis=1)
    return (y_prompt, y_sample, new_ckv, new_krope)
```

```python
import functools
import math

import jax
import jax.numpy as jnp
import numpy as np
from jax import lax
from jax.experimental import pallas as pl
from jax.experimental.pallas import tpu as pltpu

D_MODEL = 1024
BATCH = 16
SEQ = 256
DEPTH = 2
DEC_BATCH = 4
DEC_SEQ = 4096
PAST_LEN = 256
GRID_W = 64
ATTN_HEADS = 4
QK_NOPE = 128
QK_ROPE = 64
V_DIM = 128
Q_LORA = 384
KV_LORA = 256
ATTN_SCALE = (QK_NOPE + QK_ROPE) ** -0.5
ROPE_BASE = 10000.0
CHUNK = 128
GM_WIDTH = 512
GM_GROUPS = 4
GM_GROUP_DIM = 128
N_EXPERTS = 16
N_GROUPS = 4
EXPERTS_PER_GROUP = 4
EXPERT_FF = 256
SHARED_FF = 256
EPS = 1e-6

N_CTX = BATCH * SEQ
N_LAT = DEC_BATCH * DEC_SEQ
N_ROWS = N_CTX + N_LAT
N_COND = 1 + DEC_BATCH
COND_PAD = 8
HEAD_PAD = 256
IN_COLS = Q_LORA + KV_LORA + 2 * QK_ROPE + 2 * GM_WIDTH
LANES = 128
LOG2E = 1.4426950408889634

TM_PRE = 256
TM_POST = 256
TM_MOE = 512
TQ = 512
KV_CHUNK = 1024
VMEM_LIMIT = 56 * 1024 * 1024

F32 = jnp.float32
BF16 = jnp.bfloat16


def _rms(x):
    return x * lax.rsqrt(jnp.mean(x * x, axis=-1, keepdims=True) + EPS)


def _gelu(x):
    return 0.5 * x * (1.0 + jnp.tanh(math.sqrt(2.0 / math.pi) * (x + 0.044715 * (x * x * x))))


def _silu(x):
    return x * (1.0 / (1.0 + jnp.exp(-x)))


def _dot(a, b):
    return jnp.dot(a, b, preferred_element_type=F32)


def _dot_nt(a, b):
    return lax.dot_general(a, b, (((1,), (1,)), ((), ())), preferred_element_type=F32)


def _cond_row(i, tm):
    n_ctx_tiles = N_CTX // tm
    per_batch = DEC_SEQ // tm
    return jnp.where(i < n_ctx_tiles, 0, 1 + (i - n_ctx_tiles) // per_batch)


def _rope_block(i, tm):
    n_ctx_tiles = N_CTX // tm
    per_batch = DEC_SEQ // tm
    return jnp.where(i < n_ctx_tiles, 0, 1 + (i - n_ctx_tiles) % per_batch)


def _full(shape):
    n = len(shape)
    return pl.BlockSpec(shape, lambda *_: (0,) * n)


def _mod_kernel(cond_ref, w_ref, b_ref, o_ref):
    s = _silu(cond_ref[...])
    o_ref[0] = jnp.dot(s, w_ref[0], preferred_element_type=F32,
                       precision=lax.Precision.HIGHEST) + b_ref[0]


def _modulation(cond, w_ada, b_ada):
    tn = 1536
    return pl.pallas_call(
        _mod_kernel,
        out_shape=jax.ShapeDtypeStruct((DEPTH, COND_PAD, 6 * D_MODEL), F32),
        grid=(DEPTH, 6 * D_MODEL // tn),
        in_specs=[
            pl.BlockSpec((COND_PAD, D_MODEL), lambda l, j: (0, 0)),
            pl.BlockSpec((1, D_MODEL, tn), lambda l, j: (l, 0, j)),
            pl.BlockSpec((1, 1, tn), lambda l, j: (l, 0, j)),
        ],
        out_specs=pl.BlockSpec((1, COND_PAD, tn), lambda l, j: (l, 0, j)),
        compiler_params=pltpu.CompilerParams(
            dimension_semantics=("arbitrary", "arbitrary"), vmem_limit_bytes=VMEM_LIMIT),
        name="modulation",
    )(cond, w_ada, b_ada.reshape(DEPTH, 1, 6 * D_MODEL))


def _pre_kernel(x_ref, mod_ref, rope_ref, g1_ref, win_ref, qg_ref, wqb_ref, kvg_ref,
                wkvb_ref, gmg_ref, ws_ref, bs_ref, ogm_ref,
                q_ref, k_ref, v_ref, gm_ref, ckv_ref, kr_ref):
    tm = x_ref.shape[0]
    mod = mod_ref[0]
    shift1 = mod[:, 0:D_MODEL]
    scale1 = mod[:, D_MODEL:2 * D_MODEL]
    h = _rms(x_ref[...]) * g1_ref[...] * (1.0 + scale1) + shift1
    y = _dot(h.astype(BF16), win_ref[...])
    cq = y[:, 0:Q_LORA]
    ckv = y[:, Q_LORA:Q_LORA + KV_LORA]
    kr2 = y[:, 640:768]
    u = y[:, 768:768 + GM_WIDTH]
    vv = y[:, 768 + GM_WIDTH:768 + 2 * GM_WIDTH]

    rope = rope_ref[...]
    lane = lax.broadcasted_iota(jnp.int32, (tm, LANES), 1)

    kr_ref[...] = kr2[:, 0:QK_ROPE]
    t = kr2 * rope
    k_rot = jnp.where(lane < QK_ROPE, t + pltpu.roll(t, QK_ROPE, 1), 0.0).astype(BF16)

    ckv_n = _rms(ckv) * kvg_ref[...]
    ckv_ref[...] = ckv_n
    kv = _dot(ckv_n.astype(BF16), wkvb_ref[...])
    for hd in range(ATTN_HEADS):
        k_ref[:, hd * HEAD_PAD:hd * HEAD_PAD + QK_NOPE] = (
            kv[:, hd * QK_NOPE:(hd + 1) * QK_NOPE].astype(BF16))
        k_ref[:, hd * HEAD_PAD + QK_NOPE:(hd + 1) * HEAD_PAD] = k_rot
    v_ref[...] = kv[:, ATTN_HEADS * QK_NOPE:].astype(BF16)

    q = _dot((_rms(cq) * qg_ref[...]).astype(BF16), wqb_ref[...])
    for hd in range(ATTN_HEADS):
        q_ref[:, hd * HEAD_PAD:hd * HEAD_PAD + QK_NOPE] = (
            q[:, hd * HEAD_PAD:hd * HEAD_PAD + QK_NOPE].astype(BF16))
        t = q[:, hd * HEAD_PAD + QK_NOPE:(hd + 1) * HEAD_PAD] * rope
        q_ref[:, hd * HEAD_PAD + QK_NOPE:(hd + 1) * HEAD_PAD] = (
            t + pltpu.roll(t, QK_ROPE, 1)).astype(BF16)

    ug = _gelu(u)
    vg = _gelu(vv)
    cols = []
    for g in range(GM_GROUPS):
        sl = slice(g * GM_GROUP_DIM, (g + 1) * GM_GROUP_DIM)
        vn = (_rms(vg[:, sl]) * gmg_ref[:, sl]).astype(BF16)
        rows = []
        for c in range(tm // CHUNK):
            sv = _dot(ws_ref[g], vn[c * CHUNK:(c + 1) * CHUNK]) + bs_ref[:, sl]
            rows.append(ug[c * CHUNK:(c + 1) * CHUNK, sl] * sv)
        cols.append(jnp.concatenate(rows, axis=0))
    gm = jnp.concatenate(cols, axis=1)
    gm_ref[...] = (_rms(gm) * ogm_ref[...]).astype(BF16)


def _pre_mixer(x, mod, rope_tab, g1, win, qg, wqb, kvg, wkvb, gmg, ws, bs, ogm):
    tm = TM_PRE
    row = lambda i: (i, 0)
    return pl.pallas_call(
        _pre_kernel,
        out_shape=(
            jax.ShapeDtypeStruct((N_ROWS, ATTN_HEADS * HEAD_PAD), BF16),
            jax.ShapeDtypeStruct((N_ROWS, ATTN_HEADS * HEAD_PAD), BF16),
            jax.ShapeDtypeStruct((N_ROWS, ATTN_HEADS * V_DIM), BF16),
            jax.ShapeDtypeStruct((N_ROWS, GM_WIDTH), BF16),
            jax.ShapeDtypeStruct((N_ROWS, KV_LORA), F32),
            jax.ShapeDtypeStruct((N_ROWS, QK_ROPE), F32),
        ),
        grid=(N_ROWS // tm,),
        in_specs=[
            pl.BlockSpec((tm, D_MODEL), row),
            pl.BlockSpec((1, 1, 6 * D_MODEL), lambda i: (_cond_row(i, tm), 0, 0)),
            pl.BlockSpec((tm, LANES), lambda i: (_rope_block(i, tm), 0)),
            _full((1, D_MODEL)),
            _full((D_MODEL, IN_COLS)),
            _full((1, Q_LORA)),
            _full((Q_LORA, ATTN_HEADS * HEAD_PAD)),
            _full((1, KV_LORA)),
            _full((KV_LORA, ATTN_HEADS * (QK_NOPE + V_DIM))),
            _full((1, GM_WIDTH)),
            _full((GM_GROUPS, CHUNK, CHUNK)),
            _full((CHUNK, GM_WIDTH)),
            _full((1, GM_WIDTH)),
        ],
        out_specs=(
            pl.BlockSpec((tm, ATTN_HEADS * HEAD_PAD), row),
            pl.BlockSpec((tm, ATTN_HEADS * HEAD_PAD), row),
            pl.BlockSpec((tm, ATTN_HEADS * V_DIM), row),
            pl.BlockSpec((tm, GM_WIDTH), row),
            pl.BlockSpec((tm, KV_LORA), row),
            pl.BlockSpec((tm, QK_ROPE), row),
        ),
        compiler_params=pltpu.CompilerParams(
            dimension_semantics=("arbitrary",), vmem_limit_bytes=VMEM_LIMIT),
        name="pre_mixer",
    )(x, mod, rope_tab, g1, win, qg, wqb, kvg, wkvb, gmg, ws, bs, ogm)


def _cache_kv_kernel(ckv_ref, kr_ref, wkvb_ref, k_ref, v_ref):
    kv = _dot(ckv_ref[0, 0].astype(BF16), wkvb_ref[0])
    kr = kr_ref[0, 0].astype(BF16)
    for hd in range(ATTN_HEADS):
        k_ref[0, 0, :, hd * HEAD_PAD:hd * HEAD_PAD + QK_NOPE] = (
            kv[:, hd * QK_NOPE:(hd + 1) * QK_NOPE].astype(BF16))
        k_ref[0, 0, :, hd * HEAD_PAD + QK_NOPE:(hd + 1) * HEAD_PAD] = kr
    v_ref[0, 0] = kv[:, ATTN_HEADS * QK_NOPE:].astype(BF16)


def _cache_kv(cache_ckv, cache_kr_pad, wkvb):
    blk = lambda w: pl.BlockSpec((1, 1, PAST_LEN, w), lambda l, b: (b, l, 0, 0))
    return pl.pallas_call(
        _cache_kv_kernel,
        out_shape=(
            jax.ShapeDtypeStruct((DEC_BATCH, DEPTH, PAST_LEN, ATTN_HEADS * HEAD_PAD), BF16),
            jax.ShapeDtypeStruct((DEC_BATCH, DEPTH, PAST_LEN, ATTN_HEADS * V_DIM), BF16),
        ),
        grid=(DEPTH, DEC_BATCH),
        in_specs=[
            blk(KV_LORA),
            blk(LANES),
            pl.BlockSpec((1, KV_LORA, ATTN_HEADS * (QK_NOPE + V_DIM)), lambda l, b: (l, 0, 0)),
        ],
        out_specs=(blk(ATTN_HEADS * HEAD_PAD), blk(ATTN_HEADS * V_DIM)),
        compiler_params=pltpu.CompilerParams(
            dimension_semantics=("arbitrary", "arbitrary"), vmem_limit_bytes=VMEM_LIMIT),
        name="cache_kv",
    )(cache_ckv, cache_kr_pad, wkvb)


def _softmax_chunk(qh, kc, vc, m, l, acc):
    s = _dot_nt(qh, kc)
    m_new = jnp.maximum(m, jnp.max(s, axis=-1, keepdims=True))
    alpha = jnp.exp2(m - m_new)
    p = jnp.exp2(s - m_new)
    l = alpha * l + jnp.sum(p, axis=-1, keepdims=True)
    acc = alpha * acc + _dot(p.astype(BF16), vc)
    return m_new, l, acc


def _ctx_attn_kernel(q_ref, k_ref, v_ref, o_ref):
    for hd in range(ATTN_HEADS):
        hs = slice(hd * HEAD_PAD, (hd + 1) * HEAD_PAD)
        vs = slice(hd * V_DIM, (hd + 1) * V_DIM)
        s = _dot_nt(q_ref[:, hs], k_ref[:, hs])
        p = jnp.exp2(s - jnp.max(s, axis=-1, keepdims=True))
        l = jnp.sum(p, axis=-1, keepdims=True)
        o = _dot(p.astype(BF16), v_ref[:, vs])
        o_ref[:, vs] = (o / l).astype(o_ref.dtype)


def _ctx_attention(q, k, v):
    blk = lambda w: pl.BlockSpec((SEQ, w), lambda b: (b, 0))
    return pl.pallas_call(
        _ctx_attn_kernel,
        out_shape=jax.ShapeDtypeStruct((N_CTX, ATTN_HEADS * V_DIM), BF16),
        grid=(BATCH,),
        in_specs=[blk(ATTN_HEADS * HEAD_PAD), blk(ATTN_HEADS * HEAD_PAD), blk(ATTN_HEADS * V_DIM)],
        out_specs=blk(ATTN_HEADS * V_DIM),
        compiler_params=pltpu.CompilerParams(
            dimension_semantics=("arbitrary",), vmem_limit_bytes=VMEM_LIMIT),
        name="ctx_attention",
    )(q, k, v)


def _lat_attn_kernel(q_ref, k_ref, v_ref, kc_ref, vc_ref, o_ref):
    tq = q_ref.shape[0]
    for hd in range(ATTN_HEADS):
        hs = slice(hd * HEAD_PAD, (hd + 1) * HEAD_PAD)
        vs = slice(hd * V_DIM, (hd + 1) * V_DIM)
        qh = q_ref[:, hs]
        m0 = jnp.full((tq, 1), -jnp.inf, F32)
        l0 = jnp.zeros((tq, 1), F32)
        a0 = jnp.zeros((tq, V_DIM), F32)
        carry = _softmax_chunk(qh, kc_ref[0, 0, :, hs], vc_ref[0, 0, :, vs], m0, l0, a0)

        def body(c, carry):
            rows = pl.ds(pl.multiple_of(c * KV_CHUNK, KV_CHUNK), KV_CHUNK)
            return _softmax_chunk(qh, k_ref[rows, hs], v_ref[rows, vs], *carry)

        m, l, acc = lax.fori_loop(0, DEC_SEQ // KV_CHUNK, body, carry)
        o_ref[:, vs] = (acc / l).astype(o_ref.dtype)


def _lat_attention(q, k, v, kc, vc, layer):
    nq = DEC_SEQ // TQ
    ctx_q_tiles = N_CTX // TQ
    ctx_kv_blocks = N_CTX // DEC_SEQ
    return pl.pallas_call(
        _lat_attn_kernel,
        out_shape=jax.ShapeDtypeStruct((N_LAT, ATTN_HEADS * V_DIM), BF16),
        grid=(DEC_BATCH, nq),
        in_specs=[
            pl.BlockSpec((TQ, ATTN_HEADS * HEAD_PAD), lambda b, i: (ctx_q_tiles + b * nq + i, 0)),
            pl.BlockSpec((DEC_SEQ, ATTN_HEADS * HEAD_PAD), lambda b, i: (ctx_kv_blocks + b, 0)),
            pl.BlockSpec((DEC_SEQ, ATTN_HEADS * V_DIM), lambda b, i: (ctx_kv_blocks + b, 0)),
            pl.BlockSpec((1, 1, PAST_LEN, ATTN_HEADS * HEAD_PAD), lambda b, i: (b, layer, 0, 0)),
            pl.BlockSpec((1, 1, PAST_LEN, ATTN_HEADS * V_DIM), lambda b, i: (b, layer, 0, 0)),
        ],
        out_specs=pl.BlockSpec((TQ, ATTN_HEADS * V_DIM), lambda b, i: (b * nq + i, 0)),
        compiler_params=pltpu.CompilerParams(
            dimension_semantics=("arbitrary", "arbitrary"), vmem_limit_bytes=VMEM_LIMIT),
        name="lat_attention",
    )(q, k, v, kc, vc)


def _group_peer(x, lane, d, width, period):
    step = d * width
    ahead = pltpu.roll(x, LANES - step, 1)
    behind = pltpu.roll(x, period - step, 1)
    wraps = (lane & (period - 1)) + step >= period
    return jnp.where(wraps, behind, ahead), wraps


def _route(logits, bias, lane):
    valid = lane < N_EXPERTS
    s = 1.0 / (1.0 + jnp.exp(-logits))
    sb = s + bias
    rank = jnp.zeros(sb.shape, jnp.int32)
    for d in range(1, EXPERTS_PER_GROUP):
        o, wraps = _group_peer(sb, lane, d, 1, EXPERTS_PER_GROUP)
        beats = (o > sb) | ((o == sb) & wraps)
        rank = rank + beats.astype(jnp.int32)
    top2 = rank < 2
    t = jnp.where(top2, sb, 0.0)
    gscore = t
    for d in range(1, EXPERTS_PER_GROUP):
        gscore = gscore + _group_peer(t, lane, d, 1, EXPERTS_PER_GROUP)[0]
    grank = jnp.zeros(sb.shape, jnp.int32)
    for d in range(1, N_GROUPS):
        o, wraps = _group_peer(gscore, lane, d, EXPERTS_PER_GROUP, N_EXPERTS)
        beats = (o > gscore) | ((o == gscore) & wraps)
        grank = grank + beats.astype(jnp.int32)
    chosen = top2 & (grank == 0) & valid
    w = jnp.where(chosen, s, 0.0)
    denom = jnp.sum(w, axis=-1, keepdims=True)
    return w / denom


def _post_kernel(attn_ref, gm_ref, x_ref, mod_ref, oag_ref, wout_ref, g2_ref, wr_ref, rb_ref,
                 x1_ref, h2_ref, comb_ref):
    tm = x_ref.shape[0]
    mod = mod_ref[0]
    gate1 = mod[:, 2 * D_MODEL:3 * D_MODEL]
    shift2 = mod[:, 3 * D_MODEL:4 * D_MODEL]
    scale2 = mod[:, 4 * D_MODEL:5 * D_MODEL]
    an = (_rms(attn_ref[...].astype(F32)) * oag_ref[...]).astype(BF16)
    mixed = _dot(jnp.concatenate([an, gm_ref[...]], axis=1), wout_ref[...])
    x1 = x_ref[...] + gate1 * mixed
    x1_ref[...] = x1
    h2 = _rms(x1) * g2_ref[...] * (1.0 + scale2) + shift2
    h2_ref[...] = h2.astype(BF16)
    logits = jnp.dot(h2, wr_ref[...], preferred_element_type=F32,
                     precision=lax.Precision.HIGHEST)
    lane = lax.broadcasted_iota(jnp.int32, (tm, LANES), 1)
    comb_ref[...] = _route(logits, rb_ref[...], lane)


def _post_mixer(attn, gm, x, mod, oag, wout, g2, wr, rb):
    tm = TM_POST
    row = lambda i: (i, 0)
    return pl.pallas_call(
        _post_kernel,
        out_shape=(
            jax.ShapeDtypeStruct((N_ROWS, D_MODEL), F32),
            jax.ShapeDtypeStruct((N_ROWS, D_MODEL), BF16),
            jax.ShapeDtypeStruct((N_ROWS, LANES), F32),
        ),
        grid=(N_ROWS // tm,),
        in_specs=[
            pl.BlockSpec((tm, ATTN_HEADS * V_DIM), row),
            pl.BlockSpec((tm, GM_WIDTH), row),
            pl.BlockSpec((tm, D_MODEL), row),
            pl.BlockSpec((1, 1, 6 * D_MODEL), lambda i: (_cond_row(i, tm), 0, 0)),
            _full((1, ATTN_HEADS * V_DIM)),
            _full((D_MODEL, D_MODEL)),
            _full((1, D_MODEL)),
            _full((D_MODEL, LANES)),
            _full((1, LANES)),
        ],
        out_specs=(
            pl.BlockSpec((tm, D_MODEL), row),
            pl.BlockSpec((tm, D_MODEL), row),
            pl.BlockSpec((tm, LANES), row),
        ),
        compiler_params=pltpu.CompilerParams(
            dimension_semantics=("arbitrary",), vmem_limit_bytes=VMEM_LIMIT),
        name="post_mixer",
    )(attn, gm, x, mod, oag, wout, g2, wr, rb)


def _moe_kernel(h2_ref, comb_ref, x1_ref, mod_ref, wgu_ref, wd_ref, wsgu_ref, wsd_ref, fg_ref,
                o_ref, acc_ref, *, final):
    e = pl.program_id(1)
    tm = h2_ref.shape[0]
    h2 = h2_ref[...]

    @pl.when(e == 0)
    def _():
        gu = _dot(h2, wsgu_ref[...])
        act = _silu(gu[:, :SHARED_FF]) * gu[:, SHARED_FF:]
        acc_ref[...] = _dot(act.astype(BF16), wsd_ref[...])

    gu = _dot(h2, wgu_ref[0])
    act = _silu(gu[:, :EXPERT_FF]) * gu[:, EXPERT_FF:]
    lane = lax.broadcasted_iota(jnp.int32, (tm, LANES), 1)
    ce = jnp.sum(jnp.where(lane == e, comb_ref[...], 0.0), axis=-1, keepdims=True)
    acc_ref[...] += _dot((act * ce).astype(BF16), wd_ref[0])

    @pl.when(e == N_EXPERTS - 1)
    def _():
        gate2 = mod_ref[0][:, 5 * D_MODEL:6 * D_MODEL]
        x2 = x1_ref[...] + gate2 * acc_ref[...]
        if final:
            x2 = _rms(x2) * fg_ref[...]
        o_ref[...] = x2


def _experts(h2, comb, x1, mod, wgu, wd, wsgu, wsd, fg, final):
    tm = TM_MOE
    row = lambda i, e: (i, 0)
    return pl.pallas_call(
        functools.partial(_moe_kernel, final=final),
        out_shape=jax.ShapeDtypeStruct((N_ROWS, D_MODEL), F32),
        grid=(N_ROWS // tm, N_EXPERTS),
        in_specs=[
            pl.BlockSpec((tm, D_MODEL), row),
            pl.BlockSpec((tm, LANES), row),
            pl.BlockSpec((tm, D_MODEL), row),
            pl.BlockSpec((1, 1, 6 * D_MODEL), lambda i, e: (_cond_row(i, tm), 0, 0)),
            pl.BlockSpec((1, D_MODEL, 2 * EXPERT_FF), lambda i, e: (e, 0, 0)),
            pl.BlockSpec((1, EXPERT_FF, D_MODEL), lambda i, e: (e, 0, 0)),
            pl.BlockSpec((D_MODEL, 2 * SHARED_FF), lambda i, e: (0, 0)),
            pl.BlockSpec((SHARED_FF, D_MODEL), lambda i, e: (0, 0)),
            pl.BlockSpec((1, D_MODEL), lambda i, e: (0, 0)),
        ],
        out_specs=pl.BlockSpec((tm, D_MODEL), row),
        scratch_shapes=[pltpu.VMEM((tm, D_MODEL), F32)],
        compiler_params=pltpu.CompilerParams(
            dimension_semantics=("arbitrary", "arbitrary"), vmem_limit_bytes=VMEM_LIMIT),
        name="experts",
    )(h2, comb, x1, mod, wgu, wd, wsgu, wsd, fg)


def _rope_table():
    rows = DEC_SEQ // GRID_W
    row = jnp.repeat(jnp.arange(rows, dtype=F32), GRID_W)
    col = jnp.tile(jnp.arange(GRID_W, dtype=F32), rows)
    half = QK_ROPE // 2
    freqs = 1.0 / (ROPE_BASE ** (jnp.arange(0, half, 2, dtype=F32) / half))
    ang = jnp.concatenate([row[:, None] * freqs, col[:, None] * freqs], axis=-1)
    cos, sin = jnp.cos(ang), jnp.sin(ang)
    lat = jnp.concatenate([cos, cos, -sin, sin], axis=-1)
    ident = jnp.concatenate([jnp.ones((TM_PRE, QK_ROPE), F32), jnp.zeros((TM_PRE, QK_ROPE), F32)], axis=-1)
    return jnp.concatenate([ident, lat], axis=0)


_DEINT = np.concatenate([np.arange(0, QK_ROPE, 2), np.arange(1, QK_ROPE, 2)])
_SWAP = np.concatenate([np.arange(1, QK_ROPE, 2), np.arange(0, QK_ROPE, 2)])
_INTERLEAVE = np.argsort(_DEINT)


def _layout_w_in(w_in):
    cq_ckv = w_in[:, :Q_LORA + KV_LORA]
    kr = w_in[:, Q_LORA + KV_LORA:Q_LORA + KV_LORA + QK_ROPE]
    uv = w_in[:, Q_LORA + KV_LORA + QK_ROPE:]
    return jnp.concatenate([cq_ckv, kr[:, _DEINT], kr[:, _SWAP], uv], axis=1).astype(BF16)


def _layout_w_qb(w_qb):
    w = (w_qb * (ATTN_SCALE * LOG2E)).reshape(Q_LORA, ATTN_HEADS, QK_NOPE + QK_ROPE)
    nope, rope = w[..., :QK_NOPE], w[..., QK_NOPE:]
    w = jnp.concatenate([nope, rope[..., _DEINT], rope[..., _SWAP]], axis=-1)
    return w.reshape(Q_LORA, ATTN_HEADS * HEAD_PAD).astype(BF16)


def _layout_w_kvb(w_kvb):
    w = w_kvb.reshape(KV_LORA, ATTN_HEADS, QK_NOPE + V_DIM)
    k = w[..., :QK_NOPE].reshape(KV_LORA, ATTN_HEADS * QK_NOPE)
    v = w[..., QK_NOPE:].reshape(KV_LORA, ATTN_HEADS * V_DIM)
    return jnp.concatenate([k, v], axis=1).astype(BF16)


def kernel(x_prompt, x_sample, cache_ckv, cache_krope, c, c_ctx, norm1_g, w_ada, b_ada, w_in,
           q_norm_g, w_qb, kv_norm_g, w_kvb, gm_norm_g, w_spatial, b_spatial, onorm_attn_g,
           onorm_gm_g, w_out, norm2_g, w_router, router_bias, w_gate, w_up, w_down, ws_gate,
           ws_up, ws_down, final_norm_g):
    x = jnp.concatenate([x_prompt.reshape(N_CTX, D_MODEL), x_sample.reshape(N_LAT, D_MODEL)], axis=0)
    cond = jnp.concatenate([c_ctx[None, :], c, jnp.zeros((COND_PAD - N_COND, D_MODEL), F32)], axis=0)
    mod = _modulation(cond, w_ada, b_ada)
    rope_tab = _rope_table()

    wkvb = jnp.stack([_layout_w_kvb(w_kvb[l]) for l in range(DEPTH)])
    cache_kr = jnp.pad(cache_krope[..., _DEINT], ((0, 0), (0, 0), (0, 0), (0, LANES - QK_ROPE)))
    kc, vc = _cache_kv(cache_ckv, cache_kr, wkvb)

    wr = jnp.pad(w_router, ((0, 0), (0, LANES - N_EXPERTS)))
    rb = jnp.pad(router_bias, (0, LANES - N_EXPERTS)).reshape(1, LANES)
    fg = final_norm_g.reshape(1, D_MODEL)

    ckv_out, kr_out = [], []
    for l in range(DEPTH):
        mod_l = mod[l].reshape(COND_PAD, 1, 6 * D_MODEL)
        bs = jnp.broadcast_to(b_spatial[l].T[:, :, None], (CHUNK, GM_GROUPS, GM_GROUP_DIM))
        q, k, v, gm, ckv_n, kr = _pre_mixer(
            x, mod_l, rope_tab, norm1_g[l].reshape(1, -1), _layout_w_in(w_in[l]),
            q_norm_g[l].reshape(1, -1), _layout_w_qb(w_qb[l]), kv_norm_g[l].reshape(1, -1),
            wkvb[l], gm_norm_g[l].reshape(1, -1), w_spatial[l].astype(BF16),
            bs.reshape(CHUNK, GM_WIDTH), onorm_gm_g[l].reshape(1, -1))
        ckv_out.append(ckv_n[:N_CTX].reshape(BATCH, SEQ, KV_LORA))
        kr_out.append(kr[:N_CTX][:, _INTERLEAVE].reshape(BATCH, SEQ, QK_ROPE))
        attn = jnp.concatenate([_ctx_attention(q, k, v), _lat_attention(q, k, v, kc, vc, l)], axis=0)
        x1, h2, comb = _post_mixer(
            attn, gm, x, mod_l, onorm_attn_g[l].reshape(1, -1), w_out[l].astype(BF16),
            norm2_g[l].reshape(1, -1), wr, rb)
        wgu = jnp.concatenate([w_gate[l], w_up[l]], axis=-1).astype(BF16)
        wsgu = jnp.concatenate([ws_gate[l], ws_up[l]], axis=-1).astype(BF16)
        x = _experts(h2, comb, x1, mod_l, wgu, w_down[l].astype(BF16), wsgu,
                     ws_down[l].astype(BF16), fg, final=(l == DEPTH - 1))

    y_prompt = x[:N_CTX].reshape(BATCH, SEQ, D_MODEL)
    y_sample = x[N_CTX:].reshape(DEC_BATCH, DEC_SEQ, D_MODEL)
    return y_prompt, y_sample, jnp.stack(ckv_out, axis=1), jnp.stack(kr_out, axis=1)
```

```python
import functools
import math

import jax
import jax.numpy as jnp
import numpy as np
from jax import lax
from jax.experimental import pallas as pl
from jax.experimental.pallas import tpu as pltpu

D_MODEL = 1024
BATCH = 16
SEQ = 256
DEPTH = 2
DEC_BATCH = 4
DEC_SEQ = 4096
PAST_LEN = 256
GRID_W = 64
ATTN_HEADS = 4
QK_NOPE = 128
QK_ROPE = 64
V_DIM = 128
Q_LORA = 384
KV_LORA = 256
ATTN_SCALE = (QK_NOPE + QK_ROPE) ** -0.5
ROPE_BASE = 10000.0
CHUNK = 128
GM_WIDTH = 512
GM_GROUPS = 4
GM_GROUP_DIM = 128
N_EXPERTS = 16
N_GROUPS = 4
EXPERTS_PER_GROUP = 4
EXPERT_FF = 256
SHARED_FF = 256
EPS = 1e-6

N_CTX = BATCH * SEQ
N_LAT = DEC_BATCH * DEC_SEQ
N_ROWS = N_CTX + N_LAT
N_COND = 1 + DEC_BATCH
COND_PAD = 8
HEAD_PAD = 256
IN_COLS = Q_LORA + KV_LORA + 2 * QK_ROPE + 2 * GM_WIDTH
LANES = 128
LOG2E = 1.4426950408889634

TM_PRE = 256
TM_POST = 256
PIECE = 16
LS_ROWS = TM_POST + N_GROUPS * PIECE
LS_PIECES = LS_ROWS // PIECE
N_POST_TILES = N_ROWS // TM_POST
N_LS = N_POST_TILES * LS_ROWS
TM_E = 512
E_PIECES = TM_E // PIECE
N_ETILES = -(-(N_POST_TILES * (TM_POST // PIECE + N_GROUPS - 1)) // E_PIECES) + N_GROUPS
TQ = 512
KV_CHUNK = 1024
VMEM_LIMIT = 56 * 1024 * 1024

F32 = jnp.float32
BF16 = jnp.bfloat16


def _rms(x):
    return x * lax.rsqrt(jnp.mean(x * x, axis=-1, keepdims=True) + EPS)


def _gelu(x):
    return 0.5 * x * (1.0 + jnp.tanh(math.sqrt(2.0 / math.pi) * (x + 0.044715 * (x * x * x))))


def _silu(x):
    return x * (1.0 / (1.0 + jnp.exp(-x)))


def _dot(a, b):
    return jnp.dot(a, b, preferred_element_type=F32)


def _dot_nt(a, b):
    return lax.dot_general(a, b, (((1,), (1,)), ((), ())), preferred_element_type=F32)


def _cond_row(i, tm):
    n_ctx_tiles = N_CTX // tm
    per_batch = DEC_SEQ // tm
    return jnp.where(i < n_ctx_tiles, 0, 1 + (i - n_ctx_tiles) // per_batch)


def _rope_block(i, tm):
    n_ctx_tiles = N_CTX // tm
    per_batch = DEC_SEQ // tm
    return jnp.where(i < n_ctx_tiles, 0, 1 + (i - n_ctx_tiles) % per_batch)


def _full(shape):
    n = len(shape)
    return pl.BlockSpec(shape, lambda *_: (0,) * n)


def _mod_kernel(cond_ref, w_ref, b_ref, o_ref):
    s = _silu(cond_ref[...])
    o_ref[0] = jnp.dot(s, w_ref[0], preferred_element_type=F32,
                       precision=lax.Precision.HIGHEST) + b_ref[0]


def _modulation(cond, w_ada, b_ada):
    tn = 1536
    return pl.pallas_call(
        _mod_kernel,
        out_shape=jax.ShapeDtypeStruct((DEPTH, COND_PAD, 6 * D_MODEL), F32),
        grid=(DEPTH, 6 * D_MODEL // tn),
        in_specs=[
            pl.BlockSpec((COND_PAD, D_MODEL), lambda l, j: (0, 0)),
            pl.BlockSpec((1, D_MODEL, tn), lambda l, j: (l, 0, j)),
            pl.BlockSpec((1, 1, tn), lambda l, j: (l, 0, j)),
        ],
        out_specs=pl.BlockSpec((1, COND_PAD, tn), lambda l, j: (l, 0, j)),
        compiler_params=pltpu.CompilerParams(
            dimension_semantics=("arbitrary", "arbitrary"), vmem_limit_bytes=VMEM_LIMIT),
        name="modulation",
    )(cond, w_ada, b_ada.reshape(DEPTH, 1, 6 * D_MODEL))


def _pre_kernel(x_ref, mod_ref, rope_ref, g1_ref, win_ref, qg_ref, wqb_ref, kvg_ref,
                wkvb_ref, gmg_ref, ws_ref, bs_ref, ogm_ref,
                q_ref, k_ref, v_ref, gm_ref, ckv_ref, kr_ref):
    tm = x_ref.shape[0]
    mod = mod_ref[0]
    shift1 = mod[:, 0:D_MODEL]
    scale1 = mod[:, D_MODEL:2 * D_MODEL]
    h = _rms(x_ref[...]) * g1_ref[...] * (1.0 + scale1) + shift1
    y = _dot(h.astype(BF16), win_ref[...])
    cq = y[:, 0:Q_LORA]
    ckv = y[:, Q_LORA:Q_LORA + KV_LORA]
    kr2 = y[:, 640:768]
    u = y[:, 768:768 + GM_WIDTH]
    vv = y[:, 768 + GM_WIDTH:768 + 2 * GM_WIDTH]

    rope = rope_ref[...]
    lane = lax.broadcasted_iota(jnp.int32, (tm, LANES), 1)

    kr_ref[...] = kr2[:, 0:QK_ROPE]
    t = kr2 * rope
    k_rot = jnp.where(lane < QK_ROPE, t + pltpu.roll(t, QK_ROPE, 1), 0.0).astype(BF16)

    ckv_n = _rms(ckv) * kvg_ref[...]
    ckv_ref[...] = ckv_n
    kv = _dot(ckv_n.astype(BF16), wkvb_ref[...])
    for hd in range(ATTN_HEADS):
        k_ref[:, hd * HEAD_PAD:hd * HEAD_PAD + QK_NOPE] = (
            kv[:, hd * QK_NOPE:(hd + 1) * QK_NOPE].astype(BF16))
        k_ref[:, hd * HEAD_PAD + QK_NOPE:(hd + 1) * HEAD_PAD] = k_rot
    v_ref[...] = kv[:, ATTN_HEADS * QK_NOPE:].astype(BF16)

    q = _dot((_rms(cq) * qg_ref[...]).astype(BF16), wqb_ref[...])
    for hd in range(ATTN_HEADS):
        q_ref[:, hd * HEAD_PAD:hd * HEAD_PAD + QK_NOPE] = (
            q[:, hd * HEAD_PAD:hd * HEAD_PAD + QK_NOPE].astype(BF16))
        t = q[:, hd * HEAD_PAD + QK_NOPE:(hd + 1) * HEAD_PAD] * rope
        q_ref[:, hd * HEAD_PAD + QK_NOPE:(hd + 1) * HEAD_PAD] = (
            t + pltpu.roll(t, QK_ROPE, 1)).astype(BF16)

    ug = _gelu(u)
    vg = _gelu(vv)
    cols = []
    for g in range(GM_GROUPS):
        sl = slice(g * GM_GROUP_DIM, (g + 1) * GM_GROUP_DIM)
        vn = (_rms(vg[:, sl]) * gmg_ref[:, sl]).astype(BF16)
        rows = []
        for c in range(tm // CHUNK):
            sv = _dot(ws_ref[g], vn[c * CHUNK:(c + 1) * CHUNK]) + bs_ref[:, sl]
            rows.append(ug[c * CHUNK:(c + 1) * CHUNK, sl] * sv)
        cols.append(jnp.concatenate(rows, axis=0))
    gm = jnp.concatenate(cols, axis=1)
    gm_ref[...] = (_rms(gm) * ogm_ref[...]).astype(BF16)


def _pre_mixer(x, mod, rope_tab, g1, win, qg, wqb, kvg, wkvb, gmg, ws, bs, ogm):
    tm = TM_PRE
    row = lambda i: (i, 0)
    return pl.pallas_call(
        _pre_kernel,
        out_shape=(
            jax.ShapeDtypeStruct((N_ROWS, ATTN_HEADS * HEAD_PAD), BF16),
            jax.ShapeDtypeStruct((N_ROWS, ATTN_HEADS * HEAD_PAD), BF16),
            jax.ShapeDtypeStruct((N_ROWS, ATTN_HEADS * V_DIM), BF16),
            jax.ShapeDtypeStruct((N_ROWS, GM_WIDTH), BF16),
            jax.ShapeDtypeStruct((N_ROWS, KV_LORA), F32),
            jax.ShapeDtypeStruct((N_ROWS, QK_ROPE), F32),
        ),
        grid=(N_ROWS // tm,),
        in_specs=[
            pl.BlockSpec((tm, D_MODEL), row),
            pl.BlockSpec((1, 1, 6 * D_MODEL), lambda i: (_cond_row(i, tm), 0, 0)),
            pl.BlockSpec((tm, LANES), lambda i: (_rope_block(i, tm), 0)),
            _full((1, D_MODEL)),
            _full((D_MODEL, IN_COLS)),
            _full((1, Q_LORA)),
            _full((Q_LORA, ATTN_HEADS * HEAD_PAD)),
            _full((1, KV_LORA)),
            _full((KV_LORA, ATTN_HEADS * (QK_NOPE + V_DIM))),
            _full((1, GM_WIDTH)),
            _full((GM_GROUPS, CHUNK, CHUNK)),
            _full((CHUNK, GM_WIDTH)),
            _full((1, GM_WIDTH)),
        ],
        out_specs=(
            pl.BlockSpec((tm, ATTN_HEADS * HEAD_PAD), row),
            pl.BlockSpec((tm, ATTN_HEADS * HEAD_PAD), row),
            pl.BlockSpec((tm, ATTN_HEADS * V_DIM), row),
            pl.BlockSpec((tm, GM_WIDTH), row),
            pl.BlockSpec((tm, KV_LORA), row),
            pl.BlockSpec((tm, QK_ROPE), row),
        ),
        compiler_params=pltpu.CompilerParams(
            dimension_semantics=("arbitrary",), vmem_limit_bytes=VMEM_LIMIT),
        name="pre_mixer",
    )(x, mod, rope_tab, g1, win, qg, wqb, kvg, wkvb, gmg, ws, bs, ogm)


def _cache_kv_kernel(ckv_ref, kr_ref, wkvb_ref, k_ref, v_ref):
    kv = _dot(ckv_ref[0, 0].astype(BF16), wkvb_ref[0])
    kr = kr_ref[0, 0].astype(BF16)
    for hd in range(ATTN_HEADS):
        k_ref[0, 0, :, hd * HEAD_PAD:hd * HEAD_PAD + QK_NOPE] = (
            kv[:, hd * QK_NOPE:(hd + 1) * QK_NOPE].astype(BF16))
        k_ref[0, 0, :, hd * HEAD_PAD + QK_NOPE:(hd + 1) * HEAD_PAD] = kr
    v_ref[0, 0] = kv[:, ATTN_HEADS * QK_NOPE:].astype(BF16)


def _cache_kv(cache_ckv, cache_kr_pad, wkvb):
    blk = lambda w: pl.BlockSpec((1, 1, PAST_LEN, w), lambda l, b: (b, l, 0, 0))
    return pl.pallas_call(
        _cache_kv_kernel,
        out_shape=(
            jax.ShapeDtypeStruct((DEC_BATCH, DEPTH, PAST_LEN, ATTN_HEADS * HEAD_PAD), BF16),
            jax.ShapeDtypeStruct((DEC_BATCH, DEPTH, PAST_LEN, ATTN_HEADS * V_DIM), BF16),
        ),
        grid=(DEPTH, DEC_BATCH),
        in_specs=[
            blk(KV_LORA),
            blk(LANES),
            pl.BlockSpec((1, KV_LORA, ATTN_HEADS * (QK_NOPE + V_DIM)), lambda l, b: (l, 0, 0)),
        ],
        out_specs=(blk(ATTN_HEADS * HEAD_PAD), blk(ATTN_HEADS * V_DIM)),
        compiler_params=pltpu.CompilerParams(
            dimension_semantics=("arbitrary", "arbitrary"), vmem_limit_bytes=VMEM_LIMIT),
        name="cache_kv",
    )(cache_ckv, cache_kr_pad, wkvb)


def _softmax_chunk(qh, kc, vc, m, l, acc):
    s = _dot_nt(qh, kc)
    m_new = jnp.maximum(m, jnp.max(s, axis=-1, keepdims=True))
    alpha = jnp.exp2(m - m_new)
    p = jnp.exp2(s - m_new)
    l = alpha * l + jnp.sum(p, axis=-1, keepdims=True)
    acc = alpha * acc + _dot(p.astype(BF16), vc)
    return m_new, l, acc


def _ctx_attn_kernel(q_ref, k_ref, v_ref, o_ref):
    for hd in range(ATTN_HEADS):
        hs = slice(hd * HEAD_PAD, (hd + 1) * HEAD_PAD)
        vs = slice(hd * V_DIM, (hd + 1) * V_DIM)
        s = _dot_nt(q_ref[:, hs], k_ref[:, hs])
        p = jnp.exp2(s - jnp.max(s, axis=-1, keepdims=True))
        l = jnp.sum(p, axis=-1, keepdims=True)
        o = _dot(p.astype(BF16), v_ref[:, vs])
        o_ref[:, vs] = (o / l).astype(o_ref.dtype)


def _ctx_attention(q, k, v):
    blk = lambda w: pl.BlockSpec((SEQ, w), lambda b: (b, 0))
    return pl.pallas_call(
        _ctx_attn_kernel,
        out_shape=jax.ShapeDtypeStruct((N_CTX, ATTN_HEADS * V_DIM), BF16),
        grid=(BATCH,),
        in_specs=[blk(ATTN_HEADS * HEAD_PAD), blk(ATTN_HEADS * HEAD_PAD), blk(ATTN_HEADS * V_DIM)],
        out_specs=blk(ATTN_HEADS * V_DIM),
        compiler_params=pltpu.CompilerParams(
            dimension_semantics=("arbitrary",), vmem_limit_bytes=VMEM_LIMIT),
        name="ctx_attention",
    )(q, k, v)


def _lat_attn_kernel(q_ref, k_ref, v_ref, kc_ref, vc_ref, o_ref):
    tq = q_ref.shape[0]
    for hd in range(ATTN_HEADS):
        hs = slice(hd * HEAD_PAD, (hd + 1) * HEAD_PAD)
        vs = slice(hd * V_DIM, (hd + 1) * V_DIM)
        qh = q_ref[:, hs]
        m0 = jnp.full((tq, 1), -jnp.inf, F32)
        l0 = jnp.zeros((tq, 1), F32)
        a0 = jnp.zeros((tq, V_DIM), F32)
        carry = _softmax_chunk(qh, kc_ref[0, 0, :, hs], vc_ref[0, 0, :, vs], m0, l0, a0)

        def body(c, carry):
            rows = pl.ds(pl.multiple_of(c * KV_CHUNK, KV_CHUNK), KV_CHUNK)
            return _softmax_chunk(qh, k_ref[rows, hs], v_ref[rows, vs], *carry)

        m, l, acc = lax.fori_loop(0, DEC_SEQ // KV_CHUNK, body, carry)
        o_ref[:, vs] = (acc / l).astype(o_ref.dtype)


def _lat_attention(q, k, v, kc, vc, layer):
    nq = DEC_SEQ // TQ
    ctx_q_tiles = N_CTX // TQ
    ctx_kv_blocks = N_CTX // DEC_SEQ
    return pl.pallas_call(
        _lat_attn_kernel,
        out_shape=jax.ShapeDtypeStruct((N_LAT, ATTN_HEADS * V_DIM), BF16),
        grid=(DEC_BATCH, nq),
        in_specs=[
            pl.BlockSpec((TQ, ATTN_HEADS * HEAD_PAD), lambda b, i: (ctx_q_tiles + b * nq + i, 0)),
            pl.BlockSpec((DEC_SEQ, ATTN_HEADS * HEAD_PAD), lambda b, i: (ctx_kv_blocks + b, 0)),
            pl.BlockSpec((DEC_SEQ, ATTN_HEADS * V_DIM), lambda b, i: (ctx_kv_blocks + b, 0)),
            pl.BlockSpec((1, 1, PAST_LEN, ATTN_HEADS * HEAD_PAD), lambda b, i: (b, layer, 0, 0)),
            pl.BlockSpec((1, 1, PAST_LEN, ATTN_HEADS * V_DIM), lambda b, i: (b, layer, 0, 0)),
        ],
        out_specs=pl.BlockSpec((TQ, ATTN_HEADS * V_DIM), lambda b, i: (b * nq + i, 0)),
        compiler_params=pltpu.CompilerParams(
            dimension_semantics=("arbitrary", "arbitrary"), vmem_limit_bytes=VMEM_LIMIT),
        name="lat_attention",
    )(q, k, v, kc, vc)


def _group_peer(x, lane, d, width, period):
    step = d * width
    ahead = pltpu.roll(x, LANES - step, 1)
    behind = pltpu.roll(x, period - step, 1)
    wraps = (lane & (period - 1)) + step >= period
    return jnp.where(wraps, behind, ahead), wraps


def _route(logits, bias, lane):
    valid = lane < N_EXPERTS
    s = 1.0 / (1.0 + jnp.exp(-logits))
    sb = s + bias
    rank = jnp.zeros(sb.shape, jnp.int32)
    for d in range(1, EXPERTS_PER_GROUP):
        o, wraps = _group_peer(sb, lane, d, 1, EXPERTS_PER_GROUP)
        beats = (o > sb) | ((o == sb) & wraps)
        rank = rank + beats.astype(jnp.int32)
    top2 = rank < 2
    t = jnp.where(top2, sb, 0.0)
    gscore = t
    for d in range(1, EXPERTS_PER_GROUP):
        gscore = gscore + _group_peer(t, lane, d, 1, EXPERTS_PER_GROUP)[0]
    grank = jnp.zeros(sb.shape, jnp.int32)
    for d in range(1, N_GROUPS):
        o, wraps = _group_peer(gscore, lane, d, EXPERTS_PER_GROUP, N_EXPERTS)
        beats = (o > gscore) | ((o == gscore) & wraps)
        grank = grank + beats.astype(jnp.int32)
    in_group = (grank == 0) & valid
    w = jnp.where(top2 & in_group, s, 0.0)
    denom = jnp.sum(w, axis=-1, keepdims=True)
    group_flag = jnp.where(in_group & ((lane & (EXPERTS_PER_GROUP - 1)) == 0), 1.0, 0.0)
    return w / denom, group_flag


def _post_kernel(attn_ref, gm_ref, x_ref, mod_ref, oag_ref, wout_ref, g2_ref, wr_ref, rb_ref,
                 x1_ref, h2s_ref, combs_ref, y0_ref, pos_ref, cnt_ref):
    tm = x_ref.shape[0]
    mod = mod_ref[0]
    gate1 = mod[:, 2 * D_MODEL:3 * D_MODEL]
    shift2 = mod[:, 3 * D_MODEL:4 * D_MODEL]
    scale2 = mod[:, 4 * D_MODEL:5 * D_MODEL]
    an = (_rms(attn_ref[...].astype(F32)) * oag_ref[...]).astype(BF16)
    mixed = _dot(jnp.concatenate([an, gm_ref[...]], axis=1), wout_ref[...])
    x1 = x_ref[...] + gate1 * mixed
    x1_ref[...] = x1
    h2 = _rms(x1) * g2_ref[...] * (1.0 + scale2) + shift2
    logits = jnp.dot(h2, wr_ref[...], preferred_element_type=F32,
                     precision=lax.Precision.HIGHEST)
    lane = lax.broadcasted_iota(jnp.int32, (tm, LANES), 1)
    comb, flag = _route(logits, rb_ref[...], lane)

    r_i = lax.broadcasted_iota(jnp.int32, (tm, tm), 0)
    c_i = lax.broadcasted_iota(jnp.int32, (tm, tm), 1)
    before = jnp.where(c_i < r_i, 1.0, 0.0).astype(BF16)
    rank = _dot(before, flag.astype(BF16))
    count = jnp.sum(flag, axis=0, keepdims=True)
    pieces = jnp.floor((count + (PIECE - 1)) * (1.0 / PIECE))
    start = (pltpu.roll(pieces, EXPERTS_PER_GROUP, 1) + pltpu.roll(pieces, 2 * EXPERTS_PER_GROUP, 1)
             + pltpu.roll(pieces, 3 * EXPERTS_PER_GROUP, 1)) * PIECE
    pos = jnp.sum(flag * (start + rank), axis=-1, keepdims=True)
    pos_ref[...] = pos
    cnt_ref[0] = count.astype(jnp.int32)
    pos_row = jnp.transpose(jnp.broadcast_to(pos, (tm, LANES)))[0:1, :].astype(jnp.int32)
    place = lax.broadcasted_iota(jnp.int32, (LS_ROWS, tm), 0) == pos_row
    h2s_ref[...] = _dot(jnp.where(place, 1.0, 0.0).astype(BF16), h2.astype(BF16)).astype(BF16)
    combs_ref[...] = jnp.dot(jnp.where(place, 1.0, 0.0), comb, preferred_element_type=F32,
                             precision=lax.Precision.HIGHEST)
    y0_ref[...] = jnp.zeros(y0_ref.shape, y0_ref.dtype)


def _post_mixer(attn, gm, x, mod, oag, wout, g2, wr, rb):
    tm = TM_POST
    row = lambda i: (i, 0)
    return pl.pallas_call(
        _post_kernel,
        out_shape=(
            jax.ShapeDtypeStruct((N_ROWS, D_MODEL), F32),
            jax.ShapeDtypeStruct((N_LS, D_MODEL), BF16),
            jax.ShapeDtypeStruct((N_LS, LANES), F32),
            jax.ShapeDtypeStruct((N_LS, D_MODEL), BF16),
            jax.ShapeDtypeStruct((N_ROWS, 1), F32),
            jax.ShapeDtypeStruct((N_POST_TILES, 1, LANES), jnp.int32),
        ),
        grid=(N_ROWS // tm,),
        in_specs=[
            pl.BlockSpec((tm, ATTN_HEADS * V_DIM), row),
            pl.BlockSpec((tm, GM_WIDTH), row),
            pl.BlockSpec((tm, D_MODEL), row),
            pl.BlockSpec((1, 1, 6 * D_MODEL), lambda i: (_cond_row(i, tm), 0, 0)),
            _full((1, ATTN_HEADS * V_DIM)),
            _full((D_MODEL, D_MODEL)),
            _full((1, D_MODEL)),
            _full((D_MODEL, LANES)),
            _full((1, LANES)),
        ],
        out_specs=(
            pl.BlockSpec((tm, D_MODEL), row),
            pl.BlockSpec((LS_ROWS, D_MODEL), row),
            pl.BlockSpec((LS_ROWS, LANES), row),
            pl.BlockSpec((LS_ROWS, D_MODEL), row),
            pl.BlockSpec((tm, 1), row),
            pl.BlockSpec((1, 1, LANES), lambda i: (i, 0, 0)),
        ),
        compiler_params=pltpu.CompilerParams(
            dimension_semantics=("arbitrary",), vmem_limit_bytes=VMEM_LIMIT),
        name="post_mixer",
    )(attn, gm, x, mod, oag, wout, g2, wr, rb)


def _plan_kernel(cnt_ref, tg_ref, tn_ref, src_ref, nu_ref):
    def clear_src(j, c):
        src_ref[j] = 0
        return c

    lax.fori_loop(0, N_ETILES * E_PIECES, clear_src, 0)

    def clear_tile(j, c):
        tg_ref[j] = N_GROUPS - 1
        tn_ref[j] = 0
        return c

    lax.fori_loop(0, N_ETILES, clear_tile, 0)

    def n_pieces(i, g):
        return lax.shift_right_logical(cnt_ref[i, g * EXPERTS_PER_GROUP] + (PIECE - 1), 4)

    t = jnp.int32(0)
    for g in range(N_GROUPS):
        def tile_body(i, carry, g=g):
            first = i * LS_PIECES
            for gp in range(g):
                first = first + n_pieces(i, gp)

            def piece_body(p, carry):
                t, k = carry
                src_ref[t * E_PIECES + k] = first + p
                k = k + 1
                full = k == E_PIECES

                @pl.when(full)
                def _():
                    tn_ref[t] = E_PIECES
                    tg_ref[t] = g

                return jnp.where(full, t + 1, t), jnp.where(full, 0, k)

            return lax.fori_loop(0, n_pieces(i, g), piece_body, carry)

        t, k = lax.fori_loop(0, N_POST_TILES, tile_body, (t, jnp.int32(0)))

        @pl.when(k > 0)
        def _(t=t, k=k, g=g):
            tn_ref[t] = k
            tg_ref[t] = g

        t = jnp.where(k > 0, t + 1, t)
    nu_ref[0] = t


def _plan(counts):
    smem = pl.BlockSpec(memory_space=pltpu.SMEM)
    return pl.pallas_call(
        _plan_kernel,
        out_shape=(
            jax.ShapeDtypeStruct((N_ETILES,), jnp.int32),
            jax.ShapeDtypeStruct((N_ETILES,), jnp.int32),
            jax.ShapeDtypeStruct((N_ETILES * E_PIECES,), jnp.int32),
            jax.ShapeDtypeStruct((1,), jnp.int32),
        ),
        in_specs=[smem],
        out_specs=(smem, smem, smem, smem),
        name="expert_plan",
    )(counts)


W1_COLS = 2 * EXPERTS_PER_GROUP * EXPERT_FF + 2 * SHARED_FF
W2_ROWS = EXPERTS_PER_GROUP * EXPERT_FF + SHARED_FF
GATHER_X, GATHER_W, SCATTER_Y = 0, 1, 2


def _moe_kernel(tg_ref, tn_ref, src_ref, nu_ref, h2s_hbm, combs_hbm, y0_hbm, w1_ref, w2_ref,
                y_hbm, xbuf, cbuf, ybuf, sem):
    del y0_hbm
    j = pl.program_id(0)
    n_used = nu_ref[0]
    slot = lax.rem(j, 2)

    def piece_rows(t, k):
        hbm_rows = pl.ds(pl.multiple_of(src_ref[t * E_PIECES + k] * PIECE, PIECE), PIECE)
        buf_rows = pl.ds(pl.multiple_of(k * PIECE, PIECE), PIECE)
        return hbm_rows, buf_rows

    def gather_copies(t, k, slot):
        hbm_rows, buf_rows = piece_rows(t, k)
        return (
            pltpu.make_async_copy(h2s_hbm.at[hbm_rows], xbuf.at[slot, buf_rows], sem.at[GATHER_X, slot]),
            pltpu.make_async_copy(combs_hbm.at[hbm_rows], cbuf.at[slot, buf_rows], sem.at[GATHER_W, slot]),
        )

    def scatter_copies(t, k, slot):
        hbm_rows, buf_rows = piece_rows(t, k)
        return (
            pltpu.make_async_copy(ybuf.at[slot, buf_rows], y_hbm.at[hbm_rows], sem.at[SCATTER_Y, slot]),
        )

    def for_pieces(t, slot, copies, action):
        def body(k, c):
            for cp in copies(t, k, slot):
                action(cp)
            return c
        lax.fori_loop(0, tn_ref[t], body, 0)

    start = lambda cp: cp.start()
    wait = lambda cp: cp.wait()

    @pl.when(j == 0)
    def _():
        xbuf[...] = jnp.zeros(xbuf.shape, xbuf.dtype)
        cbuf[...] = jnp.zeros(cbuf.shape, cbuf.dtype)
        for_pieces(0, 0, gather_copies, start)

    @pl.when(j + 1 < n_used)
    def _():
        for_pieces(j + 1, 1 - slot, gather_copies, start)

    @pl.when(j < n_used)
    def _():
        for_pieces(j, slot, gather_copies, wait)
        x = xbuf[slot]
        gu = _dot(x, w1_ref[0])
        comb = cbuf[slot]
        lane = lax.broadcasted_iota(jnp.int32, (TM_E, LANES), 1)
        first = tg_ref[j] * EXPERTS_PER_GROUP
        routed = EXPERTS_PER_GROUP * EXPERT_FF
        parts = []
        for k in range(EXPERTS_PER_GROUP):
            wk = jnp.sum(jnp.where(lane == first + k, comb, 0.0), axis=-1, keepdims=True)
            gk = gu[:, k * EXPERT_FF:(k + 1) * EXPERT_FF]
            uk = gu[:, routed + k * EXPERT_FF:routed + (k + 1) * EXPERT_FF]
            parts.append((_silu(gk) * uk * wk).astype(BF16))
        gs = gu[:, 2 * routed:2 * routed + SHARED_FF]
        us = gu[:, 2 * routed + SHARED_FF:]
        parts.append((_silu(gs) * us).astype(BF16))
        y = _dot(jnp.concatenate(parts, axis=1), w2_ref[0])
        ybuf[slot] = y.astype(BF16)
        for_pieces(j, slot, scatter_copies, start)

    @pl.when((j >= 1) & (j < n_used))
    def _():
        for_pieces(j - 1, 1 - slot, scatter_copies, wait)

    @pl.when(j == n_used - 1)
    def _():
        for_pieces(j, slot, scatter_copies, wait)


def _experts(plan, h2s, combs, y0, w1, w2):
    tg, tn, src, nu = plan
    any_spec = pl.BlockSpec(memory_space=pl.ANY)
    return pl.pallas_call(
        _moe_kernel,
        out_shape=jax.ShapeDtypeStruct((N_LS, D_MODEL), BF16),
        grid_spec=pltpu.PrefetchScalarGridSpec(
            num_scalar_prefetch=4,
            grid=(N_ETILES,),
            in_specs=[
                any_spec, any_spec, any_spec,
                pl.BlockSpec((1, D_MODEL, W1_COLS), lambda j, tg, tn, src, nu: (tg[j], 0, 0)),
                pl.BlockSpec((1, W2_ROWS, D_MODEL), lambda j, tg, tn, src, nu: (tg[j], 0, 0)),
            ],
            out_specs=any_spec,
            scratch_shapes=[
                pltpu.VMEM((2, TM_E, D_MODEL), BF16),
                pltpu.VMEM((2, TM_E, LANES), F32),
                pltpu.VMEM((2, TM_E, D_MODEL), BF16),
                pltpu.SemaphoreType.DMA((3, 2)),
            ],
        ),
        input_output_aliases={6: 0},
        compiler_params=pltpu.CompilerParams(
            dimension_semantics=("arbitrary",), vmem_limit_bytes=VMEM_LIMIT),
        name="experts",
    )(tg, tn, src, nu, h2s, combs, y0, w1, w2)


def _combine_kernel(y_ref, pos_ref, x1_ref, mod_ref, fg_ref, o_ref, *, final):
    tm = x1_ref.shape[0]
    pick = lax.broadcasted_iota(jnp.int32, (tm, LS_ROWS), 1) == pos_ref[...].astype(jnp.int32)
    moe = _dot(jnp.where(pick, 1.0, 0.0).astype(BF16), y_ref[...])
    gate2 = mod_ref[0][:, 5 * D_MODEL:6 * D_MODEL]
    x2 = x1_ref[...] + gate2 * moe
    if final:
        x2 = _rms(x2) * fg_ref[...]
    o_ref[...] = x2


def _combine(y, pos, x1, mod, fg, final):
    tm = TM_POST
    row = lambda i: (i, 0)
    return pl.pallas_call(
        functools.partial(_combine_kernel, final=final),
        out_shape=jax.ShapeDtypeStruct((N_ROWS, D_MODEL), F32),
        grid=(N_ROWS // tm,),
        in_specs=[
            pl.BlockSpec((LS_ROWS, D_MODEL), row),
            pl.BlockSpec((tm, 1), row),
            pl.BlockSpec((tm, D_MODEL), row),
            pl.BlockSpec((1, 1, 6 * D_MODEL), lambda i: (_cond_row(i, tm), 0, 0)),
            _full((1, D_MODEL)),
        ],
        out_specs=pl.BlockSpec((tm, D_MODEL), row),
        compiler_params=pltpu.CompilerParams(
            dimension_semantics=("arbitrary",), vmem_limit_bytes=VMEM_LIMIT),
        name="combine",
    )(y, pos, x1, mod, fg)


def _rope_table():
    rows = DEC_SEQ // GRID_W
    row = jnp.repeat(jnp.arange(rows, dtype=F32), GRID_W)
    col = jnp.tile(jnp.arange(GRID_W, dtype=F32), rows)
    half = QK_ROPE // 2
    freqs = 1.0 / (ROPE_BASE ** (jnp.arange(0, half, 2, dtype=F32) / half))
    ang = jnp.concatenate([row[:, None] * freqs, col[:, None] * freqs], axis=-1)
    cos, sin = jnp.cos(ang), jnp.sin(ang)
    lat = jnp.concatenate([cos, cos, -sin, sin], axis=-1)
    ident = jnp.concatenate([jnp.ones((TM_PRE, QK_ROPE), F32), jnp.zeros((TM_PRE, QK_ROPE), F32)], axis=-1)
    return jnp.concatenate([ident, lat], axis=0)


_DEINT = np.concatenate([np.arange(0, QK_ROPE, 2), np.arange(1, QK_ROPE, 2)])
_SWAP = np.concatenate([np.arange(1, QK_ROPE, 2), np.arange(0, QK_ROPE, 2)])
_INTERLEAVE = np.argsort(_DEINT)


def _layout_w_in(w_in):
    cq_ckv = w_in[:, :Q_LORA + KV_LORA]
    kr = w_in[:, Q_LORA + KV_LORA:Q_LORA + KV_LORA + QK_ROPE]
    uv = w_in[:, Q_LORA + KV_LORA + QK_ROPE:]
    return jnp.concatenate([cq_ckv, kr[:, _DEINT], kr[:, _SWAP], uv], axis=1).astype(BF16)


def _layout_w_qb(w_qb):
    w = (w_qb * (ATTN_SCALE * LOG2E)).reshape(Q_LORA, ATTN_HEADS, QK_NOPE + QK_ROPE)
    nope, rope = w[..., :QK_NOPE], w[..., QK_NOPE:]
    w = jnp.concatenate([nope, rope[..., _DEINT], rope[..., _SWAP]], axis=-1)
    return w.reshape(Q_LORA, ATTN_HEADS * HEAD_PAD).astype(BF16)


def _layout_w_kvb(w_kvb):
    w = w_kvb.reshape(KV_LORA, ATTN_HEADS, QK_NOPE + V_DIM)
    k = w[..., :QK_NOPE].reshape(KV_LORA, ATTN_HEADS * QK_NOPE)
    v = w[..., QK_NOPE:].reshape(KV_LORA, ATTN_HEADS * V_DIM)
    return jnp.concatenate([k, v], axis=1).astype(BF16)


def _layout_experts(w_gate, w_up, w_down, ws_gate, ws_up, ws_down):
    def by_group(w):
        w = w.reshape(N_GROUPS, EXPERTS_PER_GROUP, D_MODEL, EXPERT_FF)
        return w.transpose(0, 2, 1, 3).reshape(N_GROUPS, D_MODEL, EXPERTS_PER_GROUP * EXPERT_FF)

    shared = jnp.broadcast_to(jnp.concatenate([ws_gate, ws_up], axis=1)[None],
                              (N_GROUPS, D_MODEL, 2 * SHARED_FF))
    w1 = jnp.concatenate([by_group(w_gate), by_group(w_up), shared], axis=2).astype(BF16)
    w2 = jnp.concatenate([w_down.reshape(N_GROUPS, EXPERTS_PER_GROUP * EXPERT_FF, D_MODEL),
                          jnp.broadcast_to(ws_down[None], (N_GROUPS, SHARED_FF, D_MODEL))],
                         axis=1).astype(BF16)
    return w1, w2


def kernel(x_prompt, x_sample, cache_ckv, cache_krope, c, c_ctx, norm1_g, w_ada, b_ada, w_in,
           q_norm_g, w_qb, kv_norm_g, w_kvb, gm_norm_g, w_spatial, b_spatial, onorm_attn_g,
           onorm_gm_g, w_out, norm2_g, w_router, router_bias, w_gate, w_up, w_down, ws_gate,
           ws_up, ws_down, final_norm_g):
    x = jnp.concatenate([x_prompt.reshape(N_CTX, D_MODEL), x_sample.reshape(N_LAT, D_MODEL)], axis=0)
    cond = jnp.concatenate([c_ctx[None, :], c, jnp.zeros((COND_PAD - N_COND, D_MODEL), F32)], axis=0)
    mod = _modulation(cond, w_ada, b_ada)
    rope_tab = _rope_table()

    wkvb = jnp.stack([_layout_w_kvb(w_kvb[l]) for l in range(DEPTH)])
    cache_kr = jnp.pad(cache_krope[..., _DEINT], ((0, 0), (0, 0), (0, 0), (0, LANES - QK_ROPE)))
    kc, vc = _cache_kv(cache_ckv, cache_kr, wkvb)

    wr = jnp.pad(w_router, ((0, 0), (0, LANES - N_EXPERTS)))
    rb = jnp.pad(router_bias, (0, LANES - N_EXPERTS)).reshape(1, LANES)
    fg = final_norm_g.reshape(1, D_MODEL)

    ckv_out, kr_out = [], []
    for l in range(DEPTH):
        mod_l = mod[l].reshape(COND_PAD, 1, 6 * D_MODEL)
        bs = jnp.broadcast_to(b_spatial[l].T[:, :, None], (CHUNK, GM_GROUPS, GM_GROUP_DIM))
        q, k, v, gm, ckv_n, kr = _pre_mixer(
            x, mod_l, rope_tab, norm1_g[l].reshape(1, -1), _layout_w_in(w_in[l]),
            q_norm_g[l].reshape(1, -1), _layout_w_qb(w_qb[l]), kv_norm_g[l].reshape(1, -1),
            wkvb[l], gm_norm_g[l].reshape(1, -1), w_spatial[l].astype(BF16),
            bs.reshape(CHUNK, GM_WIDTH), onorm_gm_g[l].reshape(1, -1))
        ckv_out.append(ckv_n[:N_CTX].reshape(BATCH, SEQ, KV_LORA))
        kr_out.append(kr[:N_CTX][:, _INTERLEAVE].reshape(BATCH, SEQ, QK_ROPE))
        attn = jnp.concatenate([_ctx_attention(q, k, v), _lat_attention(q, k, v, kc, vc, l)], axis=0)
        x1, h2s, combs, y0, pos, counts = _post_mixer(
            attn, gm, x, mod_l, onorm_attn_g[l].reshape(1, -1), w_out[l].astype(BF16),
            norm2_g[l].reshape(1, -1), wr, rb)
        plan = _plan(counts.reshape(N_POST_TILES, LANES))
        w1, w2 = _layout_experts(w_gate[l], w_up[l], w_down[l], ws_gate[l], ws_up[l], ws_down[l])
        y = _experts(plan, h2s, combs, y0, w1, w2)
        x = _combine(y, pos, x1, mod_l, fg, final=(l == DEPTH - 1))

    y_prompt = x[:N_CTX].reshape(BATCH, SEQ, D_MODEL)
    y_sample = x[N_CTX:].reshape(DEC_BATCH, DEC_SEQ, D_MODEL)
    return y_prompt, y_sample, jnp.stack(ckv_out, axis=1), jnp.stack(kr_out, axis=1)
```

```python
import functools
import math

import jax
import jax.numpy as jnp
import numpy as np
from jax import lax
from jax.experimental import pallas as pl
from jax.experimental.pallas import tpu as pltpu

D_MODEL = 1024
BATCH = 16
SEQ = 256
DEPTH = 2
DEC_BATCH = 4
DEC_SEQ = 4096
PAST_LEN = 256
GRID_W = 64
ATTN_HEADS = 4
QK_NOPE = 128
QK_ROPE = 64
V_DIM = 128
Q_LORA = 384
KV_LORA = 256
ATTN_SCALE = (QK_NOPE + QK_ROPE) ** -0.5
ROPE_BASE = 10000.0
CHUNK = 128
GM_WIDTH = 512
GM_GROUPS = 4
GM_GROUP_DIM = 128
N_EXPERTS = 16
N_GROUPS = 4
EXPERTS_PER_GROUP = 4
EXPERT_FF = 256
SHARED_FF = 256
EPS = 1e-6

N_CTX = BATCH * SEQ
N_LAT = DEC_BATCH * DEC_SEQ
N_ROWS = N_CTX + N_LAT
N_COND = 1 + DEC_BATCH
COND_PAD = 8
HEAD_PAD = 256
IN_COLS = Q_LORA + KV_LORA + 2 * QK_ROPE + 2 * GM_WIDTH
LANES = 128
LOG2E = 1.4426950408889634

TM_PRE = 256
TM_POST = 256
PIECE = 16
LS_ROWS = TM_POST + N_GROUPS * PIECE
LS_PIECES = LS_ROWS // PIECE
N_POST_TILES = N_ROWS // TM_POST
N_LS = N_POST_TILES * LS_ROWS
TM_E = 512
E_PIECES = TM_E // PIECE
N_ETILES = -(-(N_POST_TILES * (TM_POST // PIECE + N_GROUPS - 1)) // E_PIECES) + N_GROUPS
TQ = 512
KV_CHUNK = 1024
VMEM_LIMIT = 56 * 1024 * 1024

F32 = jnp.float32
BF16 = jnp.bfloat16


def _rms(x):
    return x * lax.rsqrt(jnp.mean(x * x, axis=-1, keepdims=True) + EPS)


def _gelu(x):
    return 0.5 * x * (1.0 + jnp.tanh(math.sqrt(2.0 / math.pi) * (x + 0.044715 * (x * x * x))))


def _silu(x):
    return x * (1.0 / (1.0 + jnp.exp(-x)))


def _dot(a, b):
    return jnp.dot(a, b, preferred_element_type=F32)


def _dot_nt(a, b):
    return lax.dot_general(a, b, (((1,), (1,)), ((), ())), preferred_element_type=F32)


def _cond_row(i, tm):
    n_ctx_tiles = N_CTX // tm
    per_batch = DEC_SEQ // tm
    return jnp.where(i < n_ctx_tiles, 0, 1 + (i - n_ctx_tiles) // per_batch)


def _rope_block(i, tm):
    n_ctx_tiles = N_CTX // tm
    per_batch = DEC_SEQ // tm
    return jnp.where(i < n_ctx_tiles, 0, 1 + (i - n_ctx_tiles) % per_batch)


def _ctx_lat_specs(tm, width):
    n_ctx_tiles = N_CTX // tm
    return [pl.BlockSpec((tm, width), lambda i: (jnp.minimum(i, n_ctx_tiles - 1), 0)),
            pl.BlockSpec((tm, width), lambda i: (jnp.maximum(i - n_ctx_tiles, 0), 0))]


def _ctx_or_lat(ctx_ref, lat_ref):
    n_ctx_tiles = N_CTX // ctx_ref.shape[0]
    return jnp.where(pl.program_id(0) < n_ctx_tiles, ctx_ref[...], lat_ref[...])


def _full(shape):
    n = len(shape)
    return pl.BlockSpec(shape, lambda *_: (0,) * n)


def _mod_kernel(cond_ref, w_ref, b_ref, o_ref):
    s = _silu(cond_ref[...])
    o_ref[0] = jnp.dot(s, w_ref[0], preferred_element_type=F32,
                       precision=lax.Precision.HIGHEST) + b_ref[0]


def _modulation(cond, w_ada, b_ada):
    tn = 1536
    return pl.pallas_call(
        _mod_kernel,
        out_shape=jax.ShapeDtypeStruct((DEPTH, COND_PAD, 6 * D_MODEL), F32),
        grid=(DEPTH, 6 * D_MODEL // tn),
        in_specs=[
            pl.BlockSpec((COND_PAD, D_MODEL), lambda l, j: (0, 0)),
            pl.BlockSpec((1, D_MODEL, tn), lambda l, j: (l, 0, j)),
            pl.BlockSpec((1, 1, tn), lambda l, j: (l, 0, j)),
        ],
        out_specs=pl.BlockSpec((1, COND_PAD, tn), lambda l, j: (l, 0, j)),
        compiler_params=pltpu.CompilerParams(
            dimension_semantics=("arbitrary", "arbitrary"), vmem_limit_bytes=VMEM_LIMIT),
        name="modulation",
    )(cond, w_ada, b_ada.reshape(DEPTH, 1, 6 * D_MODEL))


def _pre_kernel(xc_ref, xl_ref, mod_ref, rope_ref, g1_ref, win_ref, qg_ref, wqb_ref, kvg_ref,
                wkvb_ref, gmg_ref, ws_ref, bs_ref, ogm_ref,
                q_ref, k_ref, v_ref, gm_ref, ckv_ref, kr_ref):
    tm = xc_ref.shape[0]
    mod = mod_ref[0]
    shift1 = mod[:, 0:D_MODEL]
    scale1 = mod[:, D_MODEL:2 * D_MODEL]
    h = _rms(_ctx_or_lat(xc_ref, xl_ref)) * g1_ref[...] * (1.0 + scale1) + shift1
    y = _dot(h.astype(BF16), win_ref[...])
    cq = y[:, 0:Q_LORA]
    ckv = y[:, Q_LORA:Q_LORA + KV_LORA]
    kr2 = y[:, 640:768]
    u = y[:, 768:768 + GM_WIDTH]
    vv = y[:, 768 + GM_WIDTH:768 + 2 * GM_WIDTH]

    rope = rope_ref[...]
    lane = lax.broadcasted_iota(jnp.int32, (tm, LANES), 1)

    kr_ref[...] = kr2[:, 0:QK_ROPE]
    t = kr2 * rope
    k_rot = jnp.where(lane < QK_ROPE, t + pltpu.roll(t, QK_ROPE, 1), 0.0).astype(BF16)

    ckv_n = _rms(ckv) * kvg_ref[...]
    ckv_ref[...] = ckv_n
    kv = _dot(ckv_n.astype(BF16), wkvb_ref[...])
    for hd in range(ATTN_HEADS):
        k_ref[:, hd * HEAD_PAD:hd * HEAD_PAD + QK_NOPE] = (
            kv[:, hd * QK_NOPE:(hd + 1) * QK_NOPE].astype(BF16))
        k_ref[:, hd * HEAD_PAD + QK_NOPE:(hd + 1) * HEAD_PAD] = k_rot
    v_ref[...] = kv[:, ATTN_HEADS * QK_NOPE:].astype(BF16)

    q = _dot((_rms(cq) * qg_ref[...]).astype(BF16), wqb_ref[...])
    for hd in range(ATTN_HEADS):
        q_ref[:, hd * HEAD_PAD:hd * HEAD_PAD + QK_NOPE] = (
            q[:, hd * HEAD_PAD:hd * HEAD_PAD + QK_NOPE].astype(BF16))
        t = q[:, hd * HEAD_PAD + QK_NOPE:(hd + 1) * HEAD_PAD] * rope
        q_ref[:, hd * HEAD_PAD + QK_NOPE:(hd + 1) * HEAD_PAD] = (
            t + pltpu.roll(t, QK_ROPE, 1)).astype(BF16)

    ug = _gelu(u)
    vg = _gelu(vv)
    cols = []
    for g in range(GM_GROUPS):
        sl = slice(g * GM_GROUP_DIM, (g + 1) * GM_GROUP_DIM)
        vn = (_rms(vg[:, sl]) * gmg_ref[:, sl]).astype(BF16)
        rows = []
        for c in range(tm // CHUNK):
            sv = _dot(ws_ref[g], vn[c * CHUNK:(c + 1) * CHUNK]) + bs_ref[:, sl]
            rows.append(ug[c * CHUNK:(c + 1) * CHUNK, sl] * sv)
        cols.append(jnp.concatenate(rows, axis=0))
    gm = jnp.concatenate(cols, axis=1)
    gm_ref[...] = (_rms(gm) * ogm_ref[...]).astype(BF16)


def _pre_mixer(x_ctx, x_lat, mod, rope_tab, g1, win, qg, wqb, kvg, wkvb, gmg, ws, bs, ogm):
    tm = TM_PRE
    row = lambda i: (i, 0)
    return pl.pallas_call(
        _pre_kernel,
        out_shape=(
            jax.ShapeDtypeStruct((N_ROWS, ATTN_HEADS * HEAD_PAD), BF16),
            jax.ShapeDtypeStruct((N_ROWS, ATTN_HEADS * HEAD_PAD), BF16),
            jax.ShapeDtypeStruct((N_ROWS, ATTN_HEADS * V_DIM), BF16),
            jax.ShapeDtypeStruct((N_ROWS, GM_WIDTH), BF16),
            jax.ShapeDtypeStruct((N_ROWS, KV_LORA), F32),
            jax.ShapeDtypeStruct((N_ROWS, QK_ROPE), F32),
        ),
        grid=(N_ROWS // tm,),
        in_specs=_ctx_lat_specs(tm, D_MODEL) + [
            pl.BlockSpec((1, 1, 6 * D_MODEL), lambda i: (_cond_row(i, tm), 0, 0)),
            pl.BlockSpec((tm, LANES), lambda i: (_rope_block(i, tm), 0)),
            _full((1, D_MODEL)),
            _full((D_MODEL, IN_COLS)),
            _full((1, Q_LORA)),
            _full((Q_LORA, ATTN_HEADS * HEAD_PAD)),
            _full((1, KV_LORA)),
            _full((KV_LORA, ATTN_HEADS * (QK_NOPE + V_DIM))),
            _full((1, GM_WIDTH)),
            _full((GM_GROUPS, CHUNK, CHUNK)),
            _full((CHUNK, GM_WIDTH)),
            _full((1, GM_WIDTH)),
        ],
        out_specs=(
            pl.BlockSpec((tm, ATTN_HEADS * HEAD_PAD), row),
            pl.BlockSpec((tm, ATTN_HEADS * HEAD_PAD), row),
            pl.BlockSpec((tm, ATTN_HEADS * V_DIM), row),
            pl.BlockSpec((tm, GM_WIDTH), row),
            pl.BlockSpec((tm, KV_LORA), row),
            pl.BlockSpec((tm, QK_ROPE), row),
        ),
        compiler_params=pltpu.CompilerParams(
            dimension_semantics=("arbitrary",), vmem_limit_bytes=VMEM_LIMIT),
        name="pre_mixer",
    )(x_ctx, x_lat, mod, rope_tab, g1, win, qg, wqb, kvg, wkvb, gmg, ws, bs, ogm)


def _cache_kv_kernel(ckv_ref, kr_ref, wkvb_ref, k_ref, v_ref):
    kv = _dot(ckv_ref[0, 0].astype(BF16), wkvb_ref[0])
    kr = kr_ref[0, 0].astype(BF16)
    for hd in range(ATTN_HEADS):
        k_ref[0, 0, :, hd * HEAD_PAD:hd * HEAD_PAD + QK_NOPE] = (
            kv[:, hd * QK_NOPE:(hd + 1) * QK_NOPE].astype(BF16))
        k_ref[0, 0, :, hd * HEAD_PAD + QK_NOPE:(hd + 1) * HEAD_PAD] = kr
    v_ref[0, 0] = kv[:, ATTN_HEADS * QK_NOPE:].astype(BF16)


def _cache_kv(cache_ckv, cache_kr_pad, wkvb):
    blk = lambda w: pl.BlockSpec((1, 1, PAST_LEN, w), lambda l, b: (b, l, 0, 0))
    return pl.pallas_call(
        _cache_kv_kernel,
        out_shape=(
            jax.ShapeDtypeStruct((DEC_BATCH, DEPTH, PAST_LEN, ATTN_HEADS * HEAD_PAD), BF16),
            jax.ShapeDtypeStruct((DEC_BATCH, DEPTH, PAST_LEN, ATTN_HEADS * V_DIM), BF16),
        ),
        grid=(DEPTH, DEC_BATCH),
        in_specs=[
            blk(KV_LORA),
            blk(LANES),
            pl.BlockSpec((1, KV_LORA, ATTN_HEADS * (QK_NOPE + V_DIM)), lambda l, b: (l, 0, 0)),
        ],
        out_specs=(blk(ATTN_HEADS * HEAD_PAD), blk(ATTN_HEADS * V_DIM)),
        compiler_params=pltpu.CompilerParams(
            dimension_semantics=("arbitrary", "arbitrary"), vmem_limit_bytes=VMEM_LIMIT),
        name="cache_kv",
    )(cache_ckv, cache_kr_pad, wkvb)


def _softmax_chunk(qh, kc, vc, m, l, acc):
    s = _dot_nt(qh, kc)
    m_new = jnp.maximum(m, jnp.max(s, axis=-1, keepdims=True))
    alpha = jnp.exp2(m - m_new)
    p = jnp.exp2(s - m_new)
    l = alpha * l + jnp.sum(p, axis=-1, keepdims=True)
    acc = alpha * acc + _dot(p.astype(BF16), vc)
    return m_new, l, acc


def _ctx_attn_kernel(q_ref, k_ref, v_ref, o_ref):
    for hd in range(ATTN_HEADS):
        hs = slice(hd * HEAD_PAD, (hd + 1) * HEAD_PAD)
        vs = slice(hd * V_DIM, (hd + 1) * V_DIM)
        s = _dot_nt(q_ref[:, hs], k_ref[:, hs])
        p = jnp.exp2(s - jnp.max(s, axis=-1, keepdims=True))
        l = jnp.sum(p, axis=-1, keepdims=True)
        o = _dot(p.astype(BF16), v_ref[:, vs])
        o_ref[:, vs] = (o / l).astype(o_ref.dtype)


def _ctx_attention(q, k, v):
    blk = lambda w: pl.BlockSpec((SEQ, w), lambda b: (b, 0))
    return pl.pallas_call(
        _ctx_attn_kernel,
        out_shape=jax.ShapeDtypeStruct((N_CTX, ATTN_HEADS * V_DIM), BF16),
        grid=(BATCH,),
        in_specs=[blk(ATTN_HEADS * HEAD_PAD), blk(ATTN_HEADS * HEAD_PAD), blk(ATTN_HEADS * V_DIM)],
        out_specs=blk(ATTN_HEADS * V_DIM),
        compiler_params=pltpu.CompilerParams(
            dimension_semantics=("arbitrary",), vmem_limit_bytes=VMEM_LIMIT),
        name="ctx_attention",
    )(q, k, v)


def _lat_attn_kernel(q_ref, k_ref, v_ref, kc_ref, vc_ref, o_ref):
    tq = q_ref.shape[0]
    hs = [slice(hd * HEAD_PAD, (hd + 1) * HEAD_PAD) for hd in range(ATTN_HEADS)]
    vs = [slice(hd * V_DIM, (hd + 1) * V_DIM) for hd in range(ATTN_HEADS)]
    m0 = jnp.full((tq, 1), -jnp.inf, F32)
    l0 = jnp.zeros((tq, 1), F32)
    a0 = jnp.zeros((tq, V_DIM), F32)
    carry = tuple(
        _softmax_chunk(q_ref[:, hs[hd]], kc_ref[0, 0, :, hs[hd]], vc_ref[0, 0, :, vs[hd]], m0, l0, a0)
        for hd in range(ATTN_HEADS))

    def body(c, carry):
        rows = pl.ds(pl.multiple_of(c * KV_CHUNK, KV_CHUNK), KV_CHUNK)
        return tuple(
            _softmax_chunk(q_ref[:, hs[hd]], k_ref[rows, hs[hd]], v_ref[rows, vs[hd]], *carry[hd])
            for hd in range(ATTN_HEADS))

    carry = lax.fori_loop(0, DEC_SEQ // KV_CHUNK, body, carry)
    for hd in range(ATTN_HEADS):
        m, l, acc = carry[hd]
        o_ref[:, vs[hd]] = (acc / l).astype(o_ref.dtype)


def _lat_attention(q, k, v, kc, vc, layer):
    nq = DEC_SEQ // TQ
    ctx_q_tiles = N_CTX // TQ
    ctx_kv_blocks = N_CTX // DEC_SEQ
    return pl.pallas_call(
        _lat_attn_kernel,
        out_shape=jax.ShapeDtypeStruct((N_LAT, ATTN_HEADS * V_DIM), BF16),
        grid=(DEC_BATCH, nq),
        in_specs=[
            pl.BlockSpec((TQ, ATTN_HEADS * HEAD_PAD), lambda b, i: (ctx_q_tiles + b * nq + i, 0)),
            pl.BlockSpec((DEC_SEQ, ATTN_HEADS * HEAD_PAD), lambda b, i: (ctx_kv_blocks + b, 0)),
            pl.BlockSpec((DEC_SEQ, ATTN_HEADS * V_DIM), lambda b, i: (ctx_kv_blocks + b, 0)),
            pl.BlockSpec((1, 1, PAST_LEN, ATTN_HEADS * HEAD_PAD), lambda b, i: (b, layer, 0, 0)),
            pl.BlockSpec((1, 1, PAST_LEN, ATTN_HEADS * V_DIM), lambda b, i: (b, layer, 0, 0)),
        ],
        out_specs=pl.BlockSpec((TQ, ATTN_HEADS * V_DIM), lambda b, i: (b * nq + i, 0)),
        compiler_params=pltpu.CompilerParams(
            dimension_semantics=("arbitrary", "arbitrary"), vmem_limit_bytes=VMEM_LIMIT),
        name="lat_attention",
    )(q, k, v, kc, vc)


def _group_peer(x, lane, d, width, period):
    step = d * width
    ahead = pltpu.roll(x, LANES - step, 1)
    behind = pltpu.roll(x, period - step, 1)
    wraps = (lane & (period - 1)) + step >= period
    return jnp.where(wraps, behind, ahead), wraps


def _route(logits, bias, lane):
    valid = lane < N_EXPERTS
    s = 1.0 / (1.0 + jnp.exp(-logits))
    sb = s + bias
    rank = jnp.zeros(sb.shape, jnp.int32)
    for d in range(1, EXPERTS_PER_GROUP):
        o, wraps = _group_peer(sb, lane, d, 1, EXPERTS_PER_GROUP)
        beats = (o > sb) | ((o == sb) & wraps)
        rank = rank + beats.astype(jnp.int32)
    top2 = rank < 2
    t = jnp.where(top2, sb, 0.0)
    gscore = t
    for d in range(1, EXPERTS_PER_GROUP):
        gscore = gscore + _group_peer(t, lane, d, 1, EXPERTS_PER_GROUP)[0]
    grank = jnp.zeros(sb.shape, jnp.int32)
    for d in range(1, N_GROUPS):
        o, wraps = _group_peer(gscore, lane, d, EXPERTS_PER_GROUP, N_EXPERTS)
        beats = (o > gscore) | ((o == gscore) & wraps)
        grank = grank + beats.astype(jnp.int32)
    in_group = (grank == 0) & valid
    w = jnp.where(top2 & in_group, s, 0.0)
    denom = jnp.sum(w, axis=-1, keepdims=True)
    group_flag = jnp.where(in_group & ((lane & (EXPERTS_PER_GROUP - 1)) == 0), 1.0, 0.0)
    return w / denom, group_flag


def _split_bf16(x, terms):
    out = []
    for _ in range(terms - 1):
        t = x.astype(BF16)
        out.append(t)
        x = x - t.astype(F32)
    out.append(x.astype(BF16))
    return out


def _post_kernel(ac_ref, al_ref, gm_ref, xc_ref, xl_ref, mod_ref, oag_ref, wout_ref, g2_ref,
                 wr_ref, rb_ref, x1_ref, h2s_ref, combs_ref, y0_ref, pos_ref, cnt_ref):
    tm = xc_ref.shape[0]
    mod = mod_ref[0]
    gate1 = mod[:, 2 * D_MODEL:3 * D_MODEL]
    shift2 = mod[:, 3 * D_MODEL:4 * D_MODEL]
    scale2 = mod[:, 4 * D_MODEL:5 * D_MODEL]
    an = (_rms(_ctx_or_lat(ac_ref, al_ref).astype(F32)) * oag_ref[...]).astype(BF16)
    mixed = _dot(jnp.concatenate([an, gm_ref[...]], axis=1), wout_ref[...])
    x1 = _ctx_or_lat(xc_ref, xl_ref) + gate1 * mixed
    x1_ref[...] = x1
    h2 = _rms(x1) * g2_ref[...] * (1.0 + scale2) + shift2
    h2_hi, h2_lo = _split_bf16(h2, 2)
    t = _dot(jnp.concatenate([h2_hi, h2_lo], axis=0), wr_ref[...])
    logits = (t[:tm, :LANES] + t[tm:, :LANES]) + (t[:tm, LANES:] + t[tm:, LANES:])
    lane = lax.broadcasted_iota(jnp.int32, (tm, LANES), 1)
    comb, flag = _route(logits, rb_ref[...], lane)

    r_i = lax.broadcasted_iota(jnp.int32, (tm, tm), 0)
    c_i = lax.broadcasted_iota(jnp.int32, (tm, tm), 1)
    before = jnp.where(c_i < r_i, 1.0, 0.0).astype(BF16)
    rank = _dot(before, flag.astype(BF16))
    count = jnp.sum(flag, axis=0, keepdims=True)
    pieces = jnp.floor((count + (PIECE - 1)) * (1.0 / PIECE))
    start = (pltpu.roll(pieces, EXPERTS_PER_GROUP, 1) + pltpu.roll(pieces, 2 * EXPERTS_PER_GROUP, 1)
             + pltpu.roll(pieces, 3 * EXPERTS_PER_GROUP, 1)) * PIECE
    pos = jnp.sum(flag * (start + rank), axis=-1, keepdims=True)
    pos_ref[...] = pos
    cnt_ref[0] = count.astype(jnp.int32)
    pos_row = jnp.transpose(jnp.broadcast_to(pos, (tm, LANES)))[0:1, :].astype(jnp.int32)
    place = lax.broadcasted_iota(jnp.int32, (LS_ROWS, tm), 0) == pos_row
    place = jnp.where(place, 1.0, 0.0).astype(BF16)
    h2s_ref[...] = _dot(place, h2_hi).astype(BF16)
    t = _dot(place, jnp.concatenate(_split_bf16(comb, 3), axis=1))
    combs_ref[...] = t[:, :LANES] + t[:, LANES:2 * LANES] + t[:, 2 * LANES:]
    y0_ref[...] = jnp.zeros(y0_ref.shape, y0_ref.dtype)


def _post_mixer(attn_ctx, attn_lat, gm, x_ctx, x_lat, mod, oag, wout, g2, wr, rb):
    tm = TM_POST
    row = lambda i: (i, 0)
    return pl.pallas_call(
        _post_kernel,
        out_shape=(
            jax.ShapeDtypeStruct((N_ROWS, D_MODEL), F32),
            jax.ShapeDtypeStruct((N_LS, D_MODEL), BF16),
            jax.ShapeDtypeStruct((N_LS, LANES), F32),
            jax.ShapeDtypeStruct((N_LS, D_MODEL), BF16),
            jax.ShapeDtypeStruct((N_ROWS, 1), F32),
            jax.ShapeDtypeStruct((N_POST_TILES, 1, LANES), jnp.int32),
        ),
        grid=(N_ROWS // tm,),
        in_specs=_ctx_lat_specs(tm, ATTN_HEADS * V_DIM) + [
            pl.BlockSpec((tm, GM_WIDTH), row),
        ] + _ctx_lat_specs(tm, D_MODEL) + [
            pl.BlockSpec((1, 1, 6 * D_MODEL), lambda i: (_cond_row(i, tm), 0, 0)),
            _full((1, ATTN_HEADS * V_DIM)),
            _full((D_MODEL, D_MODEL)),
            _full((1, D_MODEL)),
            _full((D_MODEL, 2 * LANES)),
            _full((1, LANES)),
        ],
        out_specs=(
            pl.BlockSpec((tm, D_MODEL), row),
            pl.BlockSpec((LS_ROWS, D_MODEL), row),
            pl.BlockSpec((LS_ROWS, LANES), row),
            pl.BlockSpec((LS_ROWS, D_MODEL), row),
            pl.BlockSpec((tm, 1), row),
            pl.BlockSpec((1, 1, LANES), lambda i: (i, 0, 0)),
        ),
        compiler_params=pltpu.CompilerParams(
            dimension_semantics=("arbitrary",), vmem_limit_bytes=VMEM_LIMIT),
        name="post_mixer",
    )(attn_ctx, attn_lat, gm, x_ctx, x_lat, mod, oag, wout, g2, wr, rb)


def _plan_kernel(cnt_ref, tg_ref, tn_ref, src_ref, nu_ref):
    def clear_src(j, c):
        src_ref[j] = 0
        return c

    lax.fori_loop(0, N_ETILES * E_PIECES, clear_src, 0)

    def clear_tile(j, c):
        tg_ref[j] = N_GROUPS - 1
        tn_ref[j] = 0
        return c

    lax.fori_loop(0, N_ETILES, clear_tile, 0)

    def n_pieces(i, g):
        return lax.shift_right_logical(cnt_ref[i, g * EXPERTS_PER_GROUP] + (PIECE - 1), 4)

    t = jnp.int32(0)
    for g in range(N_GROUPS):
        def tile_body(i, carry, g=g):
            first = i * LS_PIECES
            for gp in range(g):
                first = first + n_pieces(i, gp)

            def piece_body(p, carry):
                t, k = carry
                src_ref[t * E_PIECES + k] = first + p
                k = k + 1
                full = k == E_PIECES

                @pl.when(full)
                def _():
                    tn_ref[t] = E_PIECES
                    tg_ref[t] = g

                return jnp.where(full, t + 1, t), jnp.where(full, 0, k)

            return lax.fori_loop(0, n_pieces(i, g), piece_body, carry)

        t, k = lax.fori_loop(0, N_POST_TILES, tile_body, (t, jnp.int32(0)))

        @pl.when(k > 0)
        def _(t=t, k=k, g=g):
            tn_ref[t] = k
            tg_ref[t] = g

        t = jnp.where(k > 0, t + 1, t)
    nu_ref[0] = t


def _plan(counts):
    smem = pl.BlockSpec(memory_space=pltpu.SMEM)
    return pl.pallas_call(
        _plan_kernel,
        out_shape=(
            jax.ShapeDtypeStruct((N_ETILES,), jnp.int32),
            jax.ShapeDtypeStruct((N_ETILES,), jnp.int32),
            jax.ShapeDtypeStruct((N_ETILES * E_PIECES,), jnp.int32),
            jax.ShapeDtypeStruct((1,), jnp.int32),
        ),
        in_specs=[smem],
        out_specs=(smem, smem, smem, smem),
        name="expert_plan",
    )(counts)


GATHER_X, GATHER_W, SCATTER_Y = 0, 1, 2


def _moe_kernel(tg_ref, tn_ref, src_ref, nu_ref, h2s_hbm, combs_hbm, y0_hbm, wg_ref, wu_ref, wd_ref,
                wsg_ref, wsu_ref, wsd_ref, y_hbm, xbuf, cbuf, ybuf, sem):
    del y0_hbm
    j = pl.program_id(0)
    n_used = nu_ref[0]
    slot = lax.rem(j, 2)

    def piece_rows(t, k):
        hbm_rows = pl.ds(pl.multiple_of(src_ref[t * E_PIECES + k] * PIECE, PIECE), PIECE)
        buf_rows = pl.ds(pl.multiple_of(k * PIECE, PIECE), PIECE)
        return hbm_rows, buf_rows

    def gather_copies(t, k, slot):
        hbm_rows, buf_rows = piece_rows(t, k)
        return (
            pltpu.make_async_copy(h2s_hbm.at[hbm_rows], xbuf.at[slot, buf_rows], sem.at[GATHER_X, slot]),
            pltpu.make_async_copy(combs_hbm.at[hbm_rows], cbuf.at[slot, buf_rows], sem.at[GATHER_W, slot]),
        )

    def scatter_copies(t, k, slot):
        hbm_rows, buf_rows = piece_rows(t, k)
        return (
            pltpu.make_async_copy(ybuf.at[slot, buf_rows], y_hbm.at[hbm_rows], sem.at[SCATTER_Y, slot]),
        )

    def for_pieces(t, slot, copies, action):
        def body(k, c):
            for cp in copies(t, k, slot):
                action(cp)
            return c
        lax.fori_loop(0, tn_ref[t], body, 0)

    start = lambda cp: cp.start()
    wait = lambda cp: cp.wait()

    @pl.when(j == 0)
    def _():
        xbuf[...] = jnp.zeros(xbuf.shape, xbuf.dtype)
        cbuf[...] = jnp.zeros(cbuf.shape, cbuf.dtype)
        for_pieces(0, 0, gather_copies, start)

    @pl.when(j + 1 < n_used)
    def _():
        for_pieces(j + 1, 1 - slot, gather_copies, start)

    @pl.when(j < n_used)
    def _():
        for_pieces(j, slot, gather_copies, wait)
        x = xbuf[slot]
        comb = cbuf[slot]
        lane = lax.broadcasted_iota(jnp.int32, (TM_E, LANES), 1)
        first = tg_ref[j] * EXPERTS_PER_GROUP
        y = _dot((_silu(_dot(x, wsg_ref[...])) * _dot(x, wsu_ref[...])).astype(BF16), wsd_ref[...])
        for k in range(EXPERTS_PER_GROUP):
            wk = jnp.sum(jnp.where(lane == first + k, comb, 0.0), axis=-1, keepdims=True)
            act = _silu(_dot(x, wg_ref[k])) * _dot(x, wu_ref[k]) * wk
            y = y + _dot(act.astype(BF16), wd_ref[k])
        ybuf[slot] = y.astype(BF16)
        for_pieces(j, slot, scatter_copies, start)

    @pl.when((j >= 1) & (j < n_used))
    def _():
        for_pieces(j - 1, 1 - slot, scatter_copies, wait)

    @pl.when(j == n_used - 1)
    def _():
        for_pieces(j, slot, scatter_copies, wait)


def _experts(plan, h2s, combs, y0, wg, wu, wd, wsg, wsu, wsd):
    tg, tn, src, nu = plan
    any_spec = pl.BlockSpec(memory_space=pl.ANY)
    group = lambda j, tg, tn, src, nu: (tg[j], 0, 0)
    whole = lambda j, tg, tn, src, nu: (0, 0)
    return pl.pallas_call(
        _moe_kernel,
        out_shape=jax.ShapeDtypeStruct((N_LS, D_MODEL), BF16),
        grid_spec=pltpu.PrefetchScalarGridSpec(
            num_scalar_prefetch=4,
            grid=(N_ETILES,),
            in_specs=[
                any_spec, any_spec, any_spec,
                pl.BlockSpec((EXPERTS_PER_GROUP, D_MODEL, EXPERT_FF), group),
                pl.BlockSpec((EXPERTS_PER_GROUP, D_MODEL, EXPERT_FF), group),
                pl.BlockSpec((EXPERTS_PER_GROUP, EXPERT_FF, D_MODEL), group),
                pl.BlockSpec((D_MODEL, SHARED_FF), whole),
                pl.BlockSpec((D_MODEL, SHARED_FF), whole),
                pl.BlockSpec((SHARED_FF, D_MODEL), whole),
            ],
            out_specs=any_spec,
            scratch_shapes=[
                pltpu.VMEM((2, TM_E, D_MODEL), BF16),
                pltpu.VMEM((2, TM_E, LANES), F32),
                pltpu.VMEM((2, TM_E, D_MODEL), BF16),
                pltpu.SemaphoreType.DMA((3, 2)),
            ],
        ),
        input_output_aliases={6: 0},
        compiler_params=pltpu.CompilerParams(
            dimension_semantics=("arbitrary",), vmem_limit_bytes=VMEM_LIMIT),
        name="experts",
    )(tg, tn, src, nu, h2s, combs, y0, wg, wu, wd, wsg, wsu, wsd)


def _combine_kernel(y_ref, pos_ref, x1_ref, mod_ref, fg_ref, o_ref, *, final):
    tm = x1_ref.shape[0]
    pick = lax.broadcasted_iota(jnp.int32, (tm, LS_ROWS), 1) == pos_ref[...].astype(jnp.int32)
    moe = _dot(jnp.where(pick, 1.0, 0.0).astype(BF16), y_ref[...])
    gate2 = mod_ref[0][:, 5 * D_MODEL:6 * D_MODEL]
    x2 = x1_ref[...] + gate2 * moe
    if final:
        x2 = _rms(x2) * fg_ref[...]
    o_ref[...] = x2


def _combine(y, pos, x1, mod, fg, final, first_tile, n_tiles):
    tm = TM_POST
    row = lambda i: (first_tile + i, 0)
    return pl.pallas_call(
        functools.partial(_combine_kernel, final=final),
        out_shape=jax.ShapeDtypeStruct((n_tiles * tm, D_MODEL), F32),
        grid=(n_tiles,),
        in_specs=[
            pl.BlockSpec((LS_ROWS, D_MODEL), row),
            pl.BlockSpec((tm, 1), row),
            pl.BlockSpec((tm, D_MODEL), row),
            pl.BlockSpec((1, 1, 6 * D_MODEL), lambda i: (_cond_row(first_tile + i, tm), 0, 0)),
            _full((1, D_MODEL)),
        ],
        out_specs=pl.BlockSpec((tm, D_MODEL), lambda i: (i, 0)),
        compiler_params=pltpu.CompilerParams(
            dimension_semantics=("arbitrary",), vmem_limit_bytes=VMEM_LIMIT),
        name="combine",
    )(y, pos, x1, mod, fg)


def _rope_table():
    rows = DEC_SEQ // GRID_W
    row = jnp.repeat(jnp.arange(rows, dtype=F32), GRID_W)
    col = jnp.tile(jnp.arange(GRID_W, dtype=F32), rows)
    half = QK_ROPE // 2
    freqs = 1.0 / (ROPE_BASE ** (jnp.arange(0, half, 2, dtype=F32) / half))
    ang = jnp.concatenate([row[:, None] * freqs, col[:, None] * freqs], axis=-1)
    cos, sin = jnp.cos(ang), jnp.sin(ang)
    lat = jnp.concatenate([cos, cos, -sin, sin], axis=-1)
    ident = jnp.concatenate([jnp.ones((TM_PRE, QK_ROPE), F32), jnp.zeros((TM_PRE, QK_ROPE), F32)], axis=-1)
    return jnp.concatenate([ident, lat], axis=0)


_DEINT = np.concatenate([np.arange(0, QK_ROPE, 2), np.arange(1, QK_ROPE, 2)])
_SWAP = np.concatenate([np.arange(1, QK_ROPE, 2), np.arange(0, QK_ROPE, 2)])
_INTERLEAVE = np.argsort(_DEINT)


def _layout_w_in(w_in):
    cq_ckv = w_in[:, :Q_LORA + KV_LORA]
    kr = w_in[:, Q_LORA + KV_LORA:Q_LORA + KV_LORA + QK_ROPE]
    uv = w_in[:, Q_LORA + KV_LORA + QK_ROPE:]
    return jnp.concatenate([cq_ckv, kr[:, _DEINT], kr[:, _SWAP], uv], axis=1).astype(BF16)


def _layout_w_qb(w_qb):
    w = (w_qb * (ATTN_SCALE * LOG2E)).reshape(Q_LORA, ATTN_HEADS, QK_NOPE + QK_ROPE)
    nope, rope = w[..., :QK_NOPE], w[..., QK_NOPE:]
    w = jnp.concatenate([nope, rope[..., _DEINT], rope[..., _SWAP]], axis=-1)
    return w.reshape(Q_LORA, ATTN_HEADS * HEAD_PAD).astype(BF16)


def _layout_w_kvb(w_kvb):
    w = w_kvb.reshape(KV_LORA, ATTN_HEADS, QK_NOPE + V_DIM)
    k = w[..., :QK_NOPE].reshape(KV_LORA, ATTN_HEADS * QK_NOPE)
    v = w[..., QK_NOPE:].reshape(KV_LORA, ATTN_HEADS * V_DIM)
    return jnp.concatenate([k, v], axis=1).astype(BF16)


def kernel(x_prompt, x_sample, cache_ckv, cache_krope, c, c_ctx, norm1_g, w_ada, b_ada, w_in,
           q_norm_g, w_qb, kv_norm_g, w_kvb, gm_norm_g, w_spatial, b_spatial, onorm_attn_g,
           onorm_gm_g, w_out, norm2_g, w_router, router_bias, w_gate, w_up, w_down, ws_gate,
           ws_up, ws_down, final_norm_g):
    x_ctx = x_prompt.reshape(N_CTX, D_MODEL)
    x_lat = x_sample.reshape(N_LAT, D_MODEL)
    cond = jnp.concatenate([c_ctx[None, :], c, jnp.zeros((COND_PAD - N_COND, D_MODEL), F32)], axis=0)
    mod = _modulation(cond, w_ada, b_ada)
    rope_tab = _rope_table()

    wkvb = jnp.stack([_layout_w_kvb(w_kvb[l]) for l in range(DEPTH)])
    cache_kr = jnp.pad(cache_krope[..., _DEINT], ((0, 0), (0, 0), (0, 0), (0, LANES - QK_ROPE)))
    kc, vc = _cache_kv(cache_ckv, cache_kr, wkvb)

    wr = jnp.pad(w_router, ((0, 0), (0, LANES - N_EXPERTS)))
    wr_hi = wr.astype(BF16)
    wr = jnp.concatenate([wr_hi, (wr - wr_hi.astype(F32)).astype(BF16)], axis=1)
    rb = jnp.pad(router_bias, (0, LANES - N_EXPERTS)).reshape(1, LANES)
    fg = final_norm_g.reshape(1, D_MODEL)

    ckv_out, kr_out = [], []
    for l in range(DEPTH):
        mod_l = mod[l].reshape(COND_PAD, 1, 6 * D_MODEL)
        bs = jnp.broadcast_to(b_spatial[l].T[:, :, None], (CHUNK, GM_GROUPS, GM_GROUP_DIM))
        q, k, v, gm, ckv_n, kr = _pre_mixer(
            x_ctx, x_lat, mod_l, rope_tab, norm1_g[l].reshape(1, -1), _layout_w_in(w_in[l]),
            q_norm_g[l].reshape(1, -1), _layout_w_qb(w_qb[l]), kv_norm_g[l].reshape(1, -1),
            wkvb[l], gm_norm_g[l].reshape(1, -1), w_spatial[l].astype(BF16),
            bs.reshape(CHUNK, GM_WIDTH), onorm_gm_g[l].reshape(1, -1))
        ckv_out.append(ckv_n[:N_CTX].reshape(BATCH, SEQ, KV_LORA))
        kr_out.append(kr[:N_CTX][:, _INTERLEAVE].reshape(BATCH, SEQ, QK_ROPE))
        x1, h2s, combs, y0, pos, counts = _post_mixer(
            _ctx_attention(q, k, v), _lat_attention(q, k, v, kc, vc, l), gm, x_ctx, x_lat, mod_l,
            onorm_attn_g[l].reshape(1, -1), w_out[l].astype(BF16), norm2_g[l].reshape(1, -1), wr, rb)
        plan = _plan(counts.reshape(N_POST_TILES, LANES))
        y = _experts(plan, h2s, combs, y0, w_gate[l].astype(BF16), w_up[l].astype(BF16),
                     w_down[l].astype(BF16), ws_gate[l].astype(BF16), ws_up[l].astype(BF16),
                     ws_down[l].astype(BF16))
        final = l == DEPTH - 1
        n_ctx_tiles = N_CTX // TM_POST
        x_ctx = _combine(y, pos, x1, mod_l, fg, final, 0, n_ctx_tiles)
        x_lat = _combine(y, pos, x1, mod_l, fg, final, n_ctx_tiles, N_POST_TILES - n_ctx_tiles)

    y_prompt = x_ctx.reshape(BATCH, SEQ, D_MODEL)
    y_sample = x_lat.reshape(DEC_BATCH, DEC_SEQ, D_MODEL)
    return y_prompt, y_sample, jnp.stack(ckv_out, axis=1), jnp.stack(kr_out, axis=1)
```

```python
import functools
import math

import jax
import jax.numpy as jnp
import numpy as np
from jax import lax
from jax.experimental import pallas as pl
from jax.experimental.pallas import tpu as pltpu

D_MODEL = 1024
BATCH = 16
SEQ = 256
DEPTH = 2
DEC_BATCH = 4
DEC_SEQ = 4096
PAST_LEN = 256
GRID_W = 64
ATTN_HEADS = 4
QK_NOPE = 128
QK_ROPE = 64
V_DIM = 128
Q_LORA = 384
KV_LORA = 256
ATTN_SCALE = (QK_NOPE + QK_ROPE) ** -0.5
ROPE_BASE = 10000.0
CHUNK = 128
GM_WIDTH = 512
GM_GROUPS = 4
GM_GROUP_DIM = 128
N_EXPERTS = 16
N_GROUPS = 4
EXPERTS_PER_GROUP = 4
EXPERT_FF = 256
SHARED_FF = 256
EPS = 1e-6

N_CTX = BATCH * SEQ
N_LAT = DEC_BATCH * DEC_SEQ
N_ROWS = N_CTX + N_LAT
N_COND = 1 + DEC_BATCH
COND_PAD = 8
HEAD_PAD = 256
IN_COLS = Q_LORA + KV_LORA + 2 * QK_ROPE + 2 * GM_WIDTH
LANES = 128
LOG2E = 1.4426950408889634

TM_PRE = 256
TM_POST = 256
PIECE = 16
LS_ROWS = TM_POST + N_GROUPS * PIECE
LS_PIECES = LS_ROWS // PIECE
N_POST_TILES = N_ROWS // TM_POST
N_LS = N_POST_TILES * LS_ROWS
TM_E = 512
E_PIECES = TM_E // PIECE
N_ETILES = -(-(N_POST_TILES * (TM_POST // PIECE + N_GROUPS - 1)) // E_PIECES) + N_GROUPS
TQ = 512
HEADS_PER_STEP = 4
VMEM_LIMIT = 56 * 1024 * 1024

F32 = jnp.float32
BF16 = jnp.bfloat16


def _rms(x):
    return x * lax.rsqrt(jnp.mean(x * x, axis=-1, keepdims=True) + EPS)


def _gelu(x):
    return 0.5 * x * (1.0 + jnp.tanh(math.sqrt(2.0 / math.pi) * (x + 0.044715 * (x * x * x))))


def _silu(x):
    return x * (1.0 / (1.0 + jnp.exp(-x)))


def _dot(a, b):
    return jnp.dot(a, b, preferred_element_type=F32)


def _dot_nt(a, b):
    return lax.dot_general(a, b, (((1,), (1,)), ((), ())), preferred_element_type=F32)


def _cond_row(i, tm):
    n_ctx_tiles = N_CTX // tm
    per_batch = DEC_SEQ // tm
    return jnp.where(i < n_ctx_tiles, 0, 1 + (i - n_ctx_tiles) // per_batch)


def _rope_block(i, tm):
    n_ctx_tiles = N_CTX // tm
    per_batch = DEC_SEQ // tm
    return jnp.where(i < n_ctx_tiles, 0, 1 + (i - n_ctx_tiles) % per_batch)


def _ctx_lat_specs(tm, width):
    n_ctx_tiles = N_CTX // tm
    return [pl.BlockSpec((tm, width), lambda i: (jnp.minimum(i, n_ctx_tiles - 1), 0)),
            pl.BlockSpec((tm, width), lambda i: (jnp.maximum(i - n_ctx_tiles, 0), 0))]


def _ctx_or_lat(ctx_ref, lat_ref):
    n_ctx_tiles = N_CTX // ctx_ref.shape[0]
    return jnp.where(pl.program_id(0) < n_ctx_tiles, ctx_ref[...], lat_ref[...])


def _full(shape):
    n = len(shape)
    return pl.BlockSpec(shape, lambda *_: (0,) * n)


def _mod_kernel(cond_ref, w_ref, b_ref, o_ref):
    s = _silu(cond_ref[...])
    o_ref[0] = jnp.dot(s, w_ref[0], preferred_element_type=F32,
                       precision=lax.Precision.HIGHEST) + b_ref[0]


def _modulation(cond, w_ada, b_ada):
    tn = 1536
    return pl.pallas_call(
        _mod_kernel,
        out_shape=jax.ShapeDtypeStruct((DEPTH, COND_PAD, 6 * D_MODEL), F32),
        grid=(DEPTH, 6 * D_MODEL // tn),
        in_specs=[
            pl.BlockSpec((COND_PAD, D_MODEL), lambda l, j: (0, 0)),
            pl.BlockSpec((1, D_MODEL, tn), lambda l, j: (l, 0, j)),
            pl.BlockSpec((1, 1, tn), lambda l, j: (l, 0, j)),
        ],
        out_specs=pl.BlockSpec((1, COND_PAD, tn), lambda l, j: (l, 0, j)),
        compiler_params=pltpu.CompilerParams(
            dimension_semantics=("arbitrary", "arbitrary"), vmem_limit_bytes=VMEM_LIMIT),
        name="modulation",
    )(cond, w_ada, b_ada.reshape(DEPTH, 1, 6 * D_MODEL))


def _pre_kernel(xc_ref, xl_ref, mod_ref, rope_ref, g1_ref, win_ref, qg_ref, wqb_ref, kvg_ref,
                wkvb_ref, gmg_ref, ws_ref, bs_ref, ogm_ref,
                q_ref, k_ref, vt_ref, gm_ref, ckv_ref, kr_ref):
    tm = xc_ref.shape[0]
    mod = mod_ref[0]
    shift1 = mod[:, 0:D_MODEL]
    scale1 = mod[:, D_MODEL:2 * D_MODEL]
    h = _rms(_ctx_or_lat(xc_ref, xl_ref)) * g1_ref[...] * (1.0 + scale1) + shift1
    y = _dot(h.astype(BF16), win_ref[...])
    cq = y[:, 0:Q_LORA]
    ckv = y[:, Q_LORA:Q_LORA + KV_LORA]
    kr2 = y[:, 640:768]
    u = y[:, 768:768 + GM_WIDTH]
    vv = y[:, 768 + GM_WIDTH:768 + 2 * GM_WIDTH]

    rope = rope_ref[...]
    lane = lax.broadcasted_iota(jnp.int32, (tm, LANES), 1)

    kr_ref[...] = kr2[:, 0:QK_ROPE]
    t = kr2 * rope
    k_rot = jnp.where(lane < QK_ROPE, t + pltpu.roll(t, QK_ROPE, 1), 0.0).astype(BF16)

    ckv_n = _rms(ckv) * kvg_ref[...]
    ckv_ref[...] = ckv_n
    kv = _dot(ckv_n.astype(BF16), wkvb_ref[...])
    for hd in range(ATTN_HEADS):
        k_ref[:, hd * HEAD_PAD:hd * HEAD_PAD + QK_NOPE] = (
            kv[:, hd * QK_NOPE:(hd + 1) * QK_NOPE].astype(BF16))
        k_ref[:, hd * HEAD_PAD + QK_NOPE:(hd + 1) * HEAD_PAD] = k_rot
    vt_ref[...] = kv[:, ATTN_HEADS * QK_NOPE:].T.astype(BF16)

    q = _dot((_rms(cq) * qg_ref[...]).astype(BF16), wqb_ref[...])
    for hd in range(ATTN_HEADS):
        q_ref[:, hd * HEAD_PAD:hd * HEAD_PAD + QK_NOPE] = (
            q[:, hd * HEAD_PAD:hd * HEAD_PAD + QK_NOPE].astype(BF16))
        t = q[:, hd * HEAD_PAD + QK_NOPE:(hd + 1) * HEAD_PAD] * rope
        q_ref[:, hd * HEAD_PAD + QK_NOPE:(hd + 1) * HEAD_PAD] = (
            t + pltpu.roll(t, QK_ROPE, 1)).astype(BF16)

    ug = _gelu(u)
    vg = _gelu(vv)
    cols = []
    for g in range(GM_GROUPS):
        sl = slice(g * GM_GROUP_DIM, (g + 1) * GM_GROUP_DIM)
        vn = (_rms(vg[:, sl]) * gmg_ref[:, sl]).astype(BF16)
        rows = []
        for c in range(tm // CHUNK):
            sv = _dot(ws_ref[g], vn[c * CHUNK:(c + 1) * CHUNK]) + bs_ref[:, sl]
            rows.append(ug[c * CHUNK:(c + 1) * CHUNK, sl] * sv)
        cols.append(jnp.concatenate(rows, axis=0))
    gm = jnp.concatenate(cols, axis=1)
    gm_ref[...] = (_rms(gm) * ogm_ref[...]).astype(BF16)


def _pre_mixer(x_ctx, x_lat, mod, rope_tab, g1, win, qg, wqb, kvg, wkvb, gmg, ws, bs, ogm):
    tm = TM_PRE
    row = lambda i: (i, 0)
    return pl.pallas_call(
        _pre_kernel,
        out_shape=(
            jax.ShapeDtypeStruct((N_ROWS, ATTN_HEADS * HEAD_PAD), BF16),
            jax.ShapeDtypeStruct((N_ROWS, ATTN_HEADS * HEAD_PAD), BF16),
            jax.ShapeDtypeStruct((ATTN_HEADS * V_DIM, N_ROWS), BF16),
            jax.ShapeDtypeStruct((N_ROWS, GM_WIDTH), BF16),
            jax.ShapeDtypeStruct((N_ROWS, KV_LORA), F32),
            jax.ShapeDtypeStruct((N_ROWS, QK_ROPE), F32),
        ),
        grid=(N_ROWS // tm,),
        in_specs=_ctx_lat_specs(tm, D_MODEL) + [
            pl.BlockSpec((1, 1, 6 * D_MODEL), lambda i: (_cond_row(i, tm), 0, 0)),
            pl.BlockSpec((tm, LANES), lambda i: (_rope_block(i, tm), 0)),
            _full((1, D_MODEL)),
            _full((D_MODEL, IN_COLS)),
            _full((1, Q_LORA)),
            _full((Q_LORA, ATTN_HEADS * HEAD_PAD)),
            _full((1, KV_LORA)),
            _full((KV_LORA, ATTN_HEADS * (QK_NOPE + V_DIM))),
            _full((1, GM_WIDTH)),
            _full((GM_GROUPS, CHUNK, CHUNK)),
            _full((CHUNK, GM_WIDTH)),
            _full((1, GM_WIDTH)),
        ],
        out_specs=(
            pl.BlockSpec((tm, ATTN_HEADS * HEAD_PAD), row),
            pl.BlockSpec((tm, ATTN_HEADS * HEAD_PAD), row),
            pl.BlockSpec((ATTN_HEADS * V_DIM, tm), lambda i: (0, i)),
            pl.BlockSpec((tm, GM_WIDTH), row),
            pl.BlockSpec((tm, KV_LORA), row),
            pl.BlockSpec((tm, QK_ROPE), row),
        ),
        compiler_params=pltpu.CompilerParams(
            dimension_semantics=("arbitrary",), vmem_limit_bytes=VMEM_LIMIT),
        name="pre_mixer",
    )(x_ctx, x_lat, mod, rope_tab, g1, win, qg, wqb, kvg, wkvb, gmg, ws, bs, ogm)


def _cache_kv_kernel(ckv_ref, kr_ref, wkvb_ref, k_ref, vt_ref):
    kv = _dot(ckv_ref[0, 0].astype(BF16), wkvb_ref[0])
    kr = kr_ref[0, 0].astype(BF16)
    for hd in range(ATTN_HEADS):
        k_ref[0, 0, :, hd * HEAD_PAD:hd * HEAD_PAD + QK_NOPE] = (
            kv[:, hd * QK_NOPE:(hd + 1) * QK_NOPE].astype(BF16))
        k_ref[0, 0, :, hd * HEAD_PAD + QK_NOPE:(hd + 1) * HEAD_PAD] = kr
    vt_ref[0, 0] = kv[:, ATTN_HEADS * QK_NOPE:].T.astype(BF16)


def _cache_kv(cache_ckv, cache_kr_pad, wkvb):
    blk = lambda w: pl.BlockSpec((1, 1, PAST_LEN, w), lambda l, b: (b, l, 0, 0))
    return pl.pallas_call(
        _cache_kv_kernel,
        out_shape=(
            jax.ShapeDtypeStruct((DEC_BATCH, DEPTH, PAST_LEN, ATTN_HEADS * HEAD_PAD), BF16),
            jax.ShapeDtypeStruct((DEC_BATCH, DEPTH, ATTN_HEADS * V_DIM, PAST_LEN), BF16),
        ),
        grid=(DEPTH, DEC_BATCH),
        in_specs=[
            blk(KV_LORA),
            blk(LANES),
            pl.BlockSpec((1, KV_LORA, ATTN_HEADS * (QK_NOPE + V_DIM)), lambda l, b: (l, 0, 0)),
        ],
        out_specs=(blk(ATTN_HEADS * HEAD_PAD),
                   pl.BlockSpec((1, 1, ATTN_HEADS * V_DIM, PAST_LEN), lambda l, b: (b, l, 0, 0))),
        compiler_params=pltpu.CompilerParams(
            dimension_semantics=("arbitrary", "arbitrary"), vmem_limit_bytes=VMEM_LIMIT),
        name="cache_kv",
    )(cache_ckv, cache_kr_pad, wkvb)


def _attend_head(qh, key_blocks, vt_blocks):
    s = [_dot_nt(kb, qh) for kb in key_blocks]
    m = functools.reduce(jnp.maximum, [jnp.max(si, axis=0, keepdims=True) for si in s])
    p = [jnp.exp2(si - m) for si in s]
    l = sum(jnp.sum(pi, axis=0, keepdims=True) for pi in p)
    o_t = sum(_dot(vt, pi.astype(BF16)) for vt, pi in zip(vt_blocks, p))
    return (o_t / l).T


def _ctx_attn_kernel(q_ref, k_ref, vt_ref, o_ref):
    for hd in range(ATTN_HEADS):
        hs = slice(hd * HEAD_PAD, (hd + 1) * HEAD_PAD)
        vs = slice(hd * V_DIM, (hd + 1) * V_DIM)
        o_ref[:, vs] = _attend_head(q_ref[:, hs], [k_ref[:, hs]], [vt_ref[vs, :]]).astype(o_ref.dtype)


def _ctx_attention(q, k, vt):
    blk = lambda w: pl.BlockSpec((SEQ, w), lambda b: (b, 0))
    return pl.pallas_call(
        _ctx_attn_kernel,
        out_shape=jax.ShapeDtypeStruct((N_CTX, ATTN_HEADS * V_DIM), BF16),
        grid=(BATCH,),
        in_specs=[blk(ATTN_HEADS * HEAD_PAD), blk(ATTN_HEADS * HEAD_PAD),
                  pl.BlockSpec((ATTN_HEADS * V_DIM, SEQ), lambda b: (0, b))],
        out_specs=blk(ATTN_HEADS * V_DIM),
        compiler_params=pltpu.CompilerParams(
            dimension_semantics=("arbitrary",), vmem_limit_bytes=VMEM_LIMIT),
        name="ctx_attention",
    )(q, k, vt)


def _lat_attn_kernel(q_ref, k_ref, vt_ref, kc_ref, vct_ref, o_ref):
    for hd in range(HEADS_PER_STEP):
        hs = slice(hd * HEAD_PAD, (hd + 1) * HEAD_PAD)
        vs = slice(hd * V_DIM, (hd + 1) * V_DIM)
        o_ref[:, vs] = _attend_head(
            q_ref[:, hs], [kc_ref[0, 0, :, hs], k_ref[:, hs]],
            [vct_ref[0, 0, vs, :], vt_ref[vs, :]]).astype(o_ref.dtype)


def _lat_attention(q, k, vt, kc, vct, layer):
    nq = DEC_SEQ // TQ
    ctx_q_tiles = N_CTX // TQ
    ctx_kv_blocks = N_CTX // DEC_SEQ
    qk_w = HEADS_PER_STEP * HEAD_PAD
    v_w = HEADS_PER_STEP * V_DIM
    return pl.pallas_call(
        _lat_attn_kernel,
        out_shape=jax.ShapeDtypeStruct((N_LAT, ATTN_HEADS * V_DIM), BF16),
        grid=(DEC_BATCH, ATTN_HEADS // HEADS_PER_STEP, nq),
        in_specs=[
            pl.BlockSpec((TQ, qk_w), lambda b, h, i: (ctx_q_tiles + b * nq + i, h)),
            pl.BlockSpec((DEC_SEQ, qk_w), lambda b, h, i: (ctx_kv_blocks + b, h)),
            pl.BlockSpec((v_w, DEC_SEQ), lambda b, h, i: (h, ctx_kv_blocks + b)),
            pl.BlockSpec((1, 1, PAST_LEN, qk_w), lambda b, h, i: (b, layer, 0, h)),
            pl.BlockSpec((1, 1, v_w, PAST_LEN), lambda b, h, i: (b, layer, h, 0)),
        ],
        out_specs=pl.BlockSpec((TQ, v_w), lambda b, h, i: (b * nq + i, h)),
        compiler_params=pltpu.CompilerParams(
            dimension_semantics=("arbitrary", "arbitrary", "arbitrary"),
            vmem_limit_bytes=VMEM_LIMIT),
        name="lat_attention",
    )(q, k, vt, kc, vct)


def _group_peer(x, lane, d, width, period):
    step = d * width
    ahead = pltpu.roll(x, LANES - step, 1)
    behind = pltpu.roll(x, period - step, 1)
    wraps = (lane & (period - 1)) + step >= period
    return jnp.where(wraps, behind, ahead), wraps


def _route(logits, bias, lane):
    valid = lane < N_EXPERTS
    s = 1.0 / (1.0 + jnp.exp(-logits))
    sb = s + bias
    rank = jnp.zeros(sb.shape, jnp.int32)
    for d in range(1, EXPERTS_PER_GROUP):
        o, wraps = _group_peer(sb, lane, d, 1, EXPERTS_PER_GROUP)
        beats = (o > sb) | ((o == sb) & wraps)
        rank = rank + beats.astype(jnp.int32)
    top2 = rank < 2
    t = jnp.where(top2, sb, 0.0)
    gscore = t
    for d in range(1, EXPERTS_PER_GROUP):
        gscore = gscore + _group_peer(t, lane, d, 1, EXPERTS_PER_GROUP)[0]
    grank = jnp.zeros(sb.shape, jnp.int32)
    for d in range(1, N_GROUPS):
        o, wraps = _group_peer(gscore, lane, d, EXPERTS_PER_GROUP, N_EXPERTS)
        beats = (o > gscore) | ((o == gscore) & wraps)
        grank = grank + beats.astype(jnp.int32)
    in_group = (grank == 0) & valid
    w = jnp.where(top2 & in_group, s, 0.0)
    denom = jnp.sum(w, axis=-1, keepdims=True)
    group_flag = jnp.where(in_group & ((lane & (EXPERTS_PER_GROUP - 1)) == 0), 1.0, 0.0)
    return w / denom, group_flag


def _split_bf16(x, terms):
    out = []
    for _ in range(terms - 1):
        t = x.astype(BF16)
        out.append(t)
        x = x - t.astype(F32)
    out.append(x.astype(BF16))
    return out


def _post_kernel(ac_ref, al_ref, gm_ref, xc_ref, xl_ref, mod_ref, oag_ref, wout_ref, g2_ref,
                 wr_ref, rb_ref, x1_ref, h2s_ref, combs_ref, y0_ref, pos_ref, cnt_ref):
    tm = xc_ref.shape[0]
    mod = mod_ref[0]
    gate1 = mod[:, 2 * D_MODEL:3 * D_MODEL]
    shift2 = mod[:, 3 * D_MODEL:4 * D_MODEL]
    scale2 = mod[:, 4 * D_MODEL:5 * D_MODEL]
    an = (_rms(_ctx_or_lat(ac_ref, al_ref).astype(F32)) * oag_ref[...]).astype(BF16)
    mixed = _dot(jnp.concatenate([an, gm_ref[...]], axis=1), wout_ref[...])
    x1 = _ctx_or_lat(xc_ref, xl_ref) + gate1 * mixed
    x1_ref[...] = x1
    h2 = _rms(x1) * g2_ref[...] * (1.0 + scale2) + shift2
    h2_hi, h2_lo = _split_bf16(h2, 2)
    t = _dot(jnp.concatenate([h2_hi, h2_lo], axis=0), wr_ref[...])
    logits = (t[:tm, :LANES] + t[tm:, :LANES]) + (t[:tm, LANES:] + t[tm:, LANES:])
    lane = lax.broadcasted_iota(jnp.int32, (tm, LANES), 1)
    comb, flag = _route(logits, rb_ref[...], lane)

    r_i = lax.broadcasted_iota(jnp.int32, (tm, tm), 0)
    c_i = lax.broadcasted_iota(jnp.int32, (tm, tm), 1)
    before = jnp.where(c_i < r_i, 1.0, 0.0).astype(BF16)
    rank = _dot(before, flag.astype(BF16))
    count = jnp.sum(flag, axis=0, keepdims=True)
    pieces = jnp.floor((count + (PIECE - 1)) * (1.0 / PIECE))
    start = (pltpu.roll(pieces, EXPERTS_PER_GROUP, 1) + pltpu.roll(pieces, 2 * EXPERTS_PER_GROUP, 1)
             + pltpu.roll(pieces, 3 * EXPERTS_PER_GROUP, 1)) * PIECE
    pos = jnp.sum(flag * (start + rank), axis=-1, keepdims=True)
    pos_ref[...] = pos
    cnt_ref[0] = count.astype(jnp.int32)
    pos_row = jnp.transpose(jnp.broadcast_to(pos, (tm, LANES)))[0:1, :].astype(jnp.int32)
    place = lax.broadcasted_iota(jnp.int32, (LS_ROWS, tm), 0) == pos_row
    place = jnp.where(place, 1.0, 0.0).astype(BF16)
    h2s_ref[...] = _dot(place, h2_hi).astype(BF16)
    t = _dot(place, jnp.concatenate(_split_bf16(comb, 3), axis=1))
    combs_ref[...] = t[:, :LANES] + t[:, LANES:2 * LANES] + t[:, 2 * LANES:]
    y0_ref[...] = jnp.zeros(y0_ref.shape, y0_ref.dtype)


def _post_mixer(attn_ctx, attn_lat, gm, x_ctx, x_lat, mod, oag, wout, g2, wr, rb):
    tm = TM_POST
    row = lambda i: (i, 0)
    return pl.pallas_call(
        _post_kernel,
        out_shape=(
            jax.ShapeDtypeStruct((N_ROWS, D_MODEL), F32),
            jax.ShapeDtypeStruct((N_LS, D_MODEL), BF16),
            jax.ShapeDtypeStruct((N_LS, LANES), F32),
            jax.ShapeDtypeStruct((N_LS, D_MODEL), BF16),
            jax.ShapeDtypeStruct((N_ROWS, 1), F32),
            jax.ShapeDtypeStruct((N_POST_TILES, 1, LANES), jnp.int32),
        ),
        grid=(N_ROWS // tm,),
        in_specs=_ctx_lat_specs(tm, ATTN_HEADS * V_DIM) + [
            pl.BlockSpec((tm, GM_WIDTH), row),
        ] + _ctx_lat_specs(tm, D_MODEL) + [
            pl.BlockSpec((1, 1, 6 * D_MODEL), lambda i: (_cond_row(i, tm), 0, 0)),
            _full((1, ATTN_HEADS * V_DIM)),
            _full((D_MODEL, D_MODEL)),
            _full((1, D_MODEL)),
            _full((D_MODEL, 2 * LANES)),
            _full((1, LANES)),
        ],
        out_specs=(
            pl.BlockSpec((tm, D_MODEL), row),
            pl.BlockSpec((LS_ROWS, D_MODEL), row),
            pl.BlockSpec((LS_ROWS, LANES), row),
            pl.BlockSpec((LS_ROWS, D_MODEL), row),
            pl.BlockSpec((tm, 1), row),
            pl.BlockSpec((1, 1, LANES), lambda i: (i, 0, 0)),
        ),
        compiler_params=pltpu.CompilerParams(
            dimension_semantics=("arbitrary",), vmem_limit_bytes=VMEM_LIMIT),
        name="post_mixer",
    )(attn_ctx, attn_lat, gm, x_ctx, x_lat, mod, oag, wout, g2, wr, rb)


def _plan_kernel(cnt_ref, tg_ref, tn_ref, src_ref, nu_ref):
    def clear_src(j, c):
        src_ref[j] = 0
        return c

    lax.fori_loop(0, N_ETILES * E_PIECES, clear_src, 0)

    def clear_tile(j, c):
        tg_ref[j] = N_GROUPS - 1
        tn_ref[j] = 0
        return c

    lax.fori_loop(0, N_ETILES, clear_tile, 0)

    def n_pieces(i, g):
        return lax.shift_right_logical(cnt_ref[i, g * EXPERTS_PER_GROUP] + (PIECE - 1), 4)

    t = jnp.int32(0)
    for g in range(N_GROUPS):
        def tile_body(i, carry, g=g):
            first = i * LS_PIECES
            for gp in range(g):
                first = first + n_pieces(i, gp)

            def piece_body(p, carry):
                t, k = carry
                src_ref[t * E_PIECES + k] = first + p
                k = k + 1
                full = k == E_PIECES

                @pl.when(full)
                def _():
                    tn_ref[t] = E_PIECES
                    tg_ref[t] = g

                return jnp.where(full, t + 1, t), jnp.where(full, 0, k)

            return lax.fori_loop(0, n_pieces(i, g), piece_body, carry)

        t, k = lax.fori_loop(0, N_POST_TILES, tile_body, (t, jnp.int32(0)))

        @pl.when(k > 0)
        def _(t=t, k=k, g=g):
            tn_ref[t] = k
            tg_ref[t] = g

        t = jnp.where(k > 0, t + 1, t)
    nu_ref[0] = t


def _plan(counts):
    smem = pl.BlockSpec(memory_space=pltpu.SMEM)
    return pl.pallas_call(
        _plan_kernel,
        out_shape=(
            jax.ShapeDtypeStruct((N_ETILES,), jnp.int32),
            jax.ShapeDtypeStruct((N_ETILES,), jnp.int32),
            jax.ShapeDtypeStruct((N_ETILES * E_PIECES,), jnp.int32),
            jax.ShapeDtypeStruct((1,), jnp.int32),
        ),
        in_specs=[smem],
        out_specs=(smem, smem, smem, smem),
        name="expert_plan",
    )(counts)


GATHER_X, GATHER_W, SCATTER_Y = 0, 1, 2


def _moe_kernel(tg_ref, tn_ref, src_ref, nu_ref, h2s_hbm, combs_hbm, y0_hbm, wg_ref, wu_ref, wd_ref,
                wsg_ref, wsu_ref, wsd_ref, y_hbm, xbuf, cbuf, ybuf, sem):
    del y0_hbm
    j = pl.program_id(0)
    n_used = nu_ref[0]
    slot = lax.rem(j, 2)

    def piece_rows(t, k):
        hbm_rows = pl.ds(pl.multiple_of(src_ref[t * E_PIECES + k] * PIECE, PIECE), PIECE)
        buf_rows = pl.ds(pl.multiple_of(k * PIECE, PIECE), PIECE)
        return hbm_rows, buf_rows

    def gather_copies(t, k, slot):
        hbm_rows, buf_rows = piece_rows(t, k)
        return (
            pltpu.make_async_copy(h2s_hbm.at[hbm_rows], xbuf.at[slot, buf_rows], sem.at[GATHER_X, slot]),
            pltpu.make_async_copy(combs_hbm.at[hbm_rows], cbuf.at[slot, buf_rows], sem.at[GATHER_W, slot]),
        )

    def scatter_copies(t, k, slot):
        hbm_rows, buf_rows = piece_rows(t, k)
        return (
            pltpu.make_async_copy(ybuf.at[slot, buf_rows], y_hbm.at[hbm_rows], sem.at[SCATTER_Y, slot]),
        )

    def for_pieces(t, slot, copies, action):
        def body(k, c):
            for cp in copies(t, k, slot):
                action(cp)
            return c
        lax.fori_loop(0, tn_ref[t], body, 0)

    start = lambda cp: cp.start()
    wait = lambda cp: cp.wait()

    @pl.when(j == 0)
    def _():
        xbuf[...] = jnp.zeros(xbuf.shape, xbuf.dtype)
        cbuf[...] = jnp.zeros(cbuf.shape, cbuf.dtype)
        for_pieces(0, 0, gather_copies, start)

    @pl.when(j + 1 < n_used)
    def _():
        for_pieces(j + 1, 1 - slot, gather_copies, start)

    @pl.when(j < n_used)
    def _():
        for_pieces(j, slot, gather_copies, wait)
        x = xbuf[slot]
        comb = cbuf[slot]
        lane = lax.broadcasted_iota(jnp.int32, (TM_E, LANES), 1)
        first = tg_ref[j] * EXPERTS_PER_GROUP
        y = _dot((_silu(_dot(x, wsg_ref[...])) * _dot(x, wsu_ref[...])).astype(BF16), wsd_ref[...])
        for k in range(EXPERTS_PER_GROUP):
            wk = jnp.sum(jnp.where(lane == first + k, comb, 0.0), axis=-1, keepdims=True)
            act = _silu(_dot(x, wg_ref[k])) * _dot(x, wu_ref[k]) * wk
            y = y + _dot(act.astype(BF16), wd_ref[k])
        ybuf[slot] = y.astype(BF16)
        for_pieces(j, slot, scatter_copies, start)

    @pl.when((j >= 1) & (j < n_used))
    def _():
        for_pieces(j - 1, 1 - slot, scatter_copies, wait)

    @pl.when(j == n_used - 1)
    def _():
        for_pieces(j, slot, scatter_copies, wait)


def _experts(plan, h2s, combs, y0, wg, wu, wd, wsg, wsu, wsd):
    tg, tn, src, nu = plan
    any_spec = pl.BlockSpec(memory_space=pl.ANY)
    group = lambda j, tg, tn, src, nu: (tg[j], 0, 0)
    whole = lambda j, tg, tn, src, nu: (0, 0)
    return pl.pallas_call(
        _moe_kernel,
        out_shape=jax.ShapeDtypeStruct((N_LS, D_MODEL), BF16),
        grid_spec=pltpu.PrefetchScalarGridSpec(
            num_scalar_prefetch=4,
            grid=(N_ETILES,),
            in_specs=[
                any_spec, any_spec, any_spec,
                pl.BlockSpec((EXPERTS_PER_GROUP, D_MODEL, EXPERT_FF), group),
                pl.BlockSpec((EXPERTS_PER_GROUP, D_MODEL, EXPERT_FF), group),
                pl.BlockSpec((EXPERTS_PER_GROUP, EXPERT_FF, D_MODEL), group),
                pl.BlockSpec((D_MODEL, SHARED_FF), whole),
                pl.BlockSpec((D_MODEL, SHARED_FF), whole),
                pl.BlockSpec((SHARED_FF, D_MODEL), whole),
            ],
            out_specs=any_spec,
            scratch_shapes=[
                pltpu.VMEM((2, TM_E, D_MODEL), BF16),
                pltpu.VMEM((2, TM_E, LANES), F32),
                pltpu.VMEM((2, TM_E, D_MODEL), BF16),
                pltpu.SemaphoreType.DMA((3, 2)),
            ],
        ),
        input_output_aliases={6: 0},
        compiler_params=pltpu.CompilerParams(
            dimension_semantics=("arbitrary",), vmem_limit_bytes=VMEM_LIMIT),
        name="experts",
    )(tg, tn, src, nu, h2s, combs, y0, wg, wu, wd, wsg, wsu, wsd)


def _combine_kernel(y_ref, pos_ref, x1_ref, mod_ref, fg_ref, o_ref, *, final):
    tm = x1_ref.shape[0]
    pick = lax.broadcasted_iota(jnp.int32, (tm, LS_ROWS), 1) == pos_ref[...].astype(jnp.int32)
    moe = _dot(jnp.where(pick, 1.0, 0.0).astype(BF16), y_ref[...])
    gate2 = mod_ref[0][:, 5 * D_MODEL:6 * D_MODEL]
    x2 = x1_ref[...] + gate2 * moe
    if final:
        x2 = _rms(x2) * fg_ref[...]
    o_ref[...] = x2


def _combine(y, pos, x1, mod, fg, final, first_tile, n_tiles):
    tm = TM_POST
    row = lambda i: (first_tile + i, 0)
    return pl.pallas_call(
        functools.partial(_combine_kernel, final=final),
        out_shape=jax.ShapeDtypeStruct((n_tiles * tm, D_MODEL), F32),
        grid=(n_tiles,),
        in_specs=[
            pl.BlockSpec((LS_ROWS, D_MODEL), row),
            pl.BlockSpec((tm, 1), row),
            pl.BlockSpec((tm, D_MODEL), row),
            pl.BlockSpec((1, 1, 6 * D_MODEL), lambda i: (_cond_row(first_tile + i, tm), 0, 0)),
            _full((1, D_MODEL)),
        ],
        out_specs=pl.BlockSpec((tm, D_MODEL), lambda i: (i, 0)),
        compiler_params=pltpu.CompilerParams(
            dimension_semantics=("arbitrary",), vmem_limit_bytes=VMEM_LIMIT),
        name="combine",
    )(y, pos, x1, mod, fg)


def _rope_table():
    rows = DEC_SEQ // GRID_W
    row = jnp.repeat(jnp.arange(rows, dtype=F32), GRID_W)
    col = jnp.tile(jnp.arange(GRID_W, dtype=F32), rows)
    half = QK_ROPE // 2
    freqs = 1.0 / (ROPE_BASE ** (jnp.arange(0, half, 2, dtype=F32) / half))
    ang = jnp.concatenate([row[:, None] * freqs, col[:, None] * freqs], axis=-1)
    cos, sin = jnp.cos(ang), jnp.sin(ang)
    lat = jnp.concatenate([cos, cos, -sin, sin], axis=-1)
    ident = jnp.concatenate([jnp.ones((TM_PRE, QK_ROPE), F32), jnp.zeros((TM_PRE, QK_ROPE), F32)], axis=-1)
    return jnp.concatenate([ident, lat], axis=0)


_DEINT = np.concatenate([np.arange(0, QK_ROPE, 2), np.arange(1, QK_ROPE, 2)])
_SWAP = np.concatenate([np.arange(1, QK_ROPE, 2), np.arange(0, QK_ROPE, 2)])
_INTERLEAVE = np.argsort(_DEINT)


def _layout_w_in(w_in):
    cq_ckv = w_in[:, :Q_LORA + KV_LORA]
    kr = w_in[:, Q_LORA + KV_LORA:Q_LORA + KV_LORA + QK_ROPE]
    uv = w_in[:, Q_LORA + KV_LORA + QK_ROPE:]
    return jnp.concatenate([cq_ckv, kr[:, _DEINT], kr[:, _SWAP], uv], axis=1).astype(BF16)


def _layout_w_qb(w_qb):
    w = (w_qb * (ATTN_SCALE * LOG2E)).reshape(Q_LORA, ATTN_HEADS, QK_NOPE + QK_ROPE)
    nope, rope = w[..., :QK_NOPE], w[..., QK_NOPE:]
    w = jnp.concatenate([nope, rope[..., _DEINT], rope[..., _SWAP]], axis=-1)
    return w.reshape(Q_LORA, ATTN_HEADS * HEAD_PAD).astype(BF16)


def _layout_w_kvb(w_kvb):
    w = w_kvb.reshape(KV_LORA, ATTN_HEADS, QK_NOPE + V_DIM)
    k = w[..., :QK_NOPE].reshape(KV_LORA, ATTN_HEADS * QK_NOPE)
    v = w[..., QK_NOPE:].reshape(KV_LORA, ATTN_HEADS * V_DIM)
    return jnp.concatenate([k, v], axis=1).astype(BF16)


def kernel(x_prompt, x_sample, cache_ckv, cache_krope, c, c_ctx, norm1_g, w_ada, b_ada, w_in,
           q_norm_g, w_qb, kv_norm_g, w_kvb, gm_norm_g, w_spatial, b_spatial, onorm_attn_g,
           onorm_gm_g, w_out, norm2_g, w_router, router_bias, w_gate, w_up, w_down, ws_gate,
           ws_up, ws_down, final_norm_g):
    x_ctx = x_prompt.reshape(N_CTX, D_MODEL)
    x_lat = x_sample.reshape(N_LAT, D_MODEL)
    cond = jnp.concatenate([c_ctx[None, :], c, jnp.zeros((COND_PAD - N_COND, D_MODEL), F32)], axis=0)
    mod = _modulation(cond, w_ada, b_ada)
    rope_tab = _rope_table()

    wkvb = jnp.stack([_layout_w_kvb(w_kvb[l]) for l in range(DEPTH)])
    cache_kr = jnp.pad(cache_krope[..., _DEINT], ((0, 0), (0, 0), (0, 0), (0, LANES - QK_ROPE)))
    kc, vct = _cache_kv(cache_ckv, cache_kr, wkvb)

    wr = jnp.pad(w_router, ((0, 0), (0, LANES - N_EXPERTS)))
    wr_hi = wr.astype(BF16)
    wr = jnp.concatenate([wr_hi, (wr - wr_hi.astype(F32)).astype(BF16)], axis=1)
    rb = jnp.pad(router_bias, (0, LANES - N_EXPERTS)).reshape(1, LANES)
    fg = final_norm_g.reshape(1, D_MODEL)

    ckv_out, kr_out = [], []
    for l in range(DEPTH):
        mod_l = mod[l].reshape(COND_PAD, 1, 6 * D_MODEL)
        bs = jnp.broadcast_to(b_spatial[l].T[:, :, None], (CHUNK, GM_GROUPS, GM_GROUP_DIM))
        q, k, vt, gm, ckv_n, kr = _pre_mixer(
            x_ctx, x_lat, mod_l, rope_tab, norm1_g[l].reshape(1, -1), _layout_w_in(w_in[l]),
            q_norm_g[l].reshape(1, -1), _layout_w_qb(w_qb[l]), kv_norm_g[l].reshape(1, -1),
            wkvb[l], gm_norm_g[l].reshape(1, -1), w_spatial[l].astype(BF16),
            bs.reshape(CHUNK, GM_WIDTH), onorm_gm_g[l].reshape(1, -1))
        ckv_out.append(ckv_n[:N_CTX].reshape(BATCH, SEQ, KV_LORA))
        kr_out.append(kr[:N_CTX][:, _INTERLEAVE].reshape(BATCH, SEQ, QK_ROPE))
        x1, h2s, combs, y0, pos, counts = _post_mixer(
            _ctx_attention(q, k, vt), _lat_attention(q, k, vt, kc, vct, l), gm, x_ctx, x_lat, mod_l,
            onorm_attn_g[l].reshape(1, -1), w_out[l].astype(BF16), norm2_g[l].reshape(1, -1), wr, rb)
        plan = _plan(counts.reshape(N_POST_TILES, LANES))
        y = _experts(plan, h2s, combs, y0, w_gate[l].astype(BF16), w_up[l].astype(BF16),
                     w_down[l].astype(BF16), ws_gate[l].astype(BF16), ws_up[l].astype(BF16),
                     ws_down[l].astype(BF16))
        final = l == DEPTH - 1
        n_ctx_tiles = N_CTX // TM_POST
        x_ctx = _combine(y, pos, x1, mod_l, fg, final, 0, n_ctx_tiles)
        x_lat = _combine(y, pos, x1, mod_l, fg, final, n_ctx_tiles, N_POST_TILES - n_ctx_tiles)

    y_prompt = x_ctx.reshape(BATCH, SEQ, D_MODEL)
    y_sample = x_lat.reshape(DEC_BATCH, DEC_SEQ, D_MODEL)
    return y_prompt, y_sample, jnp.stack(ckv_out, axis=1), jnp.stack(kr_out, axis=1)
```

```python
import functools
import math

import jax
import jax.numpy as jnp
import numpy as np
from jax import lax
from jax.experimental import pallas as pl
from jax.experimental.pallas import tpu as pltpu

D_MODEL = 1024
BATCH = 16
SEQ = 256
DEPTH = 2
DEC_BATCH = 4
DEC_SEQ = 4096
PAST_LEN = 256
GRID_W = 64
ATTN_HEADS = 4
QK_NOPE = 128
QK_ROPE = 64
V_DIM = 128
Q_LORA = 384
KV_LORA = 256
ATTN_SCALE = (QK_NOPE + QK_ROPE) ** -0.5
ROPE_BASE = 10000.0
CHUNK = 128
GM_WIDTH = 512
GM_GROUPS = 4
GM_GROUP_DIM = 128
N_EXPERTS = 16
N_GROUPS = 4
EXPERTS_PER_GROUP = 4
EXPERT_FF = 256
SHARED_FF = 256
EPS = 1e-6

N_CTX = BATCH * SEQ
N_LAT = DEC_BATCH * DEC_SEQ
N_ROWS = N_CTX + N_LAT
N_COND = 1 + DEC_BATCH
COND_PAD = 8
HEAD_PAD = 256
IN_COLS = Q_LORA + KV_LORA + 2 * QK_ROPE + 2 * GM_WIDTH
LANES = 128
LOG2E = 1.4426950408889634

TM_PRE = 512
TM_POST = 256
COMB_TERMS = 3
XS_COLS = D_MODEL + COMB_TERMS * LANES
POST_SUB = 4
PIECE = 16
LS_ROWS = TM_POST + N_GROUPS * PIECE
LS_PIECES = LS_ROWS // PIECE
N_POST_TILES = N_ROWS // TM_POST
N_LS = N_POST_TILES * LS_ROWS
TM_E = 512
E_PIECES = TM_E // PIECE
PIECE_SHIFT = PIECE.bit_length() - 1
E_SHIFT = E_PIECES.bit_length() - 1
N_ETILES = -(-(N_POST_TILES * (TM_POST // PIECE + N_GROUPS - 1)) // E_PIECES) + N_GROUPS
TQ = 512
HEADS_PER_STEP = 4
VMEM_LIMIT = 56 * 1024 * 1024

F32 = jnp.float32
BF16 = jnp.bfloat16


def _rms(x):
    return x * lax.rsqrt(jnp.mean(x * x, axis=-1, keepdims=True) + EPS)


def _gelu(x):
    return 0.5 * x * (1.0 + jnp.tanh(math.sqrt(2.0 / math.pi) * (x + 0.044715 * (x * x * x))))


def _silu(x):
    return x * (1.0 / (1.0 + jnp.exp(-x)))


def _dot(a, b):
    return jnp.dot(a, b, preferred_element_type=F32)


def _dot_nt(a, b):
    return lax.dot_general(a, b, (((1,), (1,)), ((), ())), preferred_element_type=F32)


def _cond_row(i, tm):
    n_ctx_tiles = N_CTX // tm
    per_batch = DEC_SEQ // tm
    return jnp.where(i < n_ctx_tiles, 0, 1 + (i - n_ctx_tiles) // per_batch)


def _rope_block(i, tm):
    n_ctx_tiles = N_CTX // tm
    per_batch = DEC_SEQ // tm
    return jnp.where(i < n_ctx_tiles, 0, 1 + (i - n_ctx_tiles) % per_batch)


def _ctx_lat_specs(tm, width):
    n_ctx_tiles = N_CTX // tm
    return [pl.BlockSpec((tm, width), lambda i: (jnp.minimum(i, n_ctx_tiles - 1), 0)),
            pl.BlockSpec((tm, width), lambda i: (jnp.maximum(i - n_ctx_tiles, 0), 0))]


def _ctx_or_lat(ctx_ref, lat_ref):
    n_ctx_tiles = N_CTX // ctx_ref.shape[0]
    return jnp.where(pl.program_id(0) < n_ctx_tiles, ctx_ref[...], lat_ref[...])


def _full(shape):
    n = len(shape)
    return pl.BlockSpec(shape, lambda *_: (0,) * n)


def _mod_kernel(cond_ref, w_ref, b_ref, o_ref):
    s = _silu(cond_ref[...])
    o_ref[0] = jnp.dot(s, w_ref[0], preferred_element_type=F32,
                       precision=lax.Precision.HIGHEST) + b_ref[0]


def _modulation(cond, w_ada, b_ada):
    tn = 1536
    return pl.pallas_call(
        _mod_kernel,
        out_shape=jax.ShapeDtypeStruct((DEPTH, COND_PAD, 6 * D_MODEL), F32),
        grid=(DEPTH, 6 * D_MODEL // tn),
        in_specs=[
            pl.BlockSpec((COND_PAD, D_MODEL), lambda l, j: (0, 0)),
            pl.BlockSpec((1, D_MODEL, tn), lambda l, j: (l, 0, j)),
            pl.BlockSpec((1, 1, tn), lambda l, j: (l, 0, j)),
        ],
        out_specs=pl.BlockSpec((1, COND_PAD, tn), lambda l, j: (l, 0, j)),
        compiler_params=pltpu.CompilerParams(
            dimension_semantics=("arbitrary", "arbitrary"), vmem_limit_bytes=VMEM_LIMIT),
        name="modulation",
    )(cond, w_ada, b_ada.reshape(DEPTH, 1, 6 * D_MODEL))


def _pre_kernel(xc_ref, xl_ref, mod_ref, rope_ref, g1_ref, win_ref, qg_ref, wqb_ref, kvg_ref,
                wkvb_ref, gmg_ref, ws_ref, bs_ref, ogm_ref,
                q_ref, k_ref, vt_ref, gm_ref, ckv_ref, kr_ref):
    tm = xc_ref.shape[0]
    mod = mod_ref[0]
    shift1 = mod[:, 0:D_MODEL]
    scale1 = mod[:, D_MODEL:2 * D_MODEL]
    h = _rms(_ctx_or_lat(xc_ref, xl_ref)) * g1_ref[...] * (1.0 + scale1) + shift1
    y = _dot(h.astype(BF16), win_ref[...])
    cq = y[:, 0:Q_LORA]
    ckv = y[:, Q_LORA:Q_LORA + KV_LORA]
    kr2 = y[:, 640:768]
    u = y[:, 768:768 + GM_WIDTH]
    vv = y[:, 768 + GM_WIDTH:768 + 2 * GM_WIDTH]

    rope = rope_ref[...]
    lane = lax.broadcasted_iota(jnp.int32, (tm, LANES), 1)

    kr_ref[...] = kr2[:, 0:QK_ROPE]
    t = kr2 * rope
    k_rot = jnp.where(lane < QK_ROPE, t + pltpu.roll(t, QK_ROPE, 1), 0.0).astype(BF16)

    ckv_n = _rms(ckv) * kvg_ref[...]
    ckv_ref[...] = ckv_n
    kv = _dot(ckv_n.astype(BF16), wkvb_ref[...])
    for hd in range(ATTN_HEADS):
        k_ref[:, hd * HEAD_PAD:hd * HEAD_PAD + QK_NOPE] = (
            kv[:, hd * QK_NOPE:(hd + 1) * QK_NOPE].astype(BF16))
        k_ref[:, hd * HEAD_PAD + QK_NOPE:(hd + 1) * HEAD_PAD] = k_rot
    vt_ref[...] = kv[:, ATTN_HEADS * QK_NOPE:].T.astype(BF16)

    q = _dot((_rms(cq) * qg_ref[...]).astype(BF16), wqb_ref[...])
    for hd in range(ATTN_HEADS):
        q_ref[:, hd * HEAD_PAD:hd * HEAD_PAD + QK_NOPE] = (
            q[:, hd * HEAD_PAD:hd * HEAD_PAD + QK_NOPE].astype(BF16))
        t = q[:, hd * HEAD_PAD + QK_NOPE:(hd + 1) * HEAD_PAD] * rope
        q_ref[:, hd * HEAD_PAD + QK_NOPE:(hd + 1) * HEAD_PAD] = (
            t + pltpu.roll(t, QK_ROPE, 1)).astype(BF16)

    ug = _gelu(u)
    vg = _gelu(vv)
    cols = []
    for g in range(GM_GROUPS):
        sl = slice(g * GM_GROUP_DIM, (g + 1) * GM_GROUP_DIM)
        vn = (_rms(vg[:, sl]) * gmg_ref[:, sl]).astype(BF16)
        rows = []
        for c in range(tm // CHUNK):
            sv = _dot(ws_ref[g], vn[c * CHUNK:(c + 1) * CHUNK]) + bs_ref[:, sl]
            rows.append(ug[c * CHUNK:(c + 1) * CHUNK, sl] * sv)
        cols.append(jnp.concatenate(rows, axis=0))
    gm = jnp.concatenate(cols, axis=1)
    gm_ref[...] = (_rms(gm) * ogm_ref[...]).astype(BF16)


def _pre_mixer(x_ctx, x_lat, mod, rope_tab, g1, win, qg, wqb, kvg, wkvb, gmg, ws, bs, ogm):
    tm = TM_PRE
    row = lambda i: (i, 0)
    return pl.pallas_call(
        _pre_kernel,
        out_shape=(
            jax.ShapeDtypeStruct((N_ROWS, ATTN_HEADS * HEAD_PAD), BF16),
            jax.ShapeDtypeStruct((N_ROWS, ATTN_HEADS * HEAD_PAD), BF16),
            jax.ShapeDtypeStruct((ATTN_HEADS * V_DIM, N_ROWS), BF16),
            jax.ShapeDtypeStruct((N_ROWS, GM_WIDTH), BF16),
            jax.ShapeDtypeStruct((N_ROWS, KV_LORA), F32),
            jax.ShapeDtypeStruct((N_ROWS, QK_ROPE), F32),
        ),
        grid=(N_ROWS // tm,),
        in_specs=_ctx_lat_specs(tm, D_MODEL) + [
            pl.BlockSpec((1, 1, 6 * D_MODEL), lambda i: (_cond_row(i, tm), 0, 0)),
            pl.BlockSpec((tm, LANES), lambda i: (_rope_block(i, tm), 0)),
            _full((1, D_MODEL)),
            _full((D_MODEL, IN_COLS)),
            _full((1, Q_LORA)),
            _full((Q_LORA, ATTN_HEADS * HEAD_PAD)),
            _full((1, KV_LORA)),
            _full((KV_LORA, ATTN_HEADS * (QK_NOPE + V_DIM))),
            _full((1, GM_WIDTH)),
            _full((GM_GROUPS, CHUNK, CHUNK)),
            _full((CHUNK, GM_WIDTH)),
            _full((1, GM_WIDTH)),
        ],
        out_specs=(
            pl.BlockSpec((tm, ATTN_HEADS * HEAD_PAD), row),
            pl.BlockSpec((tm, ATTN_HEADS * HEAD_PAD), row),
            pl.BlockSpec((ATTN_HEADS * V_DIM, tm), lambda i: (0, i)),
            pl.BlockSpec((tm, GM_WIDTH), row),
            pl.BlockSpec((tm, KV_LORA), row),
            pl.BlockSpec((tm, QK_ROPE), row),
        ),
        compiler_params=pltpu.CompilerParams(
            dimension_semantics=("arbitrary",), vmem_limit_bytes=VMEM_LIMIT),
        name="pre_mixer",
    )(x_ctx, x_lat, mod, rope_tab, g1, win, qg, wqb, kvg, wkvb, gmg, ws, bs, ogm)


def _cache_kv_kernel(ckv_ref, kr_ref, wkvb_ref, k_ref, vt_ref):
    kv = _dot(ckv_ref[0, 0].astype(BF16), wkvb_ref[0])
    kr = kr_ref[0, 0].astype(BF16)
    for hd in range(ATTN_HEADS):
        k_ref[0, 0, :, hd * HEAD_PAD:hd * HEAD_PAD + QK_NOPE] = (
            kv[:, hd * QK_NOPE:(hd + 1) * QK_NOPE].astype(BF16))
        k_ref[0, 0, :, hd * HEAD_PAD + QK_NOPE:(hd + 1) * HEAD_PAD] = kr
    vt_ref[0, 0] = kv[:, ATTN_HEADS * QK_NOPE:].T.astype(BF16)


def _cache_kv(cache_ckv, cache_kr_pad, wkvb):
    blk = lambda w: pl.BlockSpec((1, 1, PAST_LEN, w), lambda l, b: (b, l, 0, 0))
    return pl.pallas_call(
        _cache_kv_kernel,
        out_shape=(
            jax.ShapeDtypeStruct((DEC_BATCH, DEPTH, PAST_LEN, ATTN_HEADS * HEAD_PAD), BF16),
            jax.ShapeDtypeStruct((DEC_BATCH, DEPTH, ATTN_HEADS * V_DIM, PAST_LEN), BF16),
        ),
        grid=(DEPTH, DEC_BATCH),
        in_specs=[
            blk(KV_LORA),
            blk(LANES),
            pl.BlockSpec((1, KV_LORA, ATTN_HEADS * (QK_NOPE + V_DIM)), lambda l, b: (l, 0, 0)),
        ],
        out_specs=(blk(ATTN_HEADS * HEAD_PAD),
                   pl.BlockSpec((1, 1, ATTN_HEADS * V_DIM, PAST_LEN), lambda l, b: (b, l, 0, 0))),
        compiler_params=pltpu.CompilerParams(
            dimension_semantics=("arbitrary", "arbitrary"), vmem_limit_bytes=VMEM_LIMIT),
        name="cache_kv",
    )(cache_ckv, cache_kr_pad, wkvb)


def _attend_head(qh, key_blocks, vt_blocks):
    s = [_dot_nt(kb, qh) for kb in key_blocks]
    m = functools.reduce(jnp.maximum, [jnp.max(si, axis=0, keepdims=True) for si in s])
    p = [jnp.exp2(si - m) for si in s]
    l = sum(jnp.sum(pi, axis=0, keepdims=True) for pi in p)
    o_t = sum(_dot(vt, pi.astype(BF16)) for vt, pi in zip(vt_blocks, p))
    return (o_t / l).T


def _ctx_attn_kernel(q_ref, k_ref, vt_ref, o_ref):
    for hd in range(ATTN_HEADS):
        hs = slice(hd * HEAD_PAD, (hd + 1) * HEAD_PAD)
        vs = slice(hd * V_DIM, (hd + 1) * V_DIM)
        o_ref[:, vs] = _attend_head(q_ref[:, hs], [k_ref[:, hs]], [vt_ref[vs, :]]).astype(o_ref.dtype)


def _ctx_attention(q, k, vt):
    blk = lambda w: pl.BlockSpec((SEQ, w), lambda b: (b, 0))
    return pl.pallas_call(
        _ctx_attn_kernel,
        out_shape=jax.ShapeDtypeStruct((N_CTX, ATTN_HEADS * V_DIM), BF16),
        grid=(BATCH,),
        in_specs=[blk(ATTN_HEADS * HEAD_PAD), blk(ATTN_HEADS * HEAD_PAD),
                  pl.BlockSpec((ATTN_HEADS * V_DIM, SEQ), lambda b: (0, b))],
        out_specs=blk(ATTN_HEADS * V_DIM),
        compiler_params=pltpu.CompilerParams(
            dimension_semantics=("arbitrary",), vmem_limit_bytes=VMEM_LIMIT),
        name="ctx_attention",
    )(q, k, vt)


def _lat_attn_kernel(q_ref, k_ref, vt_ref, kc_ref, vct_ref, o_ref):
    for hd in range(HEADS_PER_STEP):
        hs = slice(hd * HEAD_PAD, (hd + 1) * HEAD_PAD)
        vs = slice(hd * V_DIM, (hd + 1) * V_DIM)
        o_ref[:, vs] = _attend_head(
            q_ref[:, hs], [kc_ref[0, 0, :, hs], k_ref[:, hs]],
            [vct_ref[0, 0, vs, :], vt_ref[vs, :]]).astype(o_ref.dtype)


def _lat_attention(q, k, vt, kc, vct, layer):
    nq = DEC_SEQ // TQ
    ctx_q_tiles = N_CTX // TQ
    ctx_kv_blocks = N_CTX // DEC_SEQ
    qk_w = HEADS_PER_STEP * HEAD_PAD
    v_w = HEADS_PER_STEP * V_DIM
    return pl.pallas_call(
        _lat_attn_kernel,
        out_shape=jax.ShapeDtypeStruct((N_LAT, ATTN_HEADS * V_DIM), BF16),
        grid=(DEC_BATCH, ATTN_HEADS // HEADS_PER_STEP, nq),
        in_specs=[
            pl.BlockSpec((TQ, qk_w), lambda b, h, i: (ctx_q_tiles + b * nq + i, h)),
            pl.BlockSpec((DEC_SEQ, qk_w), lambda b, h, i: (ctx_kv_blocks + b, h)),
            pl.BlockSpec((v_w, DEC_SEQ), lambda b, h, i: (h, ctx_kv_blocks + b)),
            pl.BlockSpec((1, 1, PAST_LEN, qk_w), lambda b, h, i: (b, layer, 0, h)),
            pl.BlockSpec((1, 1, v_w, PAST_LEN), lambda b, h, i: (b, layer, h, 0)),
        ],
        out_specs=pl.BlockSpec((TQ, v_w), lambda b, h, i: (b * nq + i, h)),
        compiler_params=pltpu.CompilerParams(
            dimension_semantics=("arbitrary", "arbitrary", "arbitrary"),
            vmem_limit_bytes=VMEM_LIMIT),
        name="lat_attention",
    )(q, k, vt, kc, vct)


def _group_peer(x, lane, d, width, period):
    step = d * width
    ahead = pltpu.roll(x, LANES - step, 1)
    behind = pltpu.roll(x, period - step, 1)
    wraps = (lane & (period - 1)) + step >= period
    return jnp.where(wraps, behind, ahead), wraps


def _route(logits, bias, lane):
    valid = lane < N_EXPERTS
    s = 1.0 / (1.0 + jnp.exp(-logits))
    sb = s + bias
    rank = jnp.zeros(sb.shape, jnp.int32)
    for d in range(1, EXPERTS_PER_GROUP):
        o, wraps = _group_peer(sb, lane, d, 1, EXPERTS_PER_GROUP)
        beats = (o > sb) | ((o == sb) & wraps)
        rank = rank + beats.astype(jnp.int32)
    top2 = rank < 2
    t = jnp.where(top2, sb, 0.0)
    gscore = t
    for d in range(1, EXPERTS_PER_GROUP):
        gscore = gscore + _group_peer(t, lane, d, 1, EXPERTS_PER_GROUP)[0]
    grank = jnp.zeros(sb.shape, jnp.int32)
    for d in range(1, N_GROUPS):
        o, wraps = _group_peer(gscore, lane, d, EXPERTS_PER_GROUP, N_EXPERTS)
        beats = (o > gscore) | ((o == gscore) & wraps)
        grank = grank + beats.astype(jnp.int32)
    in_group = (grank == 0) & valid
    w = jnp.where(top2 & in_group, s, 0.0)
    denom = jnp.sum(w, axis=-1, keepdims=True)
    group_flag = jnp.where(in_group & ((lane & (EXPERTS_PER_GROUP - 1)) == 0), 1.0, 0.0)
    return w / denom, group_flag


def _split_bf16(x, terms):
    out = []
    for _ in range(terms - 1):
        t = x.astype(BF16)
        out.append(t)
        x = x - t.astype(F32)
    out.append(x.astype(BF16))
    return out


def _post_kernel(ac_ref, al_ref, gm_ref, xc_ref, xl_ref, mod_ref, oag_ref, wout_ref, g2_ref,
                 wr_ref, rb_ref, x1_ref, h2s_ref, y0_ref, pos_ref, cnt_ref):
    mod = mod_ref[0]
    gate1 = mod[:, 2 * D_MODEL:3 * D_MODEL]
    shift2 = mod[:, 3 * D_MODEL:4 * D_MODEL]
    scale2 = mod[:, 4 * D_MODEL:5 * D_MODEL]
    an = (_rms(_ctx_or_lat(ac_ref, al_ref).astype(F32)) * oag_ref[...]).astype(BF16)
    mixed = _dot(jnp.concatenate([an, gm_ref[...]], axis=1), wout_ref[...])
    x1 = _ctx_or_lat(xc_ref, xl_ref) + gate1 * mixed
    x1_ref[...] = x1
    h2 = _rms(x1) * g2_ref[...] * (1.0 + scale2) + shift2
    y0_ref[...] = jnp.zeros(y0_ref.shape, y0_ref.dtype)
    for sub in range(POST_SUB):
        rows = slice(sub * TM_POST, (sub + 1) * TM_POST)
        ls_rows = slice(sub * LS_ROWS, (sub + 1) * LS_ROWS)
        _route_and_sort(h2[rows], wr_ref, rb_ref, h2s_ref.at[ls_rows], pos_ref.at[rows],
                        cnt_ref.at[sub])


def _route_and_sort(h2, wr_ref, rb_ref, h2s_ref, pos_ref, cnt_ref):
    tm = TM_POST
    h2_hi, h2_lo = _split_bf16(h2, 2)
    t = _dot(jnp.concatenate([h2_hi, h2_lo], axis=0), wr_ref[...])
    logits = (t[:tm, :LANES] + t[tm:, :LANES]) + (t[:tm, LANES:] + t[tm:, LANES:])
    lane = lax.broadcasted_iota(jnp.int32, (tm, LANES), 1)
    comb, flag = _route(logits, rb_ref[...], lane)

    r_i = lax.broadcasted_iota(jnp.int32, (tm, tm), 0)
    c_i = lax.broadcasted_iota(jnp.int32, (tm, tm), 1)
    before = jnp.where(c_i < r_i, 1.0, 0.0).astype(BF16)
    rank = _dot(before, flag.astype(BF16))
    count = jnp.sum(flag, axis=0, keepdims=True)
    pieces = jnp.floor((count + (PIECE - 1)) * (1.0 / PIECE))
    start = (pltpu.roll(pieces, EXPERTS_PER_GROUP, 1) + pltpu.roll(pieces, 2 * EXPERTS_PER_GROUP, 1)
             + pltpu.roll(pieces, 3 * EXPERTS_PER_GROUP, 1)) * PIECE
    pos = jnp.sum(flag * (start + rank), axis=-1, keepdims=True)
    pos_ref[...] = pos
    cnt_ref[...] = count.astype(jnp.int32)
    pos_row = jnp.transpose(jnp.broadcast_to(pos, (tm, LANES)))[0:1, :].astype(jnp.int32)
    place = lax.broadcasted_iota(jnp.int32, (LS_ROWS, tm), 0) == pos_row
    place = jnp.where(place, 1.0, 0.0).astype(BF16)
    wide = jnp.concatenate([h2_hi] + _split_bf16(comb, COMB_TERMS), axis=1)
    h2s_ref[...] = _dot(place, wide).astype(BF16)


def _post_mixer(attn_ctx, attn_lat, gm, x_ctx, x_lat, mod, oag, wout, g2, wr, rb):
    tm = POST_SUB * TM_POST
    ls = POST_SUB * LS_ROWS
    row = lambda i: (i, 0)
    return pl.pallas_call(
        _post_kernel,
        out_shape=(
            jax.ShapeDtypeStruct((N_ROWS, D_MODEL), F32),
            jax.ShapeDtypeStruct((N_LS, XS_COLS), BF16),
            jax.ShapeDtypeStruct((N_LS, D_MODEL), BF16),
            jax.ShapeDtypeStruct((N_ROWS, 1), F32),
            jax.ShapeDtypeStruct((N_POST_TILES, 1, LANES), jnp.int32),
        ),
        grid=(N_ROWS // tm,),
        in_specs=_ctx_lat_specs(tm, ATTN_HEADS * V_DIM) + [
            pl.BlockSpec((tm, GM_WIDTH), row),
        ] + _ctx_lat_specs(tm, D_MODEL) + [
            pl.BlockSpec((1, 1, 6 * D_MODEL), lambda i: (_cond_row(i, tm), 0, 0)),
            _full((1, ATTN_HEADS * V_DIM)),
            _full((D_MODEL, D_MODEL)),
            _full((1, D_MODEL)),
            _full((D_MODEL, 2 * LANES)),
            _full((1, LANES)),
        ],
        out_specs=(
            pl.BlockSpec((tm, D_MODEL), row),
            pl.BlockSpec((ls, XS_COLS), row),
            pl.BlockSpec((ls, D_MODEL), row),
            pl.BlockSpec((tm, 1), row),
            pl.BlockSpec((POST_SUB, 1, LANES), lambda i: (i, 0, 0)),
        ),
        compiler_params=pltpu.CompilerParams(
            dimension_semantics=("arbitrary",), vmem_limit_bytes=VMEM_LIMIT),
        name="post_mixer",
    )(attn_ctx, attn_lat, gm, x_ctx, x_lat, mod, oag, wout, g2, wr, rb)


def _plan_kernel(cnt_ref, tg_ref, tn_ref, src_ref, nu_ref):
    def clear_src(j, c):
        src_ref[j] = 0
        return c

    lax.fori_loop(0, N_ETILES * E_PIECES, clear_src, 0)

    def clear_tile(j, c):
        tg_ref[j] = N_GROUPS - 1
        tn_ref[j] = 0
        return c

    lax.fori_loop(0, N_ETILES, clear_tile, 0)

    def n_pieces(i, g):
        return lax.shift_right_logical(cnt_ref[i, g * EXPERTS_PER_GROUP] + (PIECE - 1), PIECE_SHIFT)

    t = jnp.int32(0)
    for g in range(N_GROUPS):
        def tile_body(i, s, g=g):
            first = i * LS_PIECES
            for gp in range(g):
                first = first + n_pieces(i, gp)

            def piece_body(p, s):
                src_ref[s] = first + p
                return s + 1

            return lax.fori_loop(0, n_pieces(i, g), piece_body, s)

        s0 = t * E_PIECES
        s1 = lax.fori_loop(0, N_POST_TILES, tile_body, s0)
        n = s1 - s0
        tiles = lax.shift_right_logical(n + (E_PIECES - 1), E_SHIFT)

        def mark_tile(u, c, g=g, n=n, t=t):
            tg_ref[t + u] = g
            tn_ref[t + u] = jnp.minimum(n - u * E_PIECES, E_PIECES)
            return c

        lax.fori_loop(0, tiles, mark_tile, 0)
        t = t + tiles
    nu_ref[0] = t


def _plan(counts):
    smem = pl.BlockSpec(memory_space=pltpu.SMEM)
    return pl.pallas_call(
        _plan_kernel,
        out_shape=(
            jax.ShapeDtypeStruct((N_ETILES,), jnp.int32),
            jax.ShapeDtypeStruct((N_ETILES,), jnp.int32),
            jax.ShapeDtypeStruct((N_ETILES * E_PIECES,), jnp.int32),
            jax.ShapeDtypeStruct((1,), jnp.int32),
        ),
        in_specs=[smem],
        out_specs=(smem, smem, smem, smem),
        name="expert_plan",
    )(counts)


GATHER_X, SCATTER_Y = 0, 1


def _moe_kernel(tg_ref, tn_ref, src_ref, nu_ref, h2s_hbm, y0_hbm, wg_ref, wu_ref, wd_ref,
                wsg_ref, wsu_ref, wsd_ref, y_hbm, xbuf, ybuf, sem):
    del y0_hbm
    j = pl.program_id(0)
    n_used = nu_ref[0]
    slot = lax.rem(j, 2)

    def piece_rows(t, k):
        hbm_rows = pl.ds(pl.multiple_of(src_ref[t * E_PIECES + k] * PIECE, PIECE), PIECE)
        buf_rows = pl.ds(pl.multiple_of(k * PIECE, PIECE), PIECE)
        return hbm_rows, buf_rows

    def gather_copies(t, k, slot):
        hbm_rows, buf_rows = piece_rows(t, k)
        return (
            pltpu.make_async_copy(h2s_hbm.at[hbm_rows], xbuf.at[slot, buf_rows], sem.at[GATHER_X, slot]),
        )

    def scatter_copies(t, k, slot):
        hbm_rows, buf_rows = piece_rows(t, k)
        return (
            pltpu.make_async_copy(ybuf.at[slot, buf_rows], y_hbm.at[hbm_rows], sem.at[SCATTER_Y, slot]),
        )

    def for_pieces(t, slot, copies, action):
        def body(k, c):
            for cp in copies(t, k, slot):
                action(cp)
            return c
        lax.fori_loop(0, tn_ref[t], body, 0)

    start = lambda cp: cp.start()
    wait = lambda cp: cp.wait()

    @pl.when(j == 0)
    def _():
        xbuf[...] = jnp.zeros(xbuf.shape, xbuf.dtype)
        for_pieces(0, 0, gather_copies, start)

    @pl.when(j + 1 < n_used)
    def _():
        for_pieces(j + 1, 1 - slot, gather_copies, start)

    @pl.when(j < n_used)
    def _():
        for_pieces(j, slot, gather_copies, wait)
        x = xbuf[slot, :, 0:D_MODEL]
        comb = sum(xbuf[slot, :, D_MODEL + t * LANES:D_MODEL + (t + 1) * LANES].astype(F32)
                   for t in range(COMB_TERMS))
        lane = lax.broadcasted_iota(jnp.int32, (TM_E, LANES), 1)
        first = tg_ref[j] * EXPERTS_PER_GROUP
        y = _dot((_silu(_dot(x, wsg_ref[...])) * _dot(x, wsu_ref[...])).astype(BF16), wsd_ref[...])
        for k in range(EXPERTS_PER_GROUP):
            wk = jnp.sum(jnp.where(lane == first + k, comb, 0.0), axis=-1, keepdims=True)
            act = _silu(_dot(x, wg_ref[k])) * _dot(x, wu_ref[k]) * wk
            y = y + _dot(act.astype(BF16), wd_ref[k])
        ybuf[slot] = y.astype(BF16)
        for_pieces(j, slot, scatter_copies, start)

    @pl.when((j >= 1) & (j < n_used))
    def _():
        for_pieces(j - 1, 1 - slot, scatter_copies, wait)

    @pl.when(j == n_used - 1)
    def _():
        for_pieces(j, slot, scatter_copies, wait)


def _experts(plan, h2s, y0, wg, wu, wd, wsg, wsu, wsd):
    tg, tn, src, nu = plan
    any_spec = pl.BlockSpec(memory_space=pl.ANY)
    group = lambda j, tg, tn, src, nu: (tg[j], 0, 0)
    whole = lambda j, tg, tn, src, nu: (0, 0)
    return pl.pallas_call(
        _moe_kernel,
        out_shape=jax.ShapeDtypeStruct((N_LS, D_MODEL), BF16),
        grid_spec=pltpu.PrefetchScalarGridSpec(
            num_scalar_prefetch=4,
            grid=(N_ETILES,),
            in_specs=[
                any_spec, any_spec,
                pl.BlockSpec((EXPERTS_PER_GROUP, D_MODEL, EXPERT_FF), group),
                pl.BlockSpec((EXPERTS_PER_GROUP, D_MODEL, EXPERT_FF), group),
                pl.BlockSpec((EXPERTS_PER_GROUP, EXPERT_FF, D_MODEL), group),
                pl.BlockSpec((D_MODEL, SHARED_FF), whole),
                pl.BlockSpec((D_MODEL, SHARED_FF), whole),
                pl.BlockSpec((SHARED_FF, D_MODEL), whole),
            ],
            out_specs=any_spec,
            scratch_shapes=[
                pltpu.VMEM((2, TM_E, XS_COLS), BF16),
                pltpu.VMEM((2, TM_E, D_MODEL), BF16),
                pltpu.SemaphoreType.DMA((2, 2)),
            ],
        ),
        input_output_aliases={5: 0},
        compiler_params=pltpu.CompilerParams(
            dimension_semantics=("arbitrary",), vmem_limit_bytes=VMEM_LIMIT),
        name="experts",
    )(tg, tn, src, nu, h2s, y0, wg, wu, wd, wsg, wsu, wsd)


def _combine_kernel(y_ref, pos_ref, x1_ref, mod_ref, fg_ref, o_ref, *, final):
    tm = x1_ref.shape[0]
    pick = lax.broadcasted_iota(jnp.int32, (tm, LS_ROWS), 1) == pos_ref[...].astype(jnp.int32)
    moe = _dot(jnp.where(pick, 1.0, 0.0).astype(BF16), y_ref[...])
    gate2 = mod_ref[0][:, 5 * D_MODEL:6 * D_MODEL]
    x2 = x1_ref[...] + gate2 * moe
    if final:
        x2 = _rms(x2) * fg_ref[...]
    o_ref[...] = x2


def _combine(y, pos, x1, mod, fg, final, first_tile, n_tiles):
    tm = TM_POST
    row = lambda i: (first_tile + i, 0)
    return pl.pallas_call(
        functools.partial(_combine_kernel, final=final),
        out_shape=jax.ShapeDtypeStruct((n_tiles * tm, D_MODEL), F32),
        grid=(n_tiles,),
        in_specs=[
            pl.BlockSpec((LS_ROWS, D_MODEL), row),
            pl.BlockSpec((tm, 1), row),
            pl.BlockSpec((tm, D_MODEL), row),
            pl.BlockSpec((1, 1, 6 * D_MODEL), lambda i: (_cond_row(first_tile + i, tm), 0, 0)),
            _full((1, D_MODEL)),
        ],
        out_specs=pl.BlockSpec((tm, D_MODEL), lambda i: (i, 0)),
        compiler_params=pltpu.CompilerParams(
            dimension_semantics=("arbitrary",), vmem_limit_bytes=VMEM_LIMIT),
        name="combine",
    )(y, pos, x1, mod, fg)


def _rope_table():
    rows = DEC_SEQ // GRID_W
    row = jnp.repeat(jnp.arange(rows, dtype=F32), GRID_W)
    col = jnp.tile(jnp.arange(GRID_W, dtype=F32), rows)
    half = QK_ROPE // 2
    freqs = 1.0 / (ROPE_BASE ** (jnp.arange(0, half, 2, dtype=F32) / half))
    ang = jnp.concatenate([row[:, None] * freqs, col[:, None] * freqs], axis=-1)
    cos, sin = jnp.cos(ang), jnp.sin(ang)
    lat = jnp.concatenate([cos, cos, -sin, sin], axis=-1)
    ident = jnp.concatenate([jnp.ones((TM_PRE, QK_ROPE), F32), jnp.zeros((TM_PRE, QK_ROPE), F32)], axis=-1)
    return jnp.concatenate([ident, lat], axis=0)


_DEINT = np.concatenate([np.arange(0, QK_ROPE, 2), np.arange(1, QK_ROPE, 2)])
_SWAP = np.concatenate([np.arange(1, QK_ROPE, 2), np.arange(0, QK_ROPE, 2)])
_INTERLEAVE = np.argsort(_DEINT)


def _layout_w_in(w_in):
    cq_ckv = w_in[:, :Q_LORA + KV_LORA]
    kr = w_in[:, Q_LORA + KV_LORA:Q_LORA + KV_LORA + QK_ROPE]
    uv = w_in[:, Q_LORA + KV_LORA + QK_ROPE:]
    return jnp.concatenate([cq_ckv, kr[:, _DEINT], kr[:, _SWAP], uv], axis=1).astype(BF16)


def _layout_w_qb(w_qb):
    w = (w_qb * (ATTN_SCALE * LOG2E)).reshape(Q_LORA, ATTN_HEADS, QK_NOPE + QK_ROPE)
    nope, rope = w[..., :QK_NOPE], w[..., QK_NOPE:]
    w = jnp.concatenate([nope, rope[..., _DEINT], rope[..., _SWAP]], axis=-1)
    return w.reshape(Q_LORA, ATTN_HEADS * HEAD_PAD).astype(BF16)


def _layout_w_kvb(w_kvb):
    w = w_kvb.reshape(KV_LORA, ATTN_HEADS, QK_NOPE + V_DIM)
    k = w[..., :QK_NOPE].reshape(KV_LORA, ATTN_HEADS * QK_NOPE)
    v = w[..., QK_NOPE:].reshape(KV_LORA, ATTN_HEADS * V_DIM)
    return jnp.concatenate([k, v], axis=1).astype(BF16)


def kernel(x_prompt, x_sample, cache_ckv, cache_krope, c, c_ctx, norm1_g, w_ada, b_ada, w_in,
           q_norm_g, w_qb, kv_norm_g, w_kvb, gm_norm_g, w_spatial, b_spatial, onorm_attn_g,
           onorm_gm_g, w_out, norm2_g, w_router, router_bias, w_gate, w_up, w_down, ws_gate,
           ws_up, ws_down, final_norm_g):
    x_ctx = x_prompt.reshape(N_CTX, D_MODEL)
    x_lat = x_sample.reshape(N_LAT, D_MODEL)
    cond = jnp.concatenate([c_ctx[None, :], c, jnp.zeros((COND_PAD - N_COND, D_MODEL), F32)], axis=0)
    mod = _modulation(cond, w_ada, b_ada)
    rope_tab = _rope_table()

    wkvb = jnp.stack([_layout_w_kvb(w_kvb[l]) for l in range(DEPTH)])
    cache_kr = jnp.pad(cache_krope[..., _DEINT], ((0, 0), (0, 0), (0, 0), (0, LANES - QK_ROPE)))
    kc, vct = _cache_kv(cache_ckv, cache_kr, wkvb)

    wr = jnp.pad(w_router, ((0, 0), (0, LANES - N_EXPERTS)))
    wr_hi = wr.astype(BF16)
    wr = jnp.concatenate([wr_hi, (wr - wr_hi.astype(F32)).astype(BF16)], axis=1)
    rb = jnp.pad(router_bias, (0, LANES - N_EXPERTS)).reshape(1, LANES)
    fg = final_norm_g.reshape(1, D_MODEL)

    ckv_out, kr_out = [], []
    for l in range(DEPTH):
        mod_l = mod[l].reshape(COND_PAD, 1, 6 * D_MODEL)
        bs = jnp.broadcast_to(b_spatial[l].T[:, :, None], (CHUNK, GM_GROUPS, GM_GROUP_DIM))
        q, k, vt, gm, ckv_n, kr = _pre_mixer(
            x_ctx, x_lat, mod_l, rope_tab, norm1_g[l].reshape(1, -1), _layout_w_in(w_in[l]),
            q_norm_g[l].reshape(1, -1), _layout_w_qb(w_qb[l]), kv_norm_g[l].reshape(1, -1),
            wkvb[l], gm_norm_g[l].reshape(1, -1), w_spatial[l].astype(BF16),
            bs.reshape(CHUNK, GM_WIDTH), onorm_gm_g[l].reshape(1, -1))
        ckv_out.append(ckv_n[:N_CTX].reshape(BATCH, SEQ, KV_LORA))
        kr_out.append(kr[:N_CTX][:, _INTERLEAVE].reshape(BATCH, SEQ, QK_ROPE))
        x1, h2s, y0, pos, counts = _post_mixer(
            _ctx_attention(q, k, vt), _lat_attention(q, k, vt, kc, vct, l), gm, x_ctx, x_lat, mod_l,
            onorm_attn_g[l].reshape(1, -1), w_out[l].astype(BF16), norm2_g[l].reshape(1, -1), wr, rb)
        plan = _plan(counts.reshape(N_POST_TILES, LANES))
        y = _experts(plan, h2s, y0, w_gate[l].astype(BF16), w_up[l].astype(BF16),
                     w_down[l].astype(BF16), ws_gate[l].astype(BF16), ws_up[l].astype(BF16),
                     ws_down[l].astype(BF16))
        final = l == DEPTH - 1
        n_ctx_tiles = N_CTX // TM_POST
        x_ctx = _combine(y, pos, x1, mod_l, fg, final, 0, n_ctx_tiles)
        x_lat = _combine(y, pos, x1, mod_l, fg, final, n_ctx_tiles, N_POST_TILES - n_ctx_tiles)

    y_prompt = x_ctx.reshape(BATCH, SEQ, D_MODEL)
    y_sample = x_lat.reshape(DEC_BATCH, DEC_SEQ, D_MODEL)
    return y_prompt, y_sample, jnp.stack(ckv_out, axis=1), jnp.stack(kr_out, axis=1)
```

```python
import functools
import math

import jax
import jax.numpy as jnp
import numpy as np
from jax import lax
from jax.experimental import pallas as pl
from jax.experimental.pallas import tpu as pltpu

D_MODEL = 1024
BATCH = 16
SEQ = 256
DEPTH = 2
DEC_BATCH = 4
DEC_SEQ = 4096
PAST_LEN = 256
GRID_W = 64
ATTN_HEADS = 4
QK_NOPE = 128
QK_ROPE = 64
V_DIM = 128
Q_LORA = 384
KV_LORA = 256
ATTN_SCALE = (QK_NOPE + QK_ROPE) ** -0.5
ROPE_BASE = 10000.0
CHUNK = 128
GM_WIDTH = 512
GM_GROUPS = 4
GM_GROUP_DIM = 128
N_EXPERTS = 16
N_GROUPS = 4
EXPERTS_PER_GROUP = 4
EXPERT_FF = 256
SHARED_FF = 256
EPS = 1e-6

N_CTX = BATCH * SEQ
N_LAT = DEC_BATCH * DEC_SEQ
N_ROWS = N_CTX + N_LAT
N_COND = 1 + DEC_BATCH
COND_PAD = 16
HEAD_PAD = 256
IN_COLS = Q_LORA + KV_LORA + 2 * QK_ROPE + 2 * GM_WIDTH
LANES = 128
LOG2E = 1.4426950408889634

TM_PRE = 512
TM_POST = 256
COMB_TERMS = 3
XS_COLS = D_MODEL + COMB_TERMS * LANES
COMBINE_SUB = 4
POST_SUB = 4
PIECE = 16
LS_ROWS = TM_POST + N_GROUPS * PIECE
LS_PIECES = LS_ROWS // PIECE
N_POST_TILES = N_ROWS // TM_POST
N_LS = N_POST_TILES * LS_ROWS
TM_E = 512
E_PIECES = TM_E // PIECE
PIECE_SHIFT = PIECE.bit_length() - 1
E_SHIFT = E_PIECES.bit_length() - 1
N_ETILES = -(-(N_POST_TILES * (TM_POST // PIECE + N_GROUPS - 1)) // E_PIECES) + N_GROUPS
TQ = 512
HEADS_PER_STEP = 4
VMEM_LIMIT = 56 * 1024 * 1024

F32 = jnp.float32
BF16 = jnp.bfloat16


def _rms(x):
    return x * lax.rsqrt(jnp.mean(x * x, axis=-1, keepdims=True) + EPS)


def _gelu(x):
    return 0.5 * x * (1.0 + jnp.tanh(math.sqrt(2.0 / math.pi) * (x + 0.044715 * (x * x * x))))


def _silu(x):
    return x * (1.0 / (1.0 + jnp.exp(-x)))


def _dot(a, b):
    return jnp.dot(a, b, preferred_element_type=F32)


def _dot_nt(a, b):
    return lax.dot_general(a, b, (((1,), (1,)), ((), ())), preferred_element_type=F32)


def _cond_row(i, tm):
    n_ctx_tiles = N_CTX // tm
    per_batch = DEC_SEQ // tm
    return jnp.where(i < n_ctx_tiles, 0, 1 + (i - n_ctx_tiles) // per_batch)


def _rope_block(i, tm):
    n_ctx_tiles = N_CTX // tm
    per_batch = DEC_SEQ // tm
    return jnp.where(i < n_ctx_tiles, 0, 1 + (i - n_ctx_tiles) % per_batch)


def _ctx_lat_specs(tm, width):
    n_ctx_tiles = N_CTX // tm
    return [pl.BlockSpec((tm, width), lambda i: (jnp.minimum(i, n_ctx_tiles - 1), 0)),
            pl.BlockSpec((tm, width), lambda i: (jnp.maximum(i - n_ctx_tiles, 0), 0))]


def _ctx_or_lat(ctx_ref, lat_ref):
    n_ctx_tiles = N_CTX // ctx_ref.shape[0]
    return jnp.where(pl.program_id(0) < n_ctx_tiles, ctx_ref[...], lat_ref[...])


def _full(shape):
    n = len(shape)
    return pl.BlockSpec(shape, lambda *_: (0,) * n)


def _mod_kernel(cond_ref, w_ref, b_ref, o_ref):
    s_hi, s_lo = _split_bf16(_silu(cond_ref[...]), 2)
    w_hi, w_lo = _split_bf16(w_ref[0], 2)
    t = _dot(jnp.concatenate([s_hi, s_lo], axis=0), w_hi)
    o_ref[0] = (t[:COND_PAD] + t[COND_PAD:]) + _dot(s_hi, w_lo) + b_ref[0]


def _modulation(cond, w_ada, b_ada):
    tn = 1536
    return pl.pallas_call(
        _mod_kernel,
        out_shape=jax.ShapeDtypeStruct((DEPTH, COND_PAD, 6 * D_MODEL), F32),
        grid=(DEPTH, 6 * D_MODEL // tn),
        in_specs=[
            pl.BlockSpec((COND_PAD, D_MODEL), lambda l, j: (0, 0)),
            pl.BlockSpec((1, D_MODEL, tn), lambda l, j: (l, 0, j)),
            pl.BlockSpec((1, 1, tn), lambda l, j: (l, 0, j)),
        ],
        out_specs=pl.BlockSpec((1, COND_PAD, tn), lambda l, j: (l, 0, j)),
        compiler_params=pltpu.CompilerParams(
            dimension_semantics=("arbitrary", "arbitrary"), vmem_limit_bytes=VMEM_LIMIT),
        name="modulation",
    )(cond, w_ada, b_ada.reshape(DEPTH, 1, 6 * D_MODEL))


def _pre_kernel(xc_ref, xl_ref, mod_ref, rope_ref, g1_ref, win_ref, qg_ref, wqb_ref, kvg_ref,
                wkvb_ref, gmg_ref, ws_ref, bs_ref, ogm_ref,
                q_ref, k_ref, vt_ref, gm_ref, ckv_ref, kr_ref):
    tm = xc_ref.shape[0]
    mod = mod_ref[0]
    shift1 = mod[:, 0:D_MODEL]
    scale1 = mod[:, D_MODEL:2 * D_MODEL]
    h = _rms(_ctx_or_lat(xc_ref, xl_ref)) * g1_ref[...] * (1.0 + scale1) + shift1
    y = _dot(h.astype(BF16), win_ref[...])
    cq = y[:, 0:Q_LORA]
    ckv = y[:, Q_LORA:Q_LORA + KV_LORA]
    kr2 = y[:, 640:768]
    u = y[:, 768:768 + GM_WIDTH]
    vv = y[:, 768 + GM_WIDTH:768 + 2 * GM_WIDTH]

    rope = rope_ref[...]
    lane = lax.broadcasted_iota(jnp.int32, (tm, LANES), 1)

    kr_ref[...] = kr2[:, 0:QK_ROPE]
    t = kr2 * rope
    k_rot = jnp.where(lane < QK_ROPE, t + pltpu.roll(t, QK_ROPE, 1), 0.0).astype(BF16)

    ckv_n = _rms(ckv) * kvg_ref[...]
    ckv_ref[...] = ckv_n
    kv = _dot(ckv_n.astype(BF16), wkvb_ref[...])
    for hd in range(ATTN_HEADS):
        k_ref[:, hd * HEAD_PAD:hd * HEAD_PAD + QK_NOPE] = (
            kv[:, hd * QK_NOPE:(hd + 1) * QK_NOPE].astype(BF16))
        k_ref[:, hd * HEAD_PAD + QK_NOPE:(hd + 1) * HEAD_PAD] = k_rot
    vt_ref[...] = kv[:, ATTN_HEADS * QK_NOPE:].T.astype(BF16)

    q = _dot((_rms(cq) * qg_ref[...]).astype(BF16), wqb_ref[...])
    for hd in range(ATTN_HEADS):
        q_ref[:, hd * HEAD_PAD:hd * HEAD_PAD + QK_NOPE] = (
            q[:, hd * HEAD_PAD:hd * HEAD_PAD + QK_NOPE].astype(BF16))
        t = q[:, hd * HEAD_PAD + QK_NOPE:(hd + 1) * HEAD_PAD] * rope
        q_ref[:, hd * HEAD_PAD + QK_NOPE:(hd + 1) * HEAD_PAD] = (
            t + pltpu.roll(t, QK_ROPE, 1)).astype(BF16)

    ug = _gelu(u)
    vg = _gelu(vv)
    cols = []
    for g in range(GM_GROUPS):
        sl = slice(g * GM_GROUP_DIM, (g + 1) * GM_GROUP_DIM)
        vn = (_rms(vg[:, sl]) * gmg_ref[:, sl]).astype(BF16)
        rows = []
        for c in range(tm // CHUNK):
            sv = _dot(ws_ref[g], vn[c * CHUNK:(c + 1) * CHUNK]) + bs_ref[:, sl]
            rows.append(ug[c * CHUNK:(c + 1) * CHUNK, sl] * sv)
        cols.append(jnp.concatenate(rows, axis=0))
    gm = jnp.concatenate(cols, axis=1)
    gm_ref[...] = (_rms(gm) * ogm_ref[...]).astype(BF16)


def _pre_mixer(x_ctx, x_lat, mod, rope_tab, g1, win, qg, wqb, kvg, wkvb, gmg, ws, bs, ogm):
    tm = TM_PRE
    row = lambda i: (i, 0)
    return pl.pallas_call(
        _pre_kernel,
        out_shape=(
            jax.ShapeDtypeStruct((N_ROWS, ATTN_HEADS * HEAD_PAD), BF16),
            jax.ShapeDtypeStruct((N_ROWS, ATTN_HEADS * HEAD_PAD), BF16),
            jax.ShapeDtypeStruct((ATTN_HEADS * V_DIM, N_ROWS), BF16),
            jax.ShapeDtypeStruct((N_ROWS, GM_WIDTH), BF16),
            jax.ShapeDtypeStruct((N_ROWS, KV_LORA), F32),
            jax.ShapeDtypeStruct((N_ROWS, QK_ROPE), F32),
        ),
        grid=(N_ROWS // tm,),
        in_specs=_ctx_lat_specs(tm, D_MODEL) + [
            pl.BlockSpec((1, 1, 6 * D_MODEL), lambda i: (_cond_row(i, tm), 0, 0)),
            pl.BlockSpec((tm, LANES), lambda i: (_rope_block(i, tm), 0)),
            _full((1, D_MODEL)),
            _full((D_MODEL, IN_COLS)),
            _full((1, Q_LORA)),
            _full((Q_LORA, ATTN_HEADS * HEAD_PAD)),
            _full((1, KV_LORA)),
            _full((KV_LORA, ATTN_HEADS * (QK_NOPE + V_DIM))),
            _full((1, GM_WIDTH)),
            _full((GM_GROUPS, CHUNK, CHUNK)),
            _full((CHUNK, GM_WIDTH)),
            _full((1, GM_WIDTH)),
        ],
        out_specs=(
            pl.BlockSpec((tm, ATTN_HEADS * HEAD_PAD), row),
            pl.BlockSpec((tm, ATTN_HEADS * HEAD_PAD), row),
            pl.BlockSpec((ATTN_HEADS * V_DIM, tm), lambda i: (0, i)),
            pl.BlockSpec((tm, GM_WIDTH), row),
            pl.BlockSpec((tm, KV_LORA), row),
            pl.BlockSpec((tm, QK_ROPE), row),
        ),
        compiler_params=pltpu.CompilerParams(
            dimension_semantics=("arbitrary",), vmem_limit_bytes=VMEM_LIMIT),
        name="pre_mixer",
    )(x_ctx, x_lat, mod, rope_tab, g1, win, qg, wqb, kvg, wkvb, gmg, ws, bs, ogm)


def _cache_kv_kernel(ckv_ref, kr_ref, wkvb_ref, k_ref, vt_ref):
    kv = _dot(ckv_ref[0, 0].astype(BF16), wkvb_ref[0])
    kr = kr_ref[0, 0].astype(BF16)
    for hd in range(ATTN_HEADS):
        k_ref[0, 0, :, hd * HEAD_PAD:hd * HEAD_PAD + QK_NOPE] = (
            kv[:, hd * QK_NOPE:(hd + 1) * QK_NOPE].astype(BF16))
        k_ref[0, 0, :, hd * HEAD_PAD + QK_NOPE:(hd + 1) * HEAD_PAD] = kr
    vt_ref[0, 0] = kv[:, ATTN_HEADS * QK_NOPE:].T.astype(BF16)


def _cache_kv(cache_ckv, cache_kr_pad, wkvb):
    blk = lambda w: pl.BlockSpec((1, 1, PAST_LEN, w), lambda l, b: (b, l, 0, 0))
    return pl.pallas_call(
        _cache_kv_kernel,
        out_shape=(
            jax.ShapeDtypeStruct((DEC_BATCH, DEPTH, PAST_LEN, ATTN_HEADS * HEAD_PAD), BF16),
            jax.ShapeDtypeStruct((DEC_BATCH, DEPTH, ATTN_HEADS * V_DIM, PAST_LEN), BF16),
        ),
        grid=(DEPTH, DEC_BATCH),
        in_specs=[
            blk(KV_LORA),
            blk(LANES),
            pl.BlockSpec((1, KV_LORA, ATTN_HEADS * (QK_NOPE + V_DIM)), lambda l, b: (l, 0, 0)),
        ],
        out_specs=(blk(ATTN_HEADS * HEAD_PAD),
                   pl.BlockSpec((1, 1, ATTN_HEADS * V_DIM, PAST_LEN), lambda l, b: (b, l, 0, 0))),
        compiler_params=pltpu.CompilerParams(
            dimension_semantics=("arbitrary", "arbitrary"), vmem_limit_bytes=VMEM_LIMIT),
        name="cache_kv",
    )(cache_ckv, cache_kr_pad, wkvb)


def _attn_scores(qh, key_blocks):
    return [_dot_nt(kb, qh) for kb in key_blocks]


def _attn_values(s, vt_blocks):
    m = functools.reduce(jnp.maximum, [jnp.max(si, axis=0, keepdims=True) for si in s])
    p = [jnp.exp2(si - m) for si in s]
    l = sum(jnp.sum(pi, axis=0, keepdims=True) for pi in p)
    o_t = sum(_dot(vt, pi.astype(BF16)) for vt, pi in zip(vt_blocks, p))
    return (o_t / l).T


def _attend_head(qh, key_blocks, vt_blocks):
    return _attn_values(_attn_scores(qh, key_blocks), vt_blocks)


def _ctx_attn_kernel(q_ref, k_ref, vt_ref, o_ref):
    for hd in range(ATTN_HEADS):
        hs = slice(hd * HEAD_PAD, (hd + 1) * HEAD_PAD)
        vs = slice(hd * V_DIM, (hd + 1) * V_DIM)
        o_ref[:, vs] = _attend_head(q_ref[:, hs], [k_ref[:, hs]], [vt_ref[vs, :]]).astype(o_ref.dtype)


def _ctx_attention(q, k, vt):
    blk = lambda w: pl.BlockSpec((SEQ, w), lambda b: (b, 0))
    return pl.pallas_call(
        _ctx_attn_kernel,
        out_shape=jax.ShapeDtypeStruct((N_CTX, ATTN_HEADS * V_DIM), BF16),
        grid=(BATCH,),
        in_specs=[blk(ATTN_HEADS * HEAD_PAD), blk(ATTN_HEADS * HEAD_PAD),
                  pl.BlockSpec((ATTN_HEADS * V_DIM, SEQ), lambda b: (0, b))],
        out_specs=blk(ATTN_HEADS * V_DIM),
        compiler_params=pltpu.CompilerParams(
            dimension_semantics=("arbitrary",), vmem_limit_bytes=VMEM_LIMIT),
        name="ctx_attention",
    )(q, k, vt)


def _lat_attn_kernel(q_ref, k_ref, vt_ref, kc_ref, vct_ref, o_ref):
    hs = [slice(hd * HEAD_PAD, (hd + 1) * HEAD_PAD) for hd in range(HEADS_PER_STEP)]
    vs = [slice(hd * V_DIM, (hd + 1) * V_DIM) for hd in range(HEADS_PER_STEP)]

    def scores(hd):
        return _attn_scores(q_ref[:, hs[hd]], [kc_ref[0, 0, :, hs[hd]], k_ref[:, hs[hd]]])

    s = scores(0)
    for hd in range(HEADS_PER_STEP):
        s_next = scores(hd + 1) if hd + 1 < HEADS_PER_STEP else None
        o_ref[:, vs[hd]] = _attn_values(s, [vct_ref[0, 0, vs[hd], :], vt_ref[vs[hd], :]]).astype(o_ref.dtype)
        s = s_next


def _lat_attention(q, k, vt, kc, vct, layer):
    nq = DEC_SEQ // TQ
    ctx_q_tiles = N_CTX // TQ
    ctx_kv_blocks = N_CTX // DEC_SEQ
    qk_w = HEADS_PER_STEP * HEAD_PAD
    v_w = HEADS_PER_STEP * V_DIM
    return pl.pallas_call(
        _lat_attn_kernel,
        out_shape=jax.ShapeDtypeStruct((N_LAT, ATTN_HEADS * V_DIM), BF16),
        grid=(DEC_BATCH, ATTN_HEADS // HEADS_PER_STEP, nq),
        in_specs=[
            pl.BlockSpec((TQ, qk_w), lambda b, h, i: (ctx_q_tiles + b * nq + i, h)),
            pl.BlockSpec((DEC_SEQ, qk_w), lambda b, h, i: (ctx_kv_blocks + b, h)),
            pl.BlockSpec((v_w, DEC_SEQ), lambda b, h, i: (h, ctx_kv_blocks + b)),
            pl.BlockSpec((1, 1, PAST_LEN, qk_w), lambda b, h, i: (b, layer, 0, h)),
            pl.BlockSpec((1, 1, v_w, PAST_LEN), lambda b, h, i: (b, layer, h, 0)),
        ],
        out_specs=pl.BlockSpec((TQ, v_w), lambda b, h, i: (b * nq + i, h)),
        compiler_params=pltpu.CompilerParams(
            dimension_semantics=("arbitrary", "arbitrary", "arbitrary"),
            vmem_limit_bytes=VMEM_LIMIT),
        name="lat_attention",
    )(q, k, vt, kc, vct)


def _group_peer(x, lane, d, width, period):
    step = d * width
    ahead = pltpu.roll(x, LANES - step, 1)
    behind = pltpu.roll(x, period - step, 1)
    wraps = (lane & (period - 1)) + step >= period
    return jnp.where(wraps, behind, ahead), wraps


def _route(logits, bias, lane):
    valid = lane < N_EXPERTS
    s = 1.0 / (1.0 + jnp.exp(-logits))
    sb = s + bias
    rank = jnp.zeros(sb.shape, jnp.int32)
    for d in range(1, EXPERTS_PER_GROUP):
        o, wraps = _group_peer(sb, lane, d, 1, EXPERTS_PER_GROUP)
        beats = (o > sb) | ((o == sb) & wraps)
        rank = rank + beats.astype(jnp.int32)
    top2 = rank < 2
    t = jnp.where(top2, sb, 0.0)
    gscore = t
    for d in range(1, EXPERTS_PER_GROUP):
        gscore = gscore + _group_peer(t, lane, d, 1, EXPERTS_PER_GROUP)[0]
    grank = jnp.zeros(sb.shape, jnp.int32)
    for d in range(1, N_GROUPS):
        o, wraps = _group_peer(gscore, lane, d, EXPERTS_PER_GROUP, N_EXPERTS)
        beats = (o > gscore) | ((o == gscore) & wraps)
        grank = grank + beats.astype(jnp.int32)
    in_group = (grank == 0) & valid
    w = jnp.where(top2 & in_group, s, 0.0)
    denom = jnp.sum(w, axis=-1, keepdims=True)
    group_flag = jnp.where(in_group & ((lane & (EXPERTS_PER_GROUP - 1)) == 0), 1.0, 0.0)
    return w / denom, group_flag


def _split_bf16(x, terms):
    out = []
    for _ in range(terms - 1):
        t = x.astype(BF16)
        out.append(t)
        x = x - t.astype(F32)
    out.append(x.astype(BF16))
    return out


def _post_kernel(ac_ref, al_ref, gm_ref, xc_ref, xl_ref, mod_ref, oag_ref, wout_ref, g2_ref,
                 wr_ref, rb_ref, x1_ref, h2s_ref, y0_ref, pos_ref, cnt_ref):
    mod = mod_ref[0]
    gate1 = mod[:, 2 * D_MODEL:3 * D_MODEL]
    shift2 = mod[:, 3 * D_MODEL:4 * D_MODEL]
    scale2 = mod[:, 4 * D_MODEL:5 * D_MODEL]
    an = (_rms(_ctx_or_lat(ac_ref, al_ref).astype(F32)) * oag_ref[...]).astype(BF16)
    mixed = _dot(jnp.concatenate([an, gm_ref[...]], axis=1), wout_ref[...])
    x1 = _ctx_or_lat(xc_ref, xl_ref) + gate1 * mixed
    x1_ref[...] = x1
    h2 = _rms(x1) * g2_ref[...] * (1.0 + scale2) + shift2
    y0_ref[...] = jnp.zeros(y0_ref.shape, y0_ref.dtype)
    for sub in range(POST_SUB):
        rows = slice(sub * TM_POST, (sub + 1) * TM_POST)
        ls_rows = slice(sub * LS_ROWS, (sub + 1) * LS_ROWS)
        _route_and_sort(h2[rows], wr_ref, rb_ref, h2s_ref.at[ls_rows], pos_ref.at[rows],
                        cnt_ref.at[sub])


def _route_and_sort(h2, wr_ref, rb_ref, h2s_ref, pos_ref, cnt_ref):
    tm = TM_POST
    h2_hi, h2_lo = _split_bf16(h2, 2)
    t = _dot(jnp.concatenate([h2_hi, h2_lo], axis=0), wr_ref[...])
    logits = (t[:tm, :LANES] + t[tm:, :LANES]) + (t[:tm, LANES:] + t[tm:, LANES:])
    lane = lax.broadcasted_iota(jnp.int32, (tm, LANES), 1)
    comb, flag = _route(logits, rb_ref[...], lane)

    r_i = lax.broadcasted_iota(jnp.int32, (tm, tm), 0)
    c_i = lax.broadcasted_iota(jnp.int32, (tm, tm), 1)
    before = jnp.where(c_i < r_i, 1.0, 0.0).astype(BF16)
    rank = _dot(before, flag.astype(BF16))
    count = jnp.sum(flag, axis=0, keepdims=True)
    pieces = jnp.floor((count + (PIECE - 1)) * (1.0 / PIECE))
    start = (pltpu.roll(pieces, EXPERTS_PER_GROUP, 1) + pltpu.roll(pieces, 2 * EXPERTS_PER_GROUP, 1)
             + pltpu.roll(pieces, 3 * EXPERTS_PER_GROUP, 1)) * PIECE
    pos = jnp.sum(flag * (start + rank), axis=-1, keepdims=True)
    pos_ref[...] = pos
    cnt_ref[...] = count.astype(jnp.int32)
    pos_row = jnp.transpose(jnp.broadcast_to(pos, (tm, LANES)))[0:1, :].astype(jnp.int32)
    place = lax.broadcasted_iota(jnp.int32, (LS_ROWS, tm), 0) == pos_row
    place = jnp.where(place, 1.0, 0.0).astype(BF16)
    wide = jnp.concatenate([h2_hi] + _split_bf16(comb, COMB_TERMS), axis=1)
    h2s_ref[...] = _dot(place, wide).astype(BF16)


def _post_mixer(attn_ctx, attn_lat, gm, x_ctx, x_lat, mod, oag, wout, g2, wr, rb):
    tm = POST_SUB * TM_POST
    ls = POST_SUB * LS_ROWS
    row = lambda i: (i, 0)
    return pl.pallas_call(
        _post_kernel,
        out_shape=(
            jax.ShapeDtypeStruct((N_ROWS, D_MODEL), F32),
            jax.ShapeDtypeStruct((N_LS, XS_COLS), BF16),
            jax.ShapeDtypeStruct((N_LS, D_MODEL), BF16),
            jax.ShapeDtypeStruct((N_ROWS, 1), F32),
            jax.ShapeDtypeStruct((N_POST_TILES, 1, LANES), jnp.int32),
        ),
        grid=(N_ROWS // tm,),
        in_specs=_ctx_lat_specs(tm, ATTN_HEADS * V_DIM) + [
            pl.BlockSpec((tm, GM_WIDTH), row),
        ] + _ctx_lat_specs(tm, D_MODEL) + [
            pl.BlockSpec((1, 1, 6 * D_MODEL), lambda i: (_cond_row(i, tm), 0, 0)),
            _full((1, ATTN_HEADS * V_DIM)),
            _full((D_MODEL, D_MODEL)),
            _full((1, D_MODEL)),
            _full((D_MODEL, 2 * LANES)),
            _full((1, LANES)),
        ],
        out_specs=(
            pl.BlockSpec((tm, D_MODEL), row),
            pl.BlockSpec((ls, XS_COLS), row),
            pl.BlockSpec((ls, D_MODEL), row),
            pl.BlockSpec((tm, 1), row),
            pl.BlockSpec((POST_SUB, 1, LANES), lambda i: (i, 0, 0)),
        ),
        compiler_params=pltpu.CompilerParams(
            dimension_semantics=("arbitrary",), vmem_limit_bytes=VMEM_LIMIT),
        name="post_mixer",
    )(attn_ctx, attn_lat, gm, x_ctx, x_lat, mod, oag, wout, g2, wr, rb)


def _plan_kernel(cnt_ref, tg_ref, tn_ref, src_ref, nu_ref):
    def clear_src(j, c):
        src_ref[j] = 0
        return c

    lax.fori_loop(0, N_ETILES * E_PIECES, clear_src, 0)

    def clear_tile(j, c):
        tg_ref[j] = N_GROUPS - 1
        tn_ref[j] = 0
        return c

    lax.fori_loop(0, N_ETILES, clear_tile, 0)

    def n_pieces(i, g):
        return lax.shift_right_logical(cnt_ref[i, g * EXPERTS_PER_GROUP] + (PIECE - 1), PIECE_SHIFT)

    t = jnp.int32(0)
    for g in range(N_GROUPS):
        def tile_body(i, s, g=g):
            first = i * LS_PIECES
            for gp in range(g):
                first = first + n_pieces(i, gp)

            def piece_body(p, s):
                src_ref[s] = first + p
                return s + 1

            return lax.fori_loop(0, n_pieces(i, g), piece_body, s)

        s0 = t * E_PIECES
        s1 = lax.fori_loop(0, N_POST_TILES, tile_body, s0)
        n = s1 - s0
        tiles = lax.shift_right_logical(n + (E_PIECES - 1), E_SHIFT)

        def mark_tile(u, c, g=g, n=n, t=t):
            tg_ref[t + u] = g
            tn_ref[t + u] = jnp.minimum(n - u * E_PIECES, E_PIECES)
            return c

        lax.fori_loop(0, tiles, mark_tile, 0)
        t = t + tiles
    nu_ref[0] = t


def _plan(counts):
    smem = pl.BlockSpec(memory_space=pltpu.SMEM)
    return pl.pallas_call(
        _plan_kernel,
        out_shape=(
            jax.ShapeDtypeStruct((N_ETILES,), jnp.int32),
            jax.ShapeDtypeStruct((N_ETILES,), jnp.int32),
            jax.ShapeDtypeStruct((N_ETILES * E_PIECES,), jnp.int32),
            jax.ShapeDtypeStruct((1,), jnp.int32),
        ),
        in_specs=[smem],
        out_specs=(smem, smem, smem, smem),
        name="expert_plan",
    )(counts)


GATHER_X, SCATTER_Y = 0, 1


def _moe_kernel(tg_ref, tn_ref, src_ref, nu_ref, h2s_hbm, y0_hbm, wg32_ref, wu32_ref, wd32_ref,
                wsg32_ref, wsu32_ref, wsd32_ref, y_hbm, xbuf, ybuf, wg_ref, wu_ref, wd_ref,
                wsg_ref, wsu_ref, wsd_ref, sem):
    del y0_hbm
    j = pl.program_id(0)
    n_used = nu_ref[0]
    slot = lax.rem(j, 2)

    @pl.when(j == 0)
    def _():
        wsg_ref[...] = wsg32_ref[0].astype(BF16)
        wsu_ref[...] = wsu32_ref[0].astype(BF16)
        wsd_ref[...] = wsd32_ref[0].astype(BF16)

    @pl.when((j == 0) | (tg_ref[j] != tg_ref[jnp.maximum(j - 1, 0)]))
    def _():
        for k in range(EXPERTS_PER_GROUP):
            wg_ref[k] = wg32_ref[0, k].astype(BF16)
            wu_ref[k] = wu32_ref[0, k].astype(BF16)
            wd_ref[k] = wd32_ref[0, k].astype(BF16)

    def piece_rows(t, k):
        hbm_rows = pl.ds(pl.multiple_of(src_ref[t * E_PIECES + k] * PIECE, PIECE), PIECE)
        buf_rows = pl.ds(pl.multiple_of(k * PIECE, PIECE), PIECE)
        return hbm_rows, buf_rows

    def gather_copies(t, k, slot):
        hbm_rows, buf_rows = piece_rows(t, k)
        return (
            pltpu.make_async_copy(h2s_hbm.at[hbm_rows], xbuf.at[slot, buf_rows], sem.at[GATHER_X, slot]),
        )

    def scatter_copies(t, k, slot):
        hbm_rows, buf_rows = piece_rows(t, k)
        return (
            pltpu.make_async_copy(ybuf.at[slot, buf_rows], y_hbm.at[hbm_rows], sem.at[SCATTER_Y, slot]),
        )

    def for_pieces(t, slot, copies, action):
        def body(k, c):
            for cp in copies(t, k, slot):
                action(cp)
            return c
        lax.fori_loop(0, tn_ref[t], body, 0)

    start = lambda cp: cp.start()
    wait = lambda cp: cp.wait()

    @pl.when(j == 0)
    def _():
        xbuf[...] = jnp.zeros(xbuf.shape, xbuf.dtype)
        for_pieces(0, 0, gather_copies, start)

    @pl.when(j + 1 < n_used)
    def _():
        for_pieces(j + 1, 1 - slot, gather_copies, start)

    @pl.when(j < n_used)
    def _():
        for_pieces(j, slot, gather_copies, wait)
        x = xbuf[slot, :, 0:D_MODEL]
        comb = sum(xbuf[slot, :, D_MODEL + t * LANES:D_MODEL + (t + 1) * LANES].astype(F32)
                   for t in range(COMB_TERMS))
        lane = lax.broadcasted_iota(jnp.int32, (TM_E, LANES), 1)
        first = tg_ref[j] * EXPERTS_PER_GROUP
        y = _dot((_silu(_dot(x, wsg_ref[...])) * _dot(x, wsu_ref[...])).astype(BF16), wsd_ref[...])
        for k in range(EXPERTS_PER_GROUP):
            wk = jnp.sum(jnp.where(lane == first + k, comb, 0.0), axis=-1, keepdims=True)
            act = _silu(_dot(x, wg_ref[k])) * _dot(x, wu_ref[k]) * wk
            y = y + _dot(act.astype(BF16), wd_ref[k])
        ybuf[slot] = y.astype(BF16)
        for_pieces(j, slot, scatter_copies, start)

    @pl.when((j >= 1) & (j < n_used))
    def _():
        for_pieces(j - 1, 1 - slot, scatter_copies, wait)

    @pl.when(j == n_used - 1)
    def _():
        for_pieces(j, slot, scatter_copies, wait)


def _experts(plan, h2s, y0, layer, w_gate, w_up, w_down, ws_gate, ws_up, ws_down):
    tg, tn, src, nu = plan
    any_spec = pl.BlockSpec(memory_space=pl.ANY)
    group = lambda j, tg, tn, src, nu: (layer, tg[j], 0, 0)
    whole = lambda j, tg, tn, src, nu: (layer, 0, 0)
    g = EXPERTS_PER_GROUP
    return pl.pallas_call(
        _moe_kernel,
        out_shape=jax.ShapeDtypeStruct((N_LS, D_MODEL), BF16),
        grid_spec=pltpu.PrefetchScalarGridSpec(
            num_scalar_prefetch=4,
            grid=(N_ETILES,),
            in_specs=[
                any_spec, any_spec,
                pl.BlockSpec((1, g, D_MODEL, EXPERT_FF), group),
                pl.BlockSpec((1, g, D_MODEL, EXPERT_FF), group),
                pl.BlockSpec((1, g, EXPERT_FF, D_MODEL), group),
                pl.BlockSpec((1, D_MODEL, SHARED_FF), whole),
                pl.BlockSpec((1, D_MODEL, SHARED_FF), whole),
                pl.BlockSpec((1, SHARED_FF, D_MODEL), whole),
            ],
            out_specs=any_spec,
            scratch_shapes=[
                pltpu.VMEM((2, TM_E, XS_COLS), BF16),
                pltpu.VMEM((2, TM_E, D_MODEL), BF16),
                pltpu.VMEM((g, D_MODEL, EXPERT_FF), BF16),
                pltpu.VMEM((g, D_MODEL, EXPERT_FF), BF16),
                pltpu.VMEM((g, EXPERT_FF, D_MODEL), BF16),
                pltpu.VMEM((D_MODEL, SHARED_FF), BF16),
                pltpu.VMEM((D_MODEL, SHARED_FF), BF16),
                pltpu.VMEM((SHARED_FF, D_MODEL), BF16),
                pltpu.SemaphoreType.DMA((2, 2)),
            ],
        ),
        input_output_aliases={5: 0},
        compiler_params=pltpu.CompilerParams(
            dimension_semantics=("arbitrary",), vmem_limit_bytes=VMEM_LIMIT),
        name="experts",
    )(tg, tn, src, nu, h2s, y0, w_gate, w_up, w_down, ws_gate, ws_up, ws_down)


def _unsort(y_ref, pos_ref, sub):
    pos = pos_ref[sub * TM_POST:(sub + 1) * TM_POST, :].astype(jnp.int32)
    pick = lax.broadcasted_iota(jnp.int32, (TM_POST, LS_ROWS), 1) == pos
    return _dot(jnp.where(pick, 1.0, 0.0).astype(BF16), y_ref[sub * LS_ROWS:(sub + 1) * LS_ROWS, :])


def _combine_kernel(y_ref, pos_ref, x1_ref, mod_ref, fg_ref, o_ref, *, final):
    gate2 = mod_ref[0][:, 5 * D_MODEL:6 * D_MODEL]
    for sub in range(COMBINE_SUB):
        rows = slice(sub * TM_POST, (sub + 1) * TM_POST)
        x2 = x1_ref[rows, :] + gate2 * _unsort(y_ref, pos_ref, sub)
        if final:
            x2 = _rms(x2) * fg_ref[...]
        o_ref[rows, :] = x2


def _combine(y, pos, x1, mod, fg, final, first_tile, n_tiles):
    tm = COMBINE_SUB * TM_POST
    row = lambda i: (first_tile + i, 0)
    return pl.pallas_call(
        functools.partial(_combine_kernel, final=final),
        out_shape=jax.ShapeDtypeStruct((n_tiles * tm, D_MODEL), F32),
        grid=(n_tiles,),
        in_specs=[
            pl.BlockSpec((COMBINE_SUB * LS_ROWS, D_MODEL), row),
            pl.BlockSpec((tm, 1), row),
            pl.BlockSpec((tm, D_MODEL), row),
            pl.BlockSpec((1, 1, 6 * D_MODEL), lambda i: (_cond_row(first_tile + i, tm), 0, 0)),
            _full((1, D_MODEL)),
        ],
        out_specs=pl.BlockSpec((tm, D_MODEL), lambda i: (i, 0)),
        compiler_params=pltpu.CompilerParams(
            dimension_semantics=("arbitrary",), vmem_limit_bytes=VMEM_LIMIT),
        name="combine",
    )(y, pos, x1, mod, fg)


def _rope_table():
    rows = DEC_SEQ // GRID_W
    row = jnp.repeat(jnp.arange(rows, dtype=F32), GRID_W)
    col = jnp.tile(jnp.arange(GRID_W, dtype=F32), rows)
    half = QK_ROPE // 2
    freqs = 1.0 / (ROPE_BASE ** (jnp.arange(0, half, 2, dtype=F32) / half))
    ang = jnp.concatenate([row[:, None] * freqs, col[:, None] * freqs], axis=-1)
    cos, sin = jnp.cos(ang), jnp.sin(ang)
    lat = jnp.concatenate([cos, cos, -sin, sin], axis=-1)
    ident = jnp.concatenate([jnp.ones((TM_PRE, QK_ROPE), F32), jnp.zeros((TM_PRE, QK_ROPE), F32)], axis=-1)
    return jnp.concatenate([ident, lat], axis=0)


_DEINT = np.concatenate([np.arange(0, QK_ROPE, 2), np.arange(1, QK_ROPE, 2)])
_SWAP = np.concatenate([np.arange(1, QK_ROPE, 2), np.arange(0, QK_ROPE, 2)])
_INTERLEAVE = np.argsort(_DEINT)


def _layout_w_in(w_in):
    cq_ckv = w_in[:, :Q_LORA + KV_LORA]
    kr = w_in[:, Q_LORA + KV_LORA:Q_LORA + KV_LORA + QK_ROPE]
    uv = w_in[:, Q_LORA + KV_LORA + QK_ROPE:]
    return jnp.concatenate([cq_ckv, kr[:, _DEINT], kr[:, _SWAP], uv], axis=1).astype(BF16)


def _layout_w_qb(w_qb):
    w = (w_qb * (ATTN_SCALE * LOG2E)).reshape(Q_LORA, ATTN_HEADS, QK_NOPE + QK_ROPE)
    nope, rope = w[..., :QK_NOPE], w[..., QK_NOPE:]
    w = jnp.concatenate([nope, rope[..., _DEINT], rope[..., _SWAP]], axis=-1)
    return w.reshape(Q_LORA, ATTN_HEADS * HEAD_PAD).astype(BF16)


def _layout_w_kvb(w_kvb):
    w = w_kvb.reshape(KV_LORA, ATTN_HEADS, QK_NOPE + V_DIM)
    k = w[..., :QK_NOPE].reshape(KV_LORA, ATTN_HEADS * QK_NOPE)
    v = w[..., QK_NOPE:].reshape(KV_LORA, ATTN_HEADS * V_DIM)
    return jnp.concatenate([k, v], axis=1).astype(BF16)


def kernel(x_prompt, x_sample, cache_ckv, cache_krope, c, c_ctx, norm1_g, w_ada, b_ada, w_in,
           q_norm_g, w_qb, kv_norm_g, w_kvb, gm_norm_g, w_spatial, b_spatial, onorm_attn_g,
           onorm_gm_g, w_out, norm2_g, w_router, router_bias, w_gate, w_up, w_down, ws_gate,
           ws_up, ws_down, final_norm_g):
    x_ctx = x_prompt.reshape(N_CTX, D_MODEL)
    x_lat = x_sample.reshape(N_LAT, D_MODEL)
    cond = jnp.concatenate([c_ctx[None, :], c, jnp.zeros((COND_PAD - N_COND, D_MODEL), F32)], axis=0)
    mod = _modulation(cond, w_ada, b_ada)
    rope_tab = _rope_table()

    wkvb = jnp.stack([_layout_w_kvb(w_kvb[l]) for l in range(DEPTH)])
    cache_kr = jnp.pad(cache_krope[..., _DEINT], ((0, 0), (0, 0), (0, 0), (0, LANES - QK_ROPE)))
    kc, vct = _cache_kv(cache_ckv, cache_kr, wkvb)

    wr = jnp.pad(w_router, ((0, 0), (0, LANES - N_EXPERTS)))
    wr_hi = wr.astype(BF16)
    wr = jnp.concatenate([wr_hi, (wr - wr_hi.astype(F32)).astype(BF16)], axis=1)
    rb = jnp.pad(router_bias, (0, LANES - N_EXPERTS)).reshape(1, LANES)
    fg = final_norm_g.reshape(1, D_MODEL)

    ckv_out, kr_out = [], []
    for l in range(DEPTH):
        mod_l = mod[l].reshape(COND_PAD, 1, 6 * D_MODEL)
        bs = jnp.broadcast_to(b_spatial[l].T[:, :, None], (CHUNK, GM_GROUPS, GM_GROUP_DIM))
        q, k, vt, gm, ckv_n, kr = _pre_mixer(
            x_ctx, x_lat, mod_l, rope_tab, norm1_g[l].reshape(1, -1), _layout_w_in(w_in[l]),
            q_norm_g[l].reshape(1, -1), _layout_w_qb(w_qb[l]), kv_norm_g[l].reshape(1, -1),
            wkvb[l], gm_norm_g[l].reshape(1, -1), w_spatial[l].astype(BF16),
            bs.reshape(CHUNK, GM_WIDTH), onorm_gm_g[l].reshape(1, -1))
        ckv_out.append(ckv_n[:N_CTX].reshape(BATCH, SEQ, KV_LORA))
        kr_out.append(kr[:N_CTX][:, _INTERLEAVE].reshape(BATCH, SEQ, QK_ROPE))
        x1, h2s, y0, pos, counts = _post_mixer(
            _ctx_attention(q, k, vt), _lat_attention(q, k, vt, kc, vct, l), gm, x_ctx, x_lat, mod_l,
            onorm_attn_g[l].reshape(1, -1), w_out[l].astype(BF16), norm2_g[l].reshape(1, -1), wr, rb)
        plan = _plan(counts.reshape(N_POST_TILES, LANES))
        y = _experts(plan, h2s, y0, l, w_gate, w_up, w_down, ws_gate, ws_up, ws_down)
        final = l == DEPTH - 1
        n_ctx_tiles = N_CTX // (COMBINE_SUB * TM_POST)
        n_lat_tiles = N_LAT // (COMBINE_SUB * TM_POST)
        x_ctx = _combine(y, pos, x1, mod_l, fg, final, 0, n_ctx_tiles)
        x_lat = _combine(y, pos, x1, mod_l, fg, final, n_ctx_tiles, n_lat_tiles)

    y_prompt = x_ctx.reshape(BATCH, SEQ, D_MODEL)
    y_sample = x_lat.reshape(DEC_BATCH, DEC_SEQ, D_MODEL)
    return y_prompt, y_sample, jnp.stack(ckv_out, axis=1), jnp.stack(kr_out, axis=1)
```

```python
import functools
import math

import jax
import jax.numpy as jnp
import numpy as np
from jax import lax
from jax.experimental import pallas as pl
from jax.experimental.pallas import tpu as pltpu

D_MODEL = 1024
BATCH = 16
SEQ = 256
DEPTH = 2
DEC_BATCH = 4
DEC_SEQ = 4096
PAST_LEN = 256
GRID_W = 64
ATTN_HEADS = 4
QK_NOPE = 128
QK_ROPE = 64
V_DIM = 128
Q_LORA = 384
KV_LORA = 256
ATTN_SCALE = (QK_NOPE + QK_ROPE) ** -0.5
ROPE_BASE = 10000.0
CHUNK = 128
GM_WIDTH = 512
GM_GROUPS = 4
GM_GROUP_DIM = 128
N_EXPERTS = 16
N_GROUPS = 4
EXPERTS_PER_GROUP = 4
EXPERT_FF = 256
SHARED_FF = 256
EPS = 1e-6

N_CTX = BATCH * SEQ
N_LAT = DEC_BATCH * DEC_SEQ
N_ROWS = N_CTX + N_LAT
N_COND = 1 + DEC_BATCH
COND_PAD = 16
HEAD_PAD = 256
IN_COLS = Q_LORA + KV_LORA + 2 * QK_ROPE + 2 * GM_WIDTH
LANES = 128
LOG2E = 1.4426950408889634

TM_PRE = 512
TM_POST = 256
COMB_TERMS = 3
XS_COLS = D_MODEL + COMB_TERMS * LANES
COMBINE_SUB = 4
POST_SUB = 4
PIECE = 16
LS_ROWS = TM_POST + N_GROUPS * PIECE
LS_PIECES = LS_ROWS // PIECE
N_POST_TILES = N_ROWS // TM_POST
N_LS = N_POST_TILES * LS_ROWS
TM_E = 512
E_PIECES = TM_E // PIECE
PIECE_SHIFT = PIECE.bit_length() - 1
E_SHIFT = E_PIECES.bit_length() - 1
N_ETILES = -(-(N_POST_TILES * (TM_POST // PIECE + N_GROUPS - 1)) // E_PIECES) + N_GROUPS
TQ = 512
HEADS_PER_STEP = 4
VMEM_LIMIT = 56 * 1024 * 1024

F32 = jnp.float32
BF16 = jnp.bfloat16


def _rms(x):
    return x * lax.rsqrt(jnp.mean(x * x, axis=-1, keepdims=True) + EPS)


def _gelu(x):
    return 0.5 * x * (1.0 + jnp.tanh(math.sqrt(2.0 / math.pi) * (x + 0.044715 * (x * x * x))))


def _silu(x):
    return x * (1.0 / (1.0 + jnp.exp(-x)))


def _dot(a, b):
    return jnp.dot(a, b, preferred_element_type=F32)


def _dot_nt(a, b):
    return lax.dot_general(a, b, (((1,), (1,)), ((), ())), preferred_element_type=F32)


def _cond_row(i, tm):
    n_ctx_tiles = N_CTX // tm
    per_batch = DEC_SEQ // tm
    return jnp.where(i < n_ctx_tiles, 0, 1 + (i - n_ctx_tiles) // per_batch)


def _rope_block(i, tm):
    n_ctx_tiles = N_CTX // tm
    per_batch = DEC_SEQ // tm
    return jnp.where(i < n_ctx_tiles, 0, 1 + (i - n_ctx_tiles) % per_batch)


def _ctx_lat_specs(tm, width, joint=False):
    n_ctx_tiles = N_CTX // tm
    lat_first = n_ctx_tiles if joint else 0
    return [pl.BlockSpec((tm, width), lambda i: (jnp.minimum(i, n_ctx_tiles - 1), 0)),
            pl.BlockSpec((tm, width), lambda i: (lat_first + jnp.maximum(i - n_ctx_tiles, 0), 0))]


def _ctx_or_lat(ctx_ref, lat_ref):
    n_ctx_tiles = N_CTX // ctx_ref.shape[0]
    return jnp.where(pl.program_id(0) < n_ctx_tiles, ctx_ref[...], lat_ref[...])


def _full(shape):
    n = len(shape)
    return pl.BlockSpec(shape, lambda *_: (0,) * n)


def _mod_kernel(cond_ref, w_ref, b_ref, o_ref):
    s_hi, s_lo = _split_bf16(_silu(cond_ref[...]), 2)
    w_hi, w_lo = _split_bf16(w_ref[0], 2)
    t = _dot(jnp.concatenate([s_hi, s_lo], axis=0), w_hi)
    o_ref[0] = (t[:COND_PAD] + t[COND_PAD:]) + _dot(s_hi, w_lo) + b_ref[0]


def _modulation(cond, w_ada, b_ada):
    tn = 1536
    return pl.pallas_call(
        _mod_kernel,
        out_shape=jax.ShapeDtypeStruct((DEPTH, COND_PAD, 6 * D_MODEL), F32),
        grid=(DEPTH, 6 * D_MODEL // tn),
        in_specs=[
            pl.BlockSpec((COND_PAD, D_MODEL), lambda l, j: (0, 0)),
            pl.BlockSpec((1, D_MODEL, tn), lambda l, j: (l, 0, j)),
            pl.BlockSpec((1, 1, tn), lambda l, j: (l, 0, j)),
        ],
        out_specs=pl.BlockSpec((1, COND_PAD, tn), lambda l, j: (l, 0, j)),
        compiler_params=pltpu.CompilerParams(
            dimension_semantics=("arbitrary", "arbitrary"), vmem_limit_bytes=VMEM_LIMIT),
        name="modulation",
    )(cond, w_ada, b_ada.reshape(DEPTH, 1, 6 * D_MODEL))


def _pre_kernel(*refs, after_experts):
    if after_experts:
        (y_ref, pos_ref, x1_ref, prev_mod_ref, mod_ref, rope_ref, g1_ref, win_ref, qg_ref, wqb_ref,
         kvg_ref, wkvb_ref, gmg_ref, ws_ref, bs_ref, ogm_ref,
         q_ref, k_ref, vt_ref, gm_ref, ckv_ref, kr_ref, x_ref) = refs
        tm = x1_ref.shape[0]
        gate2 = prev_mod_ref[0][:, 5 * D_MODEL:6 * D_MODEL]
        x = jnp.concatenate([x1_ref[sub * TM_POST:(sub + 1) * TM_POST, :] + gate2 * _unsort(y_ref, pos_ref, sub)
                             for sub in range(tm // TM_POST)], axis=0)
        x_ref[...] = x
    else:
        (xc_ref, xl_ref, mod_ref, rope_ref, g1_ref, win_ref, qg_ref, wqb_ref,
         kvg_ref, wkvb_ref, gmg_ref, ws_ref, bs_ref, ogm_ref,
         q_ref, k_ref, vt_ref, gm_ref, ckv_ref, kr_ref) = refs
        tm = xc_ref.shape[0]
        x = _ctx_or_lat(xc_ref, xl_ref)
    mod = mod_ref[0]
    shift1 = mod[:, 0:D_MODEL]
    scale1 = mod[:, D_MODEL:2 * D_MODEL]
    h = _rms(x) * g1_ref[...] * (1.0 + scale1) + shift1
    y = _dot(h.astype(BF16), win_ref[...])
    cq = y[:, 0:Q_LORA]
    ckv = y[:, Q_LORA:Q_LORA + KV_LORA]
    kr2 = y[:, 640:768]
    u = y[:, 768:768 + GM_WIDTH]
    vv = y[:, 768 + GM_WIDTH:768 + 2 * GM_WIDTH]

    rope = rope_ref[...]
    lane = lax.broadcasted_iota(jnp.int32, (tm, LANES), 1)

    kr_ref[...] = kr2[:, 0:QK_ROPE]
    t = kr2 * rope
    k_rot = jnp.where(lane < QK_ROPE, t + pltpu.roll(t, QK_ROPE, 1), 0.0).astype(BF16)

    ckv_n = _rms(ckv) * kvg_ref[...]
    ckv_ref[...] = ckv_n
    kv = _dot(ckv_n.astype(BF16), wkvb_ref[...])
    for hd in range(ATTN_HEADS):
        k_ref[:, hd * HEAD_PAD:hd * HEAD_PAD + QK_NOPE] = (
            kv[:, hd * QK_NOPE:(hd + 1) * QK_NOPE].astype(BF16))
        k_ref[:, hd * HEAD_PAD + QK_NOPE:(hd + 1) * HEAD_PAD] = k_rot
    vt_ref[...] = kv[:, ATTN_HEADS * QK_NOPE:].T.astype(BF16)

    q = _dot((_rms(cq) * qg_ref[...]).astype(BF16), wqb_ref[...])
    for hd in range(ATTN_HEADS):
        q_ref[:, hd * HEAD_PAD:hd * HEAD_PAD + QK_NOPE] = (
            q[:, hd * HEAD_PAD:hd * HEAD_PAD + QK_NOPE].astype(BF16))
        t = q[:, hd * HEAD_PAD + QK_NOPE:(hd + 1) * HEAD_PAD] * rope
        q_ref[:, hd * HEAD_PAD + QK_NOPE:(hd + 1) * HEAD_PAD] = (
            t + pltpu.roll(t, QK_ROPE, 1)).astype(BF16)

    ug = _gelu(u)
    vg = _gelu(vv)
    cols = []
    for g in range(GM_GROUPS):
        sl = slice(g * GM_GROUP_DIM, (g + 1) * GM_GROUP_DIM)
        vn = (_rms(vg[:, sl]) * gmg_ref[:, sl]).astype(BF16)
        rows = []
        for c in range(tm // CHUNK):
            sv = _dot(ws_ref[g], vn[c * CHUNK:(c + 1) * CHUNK]) + bs_ref[:, sl]
            rows.append(ug[c * CHUNK:(c + 1) * CHUNK, sl] * sv)
        cols.append(jnp.concatenate(rows, axis=0))
    gm = jnp.concatenate(cols, axis=1)
    gm_ref[...] = (_rms(gm) * ogm_ref[...]).astype(BF16)


def _pre_mixer(layer_input, mod, rope_tab, g1, win, qg, wqb, kvg, wkvb, gmg, ws, bs, ogm):
    tm = TM_PRE
    row = lambda i: (i, 0)
    mod_spec = pl.BlockSpec((1, 1, 6 * D_MODEL), lambda i: (_cond_row(i, tm), 0, 0))
    after_experts = len(layer_input) == 4
    if after_experts:
        input_specs = [pl.BlockSpec((tm // TM_POST * LS_ROWS, D_MODEL), row),
                       pl.BlockSpec((tm, 1), row), pl.BlockSpec((tm, D_MODEL), row), mod_spec]
        extra_shape = (jax.ShapeDtypeStruct((N_ROWS, D_MODEL), F32),)
        extra_spec = (pl.BlockSpec((tm, D_MODEL), row),)
    else:
        input_specs = _ctx_lat_specs(tm, D_MODEL)
        extra_shape = extra_spec = ()
    return pl.pallas_call(
        functools.partial(_pre_kernel, after_experts=after_experts),
        out_shape=(
            jax.ShapeDtypeStruct((N_ROWS, ATTN_HEADS * HEAD_PAD), BF16),
            jax.ShapeDtypeStruct((N_ROWS, ATTN_HEADS * HEAD_PAD), BF16),
            jax.ShapeDtypeStruct((ATTN_HEADS * V_DIM, N_ROWS), BF16),
            jax.ShapeDtypeStruct((N_ROWS, GM_WIDTH), BF16),
            jax.ShapeDtypeStruct((N_ROWS, KV_LORA), F32),
            jax.ShapeDtypeStruct((N_ROWS, QK_ROPE), F32),
        ) + extra_shape,
        grid=(N_ROWS // tm,),
        in_specs=input_specs + [
            mod_spec,
            pl.BlockSpec((tm, LANES), lambda i: (_rope_block(i, tm), 0)),
            _full((1, D_MODEL)),
            _full((D_MODEL, IN_COLS)),
            _full((1, Q_LORA)),
            _full((Q_LORA, ATTN_HEADS * HEAD_PAD)),
            _full((1, KV_LORA)),
            _full((KV_LORA, ATTN_HEADS * (QK_NOPE + V_DIM))),
            _full((1, GM_WIDTH)),
            _full((GM_GROUPS, CHUNK, CHUNK)),
            _full((CHUNK, GM_WIDTH)),
            _full((1, GM_WIDTH)),
        ],
        out_specs=(
            pl.BlockSpec((tm, ATTN_HEADS * HEAD_PAD), row),
            pl.BlockSpec((tm, ATTN_HEADS * HEAD_PAD), row),
            pl.BlockSpec((ATTN_HEADS * V_DIM, tm), lambda i: (0, i)),
            pl.BlockSpec((tm, GM_WIDTH), row),
            pl.BlockSpec((tm, KV_LORA), row),
            pl.BlockSpec((tm, QK_ROPE), row),
        ) + extra_spec,
        compiler_params=pltpu.CompilerParams(
            dimension_semantics=("arbitrary",), vmem_limit_bytes=VMEM_LIMIT),
        name="pre_mixer",
    )(*layer_input, mod, rope_tab, g1, win, qg, wqb, kvg, wkvb, gmg, ws, bs, ogm)


def _cache_kv_kernel(ckv_ref, kr_ref, wkvb_ref, k_ref, vt_ref):
    kv = _dot(ckv_ref[0, 0].astype(BF16), wkvb_ref[0])
    kr = kr_ref[0, 0].astype(BF16)
    for hd in range(ATTN_HEADS):
        k_ref[0, 0, :, hd * HEAD_PAD:hd * HEAD_PAD + QK_NOPE] = (
            kv[:, hd * QK_NOPE:(hd + 1) * QK_NOPE].astype(BF16))
        k_ref[0, 0, :, hd * HEAD_PAD + QK_NOPE:(hd + 1) * HEAD_PAD] = kr
    vt_ref[0, 0] = kv[:, ATTN_HEADS * QK_NOPE:].T.astype(BF16)


def _cache_kv(cache_ckv, cache_kr_pad, wkvb):
    blk = lambda w: pl.BlockSpec((1, 1, PAST_LEN, w), lambda l, b: (b, l, 0, 0))
    return pl.pallas_call(
        _cache_kv_kernel,
        out_shape=(
            jax.ShapeDtypeStruct((DEC_BATCH, DEPTH, PAST_LEN, ATTN_HEADS * HEAD_PAD), BF16),
            jax.ShapeDtypeStruct((DEC_BATCH, DEPTH, ATTN_HEADS * V_DIM, PAST_LEN), BF16),
        ),
        grid=(DEPTH, DEC_BATCH),
        in_specs=[
            blk(KV_LORA),
            blk(LANES),
            pl.BlockSpec((1, KV_LORA, ATTN_HEADS * (QK_NOPE + V_DIM)), lambda l, b: (l, 0, 0)),
        ],
        out_specs=(blk(ATTN_HEADS * HEAD_PAD),
                   pl.BlockSpec((1, 1, ATTN_HEADS * V_DIM, PAST_LEN), lambda l, b: (b, l, 0, 0))),
        compiler_params=pltpu.CompilerParams(
            dimension_semantics=("arbitrary", "arbitrary"), vmem_limit_bytes=VMEM_LIMIT),
        name="cache_kv",
    )(cache_ckv, cache_kr_pad, wkvb)


def _attn_scores(qh, key_blocks):
    return [_dot_nt(kb, qh) for kb in key_blocks]


def _attn_values(s, vt_blocks):
    m = functools.reduce(jnp.maximum, [jnp.max(si, axis=0, keepdims=True) for si in s])
    p = [jnp.exp2(si - m) for si in s]
    l = sum(jnp.sum(pi, axis=0, keepdims=True) for pi in p)
    o_t = sum(_dot(vt, pi.astype(BF16)) for vt, pi in zip(vt_blocks, p))
    return (o_t / l).T


def _attend_head(qh, key_blocks, vt_blocks):
    return _attn_values(_attn_scores(qh, key_blocks), vt_blocks)


def _ctx_attn_kernel(q_ref, k_ref, vt_ref, o_ref):
    for hd in range(ATTN_HEADS):
        hs = slice(hd * HEAD_PAD, (hd + 1) * HEAD_PAD)
        vs = slice(hd * V_DIM, (hd + 1) * V_DIM)
        o_ref[:, vs] = _attend_head(q_ref[:, hs], [k_ref[:, hs]], [vt_ref[vs, :]]).astype(o_ref.dtype)


def _ctx_attention(q, k, vt):
    blk = lambda w: pl.BlockSpec((SEQ, w), lambda b: (b, 0))
    return pl.pallas_call(
        _ctx_attn_kernel,
        out_shape=jax.ShapeDtypeStruct((N_CTX, ATTN_HEADS * V_DIM), BF16),
        grid=(BATCH,),
        in_specs=[blk(ATTN_HEADS * HEAD_PAD), blk(ATTN_HEADS * HEAD_PAD),
                  pl.BlockSpec((ATTN_HEADS * V_DIM, SEQ), lambda b: (0, b))],
        out_specs=blk(ATTN_HEADS * V_DIM),
        compiler_params=pltpu.CompilerParams(
            dimension_semantics=("arbitrary",), vmem_limit_bytes=VMEM_LIMIT),
        name="ctx_attention",
    )(q, k, vt)


def _lat_attn_kernel(q_ref, k_ref, vt_ref, kc_ref, vct_ref, o_ref):
    hs = [slice(hd * HEAD_PAD, (hd + 1) * HEAD_PAD) for hd in range(HEADS_PER_STEP)]
    vs = [slice(hd * V_DIM, (hd + 1) * V_DIM) for hd in range(HEADS_PER_STEP)]

    def scores(hd):
        return _attn_scores(q_ref[:, hs[hd]], [kc_ref[0, 0, :, hs[hd]], k_ref[:, hs[hd]]])

    s = scores(0)
    for hd in range(HEADS_PER_STEP):
        s_next = scores(hd + 1) if hd + 1 < HEADS_PER_STEP else None
        o_ref[:, vs[hd]] = _attn_values(s, [vct_ref[0, 0, vs[hd], :], vt_ref[vs[hd], :]]).astype(o_ref.dtype)
        s = s_next


def _lat_attention(q, k, vt, kc, vct, layer):
    nq = DEC_SEQ // TQ
    ctx_q_tiles = N_CTX // TQ
    ctx_kv_blocks = N_CTX // DEC_SEQ
    qk_w = HEADS_PER_STEP * HEAD_PAD
    v_w = HEADS_PER_STEP * V_DIM
    return pl.pallas_call(
        _lat_attn_kernel,
        out_shape=jax.ShapeDtypeStruct((N_LAT, ATTN_HEADS * V_DIM), BF16),
        grid=(DEC_BATCH, ATTN_HEADS // HEADS_PER_STEP, nq),
        in_specs=[
            pl.BlockSpec((TQ, qk_w), lambda b, h, i: (ctx_q_tiles + b * nq + i, h)),
            pl.BlockSpec((DEC_SEQ, qk_w), lambda b, h, i: (ctx_kv_blocks + b, h)),
            pl.BlockSpec((v_w, DEC_SEQ), lambda b, h, i: (h, ctx_kv_blocks + b)),
            pl.BlockSpec((1, 1, PAST_LEN, qk_w), lambda b, h, i: (b, layer, 0, h)),
            pl.BlockSpec((1, 1, v_w, PAST_LEN), lambda b, h, i: (b, layer, h, 0)),
        ],
        out_specs=pl.BlockSpec((TQ, v_w), lambda b, h, i: (b * nq + i, h)),
        compiler_params=pltpu.CompilerParams(
            dimension_semantics=("arbitrary", "arbitrary", "arbitrary"),
            vmem_limit_bytes=VMEM_LIMIT),
        name="lat_attention",
    )(q, k, vt, kc, vct)


def _group_peer(x, lane, d, width, period):
    step = d * width
    ahead = pltpu.roll(x, LANES - step, 1)
    behind = pltpu.roll(x, period - step, 1)
    wraps = (lane & (period - 1)) + step >= period
    return jnp.where(wraps, behind, ahead), wraps


def _route(logits, bias, lane):
    valid = lane < N_EXPERTS
    s = 1.0 / (1.0 + jnp.exp(-logits))
    sb = s + bias
    rank = jnp.zeros(sb.shape, jnp.int32)
    for d in range(1, EXPERTS_PER_GROUP):
        o, wraps = _group_peer(sb, lane, d, 1, EXPERTS_PER_GROUP)
        beats = (o > sb) | ((o == sb) & wraps)
        rank = rank + beats.astype(jnp.int32)
    top2 = rank < 2
    t = jnp.where(top2, sb, 0.0)
    gscore = t
    for d in range(1, EXPERTS_PER_GROUP):
        gscore = gscore + _group_peer(t, lane, d, 1, EXPERTS_PER_GROUP)[0]
    grank = jnp.zeros(sb.shape, jnp.int32)
    for d in range(1, N_GROUPS):
        o, wraps = _group_peer(gscore, lane, d, EXPERTS_PER_GROUP, N_EXPERTS)
        beats = (o > gscore) | ((o == gscore) & wraps)
        grank = grank + beats.astype(jnp.int32)
    in_group = (grank == 0) & valid
    w = jnp.where(top2 & in_group, s, 0.0)
    denom = jnp.sum(w, axis=-1, keepdims=True)
    group_flag = jnp.where(in_group & ((lane & (EXPERTS_PER_GROUP - 1)) == 0), 1.0, 0.0)
    return w / denom, group_flag


def _split_bf16(x, terms):
    out = []
    for _ in range(terms - 1):
        t = x.astype(BF16)
        out.append(t)
        x = x - t.astype(F32)
    out.append(x.astype(BF16))
    return out


def _post_kernel(ac_ref, al_ref, gm_ref, xc_ref, xl_ref, mod_ref, oag_ref, wout_ref, g2_ref,
                 wr_ref, rb_ref, x1_ref, h2s_ref, y0_ref, pos_ref, cnt_ref):
    mod = mod_ref[0]
    gate1 = mod[:, 2 * D_MODEL:3 * D_MODEL]
    shift2 = mod[:, 3 * D_MODEL:4 * D_MODEL]
    scale2 = mod[:, 4 * D_MODEL:5 * D_MODEL]
    an = (_rms(_ctx_or_lat(ac_ref, al_ref).astype(F32)) * oag_ref[...]).astype(BF16)
    mixed = _dot(jnp.concatenate([an, gm_ref[...]], axis=1), wout_ref[...])
    x1 = _ctx_or_lat(xc_ref, xl_ref) + gate1 * mixed
    x1_ref[...] = x1
    h2 = _rms(x1) * g2_ref[...] * (1.0 + scale2) + shift2
    y0_ref[...] = jnp.zeros(y0_ref.shape, y0_ref.dtype)
    for sub in range(POST_SUB):
        rows = slice(sub * TM_POST, (sub + 1) * TM_POST)
        ls_rows = slice(sub * LS_ROWS, (sub + 1) * LS_ROWS)
        _route_and_sort(h2[rows], wr_ref, rb_ref, h2s_ref.at[ls_rows], pos_ref.at[rows],
                        cnt_ref.at[sub])


def _route_and_sort(h2, wr_ref, rb_ref, h2s_ref, pos_ref, cnt_ref):
    tm = TM_POST
    h2_hi, h2_lo = _split_bf16(h2, 2)
    t = _dot(jnp.concatenate([h2_hi, h2_lo], axis=0), wr_ref[...])
    logits = (t[:tm, :LANES] + t[tm:, :LANES]) + (t[:tm, LANES:] + t[tm:, LANES:])
    lane = lax.broadcasted_iota(jnp.int32, (tm, LANES), 1)
    comb, flag = _route(logits, rb_ref[...], lane)

    r_i = lax.broadcasted_iota(jnp.int32, (tm, tm), 0)
    c_i = lax.broadcasted_iota(jnp.int32, (tm, tm), 1)
    before = jnp.where(c_i < r_i, 1.0, 0.0).astype(BF16)
    rank = _dot(before, flag.astype(BF16))
    count = jnp.sum(flag, axis=0, keepdims=True)
    pieces = jnp.floor((count + (PIECE - 1)) * (1.0 / PIECE))
    start = (pltpu.roll(pieces, EXPERTS_PER_GROUP, 1) + pltpu.roll(pieces, 2 * EXPERTS_PER_GROUP, 1)
             + pltpu.roll(pieces, 3 * EXPERTS_PER_GROUP, 1)) * PIECE
    pos = jnp.sum(flag * (start + rank), axis=-1, keepdims=True)
    pos_ref[...] = pos
    cnt_ref[...] = count.astype(jnp.int32)
    pos_row = jnp.transpose(jnp.broadcast_to(pos, (tm, LANES)))[0:1, :].astype(jnp.int32)
    place = lax.broadcasted_iota(jnp.int32, (LS_ROWS, tm), 0) == pos_row
    place = jnp.where(place, 1.0, 0.0).astype(BF16)
    wide = jnp.concatenate([h2_hi] + _split_bf16(comb, COMB_TERMS), axis=1)
    h2s_ref[...] = _dot(place, wide).astype(BF16)


def _post_mixer(attn_ctx, attn_lat, gm, x_ctx, x_lat, mod, oag, wout, g2, wr, rb):
    joint_x = x_ctx is x_lat
    tm = POST_SUB * TM_POST
    ls = POST_SUB * LS_ROWS
    row = lambda i: (i, 0)
    return pl.pallas_call(
        _post_kernel,
        out_shape=(
            jax.ShapeDtypeStruct((N_ROWS, D_MODEL), F32),
            jax.ShapeDtypeStruct((N_LS, XS_COLS), BF16),
            jax.ShapeDtypeStruct((N_LS, D_MODEL), BF16),
            jax.ShapeDtypeStruct((N_ROWS, 1), F32),
            jax.ShapeDtypeStruct((N_POST_TILES, 1, LANES), jnp.int32),
        ),
        grid=(N_ROWS // tm,),
        in_specs=_ctx_lat_specs(tm, ATTN_HEADS * V_DIM) + [
            pl.BlockSpec((tm, GM_WIDTH), row),
        ] + _ctx_lat_specs(tm, D_MODEL, joint=joint_x) + [
            pl.BlockSpec((1, 1, 6 * D_MODEL), lambda i: (_cond_row(i, tm), 0, 0)),
            _full((1, ATTN_HEADS * V_DIM)),
            _full((D_MODEL, D_MODEL)),
            _full((1, D_MODEL)),
            _full((D_MODEL, 2 * LANES)),
            _full((1, LANES)),
        ],
        out_specs=(
            pl.BlockSpec((tm, D_MODEL), row),
            pl.BlockSpec((ls, XS_COLS), row),
            pl.BlockSpec((ls, D_MODEL), row),
            pl.BlockSpec((tm, 1), row),
            pl.BlockSpec((POST_SUB, 1, LANES), lambda i: (i, 0, 0)),
        ),
        compiler_params=pltpu.CompilerParams(
            dimension_semantics=("arbitrary",), vmem_limit_bytes=VMEM_LIMIT),
        name="post_mixer",
    )(attn_ctx, attn_lat, gm, x_ctx, x_lat, mod, oag, wout, g2, wr, rb)


def _plan_kernel(cnt_ref, tg_ref, tn_ref, src_ref, nu_ref):
    def clear_src(j, c):
        src_ref[j] = 0
        return c

    lax.fori_loop(0, N_ETILES * E_PIECES, clear_src, 0)

    def clear_tile(j, c):
        tg_ref[j] = N_GROUPS - 1
        tn_ref[j] = 0
        return c

    lax.fori_loop(0, N_ETILES, clear_tile, 0)

    def n_pieces(i, g):
        return lax.shift_right_logical(cnt_ref[i, g * EXPERTS_PER_GROUP] + (PIECE - 1), PIECE_SHIFT)

    t = jnp.int32(0)
    for g in range(N_GROUPS):
        def tile_body(i, s, g=g):
            first = i * LS_PIECES
            for gp in range(g):
                first = first + n_pieces(i, gp)

            def piece_body(p, s):
                src_ref[s] = first + p
                return s + 1

            return lax.fori_loop(0, n_pieces(i, g), piece_body, s)

        s0 = t * E_PIECES
        s1 = lax.fori_loop(0, N_POST_TILES, tile_body, s0)
        n = s1 - s0
        tiles = lax.shift_right_logical(n + (E_PIECES - 1), E_SHIFT)

        def mark_tile(u, c, g=g, n=n, t=t):
            tg_ref[t + u] = g
            tn_ref[t + u] = jnp.minimum(n - u * E_PIECES, E_PIECES)
            return c

        lax.fori_loop(0, tiles, mark_tile, 0)
        t = t + tiles
    nu_ref[0] = t


def _plan(counts):
    smem = pl.BlockSpec(memory_space=pltpu.SMEM)
    return pl.pallas_call(
        _plan_kernel,
        out_shape=(
            jax.ShapeDtypeStruct((N_ETILES,), jnp.int32),
            jax.ShapeDtypeStruct((N_ETILES,), jnp.int32),
            jax.ShapeDtypeStruct((N_ETILES * E_PIECES,), jnp.int32),
            jax.ShapeDtypeStruct((1,), jnp.int32),
        ),
        in_specs=[smem],
        out_specs=(smem, smem, smem, smem),
        name="expert_plan",
    )(counts)


GATHER_X, SCATTER_Y = 0, 1


def _moe_kernel(tg_ref, tn_ref, src_ref, nu_ref, h2s_hbm, y0_hbm, wg32_ref, wu32_ref, wd32_ref,
                wsg32_ref, wsu32_ref, wsd32_ref, y_hbm, xbuf, ybuf, wg_ref, wu_ref, wd_ref,
                wsg_ref, wsu_ref, wsd_ref, sem):
    del y0_hbm
    j = pl.program_id(0)
    n_used = nu_ref[0]
    slot = lax.rem(j, 2)

    @pl.when(j == 0)
    def _():
        wsg_ref[...] = wsg32_ref[0].astype(BF16)
        wsu_ref[...] = wsu32_ref[0].astype(BF16)
        wsd_ref[...] = wsd32_ref[0].astype(BF16)

    @pl.when((j == 0) | (tg_ref[j] != tg_ref[jnp.maximum(j - 1, 0)]))
    def _():
        for k in range(EXPERTS_PER_GROUP):
            wg_ref[k] = wg32_ref[0, k].astype(BF16)
            wu_ref[k] = wu32_ref[0, k].astype(BF16)
            wd_ref[k] = wd32_ref[0, k].astype(BF16)

    def piece_rows(t, k):
        hbm_rows = pl.ds(pl.multiple_of(src_ref[t * E_PIECES + k] * PIECE, PIECE), PIECE)
        buf_rows = pl.ds(pl.multiple_of(k * PIECE, PIECE), PIECE)
        return hbm_rows, buf_rows

    def gather_copies(t, k, slot):
        hbm_rows, buf_rows = piece_rows(t, k)
        return (
            pltpu.make_async_copy(h2s_hbm.at[hbm_rows], xbuf.at[slot, buf_rows], sem.at[GATHER_X, slot]),
        )

    def scatter_copies(t, k, slot):
        hbm_rows, buf_rows = piece_rows(t, k)
        return (
            pltpu.make_async_copy(ybuf.at[slot, buf_rows], y_hbm.at[hbm_rows], sem.at[SCATTER_Y, slot]),
        )

    def for_pieces(t, slot, copies, action):
        def body(k, c):
            for cp in copies(t, k, slot):
                action(cp)
            return c
        lax.fori_loop(0, tn_ref[t], body, 0)

    start = lambda cp: cp.start()
    wait = lambda cp: cp.wait()

    @pl.when(j == 0)
    def _():
        xbuf[...] = jnp.zeros(xbuf.shape, xbuf.dtype)
        for_pieces(0, 0, gather_copies, start)

    @pl.when(j + 1 < n_used)
    def _():
        for_pieces(j + 1, 1 - slot, gather_copies, start)

    @pl.when(j < n_used)
    def _():
        for_pieces(j, slot, gather_copies, wait)
        x = xbuf[slot, :, 0:D_MODEL]
        comb = sum(xbuf[slot, :, D_MODEL + t * LANES:D_MODEL + (t + 1) * LANES].astype(F32)
                   for t in range(COMB_TERMS))
        lane = lax.broadcasted_iota(jnp.int32, (TM_E, LANES), 1)
        first = tg_ref[j] * EXPERTS_PER_GROUP
        y = _dot((_silu(_dot(x, wsg_ref[...])) * _dot(x, wsu_ref[...])).astype(BF16), wsd_ref[...])
        for k in range(EXPERTS_PER_GROUP):
            wk = jnp.sum(jnp.where(lane == first + k, comb, 0.0), axis=-1, keepdims=True)
            act = _silu(_dot(x, wg_ref[k])) * _dot(x, wu_ref[k]) * wk
            y = y + _dot(act.astype(BF16), wd_ref[k])
        ybuf[slot] = y.astype(BF16)
        for_pieces(j, slot, scatter_copies, start)

    @pl.when((j >= 1) & (j < n_used))
    def _():
        for_pieces(j - 1, 1 - slot, scatter_copies, wait)

    @pl.when(j == n_used - 1)
    def _():
        for_pieces(j, slot, scatter_copies, wait)


def _experts(plan, h2s, y0, layer, w_gate, w_up, w_down, ws_gate, ws_up, ws_down):
    tg, tn, src, nu = plan
    any_spec = pl.BlockSpec(memory_space=pl.ANY)
    group = lambda j, tg, tn, src, nu: (layer, tg[j], 0, 0)
    whole = lambda j, tg, tn, src, nu: (layer, 0, 0)
    g = EXPERTS_PER_GROUP
    return pl.pallas_call(
        _moe_kernel,
        out_shape=jax.ShapeDtypeStruct((N_LS, D_MODEL), BF16),
        grid_spec=pltpu.PrefetchScalarGridSpec(
            num_scalar_prefetch=4,
            grid=(N_ETILES,),
            in_specs=[
                any_spec, any_spec,
                pl.BlockSpec((1, g, D_MODEL, EXPERT_FF), group),
                pl.BlockSpec((1, g, D_MODEL, EXPERT_FF), group),
                pl.BlockSpec((1, g, EXPERT_FF, D_MODEL), group),
                pl.BlockSpec((1, D_MODEL, SHARED_FF), whole),
                pl.BlockSpec((1, D_MODEL, SHARED_FF), whole),
                pl.BlockSpec((1, SHARED_FF, D_MODEL), whole),
            ],
            out_specs=any_spec,
            scratch_shapes=[
                pltpu.VMEM((2, TM_E, XS_COLS), BF16),
                pltpu.VMEM((2, TM_E, D_MODEL), BF16),
                pltpu.VMEM((g, D_MODEL, EXPERT_FF), BF16),
                pltpu.VMEM((g, D_MODEL, EXPERT_FF), BF16),
                pltpu.VMEM((g, EXPERT_FF, D_MODEL), BF16),
                pltpu.VMEM((D_MODEL, SHARED_FF), BF16),
                pltpu.VMEM((D_MODEL, SHARED_FF), BF16),
                pltpu.VMEM((SHARED_FF, D_MODEL), BF16),
                pltpu.SemaphoreType.DMA((2, 2)),
            ],
        ),
        input_output_aliases={5: 0},
        compiler_params=pltpu.CompilerParams(
            dimension_semantics=("arbitrary",), vmem_limit_bytes=VMEM_LIMIT),
        name="experts",
    )(tg, tn, src, nu, h2s, y0, w_gate, w_up, w_down, ws_gate, ws_up, ws_down)


def _unsort(y_ref, pos_ref, sub):
    pos = pos_ref[sub * TM_POST:(sub + 1) * TM_POST, :].astype(jnp.int32)
    pick = lax.broadcasted_iota(jnp.int32, (TM_POST, LS_ROWS), 1) == pos
    return _dot(jnp.where(pick, 1.0, 0.0).astype(BF16), y_ref[sub * LS_ROWS:(sub + 1) * LS_ROWS, :])


def _combine_kernel(y_ref, pos_ref, x1_ref, mod_ref, fg_ref, o_ref):
    gate2 = mod_ref[0][:, 5 * D_MODEL:6 * D_MODEL]
    for sub in range(COMBINE_SUB):
        rows = slice(sub * TM_POST, (sub + 1) * TM_POST)
        x2 = x1_ref[rows, :] + gate2 * _unsort(y_ref, pos_ref, sub)
        o_ref[rows, :] = _rms(x2) * fg_ref[...]


def _combine(y, pos, x1, mod, fg, first_tile, n_tiles):
    tm = COMBINE_SUB * TM_POST
    row = lambda i: (first_tile + i, 0)
    return pl.pallas_call(
        _combine_kernel,
        out_shape=jax.ShapeDtypeStruct((n_tiles * tm, D_MODEL), F32),
        grid=(n_tiles,),
        in_specs=[
            pl.BlockSpec((COMBINE_SUB * LS_ROWS, D_MODEL), row),
            pl.BlockSpec((tm, 1), row),
            pl.BlockSpec((tm, D_MODEL), row),
            pl.BlockSpec((1, 1, 6 * D_MODEL), lambda i: (_cond_row(first_tile + i, tm), 0, 0)),
            _full((1, D_MODEL)),
        ],
        out_specs=pl.BlockSpec((tm, D_MODEL), lambda i: (i, 0)),
        compiler_params=pltpu.CompilerParams(
            dimension_semantics=("arbitrary",), vmem_limit_bytes=VMEM_LIMIT),
        name="combine",
    )(y, pos, x1, mod, fg)


def _rope_table():
    rows = DEC_SEQ // GRID_W
    row = jnp.repeat(jnp.arange(rows, dtype=F32), GRID_W)
    col = jnp.tile(jnp.arange(GRID_W, dtype=F32), rows)
    half = QK_ROPE // 2
    freqs = 1.0 / (ROPE_BASE ** (jnp.arange(0, half, 2, dtype=F32) / half))
    ang = jnp.concatenate([row[:, None] * freqs, col[:, None] * freqs], axis=-1)
    cos, sin = jnp.cos(ang), jnp.sin(ang)
    lat = jnp.concatenate([cos, cos, -sin, sin], axis=-1)
    ident = jnp.concatenate([jnp.ones((TM_PRE, QK_ROPE), F32), jnp.zeros((TM_PRE, QK_ROPE), F32)], axis=-1)
    return jnp.concatenate([ident, lat], axis=0)


_DEINT = np.concatenate([np.arange(0, QK_ROPE, 2), np.arange(1, QK_ROPE, 2)])
_SWAP = np.concatenate([np.arange(1, QK_ROPE, 2), np.arange(0, QK_ROPE, 2)])
_INTERLEAVE = np.argsort(_DEINT)


def _layout_w_in(w_in):
    cq_ckv = w_in[:, :Q_LORA + KV_LORA]
    kr = w_in[:, Q_LORA + KV_LORA:Q_LORA + KV_LORA + QK_ROPE]
    uv = w_in[:, Q_LORA + KV_LORA + QK_ROPE:]
    return jnp.concatenate([cq_ckv, kr[:, _DEINT], kr[:, _SWAP], uv], axis=1).astype(BF16)


def _layout_w_qb(w_qb):
    w = (w_qb * (ATTN_SCALE * LOG2E)).reshape(Q_LORA, ATTN_HEADS, QK_NOPE + QK_ROPE)
    nope, rope = w[..., :QK_NOPE], w[..., QK_NOPE:]
    w = jnp.concatenate([nope, rope[..., _DEINT], rope[..., _SWAP]], axis=-1)
    return w.reshape(Q_LORA, ATTN_HEADS * HEAD_PAD).astype(BF16)


def _layout_w_kvb(w_kvb):
    w = w_kvb.reshape(KV_LORA, ATTN_HEADS, QK_NOPE + V_DIM)
    k = w[..., :QK_NOPE].reshape(KV_LORA, ATTN_HEADS * QK_NOPE)
    v = w[..., QK_NOPE:].reshape(KV_LORA, ATTN_HEADS * V_DIM)
    return jnp.concatenate([k, v], axis=1).astype(BF16)


def kernel(x_prompt, x_sample, cache_ckv, cache_krope, c, c_ctx, norm1_g, w_ada, b_ada, w_in,
           q_norm_g, w_qb, kv_norm_g, w_kvb, gm_norm_g, w_spatial, b_spatial, onorm_attn_g,
           onorm_gm_g, w_out, norm2_g, w_router, router_bias, w_gate, w_up, w_down, ws_gate,
           ws_up, ws_down, final_norm_g):
    x_ctx = x_prompt.reshape(N_CTX, D_MODEL)
    x_lat = x_sample.reshape(N_LAT, D_MODEL)
    cond = jnp.concatenate([c_ctx[None, :], c, jnp.zeros((COND_PAD - N_COND, D_MODEL), F32)], axis=0)
    mod = _modulation(cond, w_ada, b_ada)
    rope_tab = _rope_table()

    wkvb = jnp.stack([_layout_w_kvb(w_kvb[l]) for l in range(DEPTH)])
    cache_kr = jnp.pad(cache_krope[..., _DEINT], ((0, 0), (0, 0), (0, 0), (0, LANES - QK_ROPE)))
    kc, vct = _cache_kv(cache_ckv, cache_kr, wkvb)

    wr = jnp.pad(w_router, ((0, 0), (0, LANES - N_EXPERTS)))
    wr_hi = wr.astype(BF16)
    wr = jnp.concatenate([wr_hi, (wr - wr_hi.astype(F32)).astype(BF16)], axis=1)
    rb = jnp.pad(router_bias, (0, LANES - N_EXPERTS)).reshape(1, LANES)
    fg = final_norm_g.reshape(1, D_MODEL)

    ckv_out, kr_out = [], []
    layer_input = (x_ctx, x_lat)
    for l in range(DEPTH):
        mod_l = mod[l].reshape(COND_PAD, 1, 6 * D_MODEL)
        bs = jnp.broadcast_to(b_spatial[l].T[:, :, None], (CHUNK, GM_GROUPS, GM_GROUP_DIM))
        q, k, vt, gm, ckv_n, kr, *formed = _pre_mixer(
            layer_input, mod_l, rope_tab, norm1_g[l].reshape(1, -1), _layout_w_in(w_in[l]),
            q_norm_g[l].reshape(1, -1), _layout_w_qb(w_qb[l]), kv_norm_g[l].reshape(1, -1),
            wkvb[l], gm_norm_g[l].reshape(1, -1), w_spatial[l].astype(BF16),
            bs.reshape(CHUNK, GM_WIDTH), onorm_gm_g[l].reshape(1, -1))
        ckv_out.append(ckv_n[:N_CTX].reshape(BATCH, SEQ, KV_LORA))
        kr_out.append(kr[:N_CTX][:, _INTERLEAVE].reshape(BATCH, SEQ, QK_ROPE))
        x_pair = (formed[0], formed[0]) if formed else layer_input
        x1, h2s, y0, pos, counts = _post_mixer(
            _ctx_attention(q, k, vt), _lat_attention(q, k, vt, kc, vct, l), gm, *x_pair, mod_l,
            onorm_attn_g[l].reshape(1, -1), w_out[l].astype(BF16), norm2_g[l].reshape(1, -1), wr, rb)
        plan = _plan(counts.reshape(N_POST_TILES, LANES))
        y = _experts(plan, h2s, y0, l, w_gate, w_up, w_down, ws_gate, ws_up, ws_down)
        layer_input = (y, pos, x1, mod_l)

    n_ctx_tiles = N_CTX // (COMBINE_SUB * TM_POST)
    n_lat_tiles = N_LAT // (COMBINE_SUB * TM_POST)
    x_ctx = _combine(y, pos, x1, mod_l, fg, 0, n_ctx_tiles)
    x_lat = _combine(y, pos, x1, mod_l, fg, n_ctx_tiles, n_lat_tiles)
    y_prompt = x_ctx.reshape(BATCH, SEQ, D_MODEL)
    y_sample = x_lat.reshape(DEC_BATCH, DEC_SEQ, D_MODEL)
    return y_prompt, y_sample, jnp.stack(ckv_out, axis=1), jnp.stack(kr_out, axis=1)
```

```python
import functools
import math

import jax
import jax.numpy as jnp
import numpy as np
from jax import lax
from jax.experimental import pallas as pl
from jax.experimental.pallas import tpu as pltpu

D_MODEL = 1024
BATCH = 16
SEQ = 256
DEPTH = 2
DEC_BATCH = 4
DEC_SEQ = 4096
PAST_LEN = 256
GRID_W = 64
ATTN_HEADS = 4
QK_NOPE = 128
QK_ROPE = 64
V_DIM = 128
Q_LORA = 384
KV_LORA = 256
ATTN_SCALE = (QK_NOPE + QK_ROPE) ** -0.5
ROPE_BASE = 10000.0
CHUNK = 128
GM_WIDTH = 512
GM_GROUPS = 4
GM_GROUP_DIM = 128
N_EXPERTS = 16
N_GROUPS = 4
EXPERTS_PER_GROUP = 4
EXPERT_FF = 256
SHARED_FF = 256
EPS = 1e-6

N_CTX = BATCH * SEQ
N_LAT = DEC_BATCH * DEC_SEQ
N_ROWS = N_CTX + N_LAT
N_COND = 1 + DEC_BATCH
COND_PAD = 16
HEAD_PAD = 256
IN_COLS = Q_LORA + KV_LORA + 2 * QK_ROPE + 2 * GM_WIDTH
LANES = 128
LOG2E = 1.4426950408889634

TM_PRE = 512
TM_POST = 256
COMB_TERMS = 3
XS_COLS = D_MODEL + COMB_TERMS * LANES
COMBINE_SUB = 4
POST_SUB = 4
PIECE = 16
LS_ROWS = TM_POST + N_GROUPS * PIECE
LS_PIECES = LS_ROWS // PIECE
N_POST_TILES = N_ROWS // TM_POST
N_LS = N_POST_TILES * LS_ROWS
TM_E = 512
E_PIECES = TM_E // PIECE
PIECE_SHIFT = PIECE.bit_length() - 1
E_SHIFT = E_PIECES.bit_length() - 1
N_ETILES = -(-(N_POST_TILES * (TM_POST // PIECE + N_GROUPS - 1)) // E_PIECES) + N_GROUPS
TQ = 512
KEY_BLOCK = 1024
HEADS_PER_STEP = 4
VMEM_LIMIT = 56 * 1024 * 1024

F32 = jnp.float32
BF16 = jnp.bfloat16


def _rms(x):
    return x * lax.rsqrt(jnp.mean(x * x, axis=-1, keepdims=True) + EPS)


def _gelu(x):
    return 0.5 * x * (1.0 + jnp.tanh(math.sqrt(2.0 / math.pi) * (x + 0.044715 * (x * x * x))))


def _silu(x):
    return x * (1.0 / (1.0 + jnp.exp(-x)))


def _dot(a, b):
    return jnp.dot(a, b, preferred_element_type=F32)


def _dot_nt(a, b):
    return lax.dot_general(a, b, (((1,), (1,)), ((), ())), preferred_element_type=F32)


def _cond_row(i, tm):
    n_ctx_tiles = N_CTX // tm
    per_batch = DEC_SEQ // tm
    return jnp.where(i < n_ctx_tiles, 0, 1 + (i - n_ctx_tiles) // per_batch)


def _rope_block(i, tm):
    n_ctx_tiles = N_CTX // tm
    per_batch = DEC_SEQ // tm
    return jnp.where(i < n_ctx_tiles, 0, 1 + (i - n_ctx_tiles) % per_batch)


def _ctx_lat_specs(tm, width, joint=False):
    n_ctx_tiles = N_CTX // tm
    lat_first = n_ctx_tiles if joint else 0
    return [pl.BlockSpec((tm, width), lambda i: (jnp.minimum(i, n_ctx_tiles - 1), 0)),
            pl.BlockSpec((tm, width), lambda i: (lat_first + jnp.maximum(i - n_ctx_tiles, 0), 0))]


def _ctx_or_lat(ctx_ref, lat_ref):
    n_ctx_tiles = N_CTX // ctx_ref.shape[0]
    return jnp.where(pl.program_id(0) < n_ctx_tiles, ctx_ref[...], lat_ref[...])


def _full(shape):
    n = len(shape)
    return pl.BlockSpec(shape, lambda *_: (0,) * n)


def _mod_kernel(cond_ref, w_ref, b_ref, o_ref):
    s_hi, s_lo = _split_bf16(_silu(cond_ref[...]), 2)
    w_hi, w_lo = _split_bf16(w_ref[0], 2)
    t = _dot(jnp.concatenate([s_hi, s_lo], axis=0), w_hi)
    o_ref[0] = (t[:COND_PAD] + t[COND_PAD:]) + _dot(s_hi, w_lo) + b_ref[0]


def _modulation(cond, w_ada, b_ada):
    tn = 1536
    return pl.pallas_call(
        _mod_kernel,
        out_shape=jax.ShapeDtypeStruct((DEPTH, COND_PAD, 6 * D_MODEL), F32),
        grid=(DEPTH, 6 * D_MODEL // tn),
        in_specs=[
            pl.BlockSpec((COND_PAD, D_MODEL), lambda l, j: (0, 0)),
            pl.BlockSpec((1, D_MODEL, tn), lambda l, j: (l, 0, j)),
            pl.BlockSpec((1, 1, tn), lambda l, j: (l, 0, j)),
        ],
        out_specs=pl.BlockSpec((1, COND_PAD, tn), lambda l, j: (l, 0, j)),
        compiler_params=pltpu.CompilerParams(
            dimension_semantics=("arbitrary", "arbitrary"), vmem_limit_bytes=VMEM_LIMIT),
        name="modulation",
    )(cond, w_ada, b_ada.reshape(DEPTH, 1, 6 * D_MODEL))


def _pre_kernel(*refs, after_experts):
    if after_experts:
        (y_ref, pos_ref, x1_ref, prev_mod_ref, mod_ref, rope_ref, g1_ref, win_ref, qg_ref, wqb_ref,
         kvg_ref, wkvb_ref, gmg_ref, ws_ref, bs_ref, ogm_ref,
         q_ref, k_ref, vt_ref, gm_ref, ckv_ref, kr_ref, x_ref) = refs
        tm = x1_ref.shape[0]
        gate2 = prev_mod_ref[0][:, 5 * D_MODEL:6 * D_MODEL]
        x = jnp.concatenate([x1_ref[sub * TM_POST:(sub + 1) * TM_POST, :] + gate2 * _unsort(y_ref, pos_ref, sub)
                             for sub in range(tm // TM_POST)], axis=0)
        x_ref[...] = x
    else:
        (xc_ref, xl_ref, mod_ref, rope_ref, g1_ref, win_ref, qg_ref, wqb_ref,
         kvg_ref, wkvb_ref, gmg_ref, ws_ref, bs_ref, ogm_ref,
         q_ref, k_ref, vt_ref, gm_ref, ckv_ref, kr_ref) = refs
        tm = xc_ref.shape[0]
        x = _ctx_or_lat(xc_ref, xl_ref)
    mod = mod_ref[0]
    shift1 = mod[:, 0:D_MODEL]
    scale1 = mod[:, D_MODEL:2 * D_MODEL]
    h = _rms(x) * g1_ref[...] * (1.0 + scale1) + shift1
    y = _dot(h.astype(BF16), win_ref[...])
    cq = y[:, 0:Q_LORA]
    ckv = y[:, Q_LORA:Q_LORA + KV_LORA]
    kr2 = y[:, 640:768]
    u = y[:, 768:768 + GM_WIDTH]
    vv = y[:, 768 + GM_WIDTH:768 + 2 * GM_WIDTH]

    rope = rope_ref[...]
    lane = lax.broadcasted_iota(jnp.int32, (tm, LANES), 1)

    kr_ref[...] = kr2[:, 0:QK_ROPE]
    t = kr2 * rope
    k_rot = jnp.where(lane < QK_ROPE, t + pltpu.roll(t, QK_ROPE, 1), 0.0).astype(BF16)

    ckv_n = _rms(ckv) * kvg_ref[...]
    ckv_ref[...] = ckv_n
    kv = _dot(ckv_n.astype(BF16), wkvb_ref[...])
    for hd in range(ATTN_HEADS):
        k_ref[:, hd * HEAD_PAD:hd * HEAD_PAD + QK_NOPE] = (
            kv[:, hd * QK_NOPE:(hd + 1) * QK_NOPE].astype(BF16))
        k_ref[:, hd * HEAD_PAD + QK_NOPE:(hd + 1) * HEAD_PAD] = k_rot
    vt_ref[...] = kv[:, ATTN_HEADS * QK_NOPE:].T.astype(BF16)

    q = _dot((_rms(cq) * qg_ref[...]).astype(BF16), wqb_ref[...])
    for hd in range(ATTN_HEADS):
        q_ref[:, hd * HEAD_PAD:hd * HEAD_PAD + QK_NOPE] = (
            q[:, hd * HEAD_PAD:hd * HEAD_PAD + QK_NOPE].astype(BF16))
        t = q[:, hd * HEAD_PAD + QK_NOPE:(hd + 1) * HEAD_PAD] * rope
        q_ref[:, hd * HEAD_PAD + QK_NOPE:(hd + 1) * HEAD_PAD] = (
            t + pltpu.roll(t, QK_ROPE, 1)).astype(BF16)

    ug = _gelu(u)
    vg = _gelu(vv)
    cols = []
    for g in range(GM_GROUPS):
        sl = slice(g * GM_GROUP_DIM, (g + 1) * GM_GROUP_DIM)
        vn = (_rms(vg[:, sl]) * gmg_ref[:, sl]).astype(BF16)
        rows = []
        for c in range(tm // CHUNK):
            sv = _dot(ws_ref[g], vn[c * CHUNK:(c + 1) * CHUNK]) + bs_ref[:, sl]
            rows.append(ug[c * CHUNK:(c + 1) * CHUNK, sl] * sv)
        cols.append(jnp.concatenate(rows, axis=0))
    gm = jnp.concatenate(cols, axis=1)
    gm_ref[...] = (_rms(gm) * ogm_ref[...]).astype(BF16)


def _pre_mixer(layer_input, mod, rope_tab, g1, win, qg, wqb, kvg, wkvb, gmg, ws, bs, ogm):
    tm = TM_PRE
    row = lambda i: (i, 0)
    mod_spec = pl.BlockSpec((1, 1, 6 * D_MODEL), lambda i: (_cond_row(i, tm), 0, 0))
    after_experts = len(layer_input) == 4
    if after_experts:
        input_specs = [pl.BlockSpec((tm // TM_POST * LS_ROWS, D_MODEL), row),
                       pl.BlockSpec((tm, 1), row), pl.BlockSpec((tm, D_MODEL), row), mod_spec]
        extra_shape = (jax.ShapeDtypeStruct((N_ROWS, D_MODEL), F32),)
        extra_spec = (pl.BlockSpec((tm, D_MODEL), row),)
    else:
        input_specs = _ctx_lat_specs(tm, D_MODEL)
        extra_shape = extra_spec = ()
    return pl.pallas_call(
        functools.partial(_pre_kernel, after_experts=after_experts),
        out_shape=(
            jax.ShapeDtypeStruct((N_ROWS, ATTN_HEADS * HEAD_PAD), BF16),
            jax.ShapeDtypeStruct((N_ROWS, ATTN_HEADS * HEAD_PAD), BF16),
            jax.ShapeDtypeStruct((ATTN_HEADS * V_DIM, N_ROWS), BF16),
            jax.ShapeDtypeStruct((N_ROWS, GM_WIDTH), BF16),
            jax.ShapeDtypeStruct((N_ROWS, KV_LORA), F32),
            jax.ShapeDtypeStruct((N_ROWS, QK_ROPE), F32),
        ) + extra_shape,
        grid=(N_ROWS // tm,),
        in_specs=input_specs + [
            mod_spec,
            pl.BlockSpec((tm, LANES), lambda i: (_rope_block(i, tm), 0)),
            _full((1, D_MODEL)),
            _full((D_MODEL, IN_COLS)),
            _full((1, Q_LORA)),
            _full((Q_LORA, ATTN_HEADS * HEAD_PAD)),
            _full((1, KV_LORA)),
            _full((KV_LORA, ATTN_HEADS * (QK_NOPE + V_DIM))),
            _full((1, GM_WIDTH)),
            _full((GM_GROUPS, CHUNK, CHUNK)),
            _full((CHUNK, GM_WIDTH)),
            _full((1, GM_WIDTH)),
        ],
        out_specs=(
            pl.BlockSpec((tm, ATTN_HEADS * HEAD_PAD), row),
            pl.BlockSpec((tm, ATTN_HEADS * HEAD_PAD), row),
            pl.BlockSpec((ATTN_HEADS * V_DIM, tm), lambda i: (0, i)),
            pl.BlockSpec((tm, GM_WIDTH), row),
            pl.BlockSpec((tm, KV_LORA), row),
            pl.BlockSpec((tm, QK_ROPE), row),
        ) + extra_spec,
        compiler_params=pltpu.CompilerParams(
            dimension_semantics=("arbitrary",), vmem_limit_bytes=VMEM_LIMIT),
        name="pre_mixer",
    )(*layer_input, mod, rope_tab, g1, win, qg, wqb, kvg, wkvb, gmg, ws, bs, ogm)


def _cache_kv_kernel(ckv_ref, kr_ref, wkvb_ref, k_ref, vt_ref):
    kv = _dot(ckv_ref[0, 0].astype(BF16), wkvb_ref[0])
    kr = kr_ref[0, 0].astype(BF16)
    for hd in range(ATTN_HEADS):
        k_ref[0, 0, :, hd * HEAD_PAD:hd * HEAD_PAD + QK_NOPE] = (
            kv[:, hd * QK_NOPE:(hd + 1) * QK_NOPE].astype(BF16))
        k_ref[0, 0, :, hd * HEAD_PAD + QK_NOPE:(hd + 1) * HEAD_PAD] = kr
    vt_ref[0, 0] = kv[:, ATTN_HEADS * QK_NOPE:].T.astype(BF16)


def _cache_kv(cache_ckv, cache_kr_pad, wkvb):
    blk = lambda w: pl.BlockSpec((1, 1, PAST_LEN, w), lambda l, b: (b, l, 0, 0))
    return pl.pallas_call(
        _cache_kv_kernel,
        out_shape=(
            jax.ShapeDtypeStruct((DEC_BATCH, DEPTH, PAST_LEN, ATTN_HEADS * HEAD_PAD), BF16),
            jax.ShapeDtypeStruct((DEC_BATCH, DEPTH, ATTN_HEADS * V_DIM, PAST_LEN), BF16),
        ),
        grid=(DEPTH, DEC_BATCH),
        in_specs=[
            blk(KV_LORA),
            blk(LANES),
            pl.BlockSpec((1, KV_LORA, ATTN_HEADS * (QK_NOPE + V_DIM)), lambda l, b: (l, 0, 0)),
        ],
        out_specs=(blk(ATTN_HEADS * HEAD_PAD),
                   pl.BlockSpec((1, 1, ATTN_HEADS * V_DIM, PAST_LEN), lambda l, b: (b, l, 0, 0))),
        compiler_params=pltpu.CompilerParams(
            dimension_semantics=("arbitrary", "arbitrary"), vmem_limit_bytes=VMEM_LIMIT),
        name="cache_kv",
    )(cache_ckv, cache_kr_pad, wkvb)


def _attn_scores(qh, key_blocks):
    return [_dot_nt(kb, qh) for kb in key_blocks]


def _attn_values(s, vt_blocks):
    m = functools.reduce(jnp.maximum, [jnp.max(si, axis=0, keepdims=True) for si in s])
    p = [jnp.exp2(si - m) for si in s]
    l = sum(jnp.sum(pi, axis=0, keepdims=True) for pi in p)
    o_t = sum(_dot(vt, pi.astype(BF16)) for vt, pi in zip(vt_blocks, p))
    return (o_t / l).T


def _attend_head(qh, key_blocks, vt_blocks):
    return _attn_values(_attn_scores(qh, key_blocks), vt_blocks)


def _ctx_attn_kernel(q_ref, k_ref, vt_ref, o_ref):
    for hd in range(ATTN_HEADS):
        hs = slice(hd * HEAD_PAD, (hd + 1) * HEAD_PAD)
        vs = slice(hd * V_DIM, (hd + 1) * V_DIM)
        o_ref[:, vs] = _attend_head(q_ref[:, hs], [k_ref[:, hs]], [vt_ref[vs, :]]).astype(o_ref.dtype)


def _ctx_attention(q, k, vt):
    blk = lambda w: pl.BlockSpec((SEQ, w), lambda b: (b, 0))
    return pl.pallas_call(
        _ctx_attn_kernel,
        out_shape=jax.ShapeDtypeStruct((N_CTX, ATTN_HEADS * V_DIM), BF16),
        grid=(BATCH,),
        in_specs=[blk(ATTN_HEADS * HEAD_PAD), blk(ATTN_HEADS * HEAD_PAD),
                  pl.BlockSpec((ATTN_HEADS * V_DIM, SEQ), lambda b: (0, b))],
        out_specs=blk(ATTN_HEADS * V_DIM),
        compiler_params=pltpu.CompilerParams(
            dimension_semantics=("arbitrary",), vmem_limit_bytes=VMEM_LIMIT),
        name="ctx_attention",
    )(q, k, vt)


def _lat_attn_kernel(q_ref, k_ref, vt_ref, kc_ref, vct_ref, o_ref):
    hs = [slice(hd * HEAD_PAD, (hd + 1) * HEAD_PAD) for hd in range(HEADS_PER_STEP)]
    vs = [slice(hd * V_DIM, (hd + 1) * V_DIM) for hd in range(HEADS_PER_STEP)]

    tq = q_ref.shape[0]
    heads = range(HEADS_PER_STEP)
    n_blocks = 1 + DEC_SEQ // KEY_BLOCK

    def keys(b, hd):
        if b == 0:
            return kc_ref[0, 0, :, hs[hd]]
        return k_ref[(b - 1) * KEY_BLOCK:b * KEY_BLOCK, hs[hd]]

    def values_t(b, hd):
        if b == 0:
            return vct_ref[0, 0, vs[hd], :]
        return vt_ref[vs[hd], (b - 1) * KEY_BLOCK:b * KEY_BLOCK]

    m = [jnp.full((1, tq), -jnp.inf, F32) for _ in heads]
    l = [jnp.zeros((1, tq), F32) for _ in heads]
    o_t = [jnp.zeros((V_DIM, tq), F32) for _ in heads]
    s_cur = [_dot_nt(keys(0, hd), q_ref[:, hs[hd]]) for hd in heads]
    for b in range(n_blocks):
        s_next = []
        for hd in heads:
            if b + 1 < n_blocks:
                s_next.append(_dot_nt(keys(b + 1, hd), q_ref[:, hs[hd]]))
            s = s_cur[hd]
            m_new = jnp.maximum(m[hd], jnp.max(s, axis=0, keepdims=True))
            alpha = jnp.exp2(m[hd] - m_new)
            p = jnp.exp2(s - m_new)
            l[hd] = alpha * l[hd] + jnp.sum(p, axis=0, keepdims=True)
            o_t[hd] = alpha * o_t[hd] + _dot(values_t(b, hd), p.astype(BF16))
            m[hd] = m_new
        s_cur = s_next
    for hd in heads:
        o_ref[:, vs[hd]] = (o_t[hd] / l[hd]).T.astype(o_ref.dtype)


def _lat_attention(q, k, vt, kc, vct, layer):
    nq = DEC_SEQ // TQ
    ctx_q_tiles = N_CTX // TQ
    ctx_kv_blocks = N_CTX // DEC_SEQ
    qk_w = HEADS_PER_STEP * HEAD_PAD
    v_w = HEADS_PER_STEP * V_DIM
    return pl.pallas_call(
        _lat_attn_kernel,
        out_shape=jax.ShapeDtypeStruct((N_LAT, ATTN_HEADS * V_DIM), BF16),
        grid=(DEC_BATCH, ATTN_HEADS // HEADS_PER_STEP, nq),
        in_specs=[
            pl.BlockSpec((TQ, qk_w), lambda b, h, i: (ctx_q_tiles + b * nq + i, h)),
            pl.BlockSpec((DEC_SEQ, qk_w), lambda b, h, i: (ctx_kv_blocks + b, h)),
            pl.BlockSpec((v_w, DEC_SEQ), lambda b, h, i: (h, ctx_kv_blocks + b)),
            pl.BlockSpec((1, 1, PAST_LEN, qk_w), lambda b, h, i: (b, layer, 0, h)),
            pl.BlockSpec((1, 1, v_w, PAST_LEN), lambda b, h, i: (b, layer, h, 0)),
        ],
        out_specs=pl.BlockSpec((TQ, v_w), lambda b, h, i: (b * nq + i, h)),
        compiler_params=pltpu.CompilerParams(
            dimension_semantics=("arbitrary", "arbitrary", "arbitrary"),
            vmem_limit_bytes=VMEM_LIMIT),
        name="lat_attention",
    )(q, k, vt, kc, vct)


def _group_peer(x, row, d, width, period):
    step = d * width
    ahead = pltpu.roll(x, N_EXPERTS - step, 0)
    wraps = (row & (period - 1)) + step >= period
    if period == N_EXPERTS:
        return ahead, wraps
    return jnp.where(wraps, pltpu.roll(x, period - step, 0), ahead), wraps


def _route(logits_t, bias):
    row = lax.broadcasted_iota(jnp.int32, logits_t.shape, 0)
    s = 1.0 / (1.0 + jnp.exp(-logits_t))
    sb = s + bias
    rank = jnp.zeros(sb.shape, jnp.int32)
    for d in range(1, EXPERTS_PER_GROUP):
        o, wraps = _group_peer(sb, row, d, 1, EXPERTS_PER_GROUP)
        beats = (o > sb) | ((o == sb) & wraps)
        rank = rank + beats.astype(jnp.int32)
    top2 = rank < 2
    t = jnp.where(top2, sb, 0.0)
    gscore = t
    for d in range(1, EXPERTS_PER_GROUP):
        gscore = gscore + _group_peer(t, row, d, 1, EXPERTS_PER_GROUP)[0]
    grank = jnp.zeros(sb.shape, jnp.int32)
    for d in range(1, N_GROUPS):
        o, wraps = _group_peer(gscore, row, d, EXPERTS_PER_GROUP, N_EXPERTS)
        beats = (o > gscore) | ((o == gscore) & wraps)
        grank = grank + beats.astype(jnp.int32)
    in_group = grank == 0
    w = jnp.where(top2 & in_group, s, 0.0)
    denom = jnp.sum(w, axis=0, keepdims=True)
    group_flag = jnp.where(in_group & ((row & (EXPERTS_PER_GROUP - 1)) == 0), 1.0, 0.0)
    return w / denom, group_flag


def _experts_to_lanes(x_t):
    pad = jnp.zeros((LANES - N_EXPERTS, x_t.shape[1]), x_t.dtype)
    return jnp.concatenate([x_t, pad], axis=0).T


def _split_bf16(x, terms):
    out = []
    for _ in range(terms - 1):
        t = x.astype(BF16)
        out.append(t)
        x = x - t.astype(F32)
    out.append(x.astype(BF16))
    return out


def _post_kernel(ac_ref, al_ref, gm_ref, xc_ref, xl_ref, mod_ref, oag_ref, wout_ref, g2_ref,
                 wr_ref, rb_ref, x1_ref, h2s_ref, y0_ref, pos_ref, cnt_ref):
    mod = mod_ref[0]
    gate1 = mod[:, 2 * D_MODEL:3 * D_MODEL]
    shift2 = mod[:, 3 * D_MODEL:4 * D_MODEL]
    scale2 = mod[:, 4 * D_MODEL:5 * D_MODEL]
    an = (_rms(_ctx_or_lat(ac_ref, al_ref).astype(F32)) * oag_ref[...]).astype(BF16)
    mixed = _dot(jnp.concatenate([an, gm_ref[...]], axis=1), wout_ref[...])
    x1 = _ctx_or_lat(xc_ref, xl_ref) + gate1 * mixed
    x1_ref[...] = x1
    h2 = _rms(x1) * g2_ref[...] * (1.0 + scale2) + shift2
    y0_ref[...] = jnp.zeros(y0_ref.shape, y0_ref.dtype)
    for sub in range(POST_SUB):
        rows = slice(sub * TM_POST, (sub + 1) * TM_POST)
        ls_rows = slice(sub * LS_ROWS, (sub + 1) * LS_ROWS)
        _route_and_sort(h2[rows], wr_ref, rb_ref, h2s_ref.at[ls_rows], pos_ref.at[rows],
                        cnt_ref.at[sub])


def _route_and_sort(h2, wr_ref, rb_ref, h2s_ref, pos_ref, cnt_ref):
    tm = TM_POST
    h2_hi, h2_lo = _split_bf16(h2, 2)
    t = _dot(jnp.concatenate([h2_hi, h2_lo], axis=0), wr_ref[...])
    logits = (t[:tm, :LANES] + t[tm:, :LANES]) + (t[:tm, LANES:] + t[tm:, LANES:])
    comb_t, flag_t = _route(logits.T[0:N_EXPERTS, :], rb_ref[...])
    comb = _experts_to_lanes(comb_t)
    flag = _experts_to_lanes(flag_t)

    r_i = lax.broadcasted_iota(jnp.int32, (tm, tm), 0)
    c_i = lax.broadcasted_iota(jnp.int32, (tm, tm), 1)
    before = jnp.where(c_i < r_i, 1.0, 0.0).astype(BF16)
    rank = _dot(before, flag.astype(BF16))
    count = jnp.sum(flag, axis=0, keepdims=True)
    pieces = jnp.floor((count + (PIECE - 1)) * (1.0 / PIECE))
    start = (pltpu.roll(pieces, EXPERTS_PER_GROUP, 1) + pltpu.roll(pieces, 2 * EXPERTS_PER_GROUP, 1)
             + pltpu.roll(pieces, 3 * EXPERTS_PER_GROUP, 1)) * PIECE
    pos = jnp.sum(flag * (start + rank), axis=-1, keepdims=True)
    pos_ref[...] = pos
    cnt_ref[...] = count.astype(jnp.int32)
    pos_row = jnp.transpose(jnp.broadcast_to(pos, (tm, LANES)))[0:1, :].astype(jnp.int32)
    place = lax.broadcasted_iota(jnp.int32, (LS_ROWS, tm), 0) == pos_row
    place = jnp.where(place, 1.0, 0.0).astype(BF16)
    wide = jnp.concatenate([h2_hi] + _split_bf16(comb, COMB_TERMS), axis=1)
    h2s_ref[...] = _dot(place, wide).astype(BF16)


def _post_mixer(attn_ctx, attn_lat, gm, x_ctx, x_lat, mod, oag, wout, g2, wr, rb):
    joint_x = x_ctx is x_lat
    tm = POST_SUB * TM_POST
    ls = POST_SUB * LS_ROWS
    row = lambda i: (i, 0)
    return pl.pallas_call(
        _post_kernel,
        out_shape=(
            jax.ShapeDtypeStruct((N_ROWS, D_MODEL), F32),
            jax.ShapeDtypeStruct((N_LS, XS_COLS), BF16),
            jax.ShapeDtypeStruct((N_LS, D_MODEL), BF16),
            jax.ShapeDtypeStruct((N_ROWS, 1), F32),
            jax.ShapeDtypeStruct((N_POST_TILES, 1, LANES), jnp.int32),
        ),
        grid=(N_ROWS // tm,),
        in_specs=_ctx_lat_specs(tm, ATTN_HEADS * V_DIM) + [
            pl.BlockSpec((tm, GM_WIDTH), row),
        ] + _ctx_lat_specs(tm, D_MODEL, joint=joint_x) + [
            pl.BlockSpec((1, 1, 6 * D_MODEL), lambda i: (_cond_row(i, tm), 0, 0)),
            _full((1, ATTN_HEADS * V_DIM)),
            _full((D_MODEL, D_MODEL)),
            _full((1, D_MODEL)),
            _full((D_MODEL, 2 * LANES)),
            _full((N_EXPERTS, 1)),
        ],
        out_specs=(
            pl.BlockSpec((tm, D_MODEL), row),
            pl.BlockSpec((ls, XS_COLS), row),
            pl.BlockSpec((ls, D_MODEL), row),
            pl.BlockSpec((tm, 1), row),
            pl.BlockSpec((POST_SUB, 1, LANES), lambda i: (i, 0, 0)),
        ),
        compiler_params=pltpu.CompilerParams(
            dimension_semantics=("arbitrary",), vmem_limit_bytes=VMEM_LIMIT),
        name="post_mixer",
    )(attn_ctx, attn_lat, gm, x_ctx, x_lat, mod, oag, wout, g2, wr, rb)


def _plan_kernel(cnt_ref, tg_ref, tn_ref, src_ref, nu_ref):
    def clear_src(j, c):
        src_ref[j] = 0
        return c

    lax.fori_loop(0, N_ETILES * E_PIECES, clear_src, 0)

    def clear_tile(j, c):
        tg_ref[j] = N_GROUPS - 1
        tn_ref[j] = 0
        return c

    lax.fori_loop(0, N_ETILES, clear_tile, 0)

    def n_pieces(i, g):
        return lax.shift_right_logical(cnt_ref[i, g * EXPERTS_PER_GROUP] + (PIECE - 1), PIECE_SHIFT)

    t = jnp.int32(0)
    for g in range(N_GROUPS):
        def tile_body(i, s, g=g):
            first = i * LS_PIECES
            for gp in range(g):
                first = first + n_pieces(i, gp)

            def piece_body(p, s):
                src_ref[s] = first + p
                return s + 1

            return lax.fori_loop(0, n_pieces(i, g), piece_body, s)

        s0 = t * E_PIECES
        s1 = lax.fori_loop(0, N_POST_TILES, tile_body, s0)
        n = s1 - s0
        tiles = lax.shift_right_logical(n + (E_PIECES - 1), E_SHIFT)

        def mark_tile(u, c, g=g, n=n, t=t):
            tg_ref[t + u] = g
            tn_ref[t + u] = jnp.minimum(n - u * E_PIECES, E_PIECES)
            return c

        lax.fori_loop(0, tiles, mark_tile, 0)
        t = t + tiles
    nu_ref[0] = t


def _plan(counts):
    smem = pl.BlockSpec(memory_space=pltpu.SMEM)
    return pl.pallas_call(
        _plan_kernel,
        out_shape=(
            jax.ShapeDtypeStruct((N_ETILES,), jnp.int32),
            jax.ShapeDtypeStruct((N_ETILES,), jnp.int32),
            jax.ShapeDtypeStruct((N_ETILES * E_PIECES,), jnp.int32),
            jax.ShapeDtypeStruct((1,), jnp.int32),
        ),
        in_specs=[smem],
        out_specs=(smem, smem, smem, smem),
        name="expert_plan",
    )(counts)


GATHER_X, SCATTER_Y = 0, 1


def _moe_kernel(tg_ref, tn_ref, src_ref, nu_ref, h2s_hbm, y0_hbm, wg32_ref, wu32_ref, wd32_ref,
                wsg32_ref, wsu32_ref, wsd32_ref, y_hbm, xbuf, ybuf, wg_ref, wu_ref, wd_ref,
                wsg_ref, wsu_ref, wsd_ref, sem):
    del y0_hbm
    j = pl.program_id(0)
    n_used = nu_ref[0]
    slot = lax.rem(j, 2)

    @pl.when(j == 0)
    def _():
        wsg_ref[...] = wsg32_ref[0].astype(BF16)
        wsu_ref[...] = wsu32_ref[0].astype(BF16)
        wsd_ref[...] = wsd32_ref[0].astype(BF16)

    @pl.when((j == 0) | (tg_ref[j] != tg_ref[jnp.maximum(j - 1, 0)]))
    def _():
        for k in range(EXPERTS_PER_GROUP):
            wg_ref[k] = wg32_ref[0, k].astype(BF16)
            wu_ref[k] = wu32_ref[0, k].astype(BF16)
            wd_ref[k] = wd32_ref[0, k].astype(BF16)

    def piece_rows(t, k):
        hbm_rows = pl.ds(pl.multiple_of(src_ref[t * E_PIECES + k] * PIECE, PIECE), PIECE)
        buf_rows = pl.ds(pl.multiple_of(k * PIECE, PIECE), PIECE)
        return hbm_rows, buf_rows

    def gather_copies(t, k, slot):
        hbm_rows, buf_rows = piece_rows(t, k)
        return (
            pltpu.make_async_copy(h2s_hbm.at[hbm_rows], xbuf.at[slot, buf_rows], sem.at[GATHER_X, slot]),
        )

    def scatter_copies(t, k, slot):
        hbm_rows, buf_rows = piece_rows(t, k)
        return (
            pltpu.make_async_copy(ybuf.at[slot, buf_rows], y_hbm.at[hbm_rows], sem.at[SCATTER_Y, slot]),
        )

    def for_pieces(t, slot, copies, action):
        def body(k, c):
            for cp in copies(t, k, slot):
                action(cp)
            return c
        lax.fori_loop(0, tn_ref[t], body, 0)

    start = lambda cp: cp.start()
    wait = lambda cp: cp.wait()

    @pl.when(j == 0)
    def _():
        xbuf[...] = jnp.zeros(xbuf.shape, xbuf.dtype)
        for_pieces(0, 0, gather_copies, start)

    @pl.when(j + 1 < n_used)
    def _():
        for_pieces(j + 1, 1 - slot, gather_copies, start)

    @pl.when(j < n_used)
    def _():
        for_pieces(j, slot, gather_copies, wait)
        x = xbuf[slot, :, 0:D_MODEL]
        comb = sum(xbuf[slot, :, D_MODEL + t * LANES:D_MODEL + (t + 1) * LANES].astype(F32)
                   for t in range(COMB_TERMS))
        lane = lax.broadcasted_iota(jnp.int32, (TM_E, LANES), 1)
        first = tg_ref[j] * EXPERTS_PER_GROUP
        y = _dot((_silu(_dot(x, wsg_ref[...])) * _dot(x, wsu_ref[...])).astype(BF16), wsd_ref[...])
        for k in range(EXPERTS_PER_GROUP):
            wk = jnp.sum(jnp.where(lane == first + k, comb, 0.0), axis=-1, keepdims=True)
            act = _silu(_dot(x, wg_ref[k])) * _dot(x, wu_ref[k]) * wk
            y = y + _dot(act.astype(BF16), wd_ref[k])
        ybuf[slot] = y.astype(BF16)
        for_pieces(j, slot, scatter_copies, start)

    @pl.when((j >= 1) & (j < n_used))
    def _():
        for_pieces(j - 1, 1 - slot, scatter_copies, wait)

    @pl.when(j == n_used - 1)
    def _():
        for_pieces(j, slot, scatter_copies, wait)


def _experts(plan, h2s, y0, layer, w_gate, w_up, w_down, ws_gate, ws_up, ws_down):
    tg, tn, src, nu = plan
    any_spec = pl.BlockSpec(memory_space=pl.ANY)
    group = lambda j, tg, tn, src, nu: (layer, tg[j], 0, 0)
    whole = lambda j, tg, tn, src, nu: (layer, 0, 0)
    g = EXPERTS_PER_GROUP
    return pl.pallas_call(
        _moe_kernel,
        out_shape=jax.ShapeDtypeStruct((N_LS, D_MODEL), BF16),
        grid_spec=pltpu.PrefetchScalarGridSpec(
            num_scalar_prefetch=4,
            grid=(N_ETILES,),
            in_specs=[
                any_spec, any_spec,
                pl.BlockSpec((1, g, D_MODEL, EXPERT_FF), group),
                pl.BlockSpec((1, g, D_MODEL, EXPERT_FF), group),
                pl.BlockSpec((1, g, EXPERT_FF, D_MODEL), group),
                pl.BlockSpec((1, D_MODEL, SHARED_FF), whole),
                pl.BlockSpec((1, D_MODEL, SHARED_FF), whole),
                pl.BlockSpec((1, SHARED_FF, D_MODEL), whole),
            ],
            out_specs=any_spec,
            scratch_shapes=[
                pltpu.VMEM((2, TM_E, XS_COLS), BF16),
                pltpu.VMEM((2, TM_E, D_MODEL), BF16),
                pltpu.VMEM((g, D_MODEL, EXPERT_FF), BF16),
                pltpu.VMEM((g, D_MODEL, EXPERT_FF), BF16),
                pltpu.VMEM((g, EXPERT_FF, D_MODEL), BF16),
                pltpu.VMEM((D_MODEL, SHARED_FF), BF16),
                pltpu.VMEM((D_MODEL, SHARED_FF), BF16),
                pltpu.VMEM((SHARED_FF, D_MODEL), BF16),
                pltpu.SemaphoreType.DMA((2, 2)),
            ],
        ),
        input_output_aliases={5: 0},
        compiler_params=pltpu.CompilerParams(
            dimension_semantics=("arbitrary",), vmem_limit_bytes=VMEM_LIMIT),
        name="experts",
    )(tg, tn, src, nu, h2s, y0, w_gate, w_up, w_down, ws_gate, ws_up, ws_down)


def _unsort(y_ref, pos_ref, sub):
    pos = pos_ref[sub * TM_POST:(sub + 1) * TM_POST, :].astype(jnp.int32)
    pick = lax.broadcasted_iota(jnp.int32, (TM_POST, LS_ROWS), 1) == pos
    return _dot(jnp.where(pick, 1.0, 0.0).astype(BF16), y_ref[sub * LS_ROWS:(sub + 1) * LS_ROWS, :])


def _combine_kernel(y_ref, pos_ref, x1_ref, mod_ref, fg_ref, o_ref):
    gate2 = mod_ref[0][:, 5 * D_MODEL:6 * D_MODEL]
    for sub in range(COMBINE_SUB):
        rows = slice(sub * TM_POST, (sub + 1) * TM_POST)
        x2 = x1_ref[rows, :] + gate2 * _unsort(y_ref, pos_ref, sub)
        o_ref[rows, :] = _rms(x2) * fg_ref[...]


def _combine(y, pos, x1, mod, fg, first_tile, n_tiles):
    tm = COMBINE_SUB * TM_POST
    row = lambda i: (first_tile + i, 0)
    return pl.pallas_call(
        _combine_kernel,
        out_shape=jax.ShapeDtypeStruct((n_tiles * tm, D_MODEL), F32),
        grid=(n_tiles,),
        in_specs=[
            pl.BlockSpec((COMBINE_SUB * LS_ROWS, D_MODEL), row),
            pl.BlockSpec((tm, 1), row),
            pl.BlockSpec((tm, D_MODEL), row),
            pl.BlockSpec((1, 1, 6 * D_MODEL), lambda i: (_cond_row(first_tile + i, tm), 0, 0)),
            _full((1, D_MODEL)),
        ],
        out_specs=pl.BlockSpec((tm, D_MODEL), lambda i: (i, 0)),
        compiler_params=pltpu.CompilerParams(
            dimension_semantics=("arbitrary",), vmem_limit_bytes=VMEM_LIMIT),
        name="combine",
    )(y, pos, x1, mod, fg)


def _rope_table():
    rows = DEC_SEQ // GRID_W
    row = jnp.repeat(jnp.arange(rows, dtype=F32), GRID_W)
    col = jnp.tile(jnp.arange(GRID_W, dtype=F32), rows)
    half = QK_ROPE // 2
    freqs = 1.0 / (ROPE_BASE ** (jnp.arange(0, half, 2, dtype=F32) / half))
    ang = jnp.concatenate([row[:, None] * freqs, col[:, None] * freqs], axis=-1)
    cos, sin = jnp.cos(ang), jnp.sin(ang)
    lat = jnp.concatenate([cos, cos, -sin, sin], axis=-1)
    ident = jnp.concatenate([jnp.ones((TM_PRE, QK_ROPE), F32), jnp.zeros((TM_PRE, QK_ROPE), F32)], axis=-1)
    return jnp.concatenate([ident, lat], axis=0)


_DEINT = np.concatenate([np.arange(0, QK_ROPE, 2), np.arange(1, QK_ROPE, 2)])
_SWAP = np.concatenate([np.arange(1, QK_ROPE, 2), np.arange(0, QK_ROPE, 2)])
_INTERLEAVE = np.argsort(_DEINT)


def _layout_w_in(w_in):
    cq_ckv = w_in[:, :Q_LORA + KV_LORA]
    kr = w_in[:, Q_LORA + KV_LORA:Q_LORA + KV_LORA + QK_ROPE]
    uv = w_in[:, Q_LORA + KV_LORA + QK_ROPE:]
    return jnp.concatenate([cq_ckv, kr[:, _DEINT], kr[:, _SWAP], uv], axis=1).astype(BF16)


def _layout_w_qb(w_qb):
    w = (w_qb * (ATTN_SCALE * LOG2E)).reshape(Q_LORA, ATTN_HEADS, QK_NOPE + QK_ROPE)
    nope, rope = w[..., :QK_NOPE], w[..., QK_NOPE:]
    w = jnp.concatenate([nope, rope[..., _DEINT], rope[..., _SWAP]], axis=-1)
    return w.reshape(Q_LORA, ATTN_HEADS * HEAD_PAD).astype(BF16)


def _layout_w_kvb(w_kvb):
    w = w_kvb.reshape(KV_LORA, ATTN_HEADS, QK_NOPE + V_DIM)
    k = w[..., :QK_NOPE].reshape(KV_LORA, ATTN_HEADS * QK_NOPE)
    v = w[..., QK_NOPE:].reshape(KV_LORA, ATTN_HEADS * V_DIM)
    return jnp.concatenate([k, v], axis=1).astype(BF16)


def kernel(x_prompt, x_sample, cache_ckv, cache_krope, c, c_ctx, norm1_g, w_ada, b_ada, w_in,
           q_norm_g, w_qb, kv_norm_g, w_kvb, gm_norm_g, w_spatial, b_spatial, onorm_attn_g,
           onorm_gm_g, w_out, norm2_g, w_router, router_bias, w_gate, w_up, w_down, ws_gate,
           ws_up, ws_down, final_norm_g):
    x_ctx = x_prompt.reshape(N_CTX, D_MODEL)
    x_lat = x_sample.reshape(N_LAT, D_MODEL)
    cond = jnp.concatenate([c_ctx[None, :], c, jnp.zeros((COND_PAD - N_COND, D_MODEL), F32)], axis=0)
    mod = _modulation(cond, w_ada, b_ada)
    rope_tab = _rope_table()

    wkvb = jnp.stack([_layout_w_kvb(w_kvb[l]) for l in range(DEPTH)])
    cache_kr = jnp.pad(cache_krope[..., _DEINT], ((0, 0), (0, 0), (0, 0), (0, LANES - QK_ROPE)))
    kc, vct = _cache_kv(cache_ckv, cache_kr, wkvb)

    wr = jnp.pad(w_router, ((0, 0), (0, LANES - N_EXPERTS)))
    wr_hi = wr.astype(BF16)
    wr = jnp.concatenate([wr_hi, (wr - wr_hi.astype(F32)).astype(BF16)], axis=1)
    rb = router_bias.reshape(N_EXPERTS, 1)
    fg = final_norm_g.reshape(1, D_MODEL)

    ckv_out, kr_out = [], []
    layer_input = (x_ctx, x_lat)
    for l in range(DEPTH):
        mod_l = mod[l].reshape(COND_PAD, 1, 6 * D_MODEL)
        bs = jnp.broadcast_to(b_spatial[l].T[:, :, None], (CHUNK, GM_GROUPS, GM_GROUP_DIM))
        q, k, vt, gm, ckv_n, kr, *formed = _pre_mixer(
            layer_input, mod_l, rope_tab, norm1_g[l].reshape(1, -1), _layout_w_in(w_in[l]),
            q_norm_g[l].reshape(1, -1), _layout_w_qb(w_qb[l]), kv_norm_g[l].reshape(1, -1),
            wkvb[l], gm_norm_g[l].reshape(1, -1), w_spatial[l].astype(BF16),
            bs.reshape(CHUNK, GM_WIDTH), onorm_gm_g[l].reshape(1, -1))
        ckv_out.append(ckv_n[:N_CTX].reshape(BATCH, SEQ, KV_LORA))
        kr_out.append(kr[:N_CTX][:, _INTERLEAVE].reshape(BATCH, SEQ, QK_ROPE))
        x_pair = (formed[0], formed[0]) if formed else layer_input
        x1, h2s, y0, pos, counts = _post_mixer(
            _ctx_attention(q, k, vt), _lat_attention(q, k, vt, kc, vct, l), gm, *x_pair, mod_l,
            onorm_attn_g[l].reshape(1, -1), w_out[l].astype(BF16), norm2_g[l].reshape(1, -1), wr, rb)
        plan = _plan(counts.reshape(N_POST_TILES, LANES))
        y = _experts(plan, h2s, y0, l, w_gate, w_up, w_down, ws_gate, ws_up, ws_down)
        layer_input = (y, pos, x1, mod_l)

    n_ctx_tiles = N_CTX // (COMBINE_SUB * TM_POST)
    n_lat_tiles = N_LAT // (COMBINE_SUB * TM_POST)
    x_ctx = _combine(y, pos, x1, mod_l, fg, 0, n_ctx_tiles)
    x_lat = _combine(y, pos, x1, mod_l, fg, n_ctx_tiles, n_lat_tiles)
    y_prompt = x_ctx.reshape(BATCH, SEQ, D_MODEL)
    y_sample = x_lat.reshape(DEC_BATCH, DEC_SEQ, D_MODEL)
    return y_prompt, y_sample, jnp.stack(ckv_out, axis=1), jnp.stack(kr_out, axis=1)
```

```python
import functools
import math

import jax
import jax.numpy as jnp
import numpy as np
from jax import lax
from jax.experimental import pallas as pl
from jax.experimental.pallas import tpu as pltpu

D_MODEL = 1024
BATCH = 16
SEQ = 256
DEPTH = 2
DEC_BATCH = 4
DEC_SEQ = 4096
PAST_LEN = 256
GRID_W = 64
ATTN_HEADS = 4
QK_NOPE = 128
QK_ROPE = 64
V_DIM = 128
V_EXT = V_DIM + 16
Q_LORA = 384
KV_LORA = 256
ATTN_SCALE = (QK_NOPE + QK_ROPE) ** -0.5
ROPE_BASE = 10000.0
CHUNK = 128
GM_WIDTH = 512
GM_GROUPS = 4
GM_GROUP_DIM = 128
N_EXPERTS = 16
N_GROUPS = 4
EXPERTS_PER_GROUP = 4
EXPERT_FF = 256
SHARED_FF = 256
EPS = 1e-6

N_CTX = BATCH * SEQ
N_LAT = DEC_BATCH * DEC_SEQ
N_ROWS = N_CTX + N_LAT
N_COND = 1 + DEC_BATCH
COND_PAD = 16
HEAD_PAD = 256
IN_COLS = Q_LORA + KV_LORA + 2 * QK_ROPE + 2 * GM_WIDTH
LANES = 128
LOG2E = 1.4426950408889634

TM_PRE = 512
TM_POST = 256
COMB_TERMS = 3
XS_COLS = D_MODEL + COMB_TERMS * LANES
COMBINE_SUB = 4
POST_SUB = 4
PIECE = 16
LS_ROWS = TM_POST + N_GROUPS * PIECE
LS_PIECES = LS_ROWS // PIECE
N_POST_TILES = N_ROWS // TM_POST
N_LS = N_POST_TILES * LS_ROWS
TM_E = 512
E_PIECES = TM_E // PIECE
PIECE_SHIFT = PIECE.bit_length() - 1
E_SHIFT = E_PIECES.bit_length() - 1
N_ETILES = -(-(N_POST_TILES * (TM_POST // PIECE + N_GROUPS - 1)) // E_PIECES) + N_GROUPS
TQ = 512
HEADS_PER_STEP = 4
VMEM_LIMIT = 56 * 1024 * 1024

F32 = jnp.float32
BF16 = jnp.bfloat16


def _rms(x):
    return x * lax.rsqrt(jnp.mean(x * x, axis=-1, keepdims=True) + EPS)


def _gelu(x):
    return 0.5 * x * (1.0 + jnp.tanh(math.sqrt(2.0 / math.pi) * (x + 0.044715 * (x * x * x))))


def _silu(x):
    return x * (1.0 / (1.0 + jnp.exp(-x)))


def _dot(a, b):
    return jnp.dot(a, b, preferred_element_type=F32)


def _dot_nt(a, b):
    return lax.dot_general(a, b, (((1,), (1,)), ((), ())), preferred_element_type=F32)


def _cond_row(i, tm):
    n_ctx_tiles = N_CTX // tm
    per_batch = DEC_SEQ // tm
    return jnp.where(i < n_ctx_tiles, 0, 1 + (i - n_ctx_tiles) // per_batch)


def _rope_block(i, tm):
    n_ctx_tiles = N_CTX // tm
    per_batch = DEC_SEQ // tm
    return jnp.where(i < n_ctx_tiles, 0, 1 + (i - n_ctx_tiles) % per_batch)


def _ctx_lat_specs(tm, width, joint=False):
    n_ctx_tiles = N_CTX // tm
    lat_first = n_ctx_tiles if joint else 0
    return [pl.BlockSpec((tm, width), lambda i: (jnp.minimum(i, n_ctx_tiles - 1), 0)),
            pl.BlockSpec((tm, width), lambda i: (lat_first + jnp.maximum(i - n_ctx_tiles, 0), 0))]


def _ctx_or_lat(ctx_ref, lat_ref):
    n_ctx_tiles = N_CTX // ctx_ref.shape[0]
    return jnp.where(pl.program_id(0) < n_ctx_tiles, ctx_ref[...], lat_ref[...])


def _full(shape):
    n = len(shape)
    return pl.BlockSpec(shape, lambda *_: (0,) * n)


def _mod_kernel(cond_ref, w_ref, b_ref, o_ref):
    s_hi, s_lo = _split_bf16(_silu(cond_ref[...]), 2)
    w_hi, w_lo = _split_bf16(w_ref[0], 2)
    t = _dot(jnp.concatenate([s_hi, s_lo], axis=0), w_hi)
    o_ref[0] = (t[:COND_PAD] + t[COND_PAD:]) + _dot(s_hi, w_lo) + b_ref[0]


def _modulation(cond, w_ada, b_ada):
    tn = 1536
    return pl.pallas_call(
        _mod_kernel,
        out_shape=jax.ShapeDtypeStruct((DEPTH, COND_PAD, 6 * D_MODEL), F32),
        grid=(DEPTH, 6 * D_MODEL // tn),
        in_specs=[
            pl.BlockSpec((COND_PAD, D_MODEL), lambda l, j: (0, 0)),
            pl.BlockSpec((1, D_MODEL, tn), lambda l, j: (l, 0, j)),
            pl.BlockSpec((1, 1, tn), lambda l, j: (l, 0, j)),
        ],
        out_specs=pl.BlockSpec((1, COND_PAD, tn), lambda l, j: (l, 0, j)),
        compiler_params=pltpu.CompilerParams(
            dimension_semantics=("arbitrary", "arbitrary"), vmem_limit_bytes=VMEM_LIMIT),
        name="modulation",
    )(cond, w_ada, b_ada.reshape(DEPTH, 1, 6 * D_MODEL))


def _store_values_t(vt_ref, v):
    t = v.shape[0]
    v_t = v.T.astype(BF16)
    for hd in range(ATTN_HEADS):
        vt_ref[hd * V_EXT:hd * V_EXT + V_DIM, :] = v_t[hd * V_DIM:(hd + 1) * V_DIM, :]
        vt_ref[hd * V_EXT + V_DIM:(hd + 1) * V_EXT, :] = jnp.ones((V_EXT - V_DIM, t), BF16)


def _pre_kernel(*refs, after_experts):
    if after_experts:
        (y_ref, pos_ref, x1_ref, prev_mod_ref, mod_ref, rope_ref, g1_ref, win_ref, qg_ref, wqb_ref,
         kvg_ref, wkvb_ref, gmg_ref, ws_ref, bs_ref, ogm_ref,
         q_ref, k_ref, vt_ref, gm_ref, ckv_ref, kr_ref, x_ref) = refs
        tm = x1_ref.shape[0]
        gate2 = prev_mod_ref[0][:, 5 * D_MODEL:6 * D_MODEL]
        x = jnp.concatenate([x1_ref[sub * TM_POST:(sub + 1) * TM_POST, :] + gate2 * _unsort(y_ref, pos_ref, sub)
                             for sub in range(tm // TM_POST)], axis=0)
        x_ref[...] = x
    else:
        (xc_ref, xl_ref, mod_ref, rope_ref, g1_ref, win_ref, qg_ref, wqb_ref,
         kvg_ref, wkvb_ref, gmg_ref, ws_ref, bs_ref, ogm_ref,
         q_ref, k_ref, vt_ref, gm_ref, ckv_ref, kr_ref) = refs
        tm = xc_ref.shape[0]
        x = _ctx_or_lat(xc_ref, xl_ref)
    mod = mod_ref[0]
    shift1 = mod[:, 0:D_MODEL]
    scale1 = mod[:, D_MODEL:2 * D_MODEL]
    h = _rms(x) * g1_ref[...] * (1.0 + scale1) + shift1
    y = _dot(h.astype(BF16), win_ref[...])
    cq = y[:, 0:Q_LORA]
    ckv = y[:, Q_LORA:Q_LORA + KV_LORA]
    kr2 = y[:, 640:768]
    u = y[:, 768:768 + GM_WIDTH]
    vv = y[:, 768 + GM_WIDTH:768 + 2 * GM_WIDTH]

    rope = rope_ref[...]
    lane = lax.broadcasted_iota(jnp.int32, (tm, LANES), 1)

    kr_ref[...] = kr2[:, 0:QK_ROPE]
    t = kr2 * rope
    k_rot = jnp.where(lane < QK_ROPE, t + pltpu.roll(t, QK_ROPE, 1), 0.0).astype(BF16)

    ckv_n = _rms(ckv) * kvg_ref[...]
    ckv_ref[...] = ckv_n
    kv = _dot(ckv_n.astype(BF16), wkvb_ref[...])
    for hd in range(ATTN_HEADS):
        k_ref[:, hd * HEAD_PAD:hd * HEAD_PAD + QK_NOPE] = (
            kv[:, hd * QK_NOPE:(hd + 1) * QK_NOPE].astype(BF16))
        k_ref[:, hd * HEAD_PAD + QK_NOPE:(hd + 1) * HEAD_PAD] = k_rot
    _store_values_t(vt_ref, kv[:, ATTN_HEADS * QK_NOPE:])

    q = _dot((_rms(cq) * qg_ref[...]).astype(BF16), wqb_ref[...])
    for hd in range(ATTN_HEADS):
        q_ref[:, hd * HEAD_PAD:hd * HEAD_PAD + QK_NOPE] = (
            q[:, hd * HEAD_PAD:hd * HEAD_PAD + QK_NOPE].astype(BF16))
        t = q[:, hd * HEAD_PAD + QK_NOPE:(hd + 1) * HEAD_PAD] * rope
        q_ref[:, hd * HEAD_PAD + QK_NOPE:(hd + 1) * HEAD_PAD] = (
            t + pltpu.roll(t, QK_ROPE, 1)).astype(BF16)

    ug = _gelu(u)
    vg = _gelu(vv)
    cols = []
    for g in range(GM_GROUPS):
        sl = slice(g * GM_GROUP_DIM, (g + 1) * GM_GROUP_DIM)
        vn = (_rms(vg[:, sl]) * gmg_ref[:, sl]).astype(BF16)
        rows = []
        for c in range(tm // CHUNK):
            sv = _dot(ws_ref[g], vn[c * CHUNK:(c + 1) * CHUNK]) + bs_ref[:, sl]
            rows.append(ug[c * CHUNK:(c + 1) * CHUNK, sl] * sv)
        cols.append(jnp.concatenate(rows, axis=0))
    gm = jnp.concatenate(cols, axis=1)
    gm_ref[...] = (_rms(gm) * ogm_ref[...]).astype(BF16)


def _pre_mixer(layer_input, mod, rope_tab, g1, win, qg, wqb, kvg, wkvb, gmg, ws, bs, ogm):
    tm = TM_PRE
    row = lambda i: (i, 0)
    mod_spec = pl.BlockSpec((1, 1, 6 * D_MODEL), lambda i: (_cond_row(i, tm), 0, 0))
    after_experts = len(layer_input) == 4
    if after_experts:
        input_specs = [pl.BlockSpec((tm // TM_POST * LS_ROWS, D_MODEL), row),
                       pl.BlockSpec((tm, 1), row), pl.BlockSpec((tm, D_MODEL), row), mod_spec]
        extra_shape = (jax.ShapeDtypeStruct((N_ROWS, D_MODEL), F32),)
        extra_spec = (pl.BlockSpec((tm, D_MODEL), row),)
    else:
        input_specs = _ctx_lat_specs(tm, D_MODEL)
        extra_shape = extra_spec = ()
    return pl.pallas_call(
        functools.partial(_pre_kernel, after_experts=after_experts),
        out_shape=(
            jax.ShapeDtypeStruct((N_ROWS, ATTN_HEADS * HEAD_PAD), BF16),
            jax.ShapeDtypeStruct((N_ROWS, ATTN_HEADS * HEAD_PAD), BF16),
            jax.ShapeDtypeStruct((ATTN_HEADS * V_EXT, N_ROWS), BF16),
            jax.ShapeDtypeStruct((N_ROWS, GM_WIDTH), BF16),
            jax.ShapeDtypeStruct((N_ROWS, KV_LORA), F32),
            jax.ShapeDtypeStruct((N_ROWS, QK_ROPE), F32),
        ) + extra_shape,
        grid=(N_ROWS // tm,),
        in_specs=input_specs + [
            mod_spec,
            pl.BlockSpec((tm, LANES), lambda i: (_rope_block(i, tm), 0)),
            _full((1, D_MODEL)),
            _full((D_MODEL, IN_COLS)),
            _full((1, Q_LORA)),
            _full((Q_LORA, ATTN_HEADS * HEAD_PAD)),
            _full((1, KV_LORA)),
            _full((KV_LORA, ATTN_HEADS * (QK_NOPE + V_DIM))),
            _full((1, GM_WIDTH)),
            _full((GM_GROUPS, CHUNK, CHUNK)),
            _full((CHUNK, GM_WIDTH)),
            _full((1, GM_WIDTH)),
        ],
        out_specs=(
            pl.BlockSpec((tm, ATTN_HEADS * HEAD_PAD), row),
            pl.BlockSpec((tm, ATTN_HEADS * HEAD_PAD), row),
            pl.BlockSpec((ATTN_HEADS * V_EXT, tm), lambda i: (0, i)),
            pl.BlockSpec((tm, GM_WIDTH), row),
            pl.BlockSpec((tm, KV_LORA), row),
            pl.BlockSpec((tm, QK_ROPE), row),
        ) + extra_spec,
        compiler_params=pltpu.CompilerParams(
            dimension_semantics=("arbitrary",), vmem_limit_bytes=VMEM_LIMIT),
        name="pre_mixer",
    )(*layer_input, mod, rope_tab, g1, win, qg, wqb, kvg, wkvb, gmg, ws, bs, ogm)


def _cache_kv_kernel(ckv_ref, kr_ref, wkvb_ref, k_ref, vt_ref):
    kv = _dot(ckv_ref[0, 0].astype(BF16), wkvb_ref[0])
    kr = kr_ref[0, 0].astype(BF16)
    for hd in range(ATTN_HEADS):
        k_ref[0, 0, :, hd * HEAD_PAD:hd * HEAD_PAD + QK_NOPE] = (
            kv[:, hd * QK_NOPE:(hd + 1) * QK_NOPE].astype(BF16))
        k_ref[0, 0, :, hd * HEAD_PAD + QK_NOPE:(hd + 1) * HEAD_PAD] = kr
    _store_values_t(vt_ref.at[0, 0], kv[:, ATTN_HEADS * QK_NOPE:])


def _cache_kv(cache_ckv, cache_kr_pad, wkvb):
    blk = lambda w: pl.BlockSpec((1, 1, PAST_LEN, w), lambda l, b: (b, l, 0, 0))
    return pl.pallas_call(
        _cache_kv_kernel,
        out_shape=(
            jax.ShapeDtypeStruct((DEC_BATCH, DEPTH, PAST_LEN, ATTN_HEADS * HEAD_PAD), BF16),
            jax.ShapeDtypeStruct((DEC_BATCH, DEPTH, ATTN_HEADS * V_EXT, PAST_LEN), BF16),
        ),
        grid=(DEPTH, DEC_BATCH),
        in_specs=[
            blk(KV_LORA),
            blk(LANES),
            pl.BlockSpec((1, KV_LORA, ATTN_HEADS * (QK_NOPE + V_DIM)), lambda l, b: (l, 0, 0)),
        ],
        out_specs=(blk(ATTN_HEADS * HEAD_PAD),
                   pl.BlockSpec((1, 1, ATTN_HEADS * V_EXT, PAST_LEN), lambda l, b: (b, l, 0, 0))),
        compiler_params=pltpu.CompilerParams(
            dimension_semantics=("arbitrary", "arbitrary"), vmem_limit_bytes=VMEM_LIMIT),
        name="cache_kv",
    )(cache_ckv, cache_kr_pad, wkvb)


def _attn_scores(qh, key_blocks):
    return [_dot_nt(kb, qh) for kb in key_blocks]


def _attn_values(s, vt_blocks):
    s = [si.astype(BF16) for si in s]
    m = functools.reduce(jnp.maximum, [jnp.max(si, axis=0, keepdims=True) for si in s])
    o_t = sum(_dot(vt, jnp.exp2(si - m)) for vt, si in zip(vt_blocks, s))
    return (o_t[0:V_DIM, :] / o_t[V_DIM:V_DIM + 1, :]).T


def _attend_head(qh, key_blocks, vt_blocks):
    return _attn_values(_attn_scores(qh, key_blocks), vt_blocks)


def _ctx_attn_kernel(q_ref, k_ref, vt_ref, o_ref):
    for hd in range(ATTN_HEADS):
        hs = slice(hd * HEAD_PAD, (hd + 1) * HEAD_PAD)
        vs = slice(hd * V_DIM, (hd + 1) * V_DIM)
        ve = slice(hd * V_EXT, (hd + 1) * V_EXT)
        o_ref[:, vs] = _attend_head(q_ref[:, hs], [k_ref[:, hs]], [vt_ref[ve, :]]).astype(o_ref.dtype)


def _ctx_attention(q, k, vt):
    blk = lambda w: pl.BlockSpec((SEQ, w), lambda b: (b, 0))
    return pl.pallas_call(
        _ctx_attn_kernel,
        out_shape=jax.ShapeDtypeStruct((N_CTX, ATTN_HEADS * V_DIM), BF16),
        grid=(BATCH,),
        in_specs=[blk(ATTN_HEADS * HEAD_PAD), blk(ATTN_HEADS * HEAD_PAD),
                  pl.BlockSpec((ATTN_HEADS * V_EXT, SEQ), lambda b: (0, b))],
        out_specs=blk(ATTN_HEADS * V_DIM),
        compiler_params=pltpu.CompilerParams(
            dimension_semantics=("arbitrary",), vmem_limit_bytes=VMEM_LIMIT),
        name="ctx_attention",
    )(q, k, vt)


def _lat_attn_kernel(q_ref, k_ref, vt_ref, kc_ref, vct_ref, o_ref):
    hs = [slice(hd * HEAD_PAD, (hd + 1) * HEAD_PAD) for hd in range(HEADS_PER_STEP)]
    vs = [slice(hd * V_DIM, (hd + 1) * V_DIM) for hd in range(HEADS_PER_STEP)]
    ve = [slice(hd * V_EXT, (hd + 1) * V_EXT) for hd in range(HEADS_PER_STEP)]

    def scores(hd):
        return _attn_scores(q_ref[:, hs[hd]], [kc_ref[0, 0, :, hs[hd]], k_ref[:, hs[hd]]])

    s = scores(0)
    for hd in range(HEADS_PER_STEP):
        s_next = scores(hd + 1) if hd + 1 < HEADS_PER_STEP else None
        o_ref[:, vs[hd]] = _attn_values(s, [vct_ref[0, 0, ve[hd], :], vt_ref[ve[hd], :]]).astype(o_ref.dtype)
        s = s_next


def _lat_attention(q, k, vt, kc, vct, layer):
    nq = DEC_SEQ // TQ
    ctx_q_tiles = N_CTX // TQ
    ctx_kv_blocks = N_CTX // DEC_SEQ
    qk_w = HEADS_PER_STEP * HEAD_PAD
    v_w = HEADS_PER_STEP * V_DIM
    vt_w = HEADS_PER_STEP * V_EXT
    return pl.pallas_call(
        _lat_attn_kernel,
        out_shape=jax.ShapeDtypeStruct((N_LAT, ATTN_HEADS * V_DIM), BF16),
        grid=(DEC_BATCH, ATTN_HEADS // HEADS_PER_STEP, nq),
        in_specs=[
            pl.BlockSpec((TQ, qk_w), lambda b, h, i: (ctx_q_tiles + b * nq + i, h)),
            pl.BlockSpec((DEC_SEQ, qk_w), lambda b, h, i: (ctx_kv_blocks + b, h)),
            pl.BlockSpec((vt_w, DEC_SEQ), lambda b, h, i: (h, ctx_kv_blocks + b)),
            pl.BlockSpec((1, 1, PAST_LEN, qk_w), lambda b, h, i: (b, layer, 0, h)),
            pl.BlockSpec((1, 1, vt_w, PAST_LEN), lambda b, h, i: (b, layer, h, 0)),
        ],
        out_specs=pl.BlockSpec((TQ, v_w), lambda b, h, i: (b * nq + i, h)),
        compiler_params=pltpu.CompilerParams(
            dimension_semantics=("arbitrary", "arbitrary", "arbitrary"),
            vmem_limit_bytes=VMEM_LIMIT),
        name="lat_attention",
    )(q, k, vt, kc, vct)


def _group_peer(x, row, d, width, period):
    step = d * width
    ahead = pltpu.roll(x, N_EXPERTS - step, 0)
    wraps = (row & (period - 1)) + step >= period
    if period == N_EXPERTS:
        return ahead, wraps
    return jnp.where(wraps, pltpu.roll(x, period - step, 0), ahead), wraps


def _route(logits_t, bias):
    row = lax.broadcasted_iota(jnp.int32, logits_t.shape, 0)
    s = 1.0 / (1.0 + jnp.exp(-logits_t))
    sb = s + bias
    rank = jnp.zeros(sb.shape, jnp.int32)
    for d in range(1, EXPERTS_PER_GROUP):
        o, wraps = _group_peer(sb, row, d, 1, EXPERTS_PER_GROUP)
        beats = (o > sb) | ((o == sb) & wraps)
        rank = rank + beats.astype(jnp.int32)
    top2 = rank < 2
    t = jnp.where(top2, sb, 0.0)
    gscore = t
    for d in range(1, EXPERTS_PER_GROUP):
        gscore = gscore + _group_peer(t, row, d, 1, EXPERTS_PER_GROUP)[0]
    grank = jnp.zeros(sb.shape, jnp.int32)
    for d in range(1, N_GROUPS):
        o, wraps = _group_peer(gscore, row, d, EXPERTS_PER_GROUP, N_EXPERTS)
        beats = (o > gscore) | ((o == gscore) & wraps)
        grank = grank + beats.astype(jnp.int32)
    in_group = grank == 0
    w = jnp.where(top2 & in_group, s, 0.0)
    denom = jnp.sum(w, axis=0, keepdims=True)
    group_flag = jnp.where(in_group & ((row & (EXPERTS_PER_GROUP - 1)) == 0), 1.0, 0.0)
    return w / denom, group_flag


def _experts_to_lanes(x_t):
    pad = jnp.zeros((LANES - N_EXPERTS, x_t.shape[1]), x_t.dtype)
    return jnp.concatenate([x_t, pad], axis=0).T


def _split_bf16(x, terms):
    out = []
    for _ in range(terms - 1):
        t = x.astype(BF16)
        out.append(t)
        x = x - t.astype(F32)
    out.append(x.astype(BF16))
    return out


def _post_kernel(ac_ref, al_ref, gm_ref, xc_ref, xl_ref, mod_ref, oag_ref, wout_ref, g2_ref,
                 wr_ref, rb_ref, x1_ref, h2s_ref, y0_ref, pos_ref, cnt_ref):
    mod = mod_ref[0]
    gate1 = mod[:, 2 * D_MODEL:3 * D_MODEL]
    shift2 = mod[:, 3 * D_MODEL:4 * D_MODEL]
    scale2 = mod[:, 4 * D_MODEL:5 * D_MODEL]
    an = (_rms(_ctx_or_lat(ac_ref, al_ref).astype(F32)) * oag_ref[...]).astype(BF16)
    mixed = _dot(jnp.concatenate([an, gm_ref[...]], axis=1), wout_ref[...])
    x1 = _ctx_or_lat(xc_ref, xl_ref) + gate1 * mixed
    x1_ref[...] = x1
    h2 = _rms(x1) * g2_ref[...] * (1.0 + scale2) + shift2
    y0_ref[...] = jnp.zeros(y0_ref.shape, y0_ref.dtype)
    for sub in range(POST_SUB):
        rows = slice(sub * TM_POST, (sub + 1) * TM_POST)
        ls_rows = slice(sub * LS_ROWS, (sub + 1) * LS_ROWS)
        _route_and_sort(h2[rows], wr_ref, rb_ref, h2s_ref.at[ls_rows], pos_ref.at[rows],
                        cnt_ref.at[sub])


def _route_and_sort(h2, wr_ref, rb_ref, h2s_ref, pos_ref, cnt_ref):
    tm = TM_POST
    h2_hi, h2_lo = _split_bf16(h2, 2)
    t = _dot(jnp.concatenate([h2_hi, h2_lo], axis=0), wr_ref[...])
    logits = (t[:tm, :LANES] + t[tm:, :LANES]) + (t[:tm, LANES:] + t[tm:, LANES:])
    comb_t, flag_t = _route(logits.T[0:N_EXPERTS, :], rb_ref[...])
    comb = _experts_to_lanes(comb_t)
    flag = _experts_to_lanes(flag_t)

    r_i = lax.broadcasted_iota(jnp.int32, (tm, tm), 0)
    c_i = lax.broadcasted_iota(jnp.int32, (tm, tm), 1)
    before = jnp.where(c_i < r_i, 1.0, 0.0).astype(BF16)
    rank = _dot(before, flag.astype(BF16))
    count = jnp.sum(flag, axis=0, keepdims=True)
    pieces = jnp.floor((count + (PIECE - 1)) * (1.0 / PIECE))
    start = (pltpu.roll(pieces, EXPERTS_PER_GROUP, 1) + pltpu.roll(pieces, 2 * EXPERTS_PER_GROUP, 1)
             + pltpu.roll(pieces, 3 * EXPERTS_PER_GROUP, 1)) * PIECE
    pos = jnp.sum(flag * (start + rank), axis=-1, keepdims=True)
    pos_ref[...] = pos
    cnt_ref[...] = count.astype(jnp.int32)
    pos_row = jnp.transpose(jnp.broadcast_to(pos, (tm, LANES)))[0:1, :].astype(jnp.int32)
    place = lax.broadcasted_iota(jnp.int32, (LS_ROWS, tm), 0) == pos_row
    place = jnp.where(place, 1.0, 0.0).astype(BF16)
    wide = jnp.concatenate([h2_hi] + _split_bf16(comb, COMB_TERMS), axis=1)
    h2s_ref[...] = _dot(place, wide).astype(BF16)


def _post_mixer(attn_ctx, attn_lat, gm, x_ctx, x_lat, mod, oag, wout, g2, wr, rb):
    joint_x = x_ctx is x_lat
    tm = POST_SUB * TM_POST
    ls = POST_SUB * LS_ROWS
    row = lambda i: (i, 0)
    return pl.pallas_call(
        _post_kernel,
        out_shape=(
            jax.ShapeDtypeStruct((N_ROWS, D_MODEL), F32),
            jax.ShapeDtypeStruct((N_LS, XS_COLS), BF16),
            jax.ShapeDtypeStruct((N_LS, D_MODEL), BF16),
            jax.ShapeDtypeStruct((N_ROWS, 1), F32),
            jax.ShapeDtypeStruct((N_POST_TILES, 1, LANES), jnp.int32),
        ),
        grid=(N_ROWS // tm,),
        in_specs=_ctx_lat_specs(tm, ATTN_HEADS * V_DIM) + [
            pl.BlockSpec((tm, GM_WIDTH), row),
        ] + _ctx_lat_specs(tm, D_MODEL, joint=joint_x) + [
            pl.BlockSpec((1, 1, 6 * D_MODEL), lambda i: (_cond_row(i, tm), 0, 0)),
            _full((1, ATTN_HEADS * V_DIM)),
            _full((D_MODEL, D_MODEL)),
            _full((1, D_MODEL)),
            _full((D_MODEL, 2 * LANES)),
            _full((N_EXPERTS, 1)),
        ],
        out_specs=(
            pl.BlockSpec((tm, D_MODEL), row),
            pl.BlockSpec((ls, XS_COLS), row),
            pl.BlockSpec((ls, D_MODEL), row),
            pl.BlockSpec((tm, 1), row),
            pl.BlockSpec((POST_SUB, 1, LANES), lambda i: (i, 0, 0)),
        ),
        compiler_params=pltpu.CompilerParams(
            dimension_semantics=("arbitrary",), vmem_limit_bytes=VMEM_LIMIT),
        name="post_mixer",
    )(attn_ctx, attn_lat, gm, x_ctx, x_lat, mod, oag, wout, g2, wr, rb)


def _plan_kernel(cnt_ref, tg_ref, tn_ref, src_ref, nu_ref):
    def clear_src(j, c):
        src_ref[j] = 0
        return c

    lax.fori_loop(0, N_ETILES * E_PIECES, clear_src, 0)

    def clear_tile(j, c):
        tg_ref[j] = N_GROUPS - 1
        tn_ref[j] = 0
        return c

    lax.fori_loop(0, N_ETILES, clear_tile, 0)

    def n_pieces(i, g):
        return lax.shift_right_logical(cnt_ref[i, g * EXPERTS_PER_GROUP] + (PIECE - 1), PIECE_SHIFT)

    t = jnp.int32(0)
    for g in range(N_GROUPS):
        def tile_body(i, s, g=g):
            first = i * LS_PIECES
            for gp in range(g):
                first = first + n_pieces(i, gp)

            def piece_body(p, s):
                src_ref[s] = first + p
                return s + 1

            return lax.fori_loop(0, n_pieces(i, g), piece_body, s)

        s0 = t * E_PIECES
        s1 = lax.fori_loop(0, N_POST_TILES, tile_body, s0)
        n = s1 - s0
        tiles = lax.shift_right_logical(n + (E_PIECES - 1), E_SHIFT)

        def mark_tile(u, c, g=g, n=n, t=t):
            tg_ref[t + u] = g
            tn_ref[t + u] = jnp.minimum(n - u * E_PIECES, E_PIECES)
            return c

        lax.fori_loop(0, tiles, mark_tile, 0)
        t = t + tiles
    nu_ref[0] = t


def _plan(counts):
    smem = pl.BlockSpec(memory_space=pltpu.SMEM)
    return pl.pallas_call(
        _plan_kernel,
        out_shape=(
            jax.ShapeDtypeStruct((N_ETILES,), jnp.int32),
            jax.ShapeDtypeStruct((N_ETILES,), jnp.int32),
            jax.ShapeDtypeStruct((N_ETILES * E_PIECES,), jnp.int32),
            jax.ShapeDtypeStruct((1,), jnp.int32),
        ),
        in_specs=[smem],
        out_specs=(smem, smem, smem, smem),
        name="expert_plan",
    )(counts)


GATHER_X, SCATTER_Y = 0, 1


def _moe_kernel(tg_ref, tn_ref, src_ref, nu_ref, h2s_hbm, y0_hbm, wg32_ref, wu32_ref, wd32_ref,
                wsg32_ref, wsu32_ref, wsd32_ref, y_hbm, xbuf, ybuf, wg_ref, wu_ref, wd_ref,
                wsg_ref, wsu_ref, wsd_ref, sem):
    del y0_hbm
    j = pl.program_id(0)
    n_used = nu_ref[0]
    slot = lax.rem(j, 2)

    @pl.when(j == 0)
    def _():
        wsg_ref[...] = wsg32_ref[0].astype(BF16)
        wsu_ref[...] = wsu32_ref[0].astype(BF16)
        wsd_ref[...] = wsd32_ref[0].astype(BF16)

    @pl.when((j == 0) | (tg_ref[j] != tg_ref[jnp.maximum(j - 1, 0)]))
    def _():
        for k in range(EXPERTS_PER_GROUP):
            wg_ref[k] = wg32_ref[0, k].astype(BF16)
            wu_ref[k] = wu32_ref[0, k].astype(BF16)
            wd_ref[k] = wd32_ref[0, k].astype(BF16)

    def piece_rows(t, k):
        hbm_rows = pl.ds(pl.multiple_of(src_ref[t * E_PIECES + k] * PIECE, PIECE), PIECE)
        buf_rows = pl.ds(pl.multiple_of(k * PIECE, PIECE), PIECE)
        return hbm_rows, buf_rows

    def gather_copies(t, k, slot):
        hbm_rows, buf_rows = piece_rows(t, k)
        return (
            pltpu.make_async_copy(h2s_hbm.at[hbm_rows], xbuf.at[slot, buf_rows], sem.at[GATHER_X, slot]),
        )

    def scatter_copies(t, k, slot):
        hbm_rows, buf_rows = piece_rows(t, k)
        return (
            pltpu.make_async_copy(ybuf.at[slot, buf_rows], y_hbm.at[hbm_rows], sem.at[SCATTER_Y, slot]),
        )

    def for_pieces(t, slot, copies, action):
        def body(k, c):
            for cp in copies(t, k, slot):
                action(cp)
            return c
        lax.fori_loop(0, tn_ref[t], body, 0)

    start = lambda cp: cp.start()
    wait = lambda cp: cp.wait()

    @pl.when(j == 0)
    def _():
        xbuf[...] = jnp.zeros(xbuf.shape, xbuf.dtype)
        for_pieces(0, 0, gather_copies, start)

    @pl.when(j + 1 < n_used)
    def _():
        for_pieces(j + 1, 1 - slot, gather_copies, start)

    @pl.when(j < n_used)
    def _():
        for_pieces(j, slot, gather_copies, wait)
        x = xbuf[slot, :, 0:D_MODEL]
        comb = sum(xbuf[slot, :, D_MODEL + t * LANES:D_MODEL + (t + 1) * LANES].astype(F32)
                   for t in range(COMB_TERMS))
        lane = lax.broadcasted_iota(jnp.int32, (TM_E, LANES), 1)
        first = tg_ref[j] * EXPERTS_PER_GROUP
        y = _dot((_silu(_dot(x, wsg_ref[...])) * _dot(x, wsu_ref[...])).astype(BF16), wsd_ref[...])
        for k in range(EXPERTS_PER_GROUP):
            wk = jnp.sum(jnp.where(lane == first + k, comb, 0.0), axis=-1, keepdims=True)
            act = _silu(_dot(x, wg_ref[k])) * _dot(x, wu_ref[k]) * wk
            y = y + _dot(act.astype(BF16), wd_ref[k])
        ybuf[slot] = y.astype(BF16)
        for_pieces(j, slot, scatter_copies, start)

    @pl.when((j >= 1) & (j < n_used))
    def _():
        for_pieces(j - 1, 1 - slot, scatter_copies, wait)

    @pl.when(j == n_used - 1)
    def _():
        for_pieces(j, slot, scatter_copies, wait)


def _experts(plan, h2s, y0, layer, w_gate, w_up, w_down, ws_gate, ws_up, ws_down):
    tg, tn, src, nu = plan
    any_spec = pl.BlockSpec(memory_space=pl.ANY)
    group = lambda j, tg, tn, src, nu: (layer, tg[j], 0, 0)
    whole = lambda j, tg, tn, src, nu: (layer, 0, 0)
    g = EXPERTS_PER_GROUP
    return pl.pallas_call(
        _moe_kernel,
        out_shape=jax.ShapeDtypeStruct((N_LS, D_MODEL), BF16),
        grid_spec=pltpu.PrefetchScalarGridSpec(
            num_scalar_prefetch=4,
            grid=(N_ETILES,),
            in_specs=[
                any_spec, any_spec,
                pl.BlockSpec((1, g, D_MODEL, EXPERT_FF), group),
                pl.BlockSpec((1, g, D_MODEL, EXPERT_FF), group),
                pl.BlockSpec((1, g, EXPERT_FF, D_MODEL), group),
                pl.BlockSpec((1, D_MODEL, SHARED_FF), whole),
                pl.BlockSpec((1, D_MODEL, SHARED_FF), whole),
                pl.BlockSpec((1, SHARED_FF, D_MODEL), whole),
            ],
            out_specs=any_spec,
            scratch_shapes=[
                pltpu.VMEM((2, TM_E, XS_COLS), BF16),
                pltpu.VMEM((2, TM_E, D_MODEL), BF16),
                pltpu.VMEM((g, D_MODEL, EXPERT_FF), BF16),
                pltpu.VMEM((g, D_MODEL, EXPERT_FF), BF16),
                pltpu.VMEM((g, EXPERT_FF, D_MODEL), BF16),
                pltpu.VMEM((D_MODEL, SHARED_FF), BF16),
                pltpu.VMEM((D_MODEL, SHARED_FF), BF16),
                pltpu.VMEM((SHARED_FF, D_MODEL), BF16),
                pltpu.SemaphoreType.DMA((2, 2)),
            ],
        ),
        input_output_aliases={5: 0},
        compiler_params=pltpu.CompilerParams(
            dimension_semantics=("arbitrary",), vmem_limit_bytes=VMEM_LIMIT),
        name="experts",
    )(tg, tn, src, nu, h2s, y0, w_gate, w_up, w_down, ws_gate, ws_up, ws_down)


def _unsort(y_ref, pos_ref, sub):
    pos = pos_ref[sub * TM_POST:(sub + 1) * TM_POST, :].astype(jnp.int32)
    pick = lax.broadcasted_iota(jnp.int32, (TM_POST, LS_ROWS), 1) == pos
    return _dot(jnp.where(pick, 1.0, 0.0).astype(BF16), y_ref[sub * LS_ROWS:(sub + 1) * LS_ROWS, :])


def _combine_kernel(y_ref, pos_ref, x1_ref, mod_ref, fg_ref, o_ref):
    gate2 = mod_ref[0][:, 5 * D_MODEL:6 * D_MODEL]
    for sub in range(COMBINE_SUB):
        rows = slice(sub * TM_POST, (sub + 1) * TM_POST)
        x2 = x1_ref[rows, :] + gate2 * _unsort(y_ref, pos_ref, sub)
        o_ref[rows, :] = _rms(x2) * fg_ref[...]


def _combine(y, pos, x1, mod, fg, first_tile, n_tiles):
    tm = COMBINE_SUB * TM_POST
    row = lambda i: (first_tile + i, 0)
    return pl.pallas_call(
        _combine_kernel,
        out_shape=jax.ShapeDtypeStruct((n_tiles * tm, D_MODEL), F32),
        grid=(n_tiles,),
        in_specs=[
            pl.BlockSpec((COMBINE_SUB * LS_ROWS, D_MODEL), row),
            pl.BlockSpec((tm, 1), row),
            pl.BlockSpec((tm, D_MODEL), row),
            pl.BlockSpec((1, 1, 6 * D_MODEL), lambda i: (_cond_row(first_tile + i, tm), 0, 0)),
            _full((1, D_MODEL)),
        ],
        out_specs=pl.BlockSpec((tm, D_MODEL), lambda i: (i, 0)),
        compiler_params=pltpu.CompilerParams(
            dimension_semantics=("arbitrary",), vmem_limit_bytes=VMEM_LIMIT),
        name="combine",
    )(y, pos, x1, mod, fg)


def _rope_table():
    rows = DEC_SEQ // GRID_W
    row = jnp.repeat(jnp.arange(rows, dtype=F32), GRID_W)
    col = jnp.tile(jnp.arange(GRID_W, dtype=F32), rows)
    half = QK_ROPE // 2
    freqs = 1.0 / (ROPE_BASE ** (jnp.arange(0, half, 2, dtype=F32) / half))
    ang = jnp.concatenate([row[:, None] * freqs, col[:, None] * freqs], axis=-1)
    cos, sin = jnp.cos(ang), jnp.sin(ang)
    lat = jnp.concatenate([cos, cos, -sin, sin], axis=-1)
    ident = jnp.concatenate([jnp.ones((TM_PRE, QK_ROPE), F32), jnp.zeros((TM_PRE, QK_ROPE), F32)], axis=-1)
    return jnp.concatenate([ident, lat], axis=0)


_DEINT = np.concatenate([np.arange(0, QK_ROPE, 2), np.arange(1, QK_ROPE, 2)])
_SWAP = np.concatenate([np.arange(1, QK_ROPE, 2), np.arange(0, QK_ROPE, 2)])
_INTERLEAVE = np.argsort(_DEINT)


def _layout_w_in(w_in):
    cq_ckv = w_in[:, :Q_LORA + KV_LORA]
    kr = w_in[:, Q_LORA + KV_LORA:Q_LORA + KV_LORA + QK_ROPE]
    uv = w_in[:, Q_LORA + KV_LORA + QK_ROPE:]
    return jnp.concatenate([cq_ckv, kr[:, _DEINT], kr[:, _SWAP], uv], axis=1).astype(BF16)


def _layout_w_qb(w_qb):
    w = (w_qb * (ATTN_SCALE * LOG2E)).reshape(Q_LORA, ATTN_HEADS, QK_NOPE + QK_ROPE)
    nope, rope = w[..., :QK_NOPE], w[..., QK_NOPE:]
    w = jnp.concatenate([nope, rope[..., _DEINT], rope[..., _SWAP]], axis=-1)
    return w.reshape(Q_LORA, ATTN_HEADS * HEAD_PAD).astype(BF16)


def _layout_w_kvb(w_kvb):
    w = w_kvb.reshape(KV_LORA, ATTN_HEADS, QK_NOPE + V_DIM)
    k = w[..., :QK_NOPE].reshape(KV_LORA, ATTN_HEADS * QK_NOPE)
    v = w[..., QK_NOPE:].reshape(KV_LORA, ATTN_HEADS * V_DIM)
    return jnp.concatenate([k, v], axis=1).astype(BF16)


def kernel(x_prompt, x_sample, cache_ckv, cache_krope, c, c_ctx, norm1_g, w_ada, b_ada, w_in,
           q_norm_g, w_qb, kv_norm_g, w_kvb, gm_norm_g, w_spatial, b_spatial, onorm_attn_g,
           onorm_gm_g, w_out, norm2_g, w_router, router_bias, w_gate, w_up, w_down, ws_gate,
           ws_up, ws_down, final_norm_g):
    x_ctx = x_prompt.reshape(N_CTX, D_MODEL)
    x_lat = x_sample.reshape(N_LAT, D_MODEL)
    cond = jnp.concatenate([c_ctx[None, :], c, jnp.zeros((COND_PAD - N_COND, D_MODEL), F32)], axis=0)
    mod = _modulation(cond, w_ada, b_ada)
    rope_tab = _rope_table()

    wkvb = jnp.stack([_layout_w_kvb(w_kvb[l]) for l in range(DEPTH)])
    cache_kr = jnp.pad(cache_krope[..., _DEINT], ((0, 0), (0, 0), (0, 0), (0, LANES - QK_ROPE)))
    kc, vct = _cache_kv(cache_ckv, cache_kr, wkvb)

    wr = jnp.pad(w_router, ((0, 0), (0, LANES - N_EXPERTS)))
    wr_hi = wr.astype(BF16)
    wr = jnp.concatenate([wr_hi, (wr - wr_hi.astype(F32)).astype(BF16)], axis=1)
    rb = router_bias.reshape(N_EXPERTS, 1)
    fg = final_norm_g.reshape(1, D_MODEL)

    ckv_out, kr_out = [], []
    layer_input = (x_ctx, x_lat)
    for l in range(DEPTH):
        mod_l = mod[l].reshape(COND_PAD, 1, 6 * D_MODEL)
        bs = jnp.broadcast_to(b_spatial[l].T[:, :, None], (CHUNK, GM_GROUPS, GM_GROUP_DIM))
        q, k, vt, gm, ckv_n, kr, *formed = _pre_mixer(
            layer_input, mod_l, rope_tab, norm1_g[l].reshape(1, -1), _layout_w_in(w_in[l]),
            q_norm_g[l].reshape(1, -1), _layout_w_qb(w_qb[l]), kv_norm_g[l].reshape(1, -1),
            wkvb[l], gm_norm_g[l].reshape(1, -1), w_spatial[l].astype(BF16),
            bs.reshape(CHUNK, GM_WIDTH), onorm_gm_g[l].reshape(1, -1))
        ckv_out.append(ckv_n[:N_CTX].reshape(BATCH, SEQ, KV_LORA))
        kr_out.append(kr[:N_CTX][:, _INTERLEAVE].reshape(BATCH, SEQ, QK_ROPE))
        x_pair = (formed[0], formed[0]) if formed else layer_input
        x1, h2s, y0, pos, counts = _post_mixer(
            _ctx_attention(q, k, vt), _lat_attention(q, k, vt, kc, vct, l), gm, *x_pair, mod_l,
            onorm_attn_g[l].reshape(1, -1), w_out[l].astype(BF16), norm2_g[l].reshape(1, -1), wr, rb)
        plan = _plan(counts.reshape(N_POST_TILES, LANES))
        y = _experts(plan, h2s, y0, l, w_gate, w_up, w_down, ws_gate, ws_up, ws_down)
        layer_input = (y, pos, x1, mod_l)

    n_ctx_tiles = N_CTX // (COMBINE_SUB * TM_POST)
    n_lat_tiles = N_LAT // (COMBINE_SUB * TM_POST)
    x_ctx = _combine(y, pos, x1, mod_l, fg, 0, n_ctx_tiles)
    x_lat = _combine(y, pos, x1, mod_l, fg, n_ctx_tiles, n_lat_tiles)
    y_prompt = x_ctx.reshape(BATCH, SEQ, D_MODEL)
    y_sample = x_lat.reshape(DEC_BATCH, DEC_SEQ, D_MODEL)
    return y_prompt, y_sample, jnp.stack(ckv_out, axis=1), jnp.stack(kr_out, axis=1)
```

```python
import functools
import math

import jax
import jax.numpy as jnp
import numpy as np
from jax import lax
from jax.experimental import pallas as pl
from jax.experimental.pallas import tpu as pltpu

D_MODEL = 1024
BATCH = 16
SEQ = 256
DEPTH = 2
DEC_BATCH = 4
DEC_SEQ = 4096
PAST_LEN = 256
GRID_W = 64
ATTN_HEADS = 4
QK_NOPE = 128
QK_ROPE = 64
V_DIM = 128
V_EXT = V_DIM + 16
Q_LORA = 384
KV_LORA = 256
ATTN_SCALE = (QK_NOPE + QK_ROPE) ** -0.5
ROPE_BASE = 10000.0
CHUNK = 128
GM_WIDTH = 512
GM_GROUPS = 4
GM_GROUP_DIM = 128
N_EXPERTS = 16
N_GROUPS = 4
EXPERTS_PER_GROUP = 4
EXPERT_FF = 256
SHARED_FF = 256
EPS = 1e-6

N_CTX = BATCH * SEQ
N_LAT = DEC_BATCH * DEC_SEQ
N_ROWS = N_CTX + N_LAT
N_COND = 1 + DEC_BATCH
COND_PAD = 16
HEAD_PAD = 256
IN_COLS = Q_LORA + KV_LORA + 2 * QK_ROPE + 2 * GM_WIDTH
LANES = 128
LOG2E = 1.4426950408889634

TM_PRE = 512
TM_POST = 256
COMB_TERMS = 3
XS_COLS = D_MODEL + COMB_TERMS * LANES
COMBINE_SUB = 4
POST_SUB = 4
PIECE = 16
LS_ROWS = TM_POST + N_GROUPS * PIECE
LS_PIECES = LS_ROWS // PIECE
N_POST_TILES = N_ROWS // TM_POST
N_LS = N_POST_TILES * LS_ROWS
TM_E = 512
E_PIECES = TM_E // PIECE
PIECE_SHIFT = PIECE.bit_length() - 1
E_SHIFT = E_PIECES.bit_length() - 1
N_ETILES = -(-(N_POST_TILES * (TM_POST // PIECE + N_GROUPS - 1)) // E_PIECES) + N_GROUPS
SRC_SLOTS = (N_ETILES + 1) * E_PIECES
TQ = 512
CTX_SUB = 2
HEADS_PER_STEP = 4
VMEM_LIMIT = 56 * 1024 * 1024

F32 = jnp.float32
BF16 = jnp.bfloat16


def _rms(x):
    return x * lax.rsqrt(jnp.mean(x * x, axis=-1, keepdims=True) + EPS)


def _gelu(x):
    return 0.5 * x * (1.0 + jnp.tanh(math.sqrt(2.0 / math.pi) * (x + 0.044715 * (x * x * x))))


def _silu(x):
    return x * (1.0 / (1.0 + jnp.exp(-x)))


def _dot(a, b):
    return jnp.dot(a, b, preferred_element_type=F32)


def _dot_nt(a, b):
    return lax.dot_general(a, b, (((1,), (1,)), ((), ())), preferred_element_type=F32)


def _cond_row(i, tm):
    n_ctx_tiles = N_CTX // tm
    per_batch = DEC_SEQ // tm
    return jnp.where(i < n_ctx_tiles, 0, 1 + (i - n_ctx_tiles) // per_batch)


def _rope_block(i, tm):
    n_ctx_tiles = N_CTX // tm
    per_batch = DEC_SEQ // tm
    return jnp.where(i < n_ctx_tiles, 0, 1 + (i - n_ctx_tiles) % per_batch)


def _ctx_lat_specs(tm, width, joint=False):
    n_ctx_tiles = N_CTX // tm
    lat_first = n_ctx_tiles if joint else 0
    return [pl.BlockSpec((tm, width), lambda i: (jnp.minimum(i, n_ctx_tiles - 1), 0)),
            pl.BlockSpec((tm, width), lambda i: (lat_first + jnp.maximum(i - n_ctx_tiles, 0), 0))]


def _ctx_or_lat(ctx_ref, lat_ref):
    n_ctx_tiles = N_CTX // ctx_ref.shape[0]
    return jnp.where(pl.program_id(0) < n_ctx_tiles, ctx_ref[...], lat_ref[...])


def _full(shape):
    n = len(shape)
    return pl.BlockSpec(shape, lambda *_: (0,) * n)


def _of_layer(layer, shape):
    n = len(shape)
    return pl.BlockSpec((None,) + tuple(shape), lambda *_: (layer,) + (0,) * n)


def _mod_spec(layer, tm, first_tile=0):
    return pl.BlockSpec((None, 1, 1, 6 * D_MODEL),
                        lambda i: (layer, _cond_row(first_tile + i, tm), 0, 0))


def _mod_kernel(cond_ref, w_ref, b_ref, o_ref):
    s_hi, s_lo = _split_bf16(_silu(cond_ref[...]), 2)
    w_hi, w_lo = _split_bf16(w_ref[0], 2)
    t = _dot(jnp.concatenate([s_hi, s_lo], axis=0), w_hi)
    o_ref[0] = (t[:COND_PAD] + t[COND_PAD:]) + _dot(s_hi, w_lo) + b_ref[0]


def _modulation(cond, w_ada, b_ada):
    tn = 1536
    return pl.pallas_call(
        _mod_kernel,
        out_shape=jax.ShapeDtypeStruct((DEPTH, COND_PAD, 6 * D_MODEL), F32),
        grid=(DEPTH, 6 * D_MODEL // tn),
        in_specs=[
            pl.BlockSpec((COND_PAD, D_MODEL), lambda l, j: (0, 0)),
            pl.BlockSpec((1, D_MODEL, tn), lambda l, j: (l, 0, j)),
            pl.BlockSpec((1, 1, tn), lambda l, j: (l, 0, j)),
        ],
        out_specs=pl.BlockSpec((1, COND_PAD, tn), lambda l, j: (l, 0, j)),
        compiler_params=pltpu.CompilerParams(
            dimension_semantics=("arbitrary", "arbitrary"), vmem_limit_bytes=VMEM_LIMIT),
        name="modulation",
    )(cond, w_ada, b_ada.reshape(DEPTH, 1, 6 * D_MODEL))


def _store_values_t(vt_ref, v):
    t = v.shape[0]
    v_t = v.T.astype(BF16)
    for hd in range(ATTN_HEADS):
        vt_ref[hd * V_EXT:hd * V_EXT + V_DIM, :] = v_t[hd * V_DIM:(hd + 1) * V_DIM, :]
        vt_ref[hd * V_EXT + V_DIM:(hd + 1) * V_EXT, :] = jnp.ones((V_EXT - V_DIM, t), BF16)


def _pre_kernel(*refs, after_experts):
    if after_experts:
        (y_ref, pos_ref, x1_ref, prev_mod_ref, mod_ref, rope_ref, g1_ref, win_ref, qg_ref, wqb_ref,
         kvg_ref, wkvb_ref, gmg_ref, ws_ref, bs_ref, ogm_ref,
         q_ref, k_ref, vt_ref, gm_ref, ckv_ref, kr_ref, x_ref) = refs
        tm = x1_ref.shape[0]
        gate2 = prev_mod_ref[0][:, 5 * D_MODEL:6 * D_MODEL]
        x = jnp.concatenate([x1_ref[sub * TM_POST:(sub + 1) * TM_POST, :] + gate2 * _unsort(y_ref, pos_ref, sub)
                             for sub in range(tm // TM_POST)], axis=0)
        x_ref[...] = x
    else:
        (xc_ref, xl_ref, mod_ref, rope_ref, g1_ref, win_ref, qg_ref, wqb_ref,
         kvg_ref, wkvb_ref, gmg_ref, ws_ref, bs_ref, ogm_ref,
         q_ref, k_ref, vt_ref, gm_ref, ckv_ref, kr_ref) = refs
        tm = xc_ref.shape[0]
        x = _ctx_or_lat(xc_ref, xl_ref)
    mod = mod_ref[0]
    shift1 = mod[:, 0:D_MODEL]
    scale1 = mod[:, D_MODEL:2 * D_MODEL]
    h = _rms(x) * g1_ref[...] * (1.0 + scale1) + shift1
    y = _dot(h.astype(BF16), win_ref[...])
    cq = y[:, 0:Q_LORA]
    ckv = y[:, Q_LORA:Q_LORA + KV_LORA]
    kr2 = y[:, 640:768]
    u = y[:, 768:768 + GM_WIDTH]
    vv = y[:, 768 + GM_WIDTH:768 + 2 * GM_WIDTH]

    rope = rope_ref[...]
    lane = lax.broadcasted_iota(jnp.int32, (tm, LANES), 1)

    kr_ref[...] = kr2[:, 0:QK_ROPE]
    t = kr2 * rope
    k_rot = jnp.where(lane < QK_ROPE, t + pltpu.roll(t, QK_ROPE, 1), 0.0).astype(BF16)

    ckv_n = _rms(ckv) * kvg_ref[...]
    ckv_ref[...] = ckv_n
    kv = _dot(ckv_n.astype(BF16), wkvb_ref[...])
    for hd in range(ATTN_HEADS):
        k_ref[:, hd * HEAD_PAD:hd * HEAD_PAD + QK_NOPE] = (
            kv[:, hd * QK_NOPE:(hd + 1) * QK_NOPE].astype(BF16))
        k_ref[:, hd * HEAD_PAD + QK_NOPE:(hd + 1) * HEAD_PAD] = k_rot
    _store_values_t(vt_ref, kv[:, ATTN_HEADS * QK_NOPE:])

    q = _dot((_rms(cq) * qg_ref[...]).astype(BF16), wqb_ref[...])
    for hd in range(ATTN_HEADS):
        q_ref[:, hd * HEAD_PAD:hd * HEAD_PAD + QK_NOPE] = (
            q[:, hd * HEAD_PAD:hd * HEAD_PAD + QK_NOPE].astype(BF16))
        t = q[:, hd * HEAD_PAD + QK_NOPE:(hd + 1) * HEAD_PAD] * rope
        q_ref[:, hd * HEAD_PAD + QK_NOPE:(hd + 1) * HEAD_PAD] = (
            t + pltpu.roll(t, QK_ROPE, 1)).astype(BF16)

    ug = _gelu(u)
    vg = _gelu(vv)
    cols = []
    for g in range(GM_GROUPS):
        sl = slice(g * GM_GROUP_DIM, (g + 1) * GM_GROUP_DIM)
        vn = (_rms(vg[:, sl]) * gmg_ref[:, sl]).astype(BF16)
        rows = []
        for c in range(tm // CHUNK):
            sv = _dot(ws_ref[g], vn[c * CHUNK:(c + 1) * CHUNK]) + bs_ref[:, sl]
            rows.append(ug[c * CHUNK:(c + 1) * CHUNK, sl] * sv)
        cols.append(jnp.concatenate(rows, axis=0))
    gm = jnp.concatenate(cols, axis=1)
    gm_ref[...] = (_rms(gm) * ogm_ref[...]).astype(BF16)


def _pre_mixer(layer, layer_input, mod, rope_tab, g1, win, qg, wqb, kvg, wkvb, gmg, ws, bs, ogm):
    tm = TM_PRE
    row = lambda i: (i, 0)
    mod_spec = _mod_spec(layer, tm)
    after_experts = len(layer_input) == 3
    if after_experts:
        layer_input = (*layer_input, mod)
        input_specs = [pl.BlockSpec((tm // TM_POST * LS_ROWS, D_MODEL), row),
                       pl.BlockSpec((tm, 1), row), pl.BlockSpec((tm, D_MODEL), row),
                       _mod_spec(layer - 1, tm)]
        extra_shape = (jax.ShapeDtypeStruct((N_ROWS, D_MODEL), F32),)
        extra_spec = (pl.BlockSpec((tm, D_MODEL), row),)
    else:
        input_specs = _ctx_lat_specs(tm, D_MODEL)
        extra_shape = extra_spec = ()
    return pl.pallas_call(
        functools.partial(_pre_kernel, after_experts=after_experts),
        out_shape=(
            jax.ShapeDtypeStruct((N_ROWS, ATTN_HEADS * HEAD_PAD), BF16),
            jax.ShapeDtypeStruct((N_ROWS, ATTN_HEADS * HEAD_PAD), BF16),
            jax.ShapeDtypeStruct((ATTN_HEADS * V_EXT, N_ROWS), BF16),
            jax.ShapeDtypeStruct((N_ROWS, GM_WIDTH), BF16),
            jax.ShapeDtypeStruct((N_ROWS, KV_LORA), F32),
            jax.ShapeDtypeStruct((N_ROWS, QK_ROPE), F32),
        ) + extra_shape,
        grid=(N_ROWS // tm,),
        in_specs=input_specs + [
            mod_spec,
            pl.BlockSpec((tm, LANES), lambda i: (_rope_block(i, tm), 0)),
            _of_layer(layer, (1, D_MODEL)),
            _of_layer(layer, (D_MODEL, IN_COLS)),
            _of_layer(layer, (1, Q_LORA)),
            _of_layer(layer, (Q_LORA, ATTN_HEADS * HEAD_PAD)),
            _of_layer(layer, (1, KV_LORA)),
            _of_layer(layer, (KV_LORA, ATTN_HEADS * (QK_NOPE + V_DIM))),
            _of_layer(layer, (1, GM_WIDTH)),
            _of_layer(layer, (GM_GROUPS, CHUNK, CHUNK)),
            _of_layer(layer, (CHUNK, GM_WIDTH)),
            _of_layer(layer, (1, GM_WIDTH)),
        ],
        out_specs=(
            pl.BlockSpec((tm, ATTN_HEADS * HEAD_PAD), row),
            pl.BlockSpec((tm, ATTN_HEADS * HEAD_PAD), row),
            pl.BlockSpec((ATTN_HEADS * V_EXT, tm), lambda i: (0, i)),
            pl.BlockSpec((tm, GM_WIDTH), row),
            pl.BlockSpec((tm, KV_LORA), row),
            pl.BlockSpec((tm, QK_ROPE), row),
        ) + extra_spec,
        compiler_params=pltpu.CompilerParams(
            dimension_semantics=("arbitrary",), vmem_limit_bytes=VMEM_LIMIT),
        name="pre_mixer",
    )(*layer_input, mod, rope_tab, g1, win, qg, wqb, kvg, wkvb, gmg, ws, bs, ogm)


def _cache_kv_kernel(ckv_ref, kr_ref, wkvb_ref, k_ref, vt_ref):
    kv = _dot(ckv_ref[0, 0].astype(BF16), wkvb_ref[0])
    kr = kr_ref[0, 0].astype(BF16)
    for hd in range(ATTN_HEADS):
        k_ref[0, 0, :, hd * HEAD_PAD:hd * HEAD_PAD + QK_NOPE] = (
            kv[:, hd * QK_NOPE:(hd + 1) * QK_NOPE].astype(BF16))
        k_ref[0, 0, :, hd * HEAD_PAD + QK_NOPE:(hd + 1) * HEAD_PAD] = kr
    _store_values_t(vt_ref.at[0, 0], kv[:, ATTN_HEADS * QK_NOPE:])


def _cache_kv(cache_ckv, cache_kr_pad, wkvb):
    blk = lambda w: pl.BlockSpec((1, 1, PAST_LEN, w), lambda l, b: (b, l, 0, 0))
    return pl.pallas_call(
        _cache_kv_kernel,
        out_shape=(
            jax.ShapeDtypeStruct((DEC_BATCH, DEPTH, PAST_LEN, ATTN_HEADS * HEAD_PAD), BF16),
            jax.ShapeDtypeStruct((DEC_BATCH, DEPTH, ATTN_HEADS * V_EXT, PAST_LEN), BF16),
        ),
        grid=(DEPTH, DEC_BATCH),
        in_specs=[
            blk(KV_LORA),
            blk(LANES),
            pl.BlockSpec((1, KV_LORA, ATTN_HEADS * (QK_NOPE + V_DIM)), lambda l, b: (l, 0, 0)),
        ],
        out_specs=(blk(ATTN_HEADS * HEAD_PAD),
                   pl.BlockSpec((1, 1, ATTN_HEADS * V_EXT, PAST_LEN), lambda l, b: (b, l, 0, 0))),
        compiler_params=pltpu.CompilerParams(
            dimension_semantics=("arbitrary", "arbitrary"), vmem_limit_bytes=VMEM_LIMIT),
        name="cache_kv",
    )(cache_ckv, cache_kr_pad, wkvb)


def _attn_scores(qh, key_blocks):
    return [_dot_nt(kb, qh) for kb in key_blocks]


def _attn_values(s, vt_blocks):
    s = [si.astype(BF16) for si in s]
    m = functools.reduce(jnp.maximum, [jnp.max(si, axis=0, keepdims=True) for si in s])
    o_t = sum(_dot(vt, jnp.exp2(si - m)) for vt, si in zip(vt_blocks, s))
    return (o_t[0:V_DIM, :] / o_t[V_DIM:V_DIM + 1, :]).T


def _ctx_attn_kernel(q_ref, k_ref, vt_ref, o_ref):
    chains = [(slice(sb * SEQ, (sb + 1) * SEQ), hd) for sb in range(CTX_SUB) for hd in range(ATTN_HEADS)]
    hs = lambda hd: slice(hd * HEAD_PAD, (hd + 1) * HEAD_PAD)
    scores = [_attn_scores(q_ref[rows, hs(hd)], [k_ref[rows, hs(hd)]]) for rows, hd in chains]
    for (rows, hd), s in zip(chains, scores):
        out = _attn_values(s, [vt_ref[hd * V_EXT:(hd + 1) * V_EXT, rows]])
        o_ref[rows, hd * V_DIM:(hd + 1) * V_DIM] = out.astype(o_ref.dtype)


def _ctx_attention(q, k, vt):
    blk = lambda w: pl.BlockSpec((CTX_SUB * SEQ, w), lambda b: (b, 0))
    return pl.pallas_call(
        _ctx_attn_kernel,
        out_shape=jax.ShapeDtypeStruct((N_CTX, ATTN_HEADS * V_DIM), BF16),
        grid=(BATCH // CTX_SUB,),
        in_specs=[blk(ATTN_HEADS * HEAD_PAD), blk(ATTN_HEADS * HEAD_PAD),
                  pl.BlockSpec((ATTN_HEADS * V_EXT, CTX_SUB * SEQ), lambda b: (0, b))],
        out_specs=blk(ATTN_HEADS * V_DIM),
        compiler_params=pltpu.CompilerParams(
            dimension_semantics=("arbitrary",), vmem_limit_bytes=VMEM_LIMIT),
        name="ctx_attention",
    )(q, k, vt)


def _lat_attn_kernel(q_ref, k_ref, vt_ref, kc_ref, vct_ref, o_ref):
    hs = [slice(hd * HEAD_PAD, (hd + 1) * HEAD_PAD) for hd in range(HEADS_PER_STEP)]
    vs = [slice(hd * V_DIM, (hd + 1) * V_DIM) for hd in range(HEADS_PER_STEP)]
    ve = [slice(hd * V_EXT, (hd + 1) * V_EXT) for hd in range(HEADS_PER_STEP)]

    def scores(hd):
        return _attn_scores(q_ref[:, hs[hd]], [kc_ref[0, 0, :, hs[hd]], k_ref[:, hs[hd]]])

    s = scores(0)
    for hd in range(HEADS_PER_STEP):
        s_next = scores(hd + 1) if hd + 1 < HEADS_PER_STEP else None
        o_ref[:, vs[hd]] = _attn_values(s, [vct_ref[0, 0, ve[hd], :], vt_ref[ve[hd], :]]).astype(o_ref.dtype)
        s = s_next


def _lat_attention(q, k, vt, kc, vct, layer):
    nq = DEC_SEQ // TQ
    ctx_q_tiles = N_CTX // TQ
    ctx_kv_blocks = N_CTX // DEC_SEQ
    qk_w = HEADS_PER_STEP * HEAD_PAD
    v_w = HEADS_PER_STEP * V_DIM
    vt_w = HEADS_PER_STEP * V_EXT
    return pl.pallas_call(
        _lat_attn_kernel,
        out_shape=jax.ShapeDtypeStruct((N_LAT, ATTN_HEADS * V_DIM), BF16),
        grid=(DEC_BATCH, ATTN_HEADS // HEADS_PER_STEP, nq),
        in_specs=[
            pl.BlockSpec((TQ, qk_w), lambda b, h, i: (ctx_q_tiles + b * nq + i, h)),
            pl.BlockSpec((DEC_SEQ, qk_w), lambda b, h, i: (ctx_kv_blocks + b, h)),
            pl.BlockSpec((vt_w, DEC_SEQ), lambda b, h, i: (h, ctx_kv_blocks + b)),
            pl.BlockSpec((1, 1, PAST_LEN, qk_w), lambda b, h, i: (b, layer, 0, h)),
            pl.BlockSpec((1, 1, vt_w, PAST_LEN), lambda b, h, i: (b, layer, h, 0)),
        ],
        out_specs=pl.BlockSpec((TQ, v_w), lambda b, h, i: (b * nq + i, h)),
        compiler_params=pltpu.CompilerParams(
            dimension_semantics=("arbitrary", "arbitrary", "arbitrary"),
            vmem_limit_bytes=VMEM_LIMIT),
        name="lat_attention",
    )(q, k, vt, kc, vct)


def _group_peer(x, row, d, width, period):
    step = d * width
    ahead = pltpu.roll(x, N_EXPERTS - step, 0)
    wraps = (row & (period - 1)) + step >= period
    if period == N_EXPERTS:
        return ahead, wraps
    return jnp.where(wraps, pltpu.roll(x, period - step, 0), ahead), wraps


def _route(logits_t, bias):
    row = lax.broadcasted_iota(jnp.int32, logits_t.shape, 0)
    s = 1.0 / (1.0 + jnp.exp(-logits_t))
    sb = s + bias
    rank = jnp.zeros(sb.shape, jnp.int32)
    for d in range(1, EXPERTS_PER_GROUP):
        o, wraps = _group_peer(sb, row, d, 1, EXPERTS_PER_GROUP)
        beats = (o > sb) | ((o == sb) & wraps)
        rank = rank + beats.astype(jnp.int32)
    top2 = rank < 2
    t = jnp.where(top2, sb, 0.0)
    gscore = t
    for d in range(1, EXPERTS_PER_GROUP):
        gscore = gscore + _group_peer(t, row, d, 1, EXPERTS_PER_GROUP)[0]
    grank = jnp.zeros(sb.shape, jnp.int32)
    for d in range(1, N_GROUPS):
        o, wraps = _group_peer(gscore, row, d, EXPERTS_PER_GROUP, N_EXPERTS)
        beats = (o > gscore) | ((o == gscore) & wraps)
        grank = grank + beats.astype(jnp.int32)
    in_group = grank == 0
    w = jnp.where(top2 & in_group, s, 0.0)
    denom = jnp.sum(w, axis=0, keepdims=True)
    group_flag = jnp.where(in_group & ((row & (EXPERTS_PER_GROUP - 1)) == 0), 1.0, 0.0)
    return w / denom, group_flag


def _experts_to_lanes(x_t):
    pad = jnp.zeros((LANES - N_EXPERTS, x_t.shape[1]), x_t.dtype)
    return jnp.concatenate([x_t, pad], axis=0).T


def _split_bf16(x, terms):
    out = []
    for _ in range(terms - 1):
        t = x.astype(BF16)
        out.append(t)
        x = x - t.astype(F32)
    out.append(x.astype(BF16))
    return out


def _post_kernel(ac_ref, al_ref, gm_ref, xc_ref, xl_ref, mod_ref, oag_ref, wout_ref, g2_ref,
                 wr_ref, rb_ref, x1_ref, h2s_ref, y0_ref, pos_ref, cnt_ref):
    mod = mod_ref[0]
    gate1 = mod[:, 2 * D_MODEL:3 * D_MODEL]
    shift2 = mod[:, 3 * D_MODEL:4 * D_MODEL]
    scale2 = mod[:, 4 * D_MODEL:5 * D_MODEL]
    an = (_rms(_ctx_or_lat(ac_ref, al_ref).astype(F32)) * oag_ref[...]).astype(BF16)
    mixed = _dot(jnp.concatenate([an, gm_ref[...]], axis=1), wout_ref[...])
    x1 = _ctx_or_lat(xc_ref, xl_ref) + gate1 * mixed
    x1_ref[...] = x1
    h2 = _rms(x1) * g2_ref[...] * (1.0 + scale2) + shift2
    y0_ref[...] = jnp.zeros(y0_ref.shape, y0_ref.dtype)
    for sub in range(POST_SUB):
        rows = slice(sub * TM_POST, (sub + 1) * TM_POST)
        ls_rows = slice(sub * LS_ROWS, (sub + 1) * LS_ROWS)
        _route_and_sort(h2[rows], wr_ref, rb_ref, h2s_ref.at[ls_rows], pos_ref.at[rows],
                        cnt_ref.at[sub])


def _route_and_sort(h2, wr_ref, rb_ref, h2s_ref, pos_ref, cnt_ref):
    tm = TM_POST
    h2_hi, h2_lo = _split_bf16(h2, 2)
    t = _dot(jnp.concatenate([h2_hi, h2_lo], axis=0), wr_ref[...])
    logits = (t[:tm, :LANES] + t[tm:, :LANES]) + (t[:tm, LANES:] + t[tm:, LANES:])
    comb_t, flag_t = _route(logits.T[0:N_EXPERTS, :], rb_ref[...])
    comb = _experts_to_lanes(comb_t)
    flag = _experts_to_lanes(flag_t)

    r_i = lax.broadcasted_iota(jnp.int32, (tm, tm), 0)
    c_i = lax.broadcasted_iota(jnp.int32, (tm, tm), 1)
    before = jnp.where(c_i < r_i, 1.0, 0.0).astype(BF16)
    rank = _dot(before, flag.astype(BF16))
    count = jnp.sum(flag, axis=0, keepdims=True)
    pieces = jnp.floor((count + (PIECE - 1)) * (1.0 / PIECE))
    start = (pltpu.roll(pieces, EXPERTS_PER_GROUP, 1) + pltpu.roll(pieces, 2 * EXPERTS_PER_GROUP, 1)
             + pltpu.roll(pieces, 3 * EXPERTS_PER_GROUP, 1)) * PIECE
    pos = jnp.sum(flag * (start + rank), axis=-1, keepdims=True)
    pos_ref[...] = pos
    cnt_ref[...] = count.astype(jnp.int32)
    pos_row = jnp.transpose(jnp.broadcast_to(pos, (tm, LANES)))[0:1, :].astype(jnp.int32)
    place = lax.broadcasted_iota(jnp.int32, (LS_ROWS, tm), 0) == pos_row
    place = jnp.where(place, 1.0, 0.0).astype(BF16)
    wide = jnp.concatenate([h2_hi] + _split_bf16(comb, COMB_TERMS), axis=1)
    h2s_ref[...] = _dot(place, wide).astype(BF16)


def _post_mixer(layer, attn_ctx, attn_lat, gm, x_ctx, x_lat, mod, oag, wout, g2, wr, rb):
    joint_x = x_ctx is x_lat
    tm = POST_SUB * TM_POST
    ls = POST_SUB * LS_ROWS
    row = lambda i: (i, 0)
    return pl.pallas_call(
        _post_kernel,
        out_shape=(
            jax.ShapeDtypeStruct((N_ROWS, D_MODEL), F32),
            jax.ShapeDtypeStruct((N_LS, XS_COLS), BF16),
            jax.ShapeDtypeStruct((N_LS, D_MODEL), BF16),
            jax.ShapeDtypeStruct((N_ROWS, 1), F32),
            jax.ShapeDtypeStruct((N_POST_TILES, 1, LANES), jnp.int32),
        ),
        grid=(N_ROWS // tm,),
        in_specs=_ctx_lat_specs(tm, ATTN_HEADS * V_DIM) + [
            pl.BlockSpec((tm, GM_WIDTH), row),
        ] + _ctx_lat_specs(tm, D_MODEL, joint=joint_x) + [
            _mod_spec(layer, tm),
            _of_layer(layer, (1, ATTN_HEADS * V_DIM)),
            _of_layer(layer, (D_MODEL, D_MODEL)),
            _of_layer(layer, (1, D_MODEL)),
            _full((D_MODEL, 2 * LANES)),
            _full((N_EXPERTS, 1)),
        ],
        out_specs=(
            pl.BlockSpec((tm, D_MODEL), row),
            pl.BlockSpec((ls, XS_COLS), row),
            pl.BlockSpec((ls, D_MODEL), row),
            pl.BlockSpec((tm, 1), row),
            pl.BlockSpec((POST_SUB, 1, LANES), lambda i: (i, 0, 0)),
        ),
        compiler_params=pltpu.CompilerParams(
            dimension_semantics=("arbitrary",), vmem_limit_bytes=VMEM_LIMIT),
        name="post_mixer",
    )(attn_ctx, attn_lat, gm, x_ctx, x_lat, mod, oag, wout, g2, wr, rb)


def _plan_kernel(cnt_ref, tg_ref, tn_ref, src_ref, nu_ref):
    def clear_src(j, c):
        for u in range(E_PIECES):
            src_ref[j * E_PIECES + u] = 0
        return c

    lax.fori_loop(0, SRC_SLOTS // E_PIECES, clear_src, 0)

    def clear_tile(j, c):
        tg_ref[j] = N_GROUPS - 1
        tn_ref[j] = 0
        return c

    lax.fori_loop(0, N_ETILES, clear_tile, 0)

    def n_pieces(i, g):
        return lax.shift_right_logical(cnt_ref[i, g * EXPERTS_PER_GROUP] + (PIECE - 1), PIECE_SHIFT)

    t = jnp.int32(0)
    for g in range(N_GROUPS):
        def tile_body(i, s, g=g):
            first = i * LS_PIECES
            for gp in range(g):
                first = first + n_pieces(i, gp)

            for p in range(TM_POST // PIECE):
                src_ref[s + p] = first + p
            return s + n_pieces(i, g)

        s0 = t * E_PIECES
        s1 = lax.fori_loop(0, N_POST_TILES, tile_body, s0)
        n = s1 - s0
        tiles = lax.shift_right_logical(n + (E_PIECES - 1), E_SHIFT)

        def mark_tile(u, c, g=g, n=n, t=t):
            tg_ref[t + u] = g
            tn_ref[t + u] = jnp.minimum(n - u * E_PIECES, E_PIECES)
            return c

        lax.fori_loop(0, tiles, mark_tile, 0)
        t = t + tiles
    nu_ref[0] = t


def _plan(counts):
    smem = pl.BlockSpec(memory_space=pltpu.SMEM)
    return pl.pallas_call(
        _plan_kernel,
        out_shape=(
            jax.ShapeDtypeStruct((N_ETILES,), jnp.int32),
            jax.ShapeDtypeStruct((N_ETILES,), jnp.int32),
            jax.ShapeDtypeStruct((SRC_SLOTS,), jnp.int32),
            jax.ShapeDtypeStruct((1,), jnp.int32),
        ),
        in_specs=[smem],
        out_specs=(smem, smem, smem, smem),
        name="expert_plan",
    )(counts)


GATHER_X, SCATTER_Y = 0, 1


def _moe_kernel(tg_ref, tn_ref, src_ref, nu_ref, h2s_hbm, y0_hbm, wg32_ref, wu32_ref, wd32_ref,
                wsg32_ref, wsu32_ref, wsd32_ref, y_hbm, xbuf, ybuf, wg_ref, wu_ref, wd_ref,
                wsg_ref, wsu_ref, wsd_ref, sem):
    del y0_hbm
    j = pl.program_id(0)
    n_used = nu_ref[0]
    slot = lax.rem(j, 2)

    @pl.when(j == 0)
    def _():
        wsg_ref[...] = wsg32_ref[0].astype(BF16)
        wsu_ref[...] = wsu32_ref[0].astype(BF16)
        wsd_ref[...] = wsd32_ref[0].astype(BF16)

    @pl.when((j == 0) | (tg_ref[j] != tg_ref[jnp.maximum(j - 1, 0)]))
    def _():
        for k in range(EXPERTS_PER_GROUP):
            wg_ref[k] = wg32_ref[0, k].astype(BF16)
            wu_ref[k] = wu32_ref[0, k].astype(BF16)
            wd_ref[k] = wd32_ref[0, k].astype(BF16)

    def piece_rows(t, k):
        hbm_rows = pl.ds(pl.multiple_of(src_ref[t * E_PIECES + k] * PIECE, PIECE), PIECE)
        buf_rows = pl.ds(pl.multiple_of(k * PIECE, PIECE), PIECE)
        return hbm_rows, buf_rows

    def gather_copies(t, k, slot):
        hbm_rows, buf_rows = piece_rows(t, k)
        return (
            pltpu.make_async_copy(h2s_hbm.at[hbm_rows], xbuf.at[slot, buf_rows], sem.at[GATHER_X, slot]),
        )

    def scatter_copies(t, k, slot):
        hbm_rows, buf_rows = piece_rows(t, k)
        return (
            pltpu.make_async_copy(ybuf.at[slot, buf_rows], y_hbm.at[hbm_rows], sem.at[SCATTER_Y, slot]),
        )

    def for_pieces(t, slot, copies, action):
        def body(k, c):
            for cp in copies(t, k, slot):
                action(cp)
            return c
        lax.fori_loop(0, tn_ref[t], body, 0)

    start = lambda cp: cp.start()
    wait = lambda cp: cp.wait()

    @pl.when(j == 0)
    def _():
        xbuf[...] = jnp.zeros(xbuf.shape, xbuf.dtype)
        for_pieces(0, 0, gather_copies, start)

    @pl.when(j + 1 < n_used)
    def _():
        for_pieces(j + 1, 1 - slot, gather_copies, start)

    @pl.when(j < n_used)
    def _():
        for_pieces(j, slot, gather_copies, wait)
        x = xbuf[slot, :, 0:D_MODEL]
        comb = sum(xbuf[slot, :, D_MODEL + t * LANES:D_MODEL + (t + 1) * LANES].astype(F32)
                   for t in range(COMB_TERMS))
        lane = lax.broadcasted_iota(jnp.int32, (TM_E, LANES), 1)
        first = tg_ref[j] * EXPERTS_PER_GROUP
        y = _dot((_silu(_dot(x, wsg_ref[...])) * _dot(x, wsu_ref[...])).astype(BF16), wsd_ref[...])
        for k in range(EXPERTS_PER_GROUP):
            wk = jnp.sum(jnp.where(lane == first + k, comb, 0.0), axis=-1, keepdims=True)
            act = _silu(_dot(x, wg_ref[k])) * _dot(x, wu_ref[k]) * wk
            y = y + _dot(act.astype(BF16), wd_ref[k])
        ybuf[slot] = y.astype(BF16)
        for_pieces(j, slot, scatter_copies, start)

    @pl.when((j >= 1) & (j < n_used))
    def _():
        for_pieces(j - 1, 1 - slot, scatter_copies, wait)

    @pl.when(j == n_used - 1)
    def _():
        for_pieces(j, slot, scatter_copies, wait)


def _experts(plan, h2s, y0, layer, w_gate, w_up, w_down, ws_gate, ws_up, ws_down):
    tg, tn, src, nu = plan
    any_spec = pl.BlockSpec(memory_space=pl.ANY)
    group = lambda j, tg, tn, src, nu: (layer, tg[j], 0, 0)
    whole = lambda j, tg, tn, src, nu: (layer, 0, 0)
    g = EXPERTS_PER_GROUP
    return pl.pallas_call(
        _moe_kernel,
        out_shape=jax.ShapeDtypeStruct((N_LS, D_MODEL), BF16),
        grid_spec=pltpu.PrefetchScalarGridSpec(
            num_scalar_prefetch=4,
            grid=(N_ETILES,),
            in_specs=[
                any_spec, any_spec,
                pl.BlockSpec((1, g, D_MODEL, EXPERT_FF), group),
                pl.BlockSpec((1, g, D_MODEL, EXPERT_FF), group),
                pl.BlockSpec((1, g, EXPERT_FF, D_MODEL), group),
                pl.BlockSpec((1, D_MODEL, SHARED_FF), whole),
                pl.BlockSpec((1, D_MODEL, SHARED_FF), whole),
                pl.BlockSpec((1, SHARED_FF, D_MODEL), whole),
            ],
            out_specs=any_spec,
            scratch_shapes=[
                pltpu.VMEM((2, TM_E, XS_COLS), BF16),
                pltpu.VMEM((2, TM_E, D_MODEL), BF16),
                pltpu.VMEM((g, D_MODEL, EXPERT_FF), BF16),
                pltpu.VMEM((g, D_MODEL, EXPERT_FF), BF16),
                pltpu.VMEM((g, EXPERT_FF, D_MODEL), BF16),
                pltpu.VMEM((D_MODEL, SHARED_FF), BF16),
                pltpu.VMEM((D_MODEL, SHARED_FF), BF16),
                pltpu.VMEM((SHARED_FF, D_MODEL), BF16),
                pltpu.SemaphoreType.DMA((2, 2)),
            ],
        ),
        input_output_aliases={5: 0},
        compiler_params=pltpu.CompilerParams(
            dimension_semantics=("arbitrary",), vmem_limit_bytes=VMEM_LIMIT),
        name="experts",
    )(tg, tn, src, nu, h2s, y0, w_gate, w_up, w_down, ws_gate, ws_up, ws_down)


def _unsort(y_ref, pos_ref, sub):
    pos = pos_ref[sub * TM_POST:(sub + 1) * TM_POST, :].astype(jnp.int32)
    pick = lax.broadcasted_iota(jnp.int32, (TM_POST, LS_ROWS), 1) == pos
    return _dot(jnp.where(pick, 1.0, 0.0).astype(BF16), y_ref[sub * LS_ROWS:(sub + 1) * LS_ROWS, :])


def _combine_kernel(y_ref, pos_ref, x1_ref, mod_ref, fg_ref, o_ref):
    gate2 = mod_ref[0][:, 5 * D_MODEL:6 * D_MODEL]
    for sub in range(COMBINE_SUB):
        rows = slice(sub * TM_POST, (sub + 1) * TM_POST)
        x2 = x1_ref[rows, :] + gate2 * _unsort(y_ref, pos_ref, sub)
        o_ref[rows, :] = _rms(x2) * fg_ref[...]


def _combine(y, pos, x1, mod, fg, first_tile, n_tiles):
    tm = COMBINE_SUB * TM_POST
    row = lambda i: (first_tile + i, 0)
    return pl.pallas_call(
        _combine_kernel,
        out_shape=jax.ShapeDtypeStruct((n_tiles * tm, D_MODEL), F32),
        grid=(n_tiles,),
        in_specs=[
            pl.BlockSpec((COMBINE_SUB * LS_ROWS, D_MODEL), row),
            pl.BlockSpec((tm, 1), row),
            pl.BlockSpec((tm, D_MODEL), row),
            _mod_spec(DEPTH - 1, tm, first_tile),
            _full((1, D_MODEL)),
        ],
        out_specs=pl.BlockSpec((tm, D_MODEL), lambda i: (i, 0)),
        compiler_params=pltpu.CompilerParams(
            dimension_semantics=("arbitrary",), vmem_limit_bytes=VMEM_LIMIT),
        name="combine",
    )(y, pos, x1, mod, fg)


def _rope_table():
    rows = DEC_SEQ // GRID_W
    row = jnp.repeat(jnp.arange(rows, dtype=F32), GRID_W)
    col = jnp.tile(jnp.arange(GRID_W, dtype=F32), rows)
    half = QK_ROPE // 2
    freqs = 1.0 / (ROPE_BASE ** (jnp.arange(0, half, 2, dtype=F32) / half))
    ang = jnp.concatenate([row[:, None] * freqs, col[:, None] * freqs], axis=-1)
    cos, sin = jnp.cos(ang), jnp.sin(ang)
    lat = jnp.concatenate([cos, cos, -sin, sin], axis=-1)
    ident = jnp.concatenate([jnp.ones((TM_PRE, QK_ROPE), F32), jnp.zeros((TM_PRE, QK_ROPE), F32)], axis=-1)
    return jnp.concatenate([ident, lat], axis=0)


_DEINT = np.concatenate([np.arange(0, QK_ROPE, 2), np.arange(1, QK_ROPE, 2)])
_SWAP = np.concatenate([np.arange(1, QK_ROPE, 2), np.arange(0, QK_ROPE, 2)])
_INTERLEAVE = np.argsort(_DEINT)


def _layout_w_in(w_in):
    cq_ckv = w_in[..., :Q_LORA + KV_LORA]
    kr = w_in[..., Q_LORA + KV_LORA:Q_LORA + KV_LORA + QK_ROPE]
    uv = w_in[..., Q_LORA + KV_LORA + QK_ROPE:]
    return jnp.concatenate([cq_ckv, kr[..., _DEINT], kr[..., _SWAP], uv], axis=-1).astype(BF16)


def _layout_w_qb(w_qb):
    w = (w_qb * (ATTN_SCALE * LOG2E)).reshape(DEPTH, Q_LORA, ATTN_HEADS, QK_NOPE + QK_ROPE)
    nope, rope = w[..., :QK_NOPE], w[..., QK_NOPE:]
    w = jnp.concatenate([nope, rope[..., _DEINT], rope[..., _SWAP]], axis=-1)
    return w.reshape(DEPTH, Q_LORA, ATTN_HEADS * HEAD_PAD).astype(BF16)


def _layout_w_kvb(w_kvb):
    w = w_kvb.reshape(DEPTH, KV_LORA, ATTN_HEADS, QK_NOPE + V_DIM)
    k = w[..., :QK_NOPE].reshape(DEPTH, KV_LORA, ATTN_HEADS * QK_NOPE)
    v = w[..., QK_NOPE:].reshape(DEPTH, KV_LORA, ATTN_HEADS * V_DIM)
    return jnp.concatenate([k, v], axis=-1).astype(BF16)


def kernel(x_prompt, x_sample, cache_ckv, cache_krope, c, c_ctx, norm1_g, w_ada, b_ada, w_in,
           q_norm_g, w_qb, kv_norm_g, w_kvb, gm_norm_g, w_spatial, b_spatial, onorm_attn_g,
           onorm_gm_g, w_out, norm2_g, w_router, router_bias, w_gate, w_up, w_down, ws_gate,
           ws_up, ws_down, final_norm_g):
    x_ctx = x_prompt.reshape(N_CTX, D_MODEL)
    x_lat = x_sample.reshape(N_LAT, D_MODEL)
    cond = jnp.concatenate([c_ctx[None, :], c, jnp.zeros((COND_PAD - N_COND, D_MODEL), F32)], axis=0)
    mod = _modulation(cond, w_ada, b_ada).reshape(DEPTH, COND_PAD, 1, 6 * D_MODEL)
    rope_tab = _rope_table()

    wkvb = _layout_w_kvb(w_kvb)
    cache_kr = jnp.pad(cache_krope[..., _DEINT], ((0, 0), (0, 0), (0, 0), (0, LANES - QK_ROPE)))
    kc, vct = _cache_kv(cache_ckv, cache_kr, wkvb)

    wr = jnp.pad(w_router, ((0, 0), (0, LANES - N_EXPERTS)))
    wr_hi = wr.astype(BF16)
    wr = jnp.concatenate([wr_hi, (wr - wr_hi.astype(F32)).astype(BF16)], axis=1)
    rb = router_bias.reshape(N_EXPERTS, 1)
    fg = final_norm_g.reshape(1, D_MODEL)

    row_stack = lambda g: g.reshape(DEPTH, 1, -1)
    bs = jnp.broadcast_to(jnp.swapaxes(b_spatial, 1, 2)[..., None],
                          (DEPTH, CHUNK, GM_GROUPS, GM_GROUP_DIM)).reshape(DEPTH, CHUNK, GM_WIDTH)
    pre_params = (row_stack(norm1_g), _layout_w_in(w_in), row_stack(q_norm_g), _layout_w_qb(w_qb),
                  row_stack(kv_norm_g), wkvb, row_stack(gm_norm_g), w_spatial.astype(BF16), bs,
                  row_stack(onorm_gm_g))
    post_params = (row_stack(onorm_attn_g), w_out.astype(BF16), row_stack(norm2_g), wr, rb)

    ckv_out, kr_out = [], []
    layer_input = (x_ctx, x_lat)
    for l in range(DEPTH):
        q, k, vt, gm, ckv_n, kr, *formed = _pre_mixer(l, layer_input, mod, rope_tab, *pre_params)
        ckv_out.append(ckv_n[:N_CTX].reshape(BATCH, SEQ, KV_LORA))
        kr_out.append(kr[:N_CTX][:, _INTERLEAVE].reshape(BATCH, SEQ, QK_ROPE))
        x_pair = (formed[0], formed[0]) if formed else layer_input
        x1, h2s, y0, pos, counts = _post_mixer(
            l, _ctx_attention(q, k, vt), _lat_attention(q, k, vt, kc, vct, l), gm, *x_pair, mod,
            *post_params)
        plan = _plan(counts.reshape(N_POST_TILES, LANES))
        y = _experts(plan, h2s, y0, l, w_gate, w_up, w_down, ws_gate, ws_up, ws_down)
        layer_input = (y, pos, x1)

    n_ctx_tiles = N_CTX // (COMBINE_SUB * TM_POST)
    n_lat_tiles = N_LAT // (COMBINE_SUB * TM_POST)
    x_ctx = _combine(y, pos, x1, mod, fg, 0, n_ctx_tiles)
    x_lat = _combine(y, pos, x1, mod, fg, n_ctx_tiles, n_lat_tiles)
    y_prompt = x_ctx.reshape(BATCH, SEQ, D_MODEL)
    y_sample = x_lat.reshape(DEC_BATCH, DEC_SEQ, D_MODEL)
    return y_prompt, y_sample, jnp.stack(ckv_out, axis=1), jnp.stack(kr_out, axis=1)
```

```python
import functools
import math

import jax
import jax.numpy as jnp
import numpy as np
from jax import lax
from jax.experimental import pallas as pl
from jax.experimental.pallas import tpu as pltpu

D_MODEL = 1024
BATCH = 16
SEQ = 256
DEPTH = 2
DEC_BATCH = 4
DEC_SEQ = 4096
PAST_LEN = 256
GRID_W = 64
ATTN_HEADS = 4
QK_NOPE = 128
QK_ROPE = 64
V_DIM = 128
V_EXT = V_DIM + 16
Q_LORA = 384
KV_LORA = 256
ATTN_SCALE = (QK_NOPE + QK_ROPE) ** -0.5
ROPE_BASE = 10000.0
CHUNK = 128
GM_WIDTH = 512
GM_GROUPS = 4
GM_GROUP_DIM = 128
N_EXPERTS = 16
N_GROUPS = 4
EXPERTS_PER_GROUP = 4
EXPERT_FF = 256
SHARED_FF = 256
EPS = 1e-6

N_CTX = BATCH * SEQ
N_LAT = DEC_BATCH * DEC_SEQ
N_ROWS = N_CTX + N_LAT
N_COND = 1 + DEC_BATCH
COND_PAD = 16
HEAD_PAD = 256
IN_COLS = Q_LORA + KV_LORA + 2 * QK_ROPE + 2 * GM_WIDTH
LANES = 128
LOG2E = 1.4426950408889634

TM_PRE = 512
TM_POST = 256
COMB_TERMS = 3
XS_COLS = D_MODEL + COMB_TERMS * LANES
COMBINE_SUB = 4
POST_SUB = 4
PIECE = 16
LS_ROWS = TM_POST + N_GROUPS * PIECE
LS_PIECES = LS_ROWS // PIECE
N_POST_TILES = N_ROWS // TM_POST
N_LS = N_POST_TILES * LS_ROWS
TM_E = 512
E_PIECES = TM_E // PIECE
PIECE_SHIFT = PIECE.bit_length() - 1
E_SHIFT = E_PIECES.bit_length() - 1
N_ETILES = -(-(N_POST_TILES * (TM_POST // PIECE + N_GROUPS - 1)) // E_PIECES) + N_GROUPS
SRC_SLOTS = (N_ETILES + 1) * E_PIECES
TQ = 512
CTX_SUB = 2
HEADS_PER_STEP = 4
VMEM_LIMIT = 56 * 1024 * 1024

F32 = jnp.float32
BF16 = jnp.bfloat16


def _rms(x):
    return x * lax.rsqrt(jnp.mean(x * x, axis=-1, keepdims=True) + EPS)


def _gelu(x):
    return 0.5 * x * (1.0 + jnp.tanh(math.sqrt(2.0 / math.pi) * (x + 0.044715 * (x * x * x))))


def _silu(x):
    return x * (1.0 / (1.0 + jnp.exp(-x)))


def _dot(a, b):
    return jnp.dot(a, b, preferred_element_type=F32)


def _dot_nt(a, b):
    return lax.dot_general(a, b, (((1,), (1,)), ((), ())), preferred_element_type=F32)


def _cond_row(i, tm):
    n_ctx_tiles = N_CTX // tm
    per_batch = DEC_SEQ // tm
    return jnp.where(i < n_ctx_tiles, 0, 1 + (i - n_ctx_tiles) // per_batch)


def _rope_block(i, tm):
    n_ctx_tiles = N_CTX // tm
    per_batch = DEC_SEQ // tm
    return jnp.where(i < n_ctx_tiles, 0, 1 + (i - n_ctx_tiles) % per_batch)


def _ctx_lat_specs(tm, width, joint=False):
    n_ctx_tiles = N_CTX // tm
    lat_first = n_ctx_tiles if joint else 0
    return [pl.BlockSpec((tm, width), lambda i: (jnp.minimum(i, n_ctx_tiles - 1), 0)),
            pl.BlockSpec((tm, width), lambda i: (lat_first + jnp.maximum(i - n_ctx_tiles, 0), 0))]


def _ctx_or_lat(ctx_ref, lat_ref):
    n_ctx_tiles = N_CTX // ctx_ref.shape[0]
    return jnp.where(pl.program_id(0) < n_ctx_tiles, ctx_ref[...], lat_ref[...])


def _full(shape):
    n = len(shape)
    return pl.BlockSpec(shape, lambda *_: (0,) * n)


def _of_layer(layer, shape):
    n = len(shape)
    return pl.BlockSpec((None,) + tuple(shape), lambda *_: (layer,) + (0,) * n)


def _mod_spec(layer, tm, first_tile=0):
    return pl.BlockSpec((None, 1, 1, 6 * D_MODEL),
                        lambda i: (layer, _cond_row(first_tile + i, tm), 0, 0))


def _mod_kernel(cond_ref, w_ref, b_ref, o_ref):
    s_hi, s_lo = _split_bf16(_silu(cond_ref[...]), 2)
    w_hi, w_lo = _split_bf16(w_ref[0], 2)
    t = _dot(jnp.concatenate([s_hi, s_lo], axis=0), w_hi)
    o_ref[0] = (t[:COND_PAD] + t[COND_PAD:]) + _dot(s_hi, w_lo) + b_ref[0]


def _modulation(cond, w_ada, b_ada):
    tn = 1536
    return pl.pallas_call(
        _mod_kernel,
        out_shape=jax.ShapeDtypeStruct((DEPTH, COND_PAD, 6 * D_MODEL), F32),
        grid=(DEPTH, 6 * D_MODEL // tn),
        in_specs=[
            pl.BlockSpec((COND_PAD, D_MODEL), lambda l, j: (0, 0)),
            pl.BlockSpec((1, D_MODEL, tn), lambda l, j: (l, 0, j)),
            pl.BlockSpec((1, 1, tn), lambda l, j: (l, 0, j)),
        ],
        out_specs=pl.BlockSpec((1, COND_PAD, tn), lambda l, j: (l, 0, j)),
        compiler_params=pltpu.CompilerParams(
            dimension_semantics=("arbitrary", "arbitrary"), vmem_limit_bytes=VMEM_LIMIT),
        name="modulation",
    )(cond, w_ada, b_ada.reshape(DEPTH, 1, 6 * D_MODEL))


def _store_values_t(vt_ref, v):
    t = v.shape[0]
    v_t = v.T.astype(BF16)
    for hd in range(ATTN_HEADS):
        vt_ref[hd * V_EXT:hd * V_EXT + V_DIM, :] = v_t[hd * V_DIM:(hd + 1) * V_DIM, :]
        vt_ref[hd * V_EXT + V_DIM:(hd + 1) * V_EXT, :] = jnp.ones((V_EXT - V_DIM, t), BF16)


def _pre_kernel(*refs, after_experts):
    if after_experts:
        (y_ref, pos_ref, x1_ref, prev_mod_ref, mod_ref, rope_ref, g1_ref, win_ref, qg_ref, wqb_ref,
         kvg_ref, wkvb_ref, gmg_ref, ws_ref, bs_ref, ogm_ref,
         q_ref, k_ref, vt_ref, gm_ref, ckv_ref, kr_ref, x_ref) = refs
        tm = x1_ref.shape[0]
        gate2 = prev_mod_ref[0][:, 5 * D_MODEL:6 * D_MODEL]
        x = jnp.concatenate([x1_ref[sub * TM_POST:(sub + 1) * TM_POST, :] + gate2 * _unsort(y_ref, pos_ref, sub)
                             for sub in range(tm // TM_POST)], axis=0)
        x_ref[...] = x
    else:
        (xc_ref, xl_ref, mod_ref, rope_ref, g1_ref, win_ref, qg_ref, wqb_ref,
         kvg_ref, wkvb_ref, gmg_ref, ws_ref, bs_ref, ogm_ref,
         q_ref, k_ref, vt_ref, gm_ref, ckv_ref, kr_ref) = refs
        tm = xc_ref.shape[0]
        x = _ctx_or_lat(xc_ref, xl_ref)
    mod = mod_ref[0]
    shift1 = mod[:, 0:D_MODEL]
    scale1 = mod[:, D_MODEL:2 * D_MODEL]
    h = _rms(x) * g1_ref[...] * (1.0 + scale1) + shift1
    y = _dot(h.astype(BF16), win_ref[...])
    cq = y[:, 0:Q_LORA]
    ckv = y[:, Q_LORA:Q_LORA + KV_LORA]
    kr2 = y[:, 640:768]
    u = y[:, 768:768 + GM_WIDTH]
    vv = y[:, 768 + GM_WIDTH:768 + 2 * GM_WIDTH]

    rope = rope_ref[...]
    lane = lax.broadcasted_iota(jnp.int32, (tm, LANES), 1)

    kr_ref[...] = kr2[:, 0:QK_ROPE]
    t = kr2 * rope
    k_rot = jnp.where(lane < QK_ROPE, t + pltpu.roll(t, QK_ROPE, 1), 0.0).astype(BF16)

    ckv_n = _rms(ckv) * kvg_ref[...]
    ckv_ref[...] = ckv_n
    kv = _dot(ckv_n.astype(BF16), wkvb_ref[...])
    for hd in range(ATTN_HEADS):
        k_ref[:, hd * HEAD_PAD:hd * HEAD_PAD + QK_NOPE] = (
            kv[:, hd * QK_NOPE:(hd + 1) * QK_NOPE].astype(BF16))
        k_ref[:, hd * HEAD_PAD + QK_NOPE:(hd + 1) * HEAD_PAD] = k_rot
    _store_values_t(vt_ref, kv[:, ATTN_HEADS * QK_NOPE:])

    q = _dot((_rms(cq) * qg_ref[...]).astype(BF16), wqb_ref[...])
    for hd in range(ATTN_HEADS):
        q_ref[:, hd * HEAD_PAD:hd * HEAD_PAD + QK_NOPE] = (
            q[:, hd * HEAD_PAD:hd * HEAD_PAD + QK_NOPE].astype(BF16))
        t = q[:, hd * HEAD_PAD + QK_NOPE:(hd + 1) * HEAD_PAD] * rope
        q_ref[:, hd * HEAD_PAD + QK_NOPE:(hd + 1) * HEAD_PAD] = (
            t + pltpu.roll(t, QK_ROPE, 1)).astype(BF16)

    ug = _gelu(u)
    vg = _gelu(vv)
    cols = []
    for g in range(GM_GROUPS):
        sl = slice(g * GM_GROUP_DIM, (g + 1) * GM_GROUP_DIM)
        vn = (_rms(vg[:, sl]) * gmg_ref[:, sl]).astype(BF16)
        rows = []
        for c in range(tm // CHUNK):
            sv = _dot(ws_ref[g], vn[c * CHUNK:(c + 1) * CHUNK]) + bs_ref[:, sl]
            rows.append(ug[c * CHUNK:(c + 1) * CHUNK, sl] * sv)
        cols.append(jnp.concatenate(rows, axis=0))
    gm = jnp.concatenate(cols, axis=1)
    gm_ref[...] = (_rms(gm) * ogm_ref[...]).astype(BF16)


def _pre_mixer(layer, layer_input, mod, rope_tab, g1, win, qg, wqb, kvg, wkvb, gmg, ws, bs, ogm):
    tm = TM_PRE
    row = lambda i: (i, 0)
    mod_spec = _mod_spec(layer, tm)
    after_experts = len(layer_input) == 3
    if after_experts:
        layer_input = (*layer_input, mod)
        input_specs = [pl.BlockSpec((tm // TM_POST * LS_ROWS, D_MODEL), row),
                       pl.BlockSpec((tm, 1), row), pl.BlockSpec((tm, D_MODEL), row),
                       _mod_spec(layer - 1, tm)]
        extra_shape = (jax.ShapeDtypeStruct((N_ROWS, D_MODEL), F32),)
        extra_spec = (pl.BlockSpec((tm, D_MODEL), row),)
    else:
        input_specs = _ctx_lat_specs(tm, D_MODEL)
        extra_shape = extra_spec = ()
    return pl.pallas_call(
        functools.partial(_pre_kernel, after_experts=after_experts),
        out_shape=(
            jax.ShapeDtypeStruct((N_ROWS, ATTN_HEADS * HEAD_PAD), BF16),
            jax.ShapeDtypeStruct((N_ROWS, ATTN_HEADS * HEAD_PAD), BF16),
            jax.ShapeDtypeStruct((ATTN_HEADS * V_EXT, N_ROWS), BF16),
            jax.ShapeDtypeStruct((N_ROWS, GM_WIDTH), BF16),
            jax.ShapeDtypeStruct((N_ROWS, KV_LORA), F32),
            jax.ShapeDtypeStruct((N_ROWS, QK_ROPE), F32),
        ) + extra_shape,
        grid=(N_ROWS // tm,),
        in_specs=input_specs + [
            mod_spec,
            pl.BlockSpec((tm, LANES), lambda i: (_rope_block(i, tm), 0)),
            _of_layer(layer, (1, D_MODEL)),
            _of_layer(layer, (D_MODEL, IN_COLS)),
            _of_layer(layer, (1, Q_LORA)),
            _of_layer(layer, (Q_LORA, ATTN_HEADS * HEAD_PAD)),
            _of_layer(layer, (1, KV_LORA)),
            _of_layer(layer, (KV_LORA, ATTN_HEADS * (QK_NOPE + V_DIM))),
            _of_layer(layer, (1, GM_WIDTH)),
            _of_layer(layer, (GM_GROUPS, CHUNK, CHUNK)),
            _of_layer(layer, (CHUNK, GM_WIDTH)),
            _of_layer(layer, (1, GM_WIDTH)),
        ],
        out_specs=(
            pl.BlockSpec((tm, ATTN_HEADS * HEAD_PAD), row),
            pl.BlockSpec((tm, ATTN_HEADS * HEAD_PAD), row),
            pl.BlockSpec((ATTN_HEADS * V_EXT, tm), lambda i: (0, i)),
            pl.BlockSpec((tm, GM_WIDTH), row),
            pl.BlockSpec((tm, KV_LORA), row),
            pl.BlockSpec((tm, QK_ROPE), row),
        ) + extra_spec,
        compiler_params=pltpu.CompilerParams(
            dimension_semantics=("arbitrary",), vmem_limit_bytes=VMEM_LIMIT),
        name="pre_mixer",
    )(*layer_input, mod, rope_tab, g1, win, qg, wqb, kvg, wkvb, gmg, ws, bs, ogm)


def _cache_kv_kernel(ckv_ref, kr_ref, wkvb_ref, k_ref, vt_ref):
    kv = _dot(ckv_ref[0, 0].astype(BF16), wkvb_ref[0])
    kr = kr_ref[0, 0].astype(BF16)
    for hd in range(ATTN_HEADS):
        k_ref[0, 0, :, hd * HEAD_PAD:hd * HEAD_PAD + QK_NOPE] = (
            kv[:, hd * QK_NOPE:(hd + 1) * QK_NOPE].astype(BF16))
        k_ref[0, 0, :, hd * HEAD_PAD + QK_NOPE:(hd + 1) * HEAD_PAD] = kr
    _store_values_t(vt_ref.at[0, 0], kv[:, ATTN_HEADS * QK_NOPE:])


def _cache_kv(cache_ckv, cache_kr_pad, wkvb):
    blk = lambda w: pl.BlockSpec((1, 1, PAST_LEN, w), lambda l, b: (b, l, 0, 0))
    return pl.pallas_call(
        _cache_kv_kernel,
        out_shape=(
            jax.ShapeDtypeStruct((DEC_BATCH, DEPTH, PAST_LEN, ATTN_HEADS * HEAD_PAD), BF16),
            jax.ShapeDtypeStruct((DEC_BATCH, DEPTH, ATTN_HEADS * V_EXT, PAST_LEN), BF16),
        ),
        grid=(DEPTH, DEC_BATCH),
        in_specs=[
            blk(KV_LORA),
            blk(LANES),
            pl.BlockSpec((1, KV_LORA, ATTN_HEADS * (QK_NOPE + V_DIM)), lambda l, b: (l, 0, 0)),
        ],
        out_specs=(blk(ATTN_HEADS * HEAD_PAD),
                   pl.BlockSpec((1, 1, ATTN_HEADS * V_EXT, PAST_LEN), lambda l, b: (b, l, 0, 0))),
        compiler_params=pltpu.CompilerParams(
            dimension_semantics=("arbitrary", "arbitrary"), vmem_limit_bytes=VMEM_LIMIT),
        name="cache_kv",
    )(cache_ckv, cache_kr_pad, wkvb)


def _attn_scores(qh, key_blocks):
    return [_dot_nt(kb, qh) for kb in key_blocks]


def _attn_values(s, vt_blocks):
    s = [si.astype(BF16) for si in s]
    m = functools.reduce(jnp.maximum, [jnp.max(si, axis=0, keepdims=True) for si in s])
    o_t = sum(_dot(vt, jnp.exp2(si - m)) for vt, si in zip(vt_blocks, s))
    return (o_t[0:V_DIM, :] / o_t[V_DIM:V_DIM + 1, :]).T


def _ctx_attn_kernel(q_ref, k_ref, vt_ref, o_ref):
    chains = [(slice(sb * SEQ, (sb + 1) * SEQ), hd) for sb in range(CTX_SUB) for hd in range(ATTN_HEADS)]
    hs = lambda hd: slice(hd * HEAD_PAD, (hd + 1) * HEAD_PAD)
    scores = [_attn_scores(q_ref[rows, hs(hd)], [k_ref[rows, hs(hd)]]) for rows, hd in chains]
    for (rows, hd), s in zip(chains, scores):
        out = _attn_values(s, [vt_ref[hd * V_EXT:(hd + 1) * V_EXT, rows]])
        o_ref[rows, hd * V_DIM:(hd + 1) * V_DIM] = out.astype(o_ref.dtype)


def _ctx_attention(q, k, vt):
    blk = lambda w: pl.BlockSpec((CTX_SUB * SEQ, w), lambda b: (b, 0))
    return pl.pallas_call(
        _ctx_attn_kernel,
        out_shape=jax.ShapeDtypeStruct((N_CTX, ATTN_HEADS * V_DIM), BF16),
        grid=(BATCH // CTX_SUB,),
        in_specs=[blk(ATTN_HEADS * HEAD_PAD), blk(ATTN_HEADS * HEAD_PAD),
                  pl.BlockSpec((ATTN_HEADS * V_EXT, CTX_SUB * SEQ), lambda b: (0, b))],
        out_specs=blk(ATTN_HEADS * V_DIM),
        compiler_params=pltpu.CompilerParams(
            dimension_semantics=("arbitrary",), vmem_limit_bytes=VMEM_LIMIT),
        name="ctx_attention",
    )(q, k, vt)


def _lat_attn_kernel(q_ref, k_ref, vt_ref, kc_ref, vct_ref, o_ref):
    hs = [slice(hd * HEAD_PAD, (hd + 1) * HEAD_PAD) for hd in range(HEADS_PER_STEP)]
    vs = [slice(hd * V_DIM, (hd + 1) * V_DIM) for hd in range(HEADS_PER_STEP)]
    ve = [slice(hd * V_EXT, (hd + 1) * V_EXT) for hd in range(HEADS_PER_STEP)]

    def scores(hd):
        return _attn_scores(q_ref[:, hs[hd]], [kc_ref[0, 0, :, hs[hd]], k_ref[:, hs[hd]]])

    s = scores(0)
    for hd in range(HEADS_PER_STEP):
        s_next = scores(hd + 1) if hd + 1 < HEADS_PER_STEP else None
        o_ref[:, vs[hd]] = _attn_values(s, [vct_ref[0, 0, ve[hd], :], vt_ref[ve[hd], :]]).astype(o_ref.dtype)
        s = s_next


def _lat_attention(q, k, vt, kc, vct, layer):
    nq = DEC_SEQ // TQ
    ctx_q_tiles = N_CTX // TQ
    ctx_kv_blocks = N_CTX // DEC_SEQ
    qk_w = HEADS_PER_STEP * HEAD_PAD
    v_w = HEADS_PER_STEP * V_DIM
    vt_w = HEADS_PER_STEP * V_EXT
    return pl.pallas_call(
        _lat_attn_kernel,
        out_shape=jax.ShapeDtypeStruct((N_LAT, ATTN_HEADS * V_DIM), BF16),
        grid=(DEC_BATCH, ATTN_HEADS // HEADS_PER_STEP, nq),
        in_specs=[
            pl.BlockSpec((TQ, qk_w), lambda b, h, i: (ctx_q_tiles + b * nq + i, h)),
            pl.BlockSpec((DEC_SEQ, qk_w), lambda b, h, i: (ctx_kv_blocks + b, h)),
            pl.BlockSpec((vt_w, DEC_SEQ), lambda b, h, i: (h, ctx_kv_blocks + b)),
            pl.BlockSpec((1, 1, PAST_LEN, qk_w), lambda b, h, i: (b, layer, 0, h)),
            pl.BlockSpec((1, 1, vt_w, PAST_LEN), lambda b, h, i: (b, layer, h, 0)),
        ],
        out_specs=pl.BlockSpec((TQ, v_w), lambda b, h, i: (b * nq + i, h)),
        compiler_params=pltpu.CompilerParams(
            dimension_semantics=("arbitrary", "arbitrary", "arbitrary"),
            vmem_limit_bytes=VMEM_LIMIT),
        name="lat_attention",
    )(q, k, vt, kc, vct)


def _group_peer(x, row, d, width, period):
    step = d * width
    ahead = pltpu.roll(x, N_EXPERTS - step, 0)
    wraps = (row & (period - 1)) + step >= period
    if period == N_EXPERTS:
        return ahead, wraps
    return jnp.where(wraps, pltpu.roll(x, period - step, 0), ahead), wraps


def _route(logits_t, bias):
    row = lax.broadcasted_iota(jnp.int32, logits_t.shape, 0)
    s = 1.0 / (1.0 + jnp.exp(-logits_t))
    sb = s + bias
    rank = jnp.zeros(sb.shape, jnp.int32)
    for d in range(1, EXPERTS_PER_GROUP):
        o, wraps = _group_peer(sb, row, d, 1, EXPERTS_PER_GROUP)
        beats = (o > sb) | ((o == sb) & wraps)
        rank = rank + beats.astype(jnp.int32)
    top2 = rank < 2
    t = jnp.where(top2, sb, 0.0)
    gscore = t
    for d in range(1, EXPERTS_PER_GROUP):
        gscore = gscore + _group_peer(t, row, d, 1, EXPERTS_PER_GROUP)[0]
    grank = jnp.zeros(sb.shape, jnp.int32)
    for d in range(1, N_GROUPS):
        o, wraps = _group_peer(gscore, row, d, EXPERTS_PER_GROUP, N_EXPERTS)
        beats = (o > gscore) | ((o == gscore) & wraps)
        grank = grank + beats.astype(jnp.int32)
    in_group = grank == 0
    w = jnp.where(top2 & in_group, s, 0.0)
    denom = jnp.sum(w, axis=0, keepdims=True)
    group_flag = jnp.where(in_group & ((row & (EXPERTS_PER_GROUP - 1)) == 0), 1.0, 0.0)
    return w / denom, group_flag


def _experts_to_lanes(x_t):
    pad = jnp.zeros((LANES - N_EXPERTS, x_t.shape[1]), x_t.dtype)
    return jnp.concatenate([x_t, pad], axis=0).T


def _split_bf16(x, terms):
    out = []
    for _ in range(terms - 1):
        t = x.astype(BF16)
        out.append(t)
        x = x - t.astype(F32)
    out.append(x.astype(BF16))
    return out


def _post_kernel(ac_ref, al_ref, gm_ref, xc_ref, xl_ref, mod_ref, oag_ref, wout_ref, g2_ref,
                 wr_ref, rb_ref, x1_ref, h2s_ref, y0_ref, pos_ref, cnt_ref):
    mod = mod_ref[0]
    gate1 = mod[:, 2 * D_MODEL:3 * D_MODEL]
    shift2 = mod[:, 3 * D_MODEL:4 * D_MODEL]
    scale2 = mod[:, 4 * D_MODEL:5 * D_MODEL]
    y0_ref[...] = jnp.zeros(y0_ref.shape, y0_ref.dtype)
    tm = xc_ref.shape[0]
    halves = [slice(0, tm // 2), slice(tm // 2, tm)]
    attn = _ctx_or_lat(ac_ref, al_ref)
    x_in = _ctx_or_lat(xc_ref, xl_ref)
    mixed = []
    for rows in halves:
        an = (_rms(attn[rows].astype(F32)) * oag_ref[...]).astype(BF16)
        mixed.append(_dot(jnp.concatenate([an, gm_ref[rows, :]], axis=1), wout_ref[...]))
    h2_hi, logits = [], []
    for rows, mix in zip(halves, mixed):
        x1 = x_in[rows] + gate1 * mix
        x1_ref[rows, :] = x1
        hi, lo = _split_bf16(_rms(x1) * g2_ref[...] * (1.0 + scale2) + shift2, 2)
        t = _dot(jnp.concatenate([hi, lo], axis=0), wr_ref[...])
        half = tm // 2
        logits.append((t[:half, :LANES] + t[half:, :LANES]) + (t[:half, LANES:] + t[half:, LANES:]))
        h2_hi.append(hi)
    h2_hi = jnp.concatenate(h2_hi, axis=0)
    logits = jnp.concatenate(logits, axis=0)
    tiles = [slice(sub * TM_POST, (sub + 1) * TM_POST) for sub in range(POST_SUB)]
    r_i = lax.broadcasted_iota(jnp.int32, (TM_POST, TM_POST), 0)
    c_i = lax.broadcasted_iota(jnp.int32, (TM_POST, TM_POST), 1)
    before = jnp.where(c_i < r_i, 1.0, 0.0).astype(BF16)
    routed = []
    for rows in tiles:
        comb_t, flag_t = _route(logits[rows].T[0:N_EXPERTS, :], rb_ref[...])
        routed.append((_experts_to_lanes(comb_t), _experts_to_lanes(flag_t)))
    ranks = [_dot(before, flag.astype(BF16)) for _, flag in routed]
    for sub, rows in enumerate(tiles):
        comb, flag = routed[sub]
        _sort_tile(h2_hi[rows], comb, flag, ranks[sub],
                   h2s_ref.at[sub * LS_ROWS:(sub + 1) * LS_ROWS], pos_ref.at[rows], cnt_ref.at[sub])


def _sort_tile(h2_hi, comb, flag, rank, h2s_ref, pos_ref, cnt_ref):
    tm = TM_POST
    count = jnp.sum(flag, axis=0, keepdims=True)
    pieces = jnp.floor((count + (PIECE - 1)) * (1.0 / PIECE))
    start = (pltpu.roll(pieces, EXPERTS_PER_GROUP, 1) + pltpu.roll(pieces, 2 * EXPERTS_PER_GROUP, 1)
             + pltpu.roll(pieces, 3 * EXPERTS_PER_GROUP, 1)) * PIECE
    pos = jnp.sum(flag * (start + rank), axis=-1, keepdims=True)
    pos_ref[...] = pos
    cnt_ref[...] = count.astype(jnp.int32)
    pos_row = jnp.transpose(jnp.broadcast_to(pos, (tm, LANES)))[0:1, :].astype(jnp.int32)
    place = lax.broadcasted_iota(jnp.int32, (LS_ROWS, tm), 0) == pos_row
    place = jnp.where(place, 1.0, 0.0).astype(BF16)
    wide = jnp.concatenate([h2_hi] + _split_bf16(comb, COMB_TERMS), axis=1)
    h2s_ref[...] = _dot(place, wide).astype(BF16)


def _post_mixer(layer, attn_ctx, attn_lat, gm, x_ctx, x_lat, mod, oag, wout, g2, wr, rb):
    joint_x = x_ctx is x_lat
    tm = POST_SUB * TM_POST
    ls = POST_SUB * LS_ROWS
    row = lambda i: (i, 0)
    return pl.pallas_call(
        _post_kernel,
        out_shape=(
            jax.ShapeDtypeStruct((N_ROWS, D_MODEL), F32),
            jax.ShapeDtypeStruct((N_LS, XS_COLS), BF16),
            jax.ShapeDtypeStruct((N_LS, D_MODEL), BF16),
            jax.ShapeDtypeStruct((N_ROWS, 1), F32),
            jax.ShapeDtypeStruct((N_POST_TILES, 1, LANES), jnp.int32),
        ),
        grid=(N_ROWS // tm,),
        in_specs=_ctx_lat_specs(tm, ATTN_HEADS * V_DIM) + [
            pl.BlockSpec((tm, GM_WIDTH), row),
        ] + _ctx_lat_specs(tm, D_MODEL, joint=joint_x) + [
            _mod_spec(layer, tm),
            _of_layer(layer, (1, ATTN_HEADS * V_DIM)),
            _of_layer(layer, (D_MODEL, D_MODEL)),
            _of_layer(layer, (1, D_MODEL)),
            _full((D_MODEL, 2 * LANES)),
            _full((N_EXPERTS, 1)),
        ],
        out_specs=(
            pl.BlockSpec((tm, D_MODEL), row),
            pl.BlockSpec((ls, XS_COLS), row),
            pl.BlockSpec((ls, D_MODEL), row),
            pl.BlockSpec((tm, 1), row),
            pl.BlockSpec((POST_SUB, 1, LANES), lambda i: (i, 0, 0)),
        ),
        compiler_params=pltpu.CompilerParams(
            dimension_semantics=("arbitrary",), vmem_limit_bytes=VMEM_LIMIT),
        name="post_mixer",
    )(attn_ctx, attn_lat, gm, x_ctx, x_lat, mod, oag, wout, g2, wr, rb)


def _plan_kernel(cnt_ref, tg_ref, tn_ref, src_ref, nu_ref):
    def clear_src(j, c):
        for u in range(E_PIECES):
            src_ref[j * E_PIECES + u] = 0
        return c

    lax.fori_loop(0, SRC_SLOTS // E_PIECES, clear_src, 0)

    def clear_tile(j, c):
        tg_ref[j] = N_GROUPS - 1
        tn_ref[j] = 0
        return c

    lax.fori_loop(0, N_ETILES, clear_tile, 0)

    def n_pieces(i, g):
        return lax.shift_right_logical(cnt_ref[i, g * EXPERTS_PER_GROUP] + (PIECE - 1), PIECE_SHIFT)

    t = jnp.int32(0)
    for g in range(N_GROUPS):
        def tile_body(i, s, g=g):
            first = i * LS_PIECES
            for gp in range(g):
                first = first + n_pieces(i, gp)

            for p in range(TM_POST // PIECE):
                src_ref[s + p] = first + p
            return s + n_pieces(i, g)

        s0 = t * E_PIECES
        s1 = lax.fori_loop(0, N_POST_TILES, tile_body, s0)
        n = s1 - s0
        tiles = lax.shift_right_logical(n + (E_PIECES - 1), E_SHIFT)

        def mark_tile(u, c, g=g, n=n, t=t):
            tg_ref[t + u] = g
            tn_ref[t + u] = jnp.minimum(n - u * E_PIECES, E_PIECES)
            return c

        lax.fori_loop(0, tiles, mark_tile, 0)
        t = t + tiles
    nu_ref[0] = t


def _plan(counts):
    smem = pl.BlockSpec(memory_space=pltpu.SMEM)
    return pl.pallas_call(
        _plan_kernel,
        out_shape=(
            jax.ShapeDtypeStruct((N_ETILES,), jnp.int32),
            jax.ShapeDtypeStruct((N_ETILES,), jnp.int32),
            jax.ShapeDtypeStruct((SRC_SLOTS,), jnp.int32),
            jax.ShapeDtypeStruct((1,), jnp.int32),
        ),
        in_specs=[smem],
        out_specs=(smem, smem, smem, smem),
        name="expert_plan",
    )(counts)


GATHER_X, SCATTER_Y = 0, 1


def _moe_kernel(tg_ref, tn_ref, src_ref, nu_ref, h2s_hbm, y0_hbm, wg32_ref, wu32_ref, wd32_ref,
                wsg32_ref, wsu32_ref, wsd32_ref, y_hbm, xbuf, ybuf, wg_ref, wu_ref, wd_ref,
                wsg_ref, wsu_ref, wsd_ref, sem):
    del y0_hbm
    j = pl.program_id(0)
    n_used = nu_ref[0]
    slot = lax.rem(j, 2)

    @pl.when(j == 0)
    def _():
        wsg_ref[...] = wsg32_ref[0].astype(BF16)
        wsu_ref[...] = wsu32_ref[0].astype(BF16)
        wsd_ref[...] = wsd32_ref[0].astype(BF16)

    @pl.when((j == 0) | (tg_ref[j] != tg_ref[jnp.maximum(j - 1, 0)]))
    def _():
        for k in range(EXPERTS_PER_GROUP):
            wg_ref[k] = wg32_ref[0, k].astype(BF16)
            wu_ref[k] = wu32_ref[0, k].astype(BF16)
            wd_ref[k] = wd32_ref[0, k].astype(BF16)

    def piece_rows(t, k):
        hbm_rows = pl.ds(pl.multiple_of(src_ref[t * E_PIECES + k] * PIECE, PIECE), PIECE)
        buf_rows = pl.ds(pl.multiple_of(k * PIECE, PIECE), PIECE)
        return hbm_rows, buf_rows

    def gather_copies(t, k, slot):
        hbm_rows, buf_rows = piece_rows(t, k)
        return (
            pltpu.make_async_copy(h2s_hbm.at[hbm_rows], xbuf.at[slot, buf_rows], sem.at[GATHER_X, slot]),
        )

    def scatter_copies(t, k, slot):
        hbm_rows, buf_rows = piece_rows(t, k)
        return (
            pltpu.make_async_copy(ybuf.at[slot, buf_rows], y_hbm.at[hbm_rows], sem.at[SCATTER_Y, slot]),
        )

    def for_pieces(t, slot, copies, action):
        def body(k, c):
            for cp in copies(t, k, slot):
                action(cp)
            return c
        lax.fori_loop(0, tn_ref[t], body, 0)

    start = lambda cp: cp.start()
    wait = lambda cp: cp.wait()

    @pl.when(j == 0)
    def _():
        xbuf[...] = jnp.zeros(xbuf.shape, xbuf.dtype)
        for_pieces(0, 0, gather_copies, start)

    @pl.when(j + 1 < n_used)
    def _():
        for_pieces(j + 1, 1 - slot, gather_copies, start)

    @pl.when(j < n_used)
    def _():
        for_pieces(j, slot, gather_copies, wait)
        x = xbuf[slot, :, 0:D_MODEL]
        comb = sum(xbuf[slot, :, D_MODEL + t * LANES:D_MODEL + (t + 1) * LANES].astype(F32)
                   for t in range(COMB_TERMS))
        lane = lax.broadcasted_iota(jnp.int32, (TM_E, LANES), 1)
        first = tg_ref[j] * EXPERTS_PER_GROUP
        y = _dot((_silu(_dot(x, wsg_ref[...])) * _dot(x, wsu_ref[...])).astype(BF16), wsd_ref[...])
        for k in range(EXPERTS_PER_GROUP):
            wk = jnp.sum(jnp.where(lane == first + k, comb, 0.0), axis=-1, keepdims=True)
            act = _silu(_dot(x, wg_ref[k])) * _dot(x, wu_ref[k]) * wk
            y = y + _dot(act.astype(BF16), wd_ref[k])
        ybuf[slot] = y.astype(BF16)
        for_pieces(j, slot, scatter_copies, start)

    @pl.when((j >= 1) & (j < n_used))
    def _():
        for_pieces(j - 1, 1 - slot, scatter_copies, wait)

    @pl.when(j == n_used - 1)
    def _():
        for_pieces(j, slot, scatter_copies, wait)


def _experts(plan, h2s, y0, layer, w_gate, w_up, w_down, ws_gate, ws_up, ws_down):
    tg, tn, src, nu = plan
    any_spec = pl.BlockSpec(memory_space=pl.ANY)
    group = lambda j, tg, tn, src, nu: (layer, tg[j], 0, 0)
    whole = lambda j, tg, tn, src, nu: (layer, 0, 0)
    g = EXPERTS_PER_GROUP
    return pl.pallas_call(
        _moe_kernel,
        out_shape=jax.ShapeDtypeStruct((N_LS, D_MODEL), BF16),
        grid_spec=pltpu.PrefetchScalarGridSpec(
            num_scalar_prefetch=4,
            grid=(N_ETILES,),
            in_specs=[
                any_spec, any_spec,
                pl.BlockSpec((1, g, D_MODEL, EXPERT_FF), group),
                pl.BlockSpec((1, g, D_MODEL, EXPERT_FF), group),
                pl.BlockSpec((1, g, EXPERT_FF, D_MODEL), group),
                pl.BlockSpec((1, D_MODEL, SHARED_FF), whole),
                pl.BlockSpec((1, D_MODEL, SHARED_FF), whole),
                pl.BlockSpec((1, SHARED_FF, D_MODEL), whole),
            ],
            out_specs=any_spec,
            scratch_shapes=[
                pltpu.VMEM((2, TM_E, XS_COLS), BF16),
                pltpu.VMEM((2, TM_E, D_MODEL), BF16),
                pltpu.VMEM((g, D_MODEL, EXPERT_FF), BF16),
                pltpu.VMEM((g, D_MODEL, EXPERT_FF), BF16),
                pltpu.VMEM((g, EXPERT_FF, D_MODEL), BF16),
                pltpu.VMEM((D_MODEL, SHARED_FF), BF16),
                pltpu.VMEM((D_MODEL, SHARED_FF), BF16),
                pltpu.VMEM((SHARED_FF, D_MODEL), BF16),
                pltpu.SemaphoreType.DMA((2, 2)),
            ],
        ),
        input_output_aliases={5: 0},
        compiler_params=pltpu.CompilerParams(
            dimension_semantics=("arbitrary",), vmem_limit_bytes=VMEM_LIMIT),
        name="experts",
    )(tg, tn, src, nu, h2s, y0, w_gate, w_up, w_down, ws_gate, ws_up, ws_down)


def _unsort(y_ref, pos_ref, sub):
    pos = pos_ref[sub * TM_POST:(sub + 1) * TM_POST, :].astype(jnp.int32)
    pick = lax.broadcasted_iota(jnp.int32, (TM_POST, LS_ROWS), 1) == pos
    return _dot(jnp.where(pick, 1.0, 0.0).astype(BF16), y_ref[sub * LS_ROWS:(sub + 1) * LS_ROWS, :])


def _combine_kernel(y_ref, pos_ref, x1_ref, mod_ref, fg_ref, o_ref):
    gate2 = mod_ref[0][:, 5 * D_MODEL:6 * D_MODEL]
    for sub in range(COMBINE_SUB):
        rows = slice(sub * TM_POST, (sub + 1) * TM_POST)
        x2 = x1_ref[rows, :] + gate2 * _unsort(y_ref, pos_ref, sub)
        o_ref[rows, :] = _rms(x2) * fg_ref[...]


def _combine(y, pos, x1, mod, fg, first_tile, n_tiles):
    tm = COMBINE_SUB * TM_POST
    row = lambda i: (first_tile + i, 0)
    return pl.pallas_call(
        _combine_kernel,
        out_shape=jax.ShapeDtypeStruct((n_tiles * tm, D_MODEL), F32),
        grid=(n_tiles,),
        in_specs=[
            pl.BlockSpec((COMBINE_SUB * LS_ROWS, D_MODEL), row),
            pl.BlockSpec((tm, 1), row),
            pl.BlockSpec((tm, D_MODEL), row),
            _mod_spec(DEPTH - 1, tm, first_tile),
            _full((1, D_MODEL)),
        ],
        out_specs=pl.BlockSpec((tm, D_MODEL), lambda i: (i, 0)),
        compiler_params=pltpu.CompilerParams(
            dimension_semantics=("arbitrary",), vmem_limit_bytes=VMEM_LIMIT),
        name="combine",
    )(y, pos, x1, mod, fg)


def _rope_table():
    rows = DEC_SEQ // GRID_W
    row = jnp.repeat(jnp.arange(rows, dtype=F32), GRID_W)
    col = jnp.tile(jnp.arange(GRID_W, dtype=F32), rows)
    half = QK_ROPE // 2
    freqs = 1.0 / (ROPE_BASE ** (jnp.arange(0, half, 2, dtype=F32) / half))
    ang = jnp.concatenate([row[:, None] * freqs, col[:, None] * freqs], axis=-1)
    cos, sin = jnp.cos(ang), jnp.sin(ang)
    lat = jnp.concatenate([cos, cos, -sin, sin], axis=-1)
    ident = jnp.concatenate([jnp.ones((TM_PRE, QK_ROPE), F32), jnp.zeros((TM_PRE, QK_ROPE), F32)], axis=-1)
    return jnp.concatenate([ident, lat], axis=0)


_DEINT = np.concatenate([np.arange(0, QK_ROPE, 2), np.arange(1, QK_ROPE, 2)])
_SWAP = np.concatenate([np.arange(1, QK_ROPE, 2), np.arange(0, QK_ROPE, 2)])
_INTERLEAVE = np.argsort(_DEINT)


def _layout_w_in(w_in):
    cq_ckv = w_in[..., :Q_LORA + KV_LORA]
    kr = w_in[..., Q_LORA + KV_LORA:Q_LORA + KV_LORA + QK_ROPE]
    uv = w_in[..., Q_LORA + KV_LORA + QK_ROPE:]
    return jnp.concatenate([cq_ckv, kr[..., _DEINT], kr[..., _SWAP], uv], axis=-1).astype(BF16)


def _layout_w_qb(w_qb):
    w = (w_qb * (ATTN_SCALE * LOG2E)).reshape(DEPTH, Q_LORA, ATTN_HEADS, QK_NOPE + QK_ROPE)
    nope, rope = w[..., :QK_NOPE], w[..., QK_NOPE:]
    w = jnp.concatenate([nope, rope[..., _DEINT], rope[..., _SWAP]], axis=-1)
    return w.reshape(DEPTH, Q_LORA, ATTN_HEADS * HEAD_PAD).astype(BF16)


def _layout_w_kvb(w_kvb):
    w = w_kvb.reshape(DEPTH, KV_LORA, ATTN_HEADS, QK_NOPE + V_DIM)
    k = w[..., :QK_NOPE].reshape(DEPTH, KV_LORA, ATTN_HEADS * QK_NOPE)
    v = w[..., QK_NOPE:].reshape(DEPTH, KV_LORA, ATTN_HEADS * V_DIM)
    return jnp.concatenate([k, v], axis=-1).astype(BF16)


def kernel(x_prompt, x_sample, cache_ckv, cache_krope, c, c_ctx, norm1_g, w_ada, b_ada, w_in,
           q_norm_g, w_qb, kv_norm_g, w_kvb, gm_norm_g, w_spatial, b_spatial, onorm_attn_g,
           onorm_gm_g, w_out, norm2_g, w_router, router_bias, w_gate, w_up, w_down, ws_gate,
           ws_up, ws_down, final_norm_g):
    x_ctx = x_prompt.reshape(N_CTX, D_MODEL)
    x_lat = x_sample.reshape(N_LAT, D_MODEL)
    cond = jnp.concatenate([c_ctx[None, :], c, jnp.zeros((COND_PAD - N_COND, D_MODEL), F32)], axis=0)
    mod = _modulation(cond, w_ada, b_ada).reshape(DEPTH, COND_PAD, 1, 6 * D_MODEL)
    rope_tab = _rope_table()

    wkvb = _layout_w_kvb(w_kvb)
    cache_kr = jnp.pad(cache_krope[..., _DEINT], ((0, 0), (0, 0), (0, 0), (0, LANES - QK_ROPE)))
    kc, vct = _cache_kv(cache_ckv, cache_kr, wkvb)

    wr = jnp.pad(w_router, ((0, 0), (0, LANES - N_EXPERTS)))
    wr_hi = wr.astype(BF16)
    wr = jnp.concatenate([wr_hi, (wr - wr_hi.astype(F32)).astype(BF16)], axis=1)
    rb = router_bias.reshape(N_EXPERTS, 1)
    fg = final_norm_g.reshape(1, D_MODEL)

    row_stack = lambda g: g.reshape(DEPTH, 1, -1)
    bs = jnp.broadcast_to(jnp.swapaxes(b_spatial, 1, 2)[..., None],
                          (DEPTH, CHUNK, GM_GROUPS, GM_GROUP_DIM)).reshape(DEPTH, CHUNK, GM_WIDTH)
    pre_params = (row_stack(norm1_g), _layout_w_in(w_in), row_stack(q_norm_g), _layout_w_qb(w_qb),
                  row_stack(kv_norm_g), wkvb, row_stack(gm_norm_g), w_spatial.astype(BF16), bs,
                  row_stack(onorm_gm_g))
    post_params = (row_stack(onorm_attn_g), w_out.astype(BF16), row_stack(norm2_g), wr, rb)

    ckv_out, kr_out = [], []
    layer_input = (x_ctx, x_lat)
    for l in range(DEPTH):
        q, k, vt, gm, ckv_n, kr, *formed = _pre_mixer(l, layer_input, mod, rope_tab, *pre_params)
        ckv_out.append(ckv_n[:N_CTX].reshape(BATCH, SEQ, KV_LORA))
        kr_out.append(kr[:N_CTX][:, _INTERLEAVE].reshape(BATCH, SEQ, QK_ROPE))
        x_pair = (formed[0], formed[0]) if formed else layer_input
        x1, h2s, y0, pos, counts = _post_mixer(
            l, _ctx_attention(q, k, vt), _lat_attention(q, k, vt, kc, vct, l), gm, *x_pair, mod,
            *post_params)
        plan = _plan(counts.reshape(N_POST_TILES, LANES))
        y = _experts(plan, h2s, y0, l, w_gate, w_up, w_down, ws_gate, ws_up, ws_down)
        layer_input = (y, pos, x1)

    n_ctx_tiles = N_CTX // (COMBINE_SUB * TM_POST)
    n_lat_tiles = N_LAT // (COMBINE_SUB * TM_POST)
    x_ctx = _combine(y, pos, x1, mod, fg, 0, n_ctx_tiles)
    x_lat = _combine(y, pos, x1, mod, fg, n_ctx_tiles, n_lat_tiles)
    y_prompt = x_ctx.reshape(BATCH, SEQ, D_MODEL)
    y_sample = x_lat.reshape(DEC_BATCH, DEC_SEQ, D_MODEL)
    return y_prompt, y_sample, jnp.stack(ckv_out, axis=1), jnp.stack(kr_out, axis=1)
```

```python
import functools
import math

import jax
import jax.numpy as jnp
import numpy as np
from jax import lax
from jax.experimental import pallas as pl
from jax.experimental.pallas import tpu as pltpu

D_MODEL = 1024
BATCH = 16
SEQ = 256
DEPTH = 2
DEC_BATCH = 4
DEC_SEQ = 4096
PAST_LEN = 256
GRID_W = 64
ATTN_HEADS = 4
QK_NOPE = 128
QK_ROPE = 64
V_DIM = 128
V_EXT = V_DIM + 16
Q_LORA = 384
KV_LORA = 256
ATTN_SCALE = (QK_NOPE + QK_ROPE) ** -0.5
ROPE_BASE = 10000.0
CHUNK = 128
GM_WIDTH = 512
GM_GROUPS = 4
GM_GROUP_DIM = 128
N_EXPERTS = 16
N_GROUPS = 4
EXPERTS_PER_GROUP = 4
EXPERT_FF = 256
SHARED_FF = 256
EPS = 1e-6

N_CTX = BATCH * SEQ
N_LAT = DEC_BATCH * DEC_SEQ
N_ROWS = N_CTX + N_LAT
N_COND = 1 + DEC_BATCH
COND_PAD = 16
HEAD_PAD = 256
IN_COLS = Q_LORA + KV_LORA + 2 * QK_ROPE + 2 * GM_WIDTH
LANES = 128
LOG2E = 1.4426950408889634

TM_PRE = 512
TM_POST = 256
COMB_TERMS = 3
XS_COLS = D_MODEL + COMB_TERMS * LANES
COMBINE_SUB = 4
POST_SUB = 4
PIECE = 16
LS_ROWS = TM_POST + N_GROUPS * PIECE
LS_PIECES = LS_ROWS // PIECE
N_POST_TILES = N_ROWS // TM_POST
N_LS = N_POST_TILES * LS_ROWS
TM_E = 512
E_PIECES = TM_E // PIECE
PIECE_SHIFT = PIECE.bit_length() - 1
E_SHIFT = E_PIECES.bit_length() - 1
N_ETILES = -(-(N_POST_TILES * (TM_POST // PIECE + N_GROUPS - 1)) // E_PIECES) + N_GROUPS
SRC_SLOTS = (N_ETILES + 1) * E_PIECES
TQ = 512
CTX_SUB = 2
SCORES_AHEAD = 4
HEADS_PER_STEP = 4
VMEM_LIMIT = 56 * 1024 * 1024

F32 = jnp.float32
BF16 = jnp.bfloat16


def _rms(x):
    return x * lax.rsqrt(jnp.mean(x * x, axis=-1, keepdims=True) + EPS)


def _gelu(x):
    return 0.5 * x * (1.0 + jnp.tanh(math.sqrt(2.0 / math.pi) * (x + 0.044715 * (x * x * x))))


def _silu(x):
    return x * (1.0 / (1.0 + jnp.exp(-x)))


def _dot(a, b):
    return jnp.dot(a, b, preferred_element_type=F32)


def _dot_nt(a, b):
    return lax.dot_general(a, b, (((1,), (1,)), ((), ())), preferred_element_type=F32)


def _cond_row(i, tm):
    n_ctx_tiles = N_CTX // tm
    per_batch = DEC_SEQ // tm
    return jnp.where(i < n_ctx_tiles, 0, 1 + (i - n_ctx_tiles) // per_batch)


def _rope_block(i, tm):
    n_ctx_tiles = N_CTX // tm
    per_batch = DEC_SEQ // tm
    return jnp.where(i < n_ctx_tiles, 0, 1 + (i - n_ctx_tiles) % per_batch)


def _ctx_lat_specs(tm, width, joint=False):
    n_ctx_tiles = N_CTX // tm
    lat_first = n_ctx_tiles if joint else 0
    return [pl.BlockSpec((tm, width), lambda i: (jnp.minimum(i, n_ctx_tiles - 1), 0)),
            pl.BlockSpec((tm, width), lambda i: (lat_first + jnp.maximum(i - n_ctx_tiles, 0), 0))]


def _ctx_or_lat(ctx_ref, lat_ref):
    n_ctx_tiles = N_CTX // ctx_ref.shape[0]
    return jnp.where(pl.program_id(0) < n_ctx_tiles, ctx_ref[...], lat_ref[...])


def _full(shape):
    n = len(shape)
    return pl.BlockSpec(shape, lambda *_: (0,) * n)


def _of_layer(layer, shape):
    n = len(shape)
    return pl.BlockSpec((None,) + tuple(shape), lambda *_: (layer,) + (0,) * n)


def _mod_spec(layer, tm, first_tile=0):
    return pl.BlockSpec((None, 1, 1, 6 * D_MODEL),
                        lambda i: (layer, _cond_row(first_tile + i, tm), 0, 0))


def _mod_kernel(cond_ref, w_ref, b_ref, o_ref):
    s_hi, s_lo = _split_bf16(_silu(cond_ref[...]), 2)
    w_hi, w_lo = _split_bf16(w_ref[0], 2)
    t = _dot(jnp.concatenate([s_hi, s_lo], axis=0), w_hi)
    o_ref[0] = (t[:COND_PAD] + t[COND_PAD:]) + _dot(s_hi, w_lo) + b_ref[0]


def _modulation(cond, w_ada, b_ada):
    tn = 1536
    return pl.pallas_call(
        _mod_kernel,
        out_shape=jax.ShapeDtypeStruct((DEPTH, COND_PAD, 6 * D_MODEL), F32),
        grid=(DEPTH, 6 * D_MODEL // tn),
        in_specs=[
            pl.BlockSpec((COND_PAD, D_MODEL), lambda l, j: (0, 0)),
            pl.BlockSpec((1, D_MODEL, tn), lambda l, j: (l, 0, j)),
            pl.BlockSpec((1, 1, tn), lambda l, j: (l, 0, j)),
        ],
        out_specs=pl.BlockSpec((1, COND_PAD, tn), lambda l, j: (l, 0, j)),
        compiler_params=pltpu.CompilerParams(
            dimension_semantics=("arbitrary", "arbitrary"), vmem_limit_bytes=VMEM_LIMIT),
        name="modulation",
    )(cond, w_ada, b_ada.reshape(DEPTH, 1, 6 * D_MODEL))


def _store_values_t(vt_ref, v):
    t = v.shape[0]
    v_t = v.T.astype(BF16)
    for hd in range(ATTN_HEADS):
        vt_ref[hd * V_EXT:hd * V_EXT + V_DIM, :] = v_t[hd * V_DIM:(hd + 1) * V_DIM, :]
        vt_ref[hd * V_EXT + V_DIM:(hd + 1) * V_EXT, :] = jnp.ones((V_EXT - V_DIM, t), BF16)


def _pre_kernel(*refs, after_experts):
    if after_experts:
        (y_ref, pos_ref, x1_ref, prev_mod_ref, mod_ref, rope_ref, g1_ref, win_ref, qg_ref, wqb_ref,
         kvg_ref, wkvb_ref, gmg_ref, ws_ref, bs_ref, ogm_ref,
         q_ref, k_ref, vt_ref, gm_ref, ckv_ref, kr_ref, x_ref) = refs
        tm = x1_ref.shape[0]
        gate2 = prev_mod_ref[0][:, 5 * D_MODEL:6 * D_MODEL]
        x = jnp.concatenate([x1_ref[sub * TM_POST:(sub + 1) * TM_POST, :] + gate2 * _unsort(y_ref, pos_ref, sub)
                             for sub in range(tm // TM_POST)], axis=0)
        x_ref[...] = x
    else:
        (xc_ref, xl_ref, mod_ref, rope_ref, g1_ref, win_ref, qg_ref, wqb_ref,
         kvg_ref, wkvb_ref, gmg_ref, ws_ref, bs_ref, ogm_ref,
         q_ref, k_ref, vt_ref, gm_ref, ckv_ref, kr_ref) = refs
        tm = xc_ref.shape[0]
        x = _ctx_or_lat(xc_ref, xl_ref)
    mod = mod_ref[0]
    shift1 = mod[:, 0:D_MODEL]
    scale1 = mod[:, D_MODEL:2 * D_MODEL]
    h = _rms(x) * g1_ref[...] * (1.0 + scale1) + shift1
    y = _dot(h.astype(BF16), win_ref[...])
    cq = y[:, 0:Q_LORA]
    ckv = y[:, Q_LORA:Q_LORA + KV_LORA]
    kr2 = y[:, 640:768]
    u = y[:, 768:768 + GM_WIDTH]
    vv = y[:, 768 + GM_WIDTH:768 + 2 * GM_WIDTH]

    rope = rope_ref[...]
    lane = lax.broadcasted_iota(jnp.int32, (tm, LANES), 1)

    kr_ref[...] = kr2[:, 0:QK_ROPE]
    t = kr2 * rope
    k_rot = jnp.where(lane < QK_ROPE, t + pltpu.roll(t, QK_ROPE, 1), 0.0).astype(BF16)

    ckv_n = _rms(ckv) * kvg_ref[...]
    ckv_ref[...] = ckv_n
    kv = _dot(ckv_n.astype(BF16), wkvb_ref[...])
    for hd in range(ATTN_HEADS):
        k_ref[:, hd * HEAD_PAD:hd * HEAD_PAD + QK_NOPE] = (
            kv[:, hd * QK_NOPE:(hd + 1) * QK_NOPE].astype(BF16))
        k_ref[:, hd * HEAD_PAD + QK_NOPE:(hd + 1) * HEAD_PAD] = k_rot
    _store_values_t(vt_ref, kv[:, ATTN_HEADS * QK_NOPE:])

    q = _dot((_rms(cq) * qg_ref[...]).astype(BF16), wqb_ref[...])
    for hd in range(ATTN_HEADS):
        q_ref[:, hd * HEAD_PAD:hd * HEAD_PAD + QK_NOPE] = (
            q[:, hd * HEAD_PAD:hd * HEAD_PAD + QK_NOPE].astype(BF16))
        t = q[:, hd * HEAD_PAD + QK_NOPE:(hd + 1) * HEAD_PAD] * rope
        q_ref[:, hd * HEAD_PAD + QK_NOPE:(hd + 1) * HEAD_PAD] = (
            t + pltpu.roll(t, QK_ROPE, 1)).astype(BF16)

    ug = _gelu(u)
    vg = _gelu(vv)
    cols = []
    for g in range(GM_GROUPS):
        sl = slice(g * GM_GROUP_DIM, (g + 1) * GM_GROUP_DIM)
        vn = (_rms(vg[:, sl]) * gmg_ref[:, sl]).astype(BF16)
        rows = []
        for c in range(tm // CHUNK):
            sv = _dot(ws_ref[g], vn[c * CHUNK:(c + 1) * CHUNK]) + bs_ref[:, sl]
            rows.append(ug[c * CHUNK:(c + 1) * CHUNK, sl] * sv)
        cols.append(jnp.concatenate(rows, axis=0))
    gm = jnp.concatenate(cols, axis=1)
    gm_ref[...] = (_rms(gm) * ogm_ref[...]).astype(BF16)


def _pre_mixer(layer, layer_input, mod, rope_tab, g1, win, qg, wqb, kvg, wkvb, gmg, ws, bs, ogm):
    tm = TM_PRE
    row = lambda i: (i, 0)
    mod_spec = _mod_spec(layer, tm)
    after_experts = len(layer_input) == 3
    if after_experts:
        layer_input = (*layer_input, mod)
        input_specs = [pl.BlockSpec((tm // TM_POST * LS_ROWS, D_MODEL), row),
                       pl.BlockSpec((tm, 1), row), pl.BlockSpec((tm, D_MODEL), row),
                       _mod_spec(layer - 1, tm)]
        extra_shape = (jax.ShapeDtypeStruct((N_ROWS, D_MODEL), F32),)
        extra_spec = (pl.BlockSpec((tm, D_MODEL), row),)
    else:
        input_specs = _ctx_lat_specs(tm, D_MODEL)
        extra_shape = extra_spec = ()
    return pl.pallas_call(
        functools.partial(_pre_kernel, after_experts=after_experts),
        out_shape=(
            jax.ShapeDtypeStruct((N_ROWS, ATTN_HEADS * HEAD_PAD), BF16),
            jax.ShapeDtypeStruct((N_ROWS, ATTN_HEADS * HEAD_PAD), BF16),
            jax.ShapeDtypeStruct((ATTN_HEADS * V_EXT, N_ROWS), BF16),
            jax.ShapeDtypeStruct((N_ROWS, GM_WIDTH), BF16),
            jax.ShapeDtypeStruct((N_ROWS, KV_LORA), F32),
            jax.ShapeDtypeStruct((N_ROWS, QK_ROPE), F32),
        ) + extra_shape,
        grid=(N_ROWS // tm,),
        in_specs=input_specs + [
            mod_spec,
            pl.BlockSpec((tm, LANES), lambda i: (_rope_block(i, tm), 0)),
            _of_layer(layer, (1, D_MODEL)),
            _of_layer(layer, (D_MODEL, IN_COLS)),
            _of_layer(layer, (1, Q_LORA)),
            _of_layer(layer, (Q_LORA, ATTN_HEADS * HEAD_PAD)),
            _of_layer(layer, (1, KV_LORA)),
            _of_layer(layer, (KV_LORA, ATTN_HEADS * (QK_NOPE + V_DIM))),
            _of_layer(layer, (1, GM_WIDTH)),
            _of_layer(layer, (GM_GROUPS, CHUNK, CHUNK)),
            _of_layer(layer, (CHUNK, GM_WIDTH)),
            _of_layer(layer, (1, GM_WIDTH)),
        ],
        out_specs=(
            pl.BlockSpec((tm, ATTN_HEADS * HEAD_PAD), row),
            pl.BlockSpec((tm, ATTN_HEADS * HEAD_PAD), row),
            pl.BlockSpec((ATTN_HEADS * V_EXT, tm), lambda i: (0, i)),
            pl.BlockSpec((tm, GM_WIDTH), row),
            pl.BlockSpec((tm, KV_LORA), row),
            pl.BlockSpec((tm, QK_ROPE), row),
        ) + extra_spec,
        compiler_params=pltpu.CompilerParams(
            dimension_semantics=("arbitrary",), vmem_limit_bytes=VMEM_LIMIT),
        name="pre_mixer",
    )(*layer_input, mod, rope_tab, g1, win, qg, wqb, kvg, wkvb, gmg, ws, bs, ogm)


def _cache_kv_kernel(ckv_ref, kr_ref, wkvb_ref, k_ref, vt_ref):
    kv = _dot(ckv_ref[0, 0].astype(BF16), wkvb_ref[0])
    kr = kr_ref[0, 0].astype(BF16)
    for hd in range(ATTN_HEADS):
        k_ref[0, 0, :, hd * HEAD_PAD:hd * HEAD_PAD + QK_NOPE] = (
            kv[:, hd * QK_NOPE:(hd + 1) * QK_NOPE].astype(BF16))
        k_ref[0, 0, :, hd * HEAD_PAD + QK_NOPE:(hd + 1) * HEAD_PAD] = kr
    _store_values_t(vt_ref.at[0, 0], kv[:, ATTN_HEADS * QK_NOPE:])


def _cache_kv(cache_ckv, cache_kr_pad, wkvb):
    blk = lambda w: pl.BlockSpec((1, 1, PAST_LEN, w), lambda l, b: (b, l, 0, 0))
    return pl.pallas_call(
        _cache_kv_kernel,
        out_shape=(
            jax.ShapeDtypeStruct((DEC_BATCH, DEPTH, PAST_LEN, ATTN_HEADS * HEAD_PAD), BF16),
            jax.ShapeDtypeStruct((DEC_BATCH, DEPTH, ATTN_HEADS * V_EXT, PAST_LEN), BF16),
        ),
        grid=(DEPTH, DEC_BATCH),
        in_specs=[
            blk(KV_LORA),
            blk(LANES),
            pl.BlockSpec((1, KV_LORA, ATTN_HEADS * (QK_NOPE + V_DIM)), lambda l, b: (l, 0, 0)),
        ],
        out_specs=(blk(ATTN_HEADS * HEAD_PAD),
                   pl.BlockSpec((1, 1, ATTN_HEADS * V_EXT, PAST_LEN), lambda l, b: (b, l, 0, 0))),
        compiler_params=pltpu.CompilerParams(
            dimension_semantics=("arbitrary", "arbitrary"), vmem_limit_bytes=VMEM_LIMIT),
        name="cache_kv",
    )(cache_ckv, cache_kr_pad, wkvb)


def _attn_scores(qh, key_blocks):
    return [_dot_nt(kb, qh) for kb in key_blocks]


def _attn_values(s, vt_blocks):
    s = [si.astype(BF16) for si in s]
    m = functools.reduce(jnp.maximum, [jnp.max(si, axis=0, keepdims=True) for si in s])
    o_t = sum(_dot(vt, jnp.exp2(si - m)) for vt, si in zip(vt_blocks, s))
    return (o_t[0:V_DIM, :] / o_t[V_DIM:V_DIM + 1, :]).T


def _ctx_attn_kernel(q_ref, k_ref, vt_ref, o_ref):
    chains = [(slice(sb * SEQ, (sb + 1) * SEQ), hd) for sb in range(CTX_SUB) for hd in range(ATTN_HEADS)]
    hs = lambda hd: slice(hd * HEAD_PAD, (hd + 1) * HEAD_PAD)
    scores = [_attn_scores(q_ref[rows, hs(hd)], [k_ref[rows, hs(hd)]]) for rows, hd in chains]
    for (rows, hd), s in zip(chains, scores):
        out = _attn_values(s, [vt_ref[hd * V_EXT:(hd + 1) * V_EXT, rows]])
        o_ref[rows, hd * V_DIM:(hd + 1) * V_DIM] = out.astype(o_ref.dtype)


def _ctx_attention(q, k, vt):
    blk = lambda w: pl.BlockSpec((CTX_SUB * SEQ, w), lambda b: (b, 0))
    return pl.pallas_call(
        _ctx_attn_kernel,
        out_shape=jax.ShapeDtypeStruct((N_CTX, ATTN_HEADS * V_DIM), BF16),
        grid=(BATCH // CTX_SUB,),
        in_specs=[blk(ATTN_HEADS * HEAD_PAD), blk(ATTN_HEADS * HEAD_PAD),
                  pl.BlockSpec((ATTN_HEADS * V_EXT, CTX_SUB * SEQ), lambda b: (0, b))],
        out_specs=blk(ATTN_HEADS * V_DIM),
        compiler_params=pltpu.CompilerParams(
            dimension_semantics=("arbitrary",), vmem_limit_bytes=VMEM_LIMIT),
        name="ctx_attention",
    )(q, k, vt)


def _lat_attn_kernel(q_ref, k_ref, vt_ref, kc_ref, vct_ref, o_ref):
    hs = [slice(hd * HEAD_PAD, (hd + 1) * HEAD_PAD) for hd in range(HEADS_PER_STEP)]
    vs = [slice(hd * V_DIM, (hd + 1) * V_DIM) for hd in range(HEADS_PER_STEP)]
    ve = [slice(hd * V_EXT, (hd + 1) * V_EXT) for hd in range(HEADS_PER_STEP)]

    def scores(hd):
        return _attn_scores(q_ref[:, hs[hd]], [kc_ref[0, 0, :, hs[hd]], k_ref[:, hs[hd]]])

    pending = [scores(hd) for hd in range(min(SCORES_AHEAD, HEADS_PER_STEP))]
    for hd in range(HEADS_PER_STEP):
        if hd + SCORES_AHEAD < HEADS_PER_STEP:
            pending.append(scores(hd + SCORES_AHEAD))
        o_ref[:, vs[hd]] = _attn_values(
            pending.pop(0), [vct_ref[0, 0, ve[hd], :], vt_ref[ve[hd], :]]).astype(o_ref.dtype)


def _lat_attention(q, k, vt, kc, vct, layer):
    nq = DEC_SEQ // TQ
    ctx_q_tiles = N_CTX // TQ
    ctx_kv_blocks = N_CTX // DEC_SEQ
    qk_w = HEADS_PER_STEP * HEAD_PAD
    v_w = HEADS_PER_STEP * V_DIM
    vt_w = HEADS_PER_STEP * V_EXT
    return pl.pallas_call(
        _lat_attn_kernel,
        out_shape=jax.ShapeDtypeStruct((N_LAT, ATTN_HEADS * V_DIM), BF16),
        grid=(DEC_BATCH, ATTN_HEADS // HEADS_PER_STEP, nq),
        in_specs=[
            pl.BlockSpec((TQ, qk_w), lambda b, h, i: (ctx_q_tiles + b * nq + i, h)),
            pl.BlockSpec((DEC_SEQ, qk_w), lambda b, h, i: (ctx_kv_blocks + b, h)),
            pl.BlockSpec((vt_w, DEC_SEQ), lambda b, h, i: (h, ctx_kv_blocks + b)),
            pl.BlockSpec((1, 1, PAST_LEN, qk_w), lambda b, h, i: (b, layer, 0, h)),
            pl.BlockSpec((1, 1, vt_w, PAST_LEN), lambda b, h, i: (b, layer, h, 0)),
        ],
        out_specs=pl.BlockSpec((TQ, v_w), lambda b, h, i: (b * nq + i, h)),
        compiler_params=pltpu.CompilerParams(
            dimension_semantics=("arbitrary", "arbitrary", "arbitrary"),
            vmem_limit_bytes=VMEM_LIMIT),
        name="lat_attention",
    )(q, k, vt, kc, vct)


def _group_peer(x, row, d, width, period):
    step = d * width
    ahead = pltpu.roll(x, N_EXPERTS - step, 0)
    wraps = (row & (period - 1)) + step >= period
    if period == N_EXPERTS:
        return ahead, wraps
    return jnp.where(wraps, pltpu.roll(x, period - step, 0), ahead), wraps


def _route(logits_t, bias):
    row = lax.broadcasted_iota(jnp.int32, logits_t.shape, 0)
    s = 1.0 / (1.0 + jnp.exp(-logits_t))
    sb = s + bias
    rank = jnp.zeros(sb.shape, jnp.int32)
    for d in range(1, EXPERTS_PER_GROUP):
        o, wraps = _group_peer(sb, row, d, 1, EXPERTS_PER_GROUP)
        beats = (o > sb) | ((o == sb) & wraps)
        rank = rank + beats.astype(jnp.int32)
    top2 = rank < 2
    t = jnp.where(top2, sb, 0.0)
    gscore = t
    for d in range(1, EXPERTS_PER_GROUP):
        gscore = gscore + _group_peer(t, row, d, 1, EXPERTS_PER_GROUP)[0]
    grank = jnp.zeros(sb.shape, jnp.int32)
    for d in range(1, N_GROUPS):
        o, wraps = _group_peer(gscore, row, d, EXPERTS_PER_GROUP, N_EXPERTS)
        beats = (o > gscore) | ((o == gscore) & wraps)
        grank = grank + beats.astype(jnp.int32)
    in_group = grank == 0
    w = jnp.where(top2 & in_group, s, 0.0)
    denom = jnp.sum(w, axis=0, keepdims=True)
    group_flag = jnp.where(in_group & ((row & (EXPERTS_PER_GROUP - 1)) == 0), 1.0, 0.0)
    return w / denom, group_flag


def _experts_to_lanes(x_t):
    pad = jnp.zeros((LANES - N_EXPERTS, x_t.shape[1]), x_t.dtype)
    return jnp.concatenate([x_t, pad], axis=0).T


def _split_bf16(x, terms):
    out = []
    for _ in range(terms - 1):
        t = x.astype(BF16)
        out.append(t)
        x = x - t.astype(F32)
    out.append(x.astype(BF16))
    return out


def _post_kernel(ac_ref, al_ref, gm_ref, xc_ref, xl_ref, mod_ref, oag_ref, wout_ref, g2_ref,
                 wr_ref, rb_ref, x1_ref, h2s_ref, y0_ref, pos_ref, cnt_ref):
    mod = mod_ref[0]
    gate1 = mod[:, 2 * D_MODEL:3 * D_MODEL]
    shift2 = mod[:, 3 * D_MODEL:4 * D_MODEL]
    scale2 = mod[:, 4 * D_MODEL:5 * D_MODEL]
    y0_ref[...] = jnp.zeros(y0_ref.shape, y0_ref.dtype)
    tm = xc_ref.shape[0]
    halves = [slice(0, tm // 2), slice(tm // 2, tm)]
    attn = _ctx_or_lat(ac_ref, al_ref)
    x_in = _ctx_or_lat(xc_ref, xl_ref)
    mixed = []
    for rows in halves:
        an = (_rms(attn[rows].astype(F32)) * oag_ref[...]).astype(BF16)
        mixed.append(_dot(jnp.concatenate([an, gm_ref[rows, :]], axis=1), wout_ref[...]))
    h2_hi, logits = [], []
    for rows, mix in zip(halves, mixed):
        x1 = x_in[rows] + gate1 * mix
        x1_ref[rows, :] = x1
        hi, lo = _split_bf16(_rms(x1) * g2_ref[...] * (1.0 + scale2) + shift2, 2)
        t = _dot(jnp.concatenate([hi, lo], axis=0), wr_ref[...])
        half = tm // 2
        logits.append((t[:half, :LANES] + t[half:, :LANES]) + (t[:half, LANES:] + t[half:, LANES:]))
        h2_hi.append(hi)
    h2_hi = jnp.concatenate(h2_hi, axis=0)
    logits = jnp.concatenate(logits, axis=0)
    tiles = [slice(sub * TM_POST, (sub + 1) * TM_POST) for sub in range(POST_SUB)]
    r_i = lax.broadcasted_iota(jnp.int32, (TM_POST, TM_POST), 0)
    c_i = lax.broadcasted_iota(jnp.int32, (TM_POST, TM_POST), 1)
    before = jnp.where(c_i < r_i, 1.0, 0.0).astype(BF16)
    routed = []
    for rows in tiles:
        comb_t, flag_t = _route(logits[rows].T[0:N_EXPERTS, :], rb_ref[...])
        routed.append((_experts_to_lanes(comb_t), _experts_to_lanes(flag_t)))
    ranks = [_dot(before, flag.astype(BF16)) for _, flag in routed]
    for sub, rows in enumerate(tiles):
        comb, flag = routed[sub]
        _sort_tile(h2_hi[rows], comb, flag, ranks[sub],
                   h2s_ref.at[sub * LS_ROWS:(sub + 1) * LS_ROWS], pos_ref.at[rows], cnt_ref.at[sub])


def _sort_tile(h2_hi, comb, flag, rank, h2s_ref, pos_ref, cnt_ref):
    tm = TM_POST
    count = jnp.sum(flag, axis=0, keepdims=True)
    pieces = jnp.floor((count + (PIECE - 1)) * (1.0 / PIECE))
    start = (pltpu.roll(pieces, EXPERTS_PER_GROUP, 1) + pltpu.roll(pieces, 2 * EXPERTS_PER_GROUP, 1)
             + pltpu.roll(pieces, 3 * EXPERTS_PER_GROUP, 1)) * PIECE
    pos = jnp.sum(flag * (start + rank), axis=-1, keepdims=True)
    pos_ref[...] = pos
    cnt_ref[...] = count.astype(jnp.int32)
    pos_row = jnp.transpose(jnp.broadcast_to(pos, (tm, LANES)))[0:1, :].astype(jnp.int32)
    place = lax.broadcasted_iota(jnp.int32, (LS_ROWS, tm), 0) == pos_row
    place = jnp.where(place, 1.0, 0.0).astype(BF16)
    wide = jnp.concatenate([h2_hi] + _split_bf16(comb, COMB_TERMS), axis=1)
    h2s_ref[...] = _dot(place, wide).astype(BF16)


def _post_mixer(layer, attn_ctx, attn_lat, gm, x_ctx, x_lat, mod, oag, wout, g2, wr, rb):
    joint_x = x_ctx is x_lat
    tm = POST_SUB * TM_POST
    ls = POST_SUB * LS_ROWS
    row = lambda i: (i, 0)
    return pl.pallas_call(
        _post_kernel,
        out_shape=(
            jax.ShapeDtypeStruct((N_ROWS, D_MODEL), F32),
            jax.ShapeDtypeStruct((N_LS, XS_COLS), BF16),
            jax.ShapeDtypeStruct((N_LS, D_MODEL), BF16),
            jax.ShapeDtypeStruct((N_ROWS, 1), F32),
            jax.ShapeDtypeStruct((N_POST_TILES, 1, LANES), jnp.int32),
        ),
        grid=(N_ROWS // tm,),
        in_specs=_ctx_lat_specs(tm, ATTN_HEADS * V_DIM) + [
            pl.BlockSpec((tm, GM_WIDTH), row),
        ] + _ctx_lat_specs(tm, D_MODEL, joint=joint_x) + [
            _mod_spec(layer, tm),
            _of_layer(layer, (1, ATTN_HEADS * V_DIM)),
            _of_layer(layer, (D_MODEL, D_MODEL)),
            _of_layer(layer, (1, D_MODEL)),
            _full((D_MODEL, 2 * LANES)),
            _full((N_EXPERTS, 1)),
        ],
        out_specs=(
            pl.BlockSpec((tm, D_MODEL), row),
            pl.BlockSpec((ls, XS_COLS), row),
            pl.BlockSpec((ls, D_MODEL), row),
            pl.BlockSpec((tm, 1), row),
            pl.BlockSpec((POST_SUB, 1, LANES), lambda i: (i, 0, 0)),
        ),
        compiler_params=pltpu.CompilerParams(
            dimension_semantics=("arbitrary",), vmem_limit_bytes=VMEM_LIMIT),
        name="post_mixer",
    )(attn_ctx, attn_lat, gm, x_ctx, x_lat, mod, oag, wout, g2, wr, rb)


def _plan_kernel(cnt_ref, tg_ref, tn_ref, src_ref, nu_ref):
    def clear_src(j, c):
        for u in range(E_PIECES):
            src_ref[j * E_PIECES + u] = 0
        return c

    lax.fori_loop(0, SRC_SLOTS // E_PIECES, clear_src, 0)

    def clear_tile(j, c):
        tg_ref[j] = N_GROUPS - 1
        tn_ref[j] = 0
        return c

    lax.fori_loop(0, N_ETILES, clear_tile, 0)

    def n_pieces(i, g):
        return lax.shift_right_logical(cnt_ref[i, g * EXPERTS_PER_GROUP] + (PIECE - 1), PIECE_SHIFT)

    t = jnp.int32(0)
    for g in range(N_GROUPS):
        def tile_body(i, s, g=g):
            first = i * LS_PIECES
            for gp in range(g):
                first = first + n_pieces(i, gp)

            for p in range(TM_POST // PIECE):
                src_ref[s + p] = first + p
            return s + n_pieces(i, g)

        s0 = t * E_PIECES
        s1 = lax.fori_loop(0, N_POST_TILES, tile_body, s0)
        n = s1 - s0
        tiles = lax.shift_right_logical(n + (E_PIECES - 1), E_SHIFT)

        def mark_tile(u, c, g=g, n=n, t=t):
            tg_ref[t + u] = g
            tn_ref[t + u] = jnp.minimum(n - u * E_PIECES, E_PIECES)
            return c

        lax.fori_loop(0, tiles, mark_tile, 0)
        t = t + tiles
    nu_ref[0] = t


def _plan(counts):
    smem = pl.BlockSpec(memory_space=pltpu.SMEM)
    return pl.pallas_call(
        _plan_kernel,
        out_shape=(
            jax.ShapeDtypeStruct((N_ETILES,), jnp.int32),
            jax.ShapeDtypeStruct((N_ETILES,), jnp.int32),
            jax.ShapeDtypeStruct((SRC_SLOTS,), jnp.int32),
            jax.ShapeDtypeStruct((1,), jnp.int32),
        ),
        in_specs=[smem],
        out_specs=(smem, smem, smem, smem),
        name="expert_plan",
    )(counts)


GATHER_X, SCATTER_Y = 0, 1


def _moe_kernel(tg_ref, tn_ref, src_ref, nu_ref, h2s_hbm, y0_hbm, wg32_ref, wu32_ref, wd32_ref,
                wsg32_ref, wsu32_ref, wsd32_ref, y_hbm, xbuf, ybuf, wg_ref, wu_ref, wd_ref,
                wsg_ref, wsu_ref, wsd_ref, sem):
    del y0_hbm
    j = pl.program_id(0)
    n_used = nu_ref[0]
    slot = lax.rem(j, 2)

    @pl.when(j == 0)
    def _():
        wsg_ref[...] = wsg32_ref[0].astype(BF16)
        wsu_ref[...] = wsu32_ref[0].astype(BF16)
        wsd_ref[...] = wsd32_ref[0].astype(BF16)

    @pl.when((j == 0) | (tg_ref[j] != tg_ref[jnp.maximum(j - 1, 0)]))
    def _():
        for k in range(EXPERTS_PER_GROUP):
            wg_ref[k] = wg32_ref[0, k].astype(BF16)
            wu_ref[k] = wu32_ref[0, k].astype(BF16)
            wd_ref[k] = wd32_ref[0, k].astype(BF16)

    def piece_rows(t, k):
        hbm_rows = pl.ds(pl.multiple_of(src_ref[t * E_PIECES + k] * PIECE, PIECE), PIECE)
        buf_rows = pl.ds(pl.multiple_of(k * PIECE, PIECE), PIECE)
        return hbm_rows, buf_rows

    def gather_copies(t, k, slot):
        hbm_rows, buf_rows = piece_rows(t, k)
        return (
            pltpu.make_async_copy(h2s_hbm.at[hbm_rows], xbuf.at[slot, buf_rows], sem.at[GATHER_X, slot]),
        )

    def scatter_copies(t, k, slot):
        hbm_rows, buf_rows = piece_rows(t, k)
        return (
            pltpu.make_async_copy(ybuf.at[slot, buf_rows], y_hbm.at[hbm_rows], sem.at[SCATTER_Y, slot]),
        )

    def for_pieces(t, slot, copies, action):
        def body(k, c):
            for cp in copies(t, k, slot):
                action(cp)
            return c

        @pl.when(tn_ref[t] == E_PIECES)
        def _():
            lax.fori_loop(0, E_PIECES, body, 0, unroll=8)

        @pl.when(tn_ref[t] != E_PIECES)
        def _():
            lax.fori_loop(0, tn_ref[t], body, 0)

    start = lambda cp: cp.start()
    wait = lambda cp: cp.wait()

    @pl.when(j == 0)
    def _():
        xbuf[...] = jnp.zeros(xbuf.shape, xbuf.dtype)
        for_pieces(0, 0, gather_copies, start)

    @pl.when(j + 1 < n_used)
    def _():
        for_pieces(j + 1, 1 - slot, gather_copies, start)

    @pl.when(j < n_used)
    def _():
        for_pieces(j, slot, gather_copies, wait)
        x = xbuf[slot, :, 0:D_MODEL]
        comb = sum(xbuf[slot, :, D_MODEL + t * LANES:D_MODEL + (t + 1) * LANES].astype(F32)
                   for t in range(COMB_TERMS))
        lane = lax.broadcasted_iota(jnp.int32, (TM_E, LANES), 1)
        first = tg_ref[j] * EXPERTS_PER_GROUP
        y = _dot((_silu(_dot(x, wsg_ref[...])) * _dot(x, wsu_ref[...])).astype(BF16), wsd_ref[...])
        for k in range(EXPERTS_PER_GROUP):
            wk = jnp.sum(jnp.where(lane == first + k, comb, 0.0), axis=-1, keepdims=True)
            act = _silu(_dot(x, wg_ref[k])) * _dot(x, wu_ref[k]) * wk
            y = y + _dot(act.astype(BF16), wd_ref[k])
        ybuf[slot] = y.astype(BF16)
        for_pieces(j, slot, scatter_copies, start)

    @pl.when((j >= 1) & (j < n_used))
    def _():
        for_pieces(j - 1, 1 - slot, scatter_copies, wait)

    @pl.when(j == n_used - 1)
    def _():
        for_pieces(j, slot, scatter_copies, wait)


def _experts(plan, h2s, y0, layer, w_gate, w_up, w_down, ws_gate, ws_up, ws_down):
    tg, tn, src, nu = plan
    any_spec = pl.BlockSpec(memory_space=pl.ANY)
    group = lambda j, tg, tn, src, nu: (layer, tg[j], 0, 0)
    whole = lambda j, tg, tn, src, nu: (layer, 0, 0)
    g = EXPERTS_PER_GROUP
    return pl.pallas_call(
        _moe_kernel,
        out_shape=jax.ShapeDtypeStruct((N_LS, D_MODEL), BF16),
        grid_spec=pltpu.PrefetchScalarGridSpec(
            num_scalar_prefetch=4,
            grid=(N_ETILES,),
            in_specs=[
                any_spec, any_spec,
                pl.BlockSpec((1, g, D_MODEL, EXPERT_FF), group),
                pl.BlockSpec((1, g, D_MODEL, EXPERT_FF), group),
                pl.BlockSpec((1, g, EXPERT_FF, D_MODEL), group),
                pl.BlockSpec((1, D_MODEL, SHARED_FF), whole),
                pl.BlockSpec((1, D_MODEL, SHARED_FF), whole),
                pl.BlockSpec((1, SHARED_FF, D_MODEL), whole),
            ],
            out_specs=any_spec,
            scratch_shapes=[
                pltpu.VMEM((2, TM_E, XS_COLS), BF16),
                pltpu.VMEM((2, TM_E, D_MODEL), BF16),
                pltpu.VMEM((g, D_MODEL, EXPERT_FF), BF16),
                pltpu.VMEM((g, D_MODEL, EXPERT_FF), BF16),
                pltpu.VMEM((g, EXPERT_FF, D_MODEL), BF16),
                pltpu.VMEM((D_MODEL, SHARED_FF), BF16),
                pltpu.VMEM((D_MODEL, SHARED_FF), BF16),
                pltpu.VMEM((SHARED_FF, D_MODEL), BF16),
                pltpu.SemaphoreType.DMA((2, 2)),
            ],
        ),
        input_output_aliases={5: 0},
        compiler_params=pltpu.CompilerParams(
            dimension_semantics=("arbitrary",), vmem_limit_bytes=VMEM_LIMIT),
        name="experts",
    )(tg, tn, src, nu, h2s, y0, w_gate, w_up, w_down, ws_gate, ws_up, ws_down)


def _unsort(y_ref, pos_ref, sub):
    pos = pos_ref[sub * TM_POST:(sub + 1) * TM_POST, :].astype(jnp.int32)
    pick = lax.broadcasted_iota(jnp.int32, (TM_POST, LS_ROWS), 1) == pos
    return _dot(jnp.where(pick, 1.0, 0.0).astype(BF16), y_ref[sub * LS_ROWS:(sub + 1) * LS_ROWS, :])


def _combine_kernel(y_ref, pos_ref, x1_ref, mod_ref, fg_ref, o_ref):
    gate2 = mod_ref[0][:, 5 * D_MODEL:6 * D_MODEL]
    for sub in range(COMBINE_SUB):
        rows = slice(sub * TM_POST, (sub + 1) * TM_POST)
        x2 = x1_ref[rows, :] + gate2 * _unsort(y_ref, pos_ref, sub)
        o_ref[rows, :] = _rms(x2) * fg_ref[...]


def _combine(y, pos, x1, mod, fg, first_tile, n_tiles):
    tm = COMBINE_SUB * TM_POST
    row = lambda i: (first_tile + i, 0)
    return pl.pallas_call(
        _combine_kernel,
        out_shape=jax.ShapeDtypeStruct((n_tiles * tm, D_MODEL), F32),
        grid=(n_tiles,),
        in_specs=[
            pl.BlockSpec((COMBINE_SUB * LS_ROWS, D_MODEL), row),
            pl.BlockSpec((tm, 1), row),
            pl.BlockSpec((tm, D_MODEL), row),
            _mod_spec(DEPTH - 1, tm, first_tile),
            _full((1, D_MODEL)),
        ],
        out_specs=pl.BlockSpec((tm, D_MODEL), lambda i: (i, 0)),
        compiler_params=pltpu.CompilerParams(
            dimension_semantics=("arbitrary",), vmem_limit_bytes=VMEM_LIMIT),
        name="combine",
    )(y, pos, x1, mod, fg)


def _rope_table():
    rows = DEC_SEQ // GRID_W
    row = jnp.repeat(jnp.arange(rows, dtype=F32), GRID_W)
    col = jnp.tile(jnp.arange(GRID_W, dtype=F32), rows)
    half = QK_ROPE // 2
    freqs = 1.0 / (ROPE_BASE ** (jnp.arange(0, half, 2, dtype=F32) / half))
    ang = jnp.concatenate([row[:, None] * freqs, col[:, None] * freqs], axis=-1)
    cos, sin = jnp.cos(ang), jnp.sin(ang)
    lat = jnp.concatenate([cos, cos, -sin, sin], axis=-1)
    ident = jnp.concatenate([jnp.ones((TM_PRE, QK_ROPE), F32), jnp.zeros((TM_PRE, QK_ROPE), F32)], axis=-1)
    return jnp.concatenate([ident, lat], axis=0)


_DEINT = np.concatenate([np.arange(0, QK_ROPE, 2), np.arange(1, QK_ROPE, 2)])
_SWAP = np.concatenate([np.arange(1, QK_ROPE, 2), np.arange(0, QK_ROPE, 2)])
_INTERLEAVE = np.argsort(_DEINT)


def _layout_w_in(w_in):
    cq_ckv = w_in[..., :Q_LORA + KV_LORA]
    kr = w_in[..., Q_LORA + KV_LORA:Q_LORA + KV_LORA + QK_ROPE]
    uv = w_in[..., Q_LORA + KV_LORA + QK_ROPE:]
    return jnp.concatenate([cq_ckv, kr[..., _DEINT], kr[..., _SWAP], uv], axis=-1).astype(BF16)


def _layout_w_qb(w_qb):
    w = (w_qb * (ATTN_SCALE * LOG2E)).reshape(DEPTH, Q_LORA, ATTN_HEADS, QK_NOPE + QK_ROPE)
    nope, rope = w[..., :QK_NOPE], w[..., QK_NOPE:]
    w = jnp.concatenate([nope, rope[..., _DEINT], rope[..., _SWAP]], axis=-1)
    return w.reshape(DEPTH, Q_LORA, ATTN_HEADS * HEAD_PAD).astype(BF16)


def _layout_w_kvb(w_kvb):
    w = w_kvb.reshape(DEPTH, KV_LORA, ATTN_HEADS, QK_NOPE + V_DIM)
    k = w[..., :QK_NOPE].reshape(DEPTH, KV_LORA, ATTN_HEADS * QK_NOPE)
    v = w[..., QK_NOPE:].reshape(DEPTH, KV_LORA, ATTN_HEADS * V_DIM)
    return jnp.concatenate([k, v], axis=-1).astype(BF16)


def kernel(x_prompt, x_sample, cache_ckv, cache_krope, c, c_ctx, norm1_g, w_ada, b_ada, w_in,
           q_norm_g, w_qb, kv_norm_g, w_kvb, gm_norm_g, w_spatial, b_spatial, onorm_attn_g,
           onorm_gm_g, w_out, norm2_g, w_router, router_bias, w_gate, w_up, w_down, ws_gate,
           ws_up, ws_down, final_norm_g):
    x_ctx = x_prompt.reshape(N_CTX, D_MODEL)
    x_lat = x_sample.reshape(N_LAT, D_MODEL)
    cond = jnp.concatenate([c_ctx[None, :], c, jnp.zeros((COND_PAD - N_COND, D_MODEL), F32)], axis=0)
    mod = _modulation(cond, w_ada, b_ada).reshape(DEPTH, COND_PAD, 1, 6 * D_MODEL)
    rope_tab = _rope_table()

    wkvb = _layout_w_kvb(w_kvb)
    cache_kr = jnp.pad(cache_krope[..., _DEINT], ((0, 0), (0, 0), (0, 0), (0, LANES - QK_ROPE)))
    kc, vct = _cache_kv(cache_ckv, cache_kr, wkvb)

    wr = jnp.pad(w_router, ((0, 0), (0, LANES - N_EXPERTS)))
    wr_hi = wr.astype(BF16)
    wr = jnp.concatenate([wr_hi, (wr - wr_hi.astype(F32)).astype(BF16)], axis=1)
    rb = router_bias.reshape(N_EXPERTS, 1)
    fg = final_norm_g.reshape(1, D_MODEL)

    row_stack = lambda g: g.reshape(DEPTH, 1, -1)
    bs = jnp.broadcast_to(jnp.swapaxes(b_spatial, 1, 2)[..., None],
                          (DEPTH, CHUNK, GM_GROUPS, GM_GROUP_DIM)).reshape(DEPTH, CHUNK, GM_WIDTH)
    pre_params = (row_stack(norm1_g), _layout_w_in(w_in), row_stack(q_norm_g), _layout_w_qb(w_qb),
                  row_stack(kv_norm_g), wkvb, row_stack(gm_norm_g), w_spatial.astype(BF16), bs,
                  row_stack(onorm_gm_g))
    post_params = (row_stack(onorm_attn_g), w_out.astype(BF16), row_stack(norm2_g), wr, rb)

    ckv_out, kr_out = [], []
    layer_input = (x_ctx, x_lat)
    for l in range(DEPTH):
        q, k, vt, gm, ckv_n, kr, *formed = _pre_mixer(l, layer_input, mod, rope_tab, *pre_params)
        ckv_out.append(ckv_n[:N_CTX].reshape(BATCH, SEQ, KV_LORA))
        kr_out.append(kr[:N_CTX][:, _INTERLEAVE].reshape(BATCH, SEQ, QK_ROPE))
        x_pair = (formed[0], formed[0]) if formed else layer_input
        x1, h2s, y0, pos, counts = _post_mixer(
            l, _ctx_attention(q, k, vt), _lat_attention(q, k, vt, kc, vct, l), gm, *x_pair, mod,
            *post_params)
        plan = _plan(counts.reshape(N_POST_TILES, LANES))
        y = _experts(plan, h2s, y0, l, w_gate, w_up, w_down, ws_gate, ws_up, ws_down)
        layer_input = (y, pos, x1)

    n_ctx_tiles = N_CTX // (COMBINE_SUB * TM_POST)
    n_lat_tiles = N_LAT // (COMBINE_SUB * TM_POST)
    x_ctx = _combine(y, pos, x1, mod, fg, 0, n_ctx_tiles)
    x_lat = _combine(y, pos, x1, mod, fg, n_ctx_tiles, n_lat_tiles)
    y_prompt = x_ctx.reshape(BATCH, SEQ, D_MODEL)
    y_sample = x_lat.reshape(DEC_BATCH, DEC_SEQ, D_MODEL)
    return y_prompt, y_sample, jnp.stack(ckv_out, axis=1), jnp.stack(kr_out, axis=1)
```

```python
import functools
import math

import jax
import jax.numpy as jnp
import numpy as np
from jax import lax
from jax.experimental import pallas as pl
from jax.experimental.pallas import tpu as pltpu

D_MODEL = 1024
BATCH = 16
SEQ = 256
DEPTH = 2
DEC_BATCH = 4
DEC_SEQ = 4096
PAST_LEN = 256
GRID_W = 64
ATTN_HEADS = 4
QK_NOPE = 128
QK_ROPE = 64
V_DIM = 128
V_EXT = V_DIM + 16
Q_LORA = 384
KV_LORA = 256
ATTN_SCALE = (QK_NOPE + QK_ROPE) ** -0.5
ROPE_BASE = 10000.0
CHUNK = 128
GM_WIDTH = 512
GM_GROUPS = 4
GM_GROUP_DIM = 128
N_EXPERTS = 16
N_GROUPS = 4
EXPERTS_PER_GROUP = 4
EXPERT_FF = 256
SHARED_FF = 256
EPS = 1e-6

N_CTX = BATCH * SEQ
N_LAT = DEC_BATCH * DEC_SEQ
N_ROWS = N_CTX + N_LAT
N_COND = 1 + DEC_BATCH
COND_PAD = 16
HEAD_PAD = 256
IN_COLS = Q_LORA + KV_LORA + 2 * QK_ROPE + 2 * GM_WIDTH
LANES = 128
LOG2E = 1.4426950408889634

TM_PRE = 512
TM_POST = 256
COMB_TERMS = 3
XS_COLS = D_MODEL + COMB_TERMS * LANES
COMBINE_SUB = 4
POST_SUB = 4
PIECE = 16
LS_ROWS = TM_POST + N_GROUPS * PIECE
LS_PIECES = LS_ROWS // PIECE
N_POST_TILES = N_ROWS // TM_POST
N_LS = N_POST_TILES * LS_ROWS
TM_E = 512
E_PIECES = TM_E // PIECE
PIECE_SHIFT = PIECE.bit_length() - 1
E_SHIFT = E_PIECES.bit_length() - 1
N_ETILES = -(-(N_POST_TILES * (TM_POST // PIECE + N_GROUPS - 1)) // E_PIECES) + N_GROUPS
SRC_SLOTS = (N_ETILES + 1) * E_PIECES
TQ = 512
CTX_SUB = 4
SCORES_AHEAD = 4
HEADS_PER_STEP = 4
VMEM_LIMIT = 56 * 1024 * 1024

F32 = jnp.float32
BF16 = jnp.bfloat16


def _rms(x):
    return x * lax.rsqrt(jnp.mean(x * x, axis=-1, keepdims=True) + EPS)


def _gelu(x):
    return 0.5 * x * (1.0 + jnp.tanh(math.sqrt(2.0 / math.pi) * (x + 0.044715 * (x * x * x))))


def _silu(x):
    return x * (1.0 / (1.0 + jnp.exp(-x)))


def _dot(a, b):
    return jnp.dot(a, b, preferred_element_type=F32)


def _dot_nt(a, b):
    return lax.dot_general(a, b, (((1,), (1,)), ((), ())), preferred_element_type=F32)


def _cond_row(i, tm):
    n_ctx_tiles = N_CTX // tm
    per_batch = DEC_SEQ // tm
    return jnp.where(i < n_ctx_tiles, 0, 1 + (i - n_ctx_tiles) // per_batch)


def _rope_block(i, tm):
    n_ctx_tiles = N_CTX // tm
    per_batch = DEC_SEQ // tm
    return jnp.where(i < n_ctx_tiles, 0, 1 + (i - n_ctx_tiles) % per_batch)


def _ctx_lat_specs(tm, width, joint=False):
    n_ctx_tiles = N_CTX // tm
    lat_first = n_ctx_tiles if joint else 0
    return [pl.BlockSpec((tm, width), lambda i: (jnp.minimum(i, n_ctx_tiles - 1), 0)),
            pl.BlockSpec((tm, width), lambda i: (lat_first + jnp.maximum(i - n_ctx_tiles, 0), 0))]


def _ctx_or_lat(ctx_ref, lat_ref):
    n_ctx_tiles = N_CTX // ctx_ref.shape[0]
    return jnp.where(pl.program_id(0) < n_ctx_tiles, ctx_ref[...], lat_ref[...])


def _full(shape):
    n = len(shape)
    return pl.BlockSpec(shape, lambda *_: (0,) * n)


def _of_layer(layer, shape):
    n = len(shape)
    return pl.BlockSpec((None,) + tuple(shape), lambda *_: (layer,) + (0,) * n)


def _mod_spec(layer, tm, first_tile=0):
    return pl.BlockSpec((None, 1, 1, 6 * D_MODEL),
                        lambda i: (layer, _cond_row(first_tile + i, tm), 0, 0))


def _mod_kernel(cond_ref, w_ref, b_ref, o_ref):
    s_hi, s_lo = _split_bf16(_silu(cond_ref[...]), 2)
    w_hi, w_lo = _split_bf16(w_ref[0], 2)
    t = _dot(jnp.concatenate([s_hi, s_lo], axis=0), w_hi)
    o_ref[0] = (t[:COND_PAD] + t[COND_PAD:]) + _dot(s_hi, w_lo) + b_ref[0]


def _modulation(cond, w_ada, b_ada):
    tn = 1536
    return pl.pallas_call(
        _mod_kernel,
        out_shape=jax.ShapeDtypeStruct((DEPTH, COND_PAD, 6 * D_MODEL), F32),
        grid=(DEPTH, 6 * D_MODEL // tn),
        in_specs=[
            pl.BlockSpec((COND_PAD, D_MODEL), lambda l, j: (0, 0)),
            pl.BlockSpec((1, D_MODEL, tn), lambda l, j: (l, 0, j)),
            pl.BlockSpec((1, 1, tn), lambda l, j: (l, 0, j)),
        ],
        out_specs=pl.BlockSpec((1, COND_PAD, tn), lambda l, j: (l, 0, j)),
        compiler_params=pltpu.CompilerParams(
            dimension_semantics=("arbitrary", "arbitrary"), vmem_limit_bytes=VMEM_LIMIT),
        name="modulation",
    )(cond, w_ada, b_ada.reshape(DEPTH, 1, 6 * D_MODEL))


def _store_values_t(vt_ref, v):
    t = v.shape[0]
    v_t = v.T.astype(BF16)
    for hd in range(ATTN_HEADS):
        vt_ref[hd * V_EXT:hd * V_EXT + V_DIM, :] = v_t[hd * V_DIM:(hd + 1) * V_DIM, :]
        vt_ref[hd * V_EXT + V_DIM:(hd + 1) * V_EXT, :] = jnp.ones((V_EXT - V_DIM, t), BF16)


def _pre_kernel(*refs, after_experts):
    if after_experts:
        (y_ref, pos_ref, x1_ref, prev_mod_ref, mod_ref, rope_ref, g1_ref, win_ref, qg_ref, wqb_ref,
         kvg_ref, wkvb_ref, gmg_ref, ws_ref, bs_ref, ogm_ref,
         q_ref, k_ref, vt_ref, gm_ref, ckv_ref, kr_ref, x_ref) = refs
        tm = x1_ref.shape[0]
        gate2 = prev_mod_ref[0][:, 5 * D_MODEL:6 * D_MODEL]
        x = jnp.concatenate([x1_ref[sub * TM_POST:(sub + 1) * TM_POST, :] + gate2 * _unsort(y_ref, pos_ref, sub)
                             for sub in range(tm // TM_POST)], axis=0)
        x_ref[...] = x
    else:
        (xc_ref, xl_ref, mod_ref, rope_ref, g1_ref, win_ref, qg_ref, wqb_ref,
         kvg_ref, wkvb_ref, gmg_ref, ws_ref, bs_ref, ogm_ref,
         q_ref, k_ref, vt_ref, gm_ref, ckv_ref, kr_ref) = refs
        tm = xc_ref.shape[0]
        x = _ctx_or_lat(xc_ref, xl_ref)
    mod = mod_ref[0]
    shift1 = mod[:, 0:D_MODEL]
    scale1 = mod[:, D_MODEL:2 * D_MODEL]
    h = _rms(x) * g1_ref[...] * (1.0 + scale1) + shift1
    y = _dot(h.astype(BF16), win_ref[...])
    cq = y[:, 0:Q_LORA]
    ckv = y[:, Q_LORA:Q_LORA + KV_LORA]
    kr2 = y[:, 640:768]
    u = y[:, 768:768 + GM_WIDTH]
    vv = y[:, 768 + GM_WIDTH:768 + 2 * GM_WIDTH]

    rope = rope_ref[...]
    lane = lax.broadcasted_iota(jnp.int32, (tm, LANES), 1)

    kr_ref[...] = kr2[:, 0:QK_ROPE]
    t = kr2 * rope
    k_rot = jnp.where(lane < QK_ROPE, t + pltpu.roll(t, QK_ROPE, 1), 0.0).astype(BF16)

    ckv_n = _rms(ckv) * kvg_ref[...]
    ckv_ref[...] = ckv_n
    kv = _dot(ckv_n.astype(BF16), wkvb_ref[...])
    for hd in range(ATTN_HEADS):
        k_ref[:, hd * HEAD_PAD:hd * HEAD_PAD + QK_NOPE] = (
            kv[:, hd * QK_NOPE:(hd + 1) * QK_NOPE].astype(BF16))
        k_ref[:, hd * HEAD_PAD + QK_NOPE:(hd + 1) * HEAD_PAD] = k_rot
    _store_values_t(vt_ref, kv[:, ATTN_HEADS * QK_NOPE:])

    q = _dot((_rms(cq) * qg_ref[...]).astype(BF16), wqb_ref[...])
    for hd in range(ATTN_HEADS):
        q_ref[:, hd * HEAD_PAD:hd * HEAD_PAD + QK_NOPE] = (
            q[:, hd * HEAD_PAD:hd * HEAD_PAD + QK_NOPE].astype(BF16))
        t = q[:, hd * HEAD_PAD + QK_NOPE:(hd + 1) * HEAD_PAD] * rope
        q_ref[:, hd * HEAD_PAD + QK_NOPE:(hd + 1) * HEAD_PAD] = (
            t + pltpu.roll(t, QK_ROPE, 1)).astype(BF16)

    ug = _gelu(u)
    vg = _gelu(vv)
    cols = []
    for g in range(GM_GROUPS):
        sl = slice(g * GM_GROUP_DIM, (g + 1) * GM_GROUP_DIM)
        vn = (_rms(vg[:, sl]) * gmg_ref[:, sl]).astype(BF16)
        rows = []
        for c in range(tm // CHUNK):
            sv = _dot(ws_ref[g], vn[c * CHUNK:(c + 1) * CHUNK]) + bs_ref[:, sl]
            rows.append(ug[c * CHUNK:(c + 1) * CHUNK, sl] * sv)
        cols.append(jnp.concatenate(rows, axis=0))
    gm = jnp.concatenate(cols, axis=1)
    gm_ref[...] = (_rms(gm) * ogm_ref[...]).astype(BF16)


def _pre_mixer(layer, layer_input, mod, rope_tab, g1, win, qg, wqb, kvg, wkvb, gmg, ws, bs, ogm):
    tm = TM_PRE
    row = lambda i: (i, 0)
    mod_spec = _mod_spec(layer, tm)
    after_experts = len(layer_input) == 3
    if after_experts:
        layer_input = (*layer_input, mod)
        input_specs = [pl.BlockSpec((tm // TM_POST * LS_ROWS, D_MODEL), row),
                       pl.BlockSpec((tm, 1), row), pl.BlockSpec((tm, D_MODEL), row),
                       _mod_spec(layer - 1, tm)]
        extra_shape = (jax.ShapeDtypeStruct((N_ROWS, D_MODEL), F32),)
        extra_spec = (pl.BlockSpec((tm, D_MODEL), row),)
    else:
        input_specs = _ctx_lat_specs(tm, D_MODEL)
        extra_shape = extra_spec = ()
    return pl.pallas_call(
        functools.partial(_pre_kernel, after_experts=after_experts),
        out_shape=(
            jax.ShapeDtypeStruct((N_ROWS, ATTN_HEADS * HEAD_PAD), BF16),
            jax.ShapeDtypeStruct((N_ROWS, ATTN_HEADS * HEAD_PAD), BF16),
            jax.ShapeDtypeStruct((ATTN_HEADS * V_EXT, N_ROWS), BF16),
            jax.ShapeDtypeStruct((N_ROWS, GM_WIDTH), BF16),
            jax.ShapeDtypeStruct((N_ROWS, KV_LORA), F32),
            jax.ShapeDtypeStruct((N_ROWS, QK_ROPE), F32),
        ) + extra_shape,
        grid=(N_ROWS // tm,),
        in_specs=input_specs + [
            mod_spec,
            pl.BlockSpec((tm, LANES), lambda i: (_rope_block(i, tm), 0)),
            _of_layer(layer, (1, D_MODEL)),
            _of_layer(layer, (D_MODEL, IN_COLS)),
            _of_layer(layer, (1, Q_LORA)),
            _of_layer(layer, (Q_LORA, ATTN_HEADS * HEAD_PAD)),
            _of_layer(layer, (1, KV_LORA)),
            _of_layer(layer, (KV_LORA, ATTN_HEADS * (QK_NOPE + V_DIM))),
            _of_layer(layer, (1, GM_WIDTH)),
            _of_layer(layer, (GM_GROUPS, CHUNK, CHUNK)),
            _of_layer(layer, (CHUNK, GM_WIDTH)),
            _of_layer(layer, (1, GM_WIDTH)),
        ],
        out_specs=(
            pl.BlockSpec((tm, ATTN_HEADS * HEAD_PAD), row),
            pl.BlockSpec((tm, ATTN_HEADS * HEAD_PAD), row),
            pl.BlockSpec((ATTN_HEADS * V_EXT, tm), lambda i: (0, i)),
            pl.BlockSpec((tm, GM_WIDTH), row),
            pl.BlockSpec((tm, KV_LORA), row),
            pl.BlockSpec((tm, QK_ROPE), row),
        ) + extra_spec,
        compiler_params=pltpu.CompilerParams(
            dimension_semantics=("arbitrary",), vmem_limit_bytes=VMEM_LIMIT),
        name="pre_mixer",
    )(*layer_input, mod, rope_tab, g1, win, qg, wqb, kvg, wkvb, gmg, ws, bs, ogm)


def _cache_kv_kernel(ckv_ref, kr_ref, wkvb_ref, k_ref, vt_ref):
    kv = _dot(ckv_ref[0, 0].astype(BF16), wkvb_ref[0])
    kr = kr_ref[0, 0].astype(BF16)
    for hd in range(ATTN_HEADS):
        k_ref[0, 0, :, hd * HEAD_PAD:hd * HEAD_PAD + QK_NOPE] = (
            kv[:, hd * QK_NOPE:(hd + 1) * QK_NOPE].astype(BF16))
        k_ref[0, 0, :, hd * HEAD_PAD + QK_NOPE:(hd + 1) * HEAD_PAD] = kr
    _store_values_t(vt_ref.at[0, 0], kv[:, ATTN_HEADS * QK_NOPE:])


def _cache_kv(cache_ckv, cache_kr_pad, wkvb):
    blk = lambda w: pl.BlockSpec((1, 1, PAST_LEN, w), lambda l, b: (b, l, 0, 0))
    return pl.pallas_call(
        _cache_kv_kernel,
        out_shape=(
            jax.ShapeDtypeStruct((DEC_BATCH, DEPTH, PAST_LEN, ATTN_HEADS * HEAD_PAD), BF16),
            jax.ShapeDtypeStruct((DEC_BATCH, DEPTH, ATTN_HEADS * V_EXT, PAST_LEN), BF16),
        ),
        grid=(DEPTH, DEC_BATCH),
        in_specs=[
            blk(KV_LORA),
            blk(LANES),
            pl.BlockSpec((1, KV_LORA, ATTN_HEADS * (QK_NOPE + V_DIM)), lambda l, b: (l, 0, 0)),
        ],
        out_specs=(blk(ATTN_HEADS * HEAD_PAD),
                   pl.BlockSpec((1, 1, ATTN_HEADS * V_EXT, PAST_LEN), lambda l, b: (b, l, 0, 0))),
        compiler_params=pltpu.CompilerParams(
            dimension_semantics=("arbitrary", "arbitrary"), vmem_limit_bytes=VMEM_LIMIT),
        name="cache_kv",
    )(cache_ckv, cache_kr_pad, wkvb)


def _attn_scores(qh, key_blocks):
    return [_dot_nt(kb, qh) for kb in key_blocks]


def _attn_values(s, vt_blocks):
    s = [si.astype(BF16) for si in s]
    m = functools.reduce(jnp.maximum, [jnp.max(si, axis=0, keepdims=True) for si in s])
    o_t = sum(_dot(vt, jnp.exp2(si - m)) for vt, si in zip(vt_blocks, s))
    return (o_t[0:V_DIM, :] / o_t[V_DIM:V_DIM + 1, :]).T


def _ctx_attn_kernel(q_ref, k_ref, vt_ref, o_ref):
    chains = [(slice(sb * SEQ, (sb + 1) * SEQ), hd) for sb in range(CTX_SUB) for hd in range(ATTN_HEADS)]
    hs = lambda hd: slice(hd * HEAD_PAD, (hd + 1) * HEAD_PAD)
    scores = [_attn_scores(q_ref[rows, hs(hd)], [k_ref[rows, hs(hd)]]) for rows, hd in chains]
    for (rows, hd), s in zip(chains, scores):
        out = _attn_values(s, [vt_ref[hd * V_EXT:(hd + 1) * V_EXT, rows]])
        o_ref[rows, hd * V_DIM:(hd + 1) * V_DIM] = out.astype(o_ref.dtype)


def _ctx_attention(q, k, vt):
    blk = lambda w: pl.BlockSpec((CTX_SUB * SEQ, w), lambda b: (b, 0))
    return pl.pallas_call(
        _ctx_attn_kernel,
        out_shape=jax.ShapeDtypeStruct((N_CTX, ATTN_HEADS * V_DIM), BF16),
        grid=(BATCH // CTX_SUB,),
        in_specs=[blk(ATTN_HEADS * HEAD_PAD), blk(ATTN_HEADS * HEAD_PAD),
                  pl.BlockSpec((ATTN_HEADS * V_EXT, CTX_SUB * SEQ), lambda b: (0, b))],
        out_specs=blk(ATTN_HEADS * V_DIM),
        compiler_params=pltpu.CompilerParams(
            dimension_semantics=("arbitrary",), vmem_limit_bytes=VMEM_LIMIT),
        name="ctx_attention",
    )(q, k, vt)


def _lat_attn_kernel(q_ref, k_ref, vt_ref, kc_ref, vct_ref, o_ref):
    hs = [slice(hd * HEAD_PAD, (hd + 1) * HEAD_PAD) for hd in range(HEADS_PER_STEP)]
    vs = [slice(hd * V_DIM, (hd + 1) * V_DIM) for hd in range(HEADS_PER_STEP)]
    ve = [slice(hd * V_EXT, (hd + 1) * V_EXT) for hd in range(HEADS_PER_STEP)]

    def scores(hd):
        return _attn_scores(q_ref[:, hs[hd]], [kc_ref[0, 0, :, hs[hd]], k_ref[:, hs[hd]]])

    pending = [scores(hd) for hd in range(min(SCORES_AHEAD, HEADS_PER_STEP))]
    for hd in range(HEADS_PER_STEP):
        if hd + SCORES_AHEAD < HEADS_PER_STEP:
            pending.append(scores(hd + SCORES_AHEAD))
        o_ref[:, vs[hd]] = _attn_values(
            pending.pop(0), [vct_ref[0, 0, ve[hd], :], vt_ref[ve[hd], :]]).astype(o_ref.dtype)


def _lat_attention(q, k, vt, kc, vct, layer):
    nq = DEC_SEQ // TQ
    ctx_q_tiles = N_CTX // TQ
    ctx_kv_blocks = N_CTX // DEC_SEQ
    qk_w = HEADS_PER_STEP * HEAD_PAD
    v_w = HEADS_PER_STEP * V_DIM
    vt_w = HEADS_PER_STEP * V_EXT
    return pl.pallas_call(
        _lat_attn_kernel,
        out_shape=jax.ShapeDtypeStruct((N_LAT, ATTN_HEADS * V_DIM), BF16),
        grid=(DEC_BATCH, ATTN_HEADS // HEADS_PER_STEP, nq),
        in_specs=[
            pl.BlockSpec((TQ, qk_w), lambda b, h, i: (ctx_q_tiles + b * nq + i, h)),
            pl.BlockSpec((DEC_SEQ, qk_w), lambda b, h, i: (ctx_kv_blocks + b, h)),
            pl.BlockSpec((vt_w, DEC_SEQ), lambda b, h, i: (h, ctx_kv_blocks + b)),
            pl.BlockSpec((1, 1, PAST_LEN, qk_w), lambda b, h, i: (b, layer, 0, h)),
            pl.BlockSpec((1, 1, vt_w, PAST_LEN), lambda b, h, i: (b, layer, h, 0)),
        ],
        out_specs=pl.BlockSpec((TQ, v_w), lambda b, h, i: (b * nq + i, h)),
        compiler_params=pltpu.CompilerParams(
            dimension_semantics=("arbitrary", "arbitrary", "arbitrary"),
            vmem_limit_bytes=VMEM_LIMIT),
        name="lat_attention",
    )(q, k, vt, kc, vct)


def _group_peer(x, row, d, width, period):
    step = d * width
    ahead = pltpu.roll(x, N_EXPERTS - step, 0)
    wraps = (row & (period - 1)) + step >= period
    if period == N_EXPERTS:
        return ahead, wraps
    return jnp.where(wraps, pltpu.roll(x, period - step, 0), ahead), wraps


def _route(logits_t, bias):
    row = lax.broadcasted_iota(jnp.int32, logits_t.shape, 0)
    s = 1.0 / (1.0 + jnp.exp(-logits_t))
    sb = s + bias
    rank = jnp.zeros(sb.shape, jnp.int32)
    for d in range(1, EXPERTS_PER_GROUP):
        o, wraps = _group_peer(sb, row, d, 1, EXPERTS_PER_GROUP)
        beats = (o > sb) | ((o == sb) & wraps)
        rank = rank + beats.astype(jnp.int32)
    top2 = rank < 2
    t = jnp.where(top2, sb, 0.0)
    gscore = t
    for d in range(1, EXPERTS_PER_GROUP):
        gscore = gscore + _group_peer(t, row, d, 1, EXPERTS_PER_GROUP)[0]
    grank = jnp.zeros(sb.shape, jnp.int32)
    for d in range(1, N_GROUPS):
        o, wraps = _group_peer(gscore, row, d, EXPERTS_PER_GROUP, N_EXPERTS)
        beats = (o > gscore) | ((o == gscore) & wraps)
        grank = grank + beats.astype(jnp.int32)
    in_group = grank == 0
    w = jnp.where(top2 & in_group, s, 0.0)
    denom = jnp.sum(w, axis=0, keepdims=True)
    group_flag = jnp.where(in_group & ((row & (EXPERTS_PER_GROUP - 1)) == 0), 1.0, 0.0)
    return w / denom, group_flag


def _experts_to_lanes(x_t):
    pad = jnp.zeros((LANES - N_EXPERTS, x_t.shape[1]), x_t.dtype)
    return jnp.concatenate([x_t, pad], axis=0).T


def _split_bf16(x, terms):
    out = []
    for _ in range(terms - 1):
        t = x.astype(BF16)
        out.append(t)
        x = x - t.astype(F32)
    out.append(x.astype(BF16))
    return out


def _post_kernel(ac_ref, al_ref, gm_ref, xc_ref, xl_ref, mod_ref, oag_ref, wout_ref, g2_ref,
                 wr_ref, rb_ref, x1_ref, h2s_ref, y0_ref, pos_ref, cnt_ref):
    mod = mod_ref[0]
    gate1 = mod[:, 2 * D_MODEL:3 * D_MODEL]
    shift2 = mod[:, 3 * D_MODEL:4 * D_MODEL]
    scale2 = mod[:, 4 * D_MODEL:5 * D_MODEL]
    y0_ref[...] = jnp.zeros(y0_ref.shape, y0_ref.dtype)
    tm = xc_ref.shape[0]
    halves = [slice(0, tm // 2), slice(tm // 2, tm)]
    attn = _ctx_or_lat(ac_ref, al_ref)
    x_in = _ctx_or_lat(xc_ref, xl_ref)
    mixed = []
    for rows in halves:
        an = (_rms(attn[rows].astype(F32)) * oag_ref[...]).astype(BF16)
        mixed.append(_dot(jnp.concatenate([an, gm_ref[rows, :]], axis=1), wout_ref[...]))
    h2_hi, logits = [], []
    for rows, mix in zip(halves, mixed):
        x1 = x_in[rows] + gate1 * mix
        x1_ref[rows, :] = x1
        hi, lo = _split_bf16(_rms(x1) * g2_ref[...] * (1.0 + scale2) + shift2, 2)
        t = _dot(jnp.concatenate([hi, lo], axis=0), wr_ref[...])
        half = tm // 2
        logits.append((t[:half, :LANES] + t[half:, :LANES]) + (t[:half, LANES:] + t[half:, LANES:]))
        h2_hi.append(hi)
    h2_hi = jnp.concatenate(h2_hi, axis=0)
    logits = jnp.concatenate(logits, axis=0)
    tiles = [slice(sub * TM_POST, (sub + 1) * TM_POST) for sub in range(POST_SUB)]
    r_i = lax.broadcasted_iota(jnp.int32, (TM_POST, TM_POST), 0)
    c_i = lax.broadcasted_iota(jnp.int32, (TM_POST, TM_POST), 1)
    before = jnp.where(c_i < r_i, 1.0, 0.0).astype(BF16)
    routed = []
    for rows in tiles:
        comb_t, flag_t = _route(logits[rows].T[0:N_EXPERTS, :], rb_ref[...])
        routed.append((_experts_to_lanes(comb_t), _experts_to_lanes(flag_t)))
    ranks = [_dot(before, flag.astype(BF16)) for _, flag in routed]
    for sub, rows in enumerate(tiles):
        comb, flag = routed[sub]
        _sort_tile(h2_hi[rows], comb, flag, ranks[sub],
                   h2s_ref.at[sub * LS_ROWS:(sub + 1) * LS_ROWS], pos_ref.at[rows], cnt_ref.at[sub])


def _sort_tile(h2_hi, comb, flag, rank, h2s_ref, pos_ref, cnt_ref):
    tm = TM_POST
    count = jnp.sum(flag, axis=0, keepdims=True)
    pieces = jnp.floor((count + (PIECE - 1)) * (1.0 / PIECE))
    start = (pltpu.roll(pieces, EXPERTS_PER_GROUP, 1) + pltpu.roll(pieces, 2 * EXPERTS_PER_GROUP, 1)
             + pltpu.roll(pieces, 3 * EXPERTS_PER_GROUP, 1)) * PIECE
    pos = jnp.sum(flag * (start + rank), axis=-1, keepdims=True)
    pos_ref[...] = pos
    cnt_ref[...] = count.astype(jnp.int32)
    pos_row = jnp.transpose(jnp.broadcast_to(pos, (tm, LANES)))[0:1, :].astype(jnp.int32)
    place = lax.broadcasted_iota(jnp.int32, (LS_ROWS, tm), 0) == pos_row
    place = jnp.where(place, 1.0, 0.0).astype(BF16)
    wide = jnp.concatenate([h2_hi] + _split_bf16(comb, COMB_TERMS), axis=1)
    h2s_ref[...] = _dot(place, wide).astype(BF16)


def _post_mixer(layer, attn_ctx, attn_lat, gm, x_ctx, x_lat, mod, oag, wout, g2, wr, rb):
    joint_x = x_ctx is x_lat
    tm = POST_SUB * TM_POST
    ls = POST_SUB * LS_ROWS
    row = lambda i: (i, 0)
    return pl.pallas_call(
        _post_kernel,
        out_shape=(
            jax.ShapeDtypeStruct((N_ROWS, D_MODEL), F32),
            jax.ShapeDtypeStruct((N_LS, XS_COLS), BF16),
            jax.ShapeDtypeStruct((N_LS, D_MODEL), BF16),
            jax.ShapeDtypeStruct((N_ROWS, 1), F32),
            jax.ShapeDtypeStruct((N_POST_TILES, 1, LANES), jnp.int32),
        ),
        grid=(N_ROWS // tm,),
        in_specs=_ctx_lat_specs(tm, ATTN_HEADS * V_DIM) + [
            pl.BlockSpec((tm, GM_WIDTH), row),
        ] + _ctx_lat_specs(tm, D_MODEL, joint=joint_x) + [
            _mod_spec(layer, tm),
            _of_layer(layer, (1, ATTN_HEADS * V_DIM)),
            _of_layer(layer, (D_MODEL, D_MODEL)),
            _of_layer(layer, (1, D_MODEL)),
            _full((D_MODEL, 2 * LANES)),
            _full((N_EXPERTS, 1)),
        ],
        out_specs=(
            pl.BlockSpec((tm, D_MODEL), row),
            pl.BlockSpec((ls, XS_COLS), row),
            pl.BlockSpec((ls, D_MODEL), row),
            pl.BlockSpec((tm, 1), row),
            pl.BlockSpec((POST_SUB, 1, LANES), lambda i: (i, 0, 0)),
        ),
        compiler_params=pltpu.CompilerParams(
            dimension_semantics=("arbitrary",), vmem_limit_bytes=VMEM_LIMIT),
        name="post_mixer",
    )(attn_ctx, attn_lat, gm, x_ctx, x_lat, mod, oag, wout, g2, wr, rb)


def _plan_kernel(cnt_ref, tg_ref, tn_ref, src_ref, nu_ref):
    def clear_src(j, c):
        for u in range(E_PIECES):
            src_ref[j * E_PIECES + u] = 0
        return c

    lax.fori_loop(0, SRC_SLOTS // E_PIECES, clear_src, 0)

    def clear_tile(j, c):
        tg_ref[j] = N_GROUPS - 1
        tn_ref[j] = 0
        return c

    lax.fori_loop(0, N_ETILES, clear_tile, 0)

    def n_pieces(i, g):
        return lax.shift_right_logical(cnt_ref[i, g * EXPERTS_PER_GROUP] + (PIECE - 1), PIECE_SHIFT)

    t = jnp.int32(0)
    for g in range(N_GROUPS):
        def tile_body(i, s, g=g):
            first = i * LS_PIECES
            for gp in range(g):
                first = first + n_pieces(i, gp)

            for p in range(TM_POST // PIECE):
                src_ref[s + p] = first + p
            return s + n_pieces(i, g)

        s0 = t * E_PIECES
        s1 = lax.fori_loop(0, N_POST_TILES, tile_body, s0)
        n = s1 - s0
        tiles = lax.shift_right_logical(n + (E_PIECES - 1), E_SHIFT)

        def mark_tile(u, c, g=g, n=n, t=t):
            tg_ref[t + u] = g
            tn_ref[t + u] = jnp.minimum(n - u * E_PIECES, E_PIECES)
            return c

        lax.fori_loop(0, tiles, mark_tile, 0)
        t = t + tiles
    nu_ref[0] = t


def _plan(counts):
    smem = pl.BlockSpec(memory_space=pltpu.SMEM)
    return pl.pallas_call(
        _plan_kernel,
        out_shape=(
            jax.ShapeDtypeStruct((N_ETILES,), jnp.int32),
            jax.ShapeDtypeStruct((N_ETILES,), jnp.int32),
            jax.ShapeDtypeStruct((SRC_SLOTS,), jnp.int32),
            jax.ShapeDtypeStruct((1,), jnp.int32),
        ),
        in_specs=[smem],
        out_specs=(smem, smem, smem, smem),
        name="expert_plan",
    )(counts)


GATHER_X, SCATTER_Y = 0, 1


def _moe_kernel(tg_ref, tn_ref, src_ref, nu_ref, h2s_hbm, y0_hbm, wg32_ref, wu32_ref, wd32_ref,
                wsg32_ref, wsu32_ref, wsd32_ref, y_hbm, xbuf, ybuf, wg_ref, wu_ref, wd_ref,
                wsg_ref, wsu_ref, wsd_ref, sem):
    del y0_hbm
    j = pl.program_id(0)
    n_used = nu_ref[0]
    slot = lax.rem(j, 2)

    @pl.when(j == 0)
    def _():
        wsg_ref[...] = wsg32_ref[0].astype(BF16)
        wsu_ref[...] = wsu32_ref[0].astype(BF16)
        wsd_ref[...] = wsd32_ref[0].astype(BF16)

    @pl.when((j == 0) | (tg_ref[j] != tg_ref[jnp.maximum(j - 1, 0)]))
    def _():
        for k in range(EXPERTS_PER_GROUP):
            wg_ref[k] = wg32_ref[0, k].astype(BF16)
            wu_ref[k] = wu32_ref[0, k].astype(BF16)
            wd_ref[k] = wd32_ref[0, k].astype(BF16)

    def piece_rows(t, k):
        hbm_rows = pl.ds(pl.multiple_of(src_ref[t * E_PIECES + k] * PIECE, PIECE), PIECE)
        buf_rows = pl.ds(pl.multiple_of(k * PIECE, PIECE), PIECE)
        return hbm_rows, buf_rows

    def gather_copies(t, k, slot):
        hbm_rows, buf_rows = piece_rows(t, k)
        return (
            pltpu.make_async_copy(h2s_hbm.at[hbm_rows], xbuf.at[slot, buf_rows], sem.at[GATHER_X, slot]),
        )

    def scatter_copies(t, k, slot):
        hbm_rows, buf_rows = piece_rows(t, k)
        return (
            pltpu.make_async_copy(ybuf.at[slot, buf_rows], y_hbm.at[hbm_rows], sem.at[SCATTER_Y, slot]),
        )

    def for_pieces(t, slot, copies, action):
        def body(k, c):
            for cp in copies(t, k, slot):
                action(cp)
            return c

        @pl.when(tn_ref[t] == E_PIECES)
        def _():
            lax.fori_loop(0, E_PIECES, body, 0, unroll=8)

        @pl.when(tn_ref[t] != E_PIECES)
        def _():
            lax.fori_loop(0, tn_ref[t], body, 0)

    start = lambda cp: cp.start()
    wait = lambda cp: cp.wait()

    @pl.when(j == 0)
    def _():
        xbuf[...] = jnp.zeros(xbuf.shape, xbuf.dtype)
        for_pieces(0, 0, gather_copies, start)

    @pl.when(j + 1 < n_used)
    def _():
        for_pieces(j + 1, 1 - slot, gather_copies, start)

    @pl.when(j < n_used)
    def _():
        for_pieces(j, slot, gather_copies, wait)
        x = xbuf[slot, :, 0:D_MODEL]
        comb = sum(xbuf[slot, :, D_MODEL + t * LANES:D_MODEL + (t + 1) * LANES].astype(F32)
                   for t in range(COMB_TERMS))
        lane = lax.broadcasted_iota(jnp.int32, (TM_E, LANES), 1)
        first = tg_ref[j] * EXPERTS_PER_GROUP
        y = _dot((_silu(_dot(x, wsg_ref[...])) * _dot(x, wsu_ref[...])).astype(BF16), wsd_ref[...])
        for k in range(EXPERTS_PER_GROUP):
            wk = jnp.sum(jnp.where(lane == first + k, comb, 0.0), axis=-1, keepdims=True)
            act = _silu(_dot(x, wg_ref[k])) * _dot(x, wu_ref[k]) * wk
            y = y + _dot(act.astype(BF16), wd_ref[k])
        ybuf[slot] = y.astype(BF16)
        for_pieces(j, slot, scatter_copies, start)

    @pl.when((j >= 1) & (j < n_used))
    def _():
        for_pieces(j - 1, 1 - slot, scatter_copies, wait)

    @pl.when(j == n_used - 1)
    def _():
        for_pieces(j, slot, scatter_copies, wait)


def _experts(plan, h2s, y0, layer, w_gate, w_up, w_down, ws_gate, ws_up, ws_down):
    tg, tn, src, nu = plan
    any_spec = pl.BlockSpec(memory_space=pl.ANY)
    group = lambda j, tg, tn, src, nu: (layer, tg[j], 0, 0)
    whole = lambda j, tg, tn, src, nu: (layer, 0, 0)
    g = EXPERTS_PER_GROUP
    return pl.pallas_call(
        _moe_kernel,
        out_shape=jax.ShapeDtypeStruct((N_LS, D_MODEL), BF16),
        grid_spec=pltpu.PrefetchScalarGridSpec(
            num_scalar_prefetch=4,
            grid=(N_ETILES,),
            in_specs=[
                any_spec, any_spec,
                pl.BlockSpec((1, g, D_MODEL, EXPERT_FF), group),
                pl.BlockSpec((1, g, D_MODEL, EXPERT_FF), group),
                pl.BlockSpec((1, g, EXPERT_FF, D_MODEL), group),
                pl.BlockSpec((1, D_MODEL, SHARED_FF), whole),
                pl.BlockSpec((1, D_MODEL, SHARED_FF), whole),
                pl.BlockSpec((1, SHARED_FF, D_MODEL), whole),
            ],
            out_specs=any_spec,
            scratch_shapes=[
                pltpu.VMEM((2, TM_E, XS_COLS), BF16),
                pltpu.VMEM((2, TM_E, D_MODEL), BF16),
                pltpu.VMEM((g, D_MODEL, EXPERT_FF), BF16),
                pltpu.VMEM((g, D_MODEL, EXPERT_FF), BF16),
                pltpu.VMEM((g, EXPERT_FF, D_MODEL), BF16),
                pltpu.VMEM((D_MODEL, SHARED_FF), BF16),
                pltpu.VMEM((D_MODEL, SHARED_FF), BF16),
                pltpu.VMEM((SHARED_FF, D_MODEL), BF16),
                pltpu.SemaphoreType.DMA((2, 2)),
            ],
        ),
        input_output_aliases={5: 0},
        compiler_params=pltpu.CompilerParams(
            dimension_semantics=("arbitrary",), vmem_limit_bytes=VMEM_LIMIT),
        name="experts",
    )(tg, tn, src, nu, h2s, y0, w_gate, w_up, w_down, ws_gate, ws_up, ws_down)


def _unsort(y_ref, pos_ref, sub):
    pos = pos_ref[sub * TM_POST:(sub + 1) * TM_POST, :].astype(jnp.int32)
    pick = lax.broadcasted_iota(jnp.int32, (TM_POST, LS_ROWS), 1) == pos
    return _dot(jnp.where(pick, 1.0, 0.0).astype(BF16), y_ref[sub * LS_ROWS:(sub + 1) * LS_ROWS, :])


def _combine_kernel(y_ref, pos_ref, x1_ref, mod_ref, fg_ref, o_ref):
    gate2 = mod_ref[0][:, 5 * D_MODEL:6 * D_MODEL]
    for sub in range(COMBINE_SUB):
        rows = slice(sub * TM_POST, (sub + 1) * TM_POST)
        x2 = x1_ref[rows, :] + gate2 * _unsort(y_ref, pos_ref, sub)
        o_ref[rows, :] = _rms(x2) * fg_ref[...]


def _combine(y, pos, x1, mod, fg, first_tile, n_tiles):
    tm = COMBINE_SUB * TM_POST
    row = lambda i: (first_tile + i, 0)
    return pl.pallas_call(
        _combine_kernel,
        out_shape=jax.ShapeDtypeStruct((n_tiles * tm, D_MODEL), F32),
        grid=(n_tiles,),
        in_specs=[
            pl.BlockSpec((COMBINE_SUB * LS_ROWS, D_MODEL), row),
            pl.BlockSpec((tm, 1), row),
            pl.BlockSpec((tm, D_MODEL), row),
            _mod_spec(DEPTH - 1, tm, first_tile),
            _full((1, D_MODEL)),
        ],
        out_specs=pl.BlockSpec((tm, D_MODEL), lambda i: (i, 0)),
        compiler_params=pltpu.CompilerParams(
            dimension_semantics=("arbitrary",), vmem_limit_bytes=VMEM_LIMIT),
        name="combine",
    )(y, pos, x1, mod, fg)


def _rope_table():
    rows = DEC_SEQ // GRID_W
    row = jnp.repeat(jnp.arange(rows, dtype=F32), GRID_W)
    col = jnp.tile(jnp.arange(GRID_W, dtype=F32), rows)
    half = QK_ROPE // 2
    freqs = 1.0 / (ROPE_BASE ** (jnp.arange(0, half, 2, dtype=F32) / half))
    ang = jnp.concatenate([row[:, None] * freqs, col[:, None] * freqs], axis=-1)
    cos, sin = jnp.cos(ang), jnp.sin(ang)
    lat = jnp.concatenate([cos, cos, -sin, sin], axis=-1)
    ident = jnp.concatenate([jnp.ones((TM_PRE, QK_ROPE), F32), jnp.zeros((TM_PRE, QK_ROPE), F32)], axis=-1)
    return jnp.concatenate([ident, lat], axis=0)


_DEINT = np.concatenate([np.arange(0, QK_ROPE, 2), np.arange(1, QK_ROPE, 2)])
_SWAP = np.concatenate([np.arange(1, QK_ROPE, 2), np.arange(0, QK_ROPE, 2)])
_INTERLEAVE = np.argsort(_DEINT)


def _layout_w_in(w_in):
    cq_ckv = w_in[..., :Q_LORA + KV_LORA]
    kr = w_in[..., Q_LORA + KV_LORA:Q_LORA + KV_LORA + QK_ROPE]
    uv = w_in[..., Q_LORA + KV_LORA + QK_ROPE:]
    return jnp.concatenate([cq_ckv, kr[..., _DEINT], kr[..., _SWAP], uv], axis=-1).astype(BF16)


def _layout_w_qb(w_qb):
    w = (w_qb * (ATTN_SCALE * LOG2E)).reshape(DEPTH, Q_LORA, ATTN_HEADS, QK_NOPE + QK_ROPE)
    nope, rope = w[..., :QK_NOPE], w[..., QK_NOPE:]
    w = jnp.concatenate([nope, rope[..., _DEINT], rope[..., _SWAP]], axis=-1)
    return w.reshape(DEPTH, Q_LORA, ATTN_HEADS * HEAD_PAD).astype(BF16)


def _layout_w_kvb(w_kvb):
    w = w_kvb.reshape(DEPTH, KV_LORA, ATTN_HEADS, QK_NOPE + V_DIM)
    k = w[..., :QK_NOPE].reshape(DEPTH, KV_LORA, ATTN_HEADS * QK_NOPE)
    v = w[..., QK_NOPE:].reshape(DEPTH, KV_LORA, ATTN_HEADS * V_DIM)
    return jnp.concatenate([k, v], axis=-1).astype(BF16)


def kernel(x_prompt, x_sample, cache_ckv, cache_krope, c, c_ctx, norm1_g, w_ada, b_ada, w_in,
           q_norm_g, w_qb, kv_norm_g, w_kvb, gm_norm_g, w_spatial, b_spatial, onorm_attn_g,
           onorm_gm_g, w_out, norm2_g, w_router, router_bias, w_gate, w_up, w_down, ws_gate,
           ws_up, ws_down, final_norm_g):
    x_ctx = x_prompt.reshape(N_CTX, D_MODEL)
    x_lat = x_sample.reshape(N_LAT, D_MODEL)
    cond = jnp.concatenate([c_ctx[None, :], c, jnp.zeros((COND_PAD - N_COND, D_MODEL), F32)], axis=0)
    mod = _modulation(cond, w_ada, b_ada).reshape(DEPTH, COND_PAD, 1, 6 * D_MODEL)
    rope_tab = _rope_table()

    wkvb = _layout_w_kvb(w_kvb)
    cache_kr = jnp.pad(cache_krope[..., _DEINT], ((0, 0), (0, 0), (0, 0), (0, LANES - QK_ROPE)))
    kc, vct = _cache_kv(cache_ckv, cache_kr, wkvb)

    wr = jnp.pad(w_router, ((0, 0), (0, LANES - N_EXPERTS)))
    wr_hi = wr.astype(BF16)
    wr = jnp.concatenate([wr_hi, (wr - wr_hi.astype(F32)).astype(BF16)], axis=1)
    rb = router_bias.reshape(N_EXPERTS, 1)
    fg = final_norm_g.reshape(1, D_MODEL)

    row_stack = lambda g: g.reshape(DEPTH, 1, -1)
    bs = jnp.broadcast_to(jnp.swapaxes(b_spatial, 1, 2)[..., None],
                          (DEPTH, CHUNK, GM_GROUPS, GM_GROUP_DIM)).reshape(DEPTH, CHUNK, GM_WIDTH)
    pre_params = (row_stack(norm1_g), _layout_w_in(w_in), row_stack(q_norm_g), _layout_w_qb(w_qb),
                  row_stack(kv_norm_g), wkvb, row_stack(gm_norm_g), w_spatial.astype(BF16), bs,
                  row_stack(onorm_gm_g))
    post_params = (row_stack(onorm_attn_g), w_out.astype(BF16), row_stack(norm2_g), wr, rb)

    ckv_out, kr_out = [], []
    layer_input = (x_ctx, x_lat)
    for l in range(DEPTH):
        q, k, vt, gm, ckv_n, kr, *formed = _pre_mixer(l, layer_input, mod, rope_tab, *pre_params)
        ckv_out.append(ckv_n[:N_CTX].reshape(BATCH, SEQ, KV_LORA))
        kr_out.append(kr[:N_CTX][:, _INTERLEAVE].reshape(BATCH, SEQ, QK_ROPE))
        x_pair = (formed[0], formed[0]) if formed else layer_input
        x1, h2s, y0, pos, counts = _post_mixer(
            l, _ctx_attention(q, k, vt), _lat_attention(q, k, vt, kc, vct, l), gm, *x_pair, mod,
            *post_params)
        plan = _plan(counts.reshape(N_POST_TILES, LANES))
        y = _experts(plan, h2s, y0, l, w_gate, w_up, w_down, ws_gate, ws_up, ws_down)
        layer_input = (y, pos, x1)

    n_ctx_tiles = N_CTX // (COMBINE_SUB * TM_POST)
    n_lat_tiles = N_LAT // (COMBINE_SUB * TM_POST)
    x_ctx = _combine(y, pos, x1, mod, fg, 0, n_ctx_tiles)
    x_lat = _combine(y, pos, x1, mod, fg, n_ctx_tiles, n_lat_tiles)
    y_prompt = x_ctx.reshape(BATCH, SEQ, D_MODEL)
    y_sample = x_lat.reshape(DEC_BATCH, DEC_SEQ, D_MODEL)
    return y_prompt, y_sample, jnp.stack(ckv_out, axis=1), jnp.stack(kr_out, axis=1)
```

```python
import functools
import math

import jax
import jax.numpy as jnp
import numpy as np
from jax import lax
from jax.experimental import pallas as pl
from jax.experimental.pallas import tpu as pltpu

D_MODEL = 1024
BATCH = 16
SEQ = 256
DEPTH = 2
DEC_BATCH = 4
DEC_SEQ = 4096
PAST_LEN = 256
GRID_W = 64
ATTN_HEADS = 4
QK_NOPE = 128
QK_ROPE = 64
V_DIM = 128
V_EXT = V_DIM + 16
Q_LORA = 384
KV_LORA = 256
ATTN_SCALE = (QK_NOPE + QK_ROPE) ** -0.5
ROPE_BASE = 10000.0
CHUNK = 128
GM_WIDTH = 512
GM_GROUPS = 4
GM_GROUP_DIM = 128
N_EXPERTS = 16
N_GROUPS = 4
EXPERTS_PER_GROUP = 4
EXPERT_FF = 256
SHARED_FF = 256
EPS = 1e-6

N_CTX = BATCH * SEQ
N_LAT = DEC_BATCH * DEC_SEQ
N_ROWS = N_CTX + N_LAT
N_COND = 1 + DEC_BATCH
COND_PAD = 16
HEAD_PAD = 256
IN_COLS = Q_LORA + KV_LORA + 2 * QK_ROPE + 2 * GM_WIDTH
LANES = 128
LOG2E = 1.4426950408889634

TM_PRE = 512
TM_POST = 256
COMB_TERMS = 3
XS_COLS = D_MODEL + COMB_TERMS * LANES
COMBINE_SUB = 4
POST_SUB = 4
PIECE = 16
LS_ROWS = TM_POST + N_GROUPS * PIECE
LS_PIECES = LS_ROWS // PIECE
SPARE_PIECES = LS_PIECES - TM_POST // PIECE
N_POST_TILES = N_ROWS // TM_POST
N_LS = N_POST_TILES * LS_ROWS
TM_E = 512
E_PIECES = TM_E // PIECE
PIECE_SHIFT = PIECE.bit_length() - 1
E_SHIFT = E_PIECES.bit_length() - 1
N_ETILES = -(-(N_POST_TILES * (TM_POST // PIECE + N_GROUPS - 1)) // E_PIECES) + N_GROUPS
SRC_SLOTS = (N_ETILES + 1) * E_PIECES
TQ = 512
CTX_SUB = 4
SCORES_AHEAD = 4
HEADS_PER_STEP = 4
VMEM_LIMIT = 56 * 1024 * 1024

F32 = jnp.float32
BF16 = jnp.bfloat16


def _rms(x):
    return x * lax.rsqrt(jnp.mean(x * x, axis=-1, keepdims=True) + EPS)


def _gelu(x):
    return 0.5 * x * (1.0 + jnp.tanh(math.sqrt(2.0 / math.pi) * (x + 0.044715 * (x * x * x))))


def _silu(x):
    return x * (1.0 / (1.0 + jnp.exp(-x)))


def _dot(a, b):
    return jnp.dot(a, b, preferred_element_type=F32)


def _dot_nt(a, b):
    return lax.dot_general(a, b, (((1,), (1,)), ((), ())), preferred_element_type=F32)


def _cond_row(i, tm):
    n_ctx_tiles = N_CTX // tm
    per_batch = DEC_SEQ // tm
    return jnp.where(i < n_ctx_tiles, 0, 1 + (i - n_ctx_tiles) // per_batch)


def _rope_block(i, tm):
    n_ctx_tiles = N_CTX // tm
    per_batch = DEC_SEQ // tm
    return jnp.where(i < n_ctx_tiles, 0, 1 + (i - n_ctx_tiles) % per_batch)


def _ctx_lat_specs(tm, width, joint=False):
    n_ctx_tiles = N_CTX // tm
    lat_first = n_ctx_tiles if joint else 0
    return [pl.BlockSpec((tm, width), lambda i: (jnp.minimum(i, n_ctx_tiles - 1), 0)),
            pl.BlockSpec((tm, width), lambda i: (lat_first + jnp.maximum(i - n_ctx_tiles, 0), 0))]


def _ctx_or_lat(ctx_ref, lat_ref):
    n_ctx_tiles = N_CTX // ctx_ref.shape[0]
    return jnp.where(pl.program_id(0) < n_ctx_tiles, ctx_ref[...], lat_ref[...])


def _full(shape):
    n = len(shape)
    return pl.BlockSpec(shape, lambda *_: (0,) * n)


def _of_layer(layer, shape):
    n = len(shape)
    return pl.BlockSpec((None,) + tuple(shape), lambda *_: (layer,) + (0,) * n)


def _mod_spec(layer, tm, first_tile=0):
    return pl.BlockSpec((None, 1, 1, 6 * D_MODEL),
                        lambda i: (layer, _cond_row(first_tile + i, tm), 0, 0))


def _mod_kernel(cond_ref, w_ref, b_ref, o_ref):
    s_hi, s_lo = _split_bf16(_silu(cond_ref[...]), 2)
    w_hi, w_lo = _split_bf16(w_ref[0], 2)
    t = _dot(jnp.concatenate([s_hi, s_lo], axis=0), w_hi)
    o_ref[0] = (t[:COND_PAD] + t[COND_PAD:]) + _dot(s_hi, w_lo) + b_ref[0]


def _modulation(cond, w_ada, b_ada):
    tn = 1536
    return pl.pallas_call(
        _mod_kernel,
        out_shape=jax.ShapeDtypeStruct((DEPTH, COND_PAD, 6 * D_MODEL), F32),
        grid=(DEPTH, 6 * D_MODEL // tn),
        in_specs=[
            pl.BlockSpec((COND_PAD, D_MODEL), lambda l, j: (0, 0)),
            pl.BlockSpec((1, D_MODEL, tn), lambda l, j: (l, 0, j)),
            pl.BlockSpec((1, 1, tn), lambda l, j: (l, 0, j)),
        ],
        out_specs=pl.BlockSpec((1, COND_PAD, tn), lambda l, j: (l, 0, j)),
        compiler_params=pltpu.CompilerParams(
            dimension_semantics=("arbitrary", "arbitrary"), vmem_limit_bytes=VMEM_LIMIT),
        name="modulation",
    )(cond, w_ada, b_ada.reshape(DEPTH, 1, 6 * D_MODEL))


def _store_values_t(vt_ref, v):
    t = v.shape[0]
    v_t = v.T.astype(BF16)
    for hd in range(ATTN_HEADS):
        vt_ref[hd * V_EXT:hd * V_EXT + V_DIM, :] = v_t[hd * V_DIM:(hd + 1) * V_DIM, :]
        vt_ref[hd * V_EXT + V_DIM:(hd + 1) * V_EXT, :] = jnp.ones((V_EXT - V_DIM, t), BF16)


def _pre_kernel(*refs, after_experts):
    if after_experts:
        (y_ref, pos_ref, x1_ref, prev_mod_ref, mod_ref, rope_ref, g1_ref, win_ref, qg_ref, wqb_ref,
         kvg_ref, wkvb_ref, gmg_ref, ws_ref, bs_ref, ogm_ref,
         q_ref, k_ref, vt_ref, gm_ref, ckv_ref, kr_ref, x_ref) = refs
        tm = x1_ref.shape[0]
        gate2 = prev_mod_ref[0][:, 5 * D_MODEL:6 * D_MODEL]
        x = jnp.concatenate([x1_ref[sub * TM_POST:(sub + 1) * TM_POST, :] + gate2 * _unsort(y_ref, pos_ref, sub)
                             for sub in range(tm // TM_POST)], axis=0)
        x_ref[...] = x
    else:
        (xc_ref, xl_ref, mod_ref, rope_ref, g1_ref, win_ref, qg_ref, wqb_ref,
         kvg_ref, wkvb_ref, gmg_ref, ws_ref, bs_ref, ogm_ref,
         q_ref, k_ref, vt_ref, gm_ref, ckv_ref, kr_ref) = refs
        tm = xc_ref.shape[0]
        x = _ctx_or_lat(xc_ref, xl_ref)
    mod = mod_ref[0]
    shift1 = mod[:, 0:D_MODEL]
    scale1 = mod[:, D_MODEL:2 * D_MODEL]
    h = _rms(x) * g1_ref[...] * (1.0 + scale1) + shift1
    y = _dot(h.astype(BF16), win_ref[...])
    cq = y[:, 0:Q_LORA]
    ckv = y[:, Q_LORA:Q_LORA + KV_LORA]
    kr2 = y[:, 640:768]
    u = y[:, 768:768 + GM_WIDTH]
    vv = y[:, 768 + GM_WIDTH:768 + 2 * GM_WIDTH]

    rope = rope_ref[...]
    lane = lax.broadcasted_iota(jnp.int32, (tm, LANES), 1)

    kr_ref[...] = kr2[:, 0:QK_ROPE]
    t = kr2 * rope
    k_rot = jnp.where(lane < QK_ROPE, t + pltpu.roll(t, QK_ROPE, 1), 0.0).astype(BF16)

    ckv_n = _rms(ckv) * kvg_ref[...]
    ckv_ref[...] = ckv_n
    kv = _dot(ckv_n.astype(BF16), wkvb_ref[...])
    for hd in range(ATTN_HEADS):
        k_ref[:, hd * HEAD_PAD:hd * HEAD_PAD + QK_NOPE] = (
            kv[:, hd * QK_NOPE:(hd + 1) * QK_NOPE].astype(BF16))
        k_ref[:, hd * HEAD_PAD + QK_NOPE:(hd + 1) * HEAD_PAD] = k_rot
    _store_values_t(vt_ref, kv[:, ATTN_HEADS * QK_NOPE:])

    q = _dot((_rms(cq) * qg_ref[...]).astype(BF16), wqb_ref[...])
    for hd in range(ATTN_HEADS):
        q_ref[:, hd * HEAD_PAD:hd * HEAD_PAD + QK_NOPE] = (
            q[:, hd * HEAD_PAD:hd * HEAD_PAD + QK_NOPE].astype(BF16))
        t = q[:, hd * HEAD_PAD + QK_NOPE:(hd + 1) * HEAD_PAD] * rope
        q_ref[:, hd * HEAD_PAD + QK_NOPE:(hd + 1) * HEAD_PAD] = (
            t + pltpu.roll(t, QK_ROPE, 1)).astype(BF16)

    ug = _gelu(u)
    vg = _gelu(vv)
    cols = []
    for g in range(GM_GROUPS):
        sl = slice(g * GM_GROUP_DIM, (g + 1) * GM_GROUP_DIM)
        vn = (_rms(vg[:, sl]) * gmg_ref[:, sl]).astype(BF16)
        rows = []
        for c in range(tm // CHUNK):
            sv = _dot(ws_ref[g], vn[c * CHUNK:(c + 1) * CHUNK]) + bs_ref[:, sl]
            rows.append(ug[c * CHUNK:(c + 1) * CHUNK, sl] * sv)
        cols.append(jnp.concatenate(rows, axis=0))
    gm = jnp.concatenate(cols, axis=1)
    gm_ref[...] = (_rms(gm) * ogm_ref[...]).astype(BF16)


def _pre_mixer(layer, layer_input, mod, rope_tab, g1, win, qg, wqb, kvg, wkvb, gmg, ws, bs, ogm):
    tm = TM_PRE
    row = lambda i: (i, 0)
    mod_spec = _mod_spec(layer, tm)
    after_experts = len(layer_input) == 3
    if after_experts:
        layer_input = (*layer_input, mod)
        input_specs = [pl.BlockSpec((tm // TM_POST * LS_ROWS, D_MODEL), row),
                       pl.BlockSpec((tm, 1), row), pl.BlockSpec((tm, D_MODEL), row),
                       _mod_spec(layer - 1, tm)]
        extra_shape = (jax.ShapeDtypeStruct((N_ROWS, D_MODEL), F32),)
        extra_spec = (pl.BlockSpec((tm, D_MODEL), row),)
    else:
        input_specs = _ctx_lat_specs(tm, D_MODEL)
        extra_shape = extra_spec = ()
    return pl.pallas_call(
        functools.partial(_pre_kernel, after_experts=after_experts),
        out_shape=(
            jax.ShapeDtypeStruct((N_ROWS, ATTN_HEADS * HEAD_PAD), BF16),
            jax.ShapeDtypeStruct((N_ROWS, ATTN_HEADS * HEAD_PAD), BF16),
            jax.ShapeDtypeStruct((ATTN_HEADS * V_EXT, N_ROWS), BF16),
            jax.ShapeDtypeStruct((N_ROWS, GM_WIDTH), BF16),
            jax.ShapeDtypeStruct((N_ROWS, KV_LORA), F32),
            jax.ShapeDtypeStruct((N_ROWS, QK_ROPE), F32),
        ) + extra_shape,
        grid=(N_ROWS // tm,),
        in_specs=input_specs + [
            mod_spec,
            pl.BlockSpec((tm, LANES), lambda i: (_rope_block(i, tm), 0)),
            _of_layer(layer, (1, D_MODEL)),
            _of_layer(layer, (D_MODEL, IN_COLS)),
            _of_layer(layer, (1, Q_LORA)),
            _of_layer(layer, (Q_LORA, ATTN_HEADS * HEAD_PAD)),
            _of_layer(layer, (1, KV_LORA)),
            _of_layer(layer, (KV_LORA, ATTN_HEADS * (QK_NOPE + V_DIM))),
            _of_layer(layer, (1, GM_WIDTH)),
            _of_layer(layer, (GM_GROUPS, CHUNK, CHUNK)),
            _of_layer(layer, (CHUNK, GM_WIDTH)),
            _of_layer(layer, (1, GM_WIDTH)),
        ],
        out_specs=(
            pl.BlockSpec((tm, ATTN_HEADS * HEAD_PAD), row),
            pl.BlockSpec((tm, ATTN_HEADS * HEAD_PAD), row),
            pl.BlockSpec((ATTN_HEADS * V_EXT, tm), lambda i: (0, i)),
            pl.BlockSpec((tm, GM_WIDTH), row),
            pl.BlockSpec((tm, KV_LORA), row),
            pl.BlockSpec((tm, QK_ROPE), row),
        ) + extra_spec,
        compiler_params=pltpu.CompilerParams(
            dimension_semantics=("arbitrary",), vmem_limit_bytes=VMEM_LIMIT),
        name="pre_mixer",
    )(*layer_input, mod, rope_tab, g1, win, qg, wqb, kvg, wkvb, gmg, ws, bs, ogm)


def _cache_kv_kernel(ckv_ref, kr_ref, wkvb_ref, k_ref, vt_ref):
    kv = _dot(ckv_ref[0, 0].astype(BF16), wkvb_ref[0])
    kr = kr_ref[0, 0].astype(BF16)
    for hd in range(ATTN_HEADS):
        k_ref[0, 0, :, hd * HEAD_PAD:hd * HEAD_PAD + QK_NOPE] = (
            kv[:, hd * QK_NOPE:(hd + 1) * QK_NOPE].astype(BF16))
        k_ref[0, 0, :, hd * HEAD_PAD + QK_NOPE:(hd + 1) * HEAD_PAD] = kr
    _store_values_t(vt_ref.at[0, 0], kv[:, ATTN_HEADS * QK_NOPE:])


def _cache_kv(cache_ckv, cache_kr_pad, wkvb):
    blk = lambda w: pl.BlockSpec((1, 1, PAST_LEN, w), lambda l, b: (b, l, 0, 0))
    return pl.pallas_call(
        _cache_kv_kernel,
        out_shape=(
            jax.ShapeDtypeStruct((DEC_BATCH, DEPTH, PAST_LEN, ATTN_HEADS * HEAD_PAD), BF16),
            jax.ShapeDtypeStruct((DEC_BATCH, DEPTH, ATTN_HEADS * V_EXT, PAST_LEN), BF16),
        ),
        grid=(DEPTH, DEC_BATCH),
        in_specs=[
            blk(KV_LORA),
            blk(LANES),
            pl.BlockSpec((1, KV_LORA, ATTN_HEADS * (QK_NOPE + V_DIM)), lambda l, b: (l, 0, 0)),
        ],
        out_specs=(blk(ATTN_HEADS * HEAD_PAD),
                   pl.BlockSpec((1, 1, ATTN_HEADS * V_EXT, PAST_LEN), lambda l, b: (b, l, 0, 0))),
        compiler_params=pltpu.CompilerParams(
            dimension_semantics=("arbitrary", "arbitrary"), vmem_limit_bytes=VMEM_LIMIT),
        name="cache_kv",
    )(cache_ckv, cache_kr_pad, wkvb)


def _attn_scores(qh, key_blocks):
    return [_dot_nt(kb, qh) for kb in key_blocks]


def _attn_values(s, vt_blocks):
    s = [si.astype(BF16) for si in s]
    m = functools.reduce(jnp.maximum, [jnp.max(si, axis=0, keepdims=True) for si in s])
    o_t = sum(_dot(vt, jnp.exp2(si - m)) for vt, si in zip(vt_blocks, s))
    return (o_t[0:V_DIM, :] / o_t[V_DIM:V_DIM + 1, :]).T


def _ctx_attn_kernel(q_ref, k_ref, vt_ref, o_ref):
    chains = [(slice(sb * SEQ, (sb + 1) * SEQ), hd) for sb in range(CTX_SUB) for hd in range(ATTN_HEADS)]
    hs = lambda hd: slice(hd * HEAD_PAD, (hd + 1) * HEAD_PAD)
    scores = [_attn_scores(q_ref[rows, hs(hd)], [k_ref[rows, hs(hd)]]) for rows, hd in chains]
    for (rows, hd), s in zip(chains, scores):
        out = _attn_values(s, [vt_ref[hd * V_EXT:(hd + 1) * V_EXT, rows]])
        o_ref[rows, hd * V_DIM:(hd + 1) * V_DIM] = out.astype(o_ref.dtype)


def _ctx_attention(q, k, vt):
    blk = lambda w: pl.BlockSpec((CTX_SUB * SEQ, w), lambda b: (b, 0))
    return pl.pallas_call(
        _ctx_attn_kernel,
        out_shape=jax.ShapeDtypeStruct((N_CTX, ATTN_HEADS * V_DIM), BF16),
        grid=(BATCH // CTX_SUB,),
        in_specs=[blk(ATTN_HEADS * HEAD_PAD), blk(ATTN_HEADS * HEAD_PAD),
                  pl.BlockSpec((ATTN_HEADS * V_EXT, CTX_SUB * SEQ), lambda b: (0, b))],
        out_specs=blk(ATTN_HEADS * V_DIM),
        compiler_params=pltpu.CompilerParams(
            dimension_semantics=("arbitrary",), vmem_limit_bytes=VMEM_LIMIT),
        name="ctx_attention",
    )(q, k, vt)


def _lat_attn_kernel(q_ref, k_ref, vt_ref, kc_ref, vct_ref, o_ref):
    hs = [slice(hd * HEAD_PAD, (hd + 1) * HEAD_PAD) for hd in range(HEADS_PER_STEP)]
    vs = [slice(hd * V_DIM, (hd + 1) * V_DIM) for hd in range(HEADS_PER_STEP)]
    ve = [slice(hd * V_EXT, (hd + 1) * V_EXT) for hd in range(HEADS_PER_STEP)]

    def scores(hd):
        return _attn_scores(q_ref[:, hs[hd]], [kc_ref[0, 0, :, hs[hd]], k_ref[:, hs[hd]]])

    pending = [scores(hd) for hd in range(min(SCORES_AHEAD, HEADS_PER_STEP))]
    for hd in range(HEADS_PER_STEP):
        if hd + SCORES_AHEAD < HEADS_PER_STEP:
            pending.append(scores(hd + SCORES_AHEAD))
        o_ref[:, vs[hd]] = _attn_values(
            pending.pop(0), [vct_ref[0, 0, ve[hd], :], vt_ref[ve[hd], :]]).astype(o_ref.dtype)


def _lat_attention(q, k, vt, kc, vct, layer):
    nq = DEC_SEQ // TQ
    ctx_q_tiles = N_CTX // TQ
    ctx_kv_blocks = N_CTX // DEC_SEQ
    qk_w = HEADS_PER_STEP * HEAD_PAD
    v_w = HEADS_PER_STEP * V_DIM
    vt_w = HEADS_PER_STEP * V_EXT
    return pl.pallas_call(
        _lat_attn_kernel,
        out_shape=jax.ShapeDtypeStruct((N_LAT, ATTN_HEADS * V_DIM), BF16),
        grid=(DEC_BATCH, ATTN_HEADS // HEADS_PER_STEP, nq),
        in_specs=[
            pl.BlockSpec((TQ, qk_w), lambda b, h, i: (ctx_q_tiles + b * nq + i, h)),
            pl.BlockSpec((DEC_SEQ, qk_w), lambda b, h, i: (ctx_kv_blocks + b, h)),
            pl.BlockSpec((vt_w, DEC_SEQ), lambda b, h, i: (h, ctx_kv_blocks + b)),
            pl.BlockSpec((1, 1, PAST_LEN, qk_w), lambda b, h, i: (b, layer, 0, h)),
            pl.BlockSpec((1, 1, vt_w, PAST_LEN), lambda b, h, i: (b, layer, h, 0)),
        ],
        out_specs=pl.BlockSpec((TQ, v_w), lambda b, h, i: (b * nq + i, h)),
        compiler_params=pltpu.CompilerParams(
            dimension_semantics=("arbitrary", "arbitrary", "arbitrary"),
            vmem_limit_bytes=VMEM_LIMIT),
        name="lat_attention",
    )(q, k, vt, kc, vct)


def _group_peer(x, row, d, width, period):
    step = d * width
    ahead = pltpu.roll(x, N_EXPERTS - step, 0)
    wraps = (row & (period - 1)) + step >= period
    if period == N_EXPERTS:
        return ahead, wraps
    return jnp.where(wraps, pltpu.roll(x, period - step, 0), ahead), wraps


def _route(logits_t, bias):
    row = lax.broadcasted_iota(jnp.int32, logits_t.shape, 0)
    s = 1.0 / (1.0 + jnp.exp(-logits_t))
    sb = s + bias
    rank = jnp.zeros(sb.shape, jnp.int32)
    for d in range(1, EXPERTS_PER_GROUP):
        o, wraps = _group_peer(sb, row, d, 1, EXPERTS_PER_GROUP)
        beats = (o > sb) | ((o == sb) & wraps)
        rank = rank + beats.astype(jnp.int32)
    top2 = rank < 2
    t = jnp.where(top2, sb, 0.0)
    gscore = t
    for d in range(1, EXPERTS_PER_GROUP):
        gscore = gscore + _group_peer(t, row, d, 1, EXPERTS_PER_GROUP)[0]
    grank = jnp.zeros(sb.shape, jnp.int32)
    for d in range(1, N_GROUPS):
        o, wraps = _group_peer(gscore, row, d, EXPERTS_PER_GROUP, N_EXPERTS)
        beats = (o > gscore) | ((o == gscore) & wraps)
        grank = grank + beats.astype(jnp.int32)
    in_group = grank == 0
    w = jnp.where(top2 & in_group, s, 0.0)
    denom = jnp.sum(w, axis=0, keepdims=True)
    group_flag = jnp.where(in_group & ((row & (EXPERTS_PER_GROUP - 1)) == 0), 1.0, 0.0)
    return w / denom, group_flag


def _experts_to_lanes(x_t):
    pad = jnp.zeros((LANES - N_EXPERTS, x_t.shape[1]), x_t.dtype)
    return jnp.concatenate([x_t, pad], axis=0).T


def _split_bf16(x, terms):
    out = []
    for _ in range(terms - 1):
        t = x.astype(BF16)
        out.append(t)
        x = x - t.astype(F32)
    out.append(x.astype(BF16))
    return out


def _post_kernel(ac_ref, al_ref, gm_ref, xc_ref, xl_ref, mod_ref, oag_ref, wout_ref, g2_ref,
                 wr_ref, rb_ref, x1_ref, h2s_ref, pos_ref, cnt_ref):
    mod = mod_ref[0]
    gate1 = mod[:, 2 * D_MODEL:3 * D_MODEL]
    shift2 = mod[:, 3 * D_MODEL:4 * D_MODEL]
    scale2 = mod[:, 4 * D_MODEL:5 * D_MODEL]
    tm = xc_ref.shape[0]
    halves = [slice(0, tm // 2), slice(tm // 2, tm)]
    attn = _ctx_or_lat(ac_ref, al_ref)
    x_in = _ctx_or_lat(xc_ref, xl_ref)
    mixed = []
    for rows in halves:
        an = (_rms(attn[rows].astype(F32)) * oag_ref[...]).astype(BF16)
        mixed.append(_dot(jnp.concatenate([an, gm_ref[rows, :]], axis=1), wout_ref[...]))
    h2_hi, logits = [], []
    for rows, mix in zip(halves, mixed):
        x1 = x_in[rows] + gate1 * mix
        x1_ref[rows, :] = x1
        hi, lo = _split_bf16(_rms(x1) * g2_ref[...] * (1.0 + scale2) + shift2, 2)
        t = _dot(jnp.concatenate([hi, lo], axis=0), wr_ref[...])
        half = tm // 2
        logits.append((t[:half, :LANES] + t[half:, :LANES]) + (t[:half, LANES:] + t[half:, LANES:]))
        h2_hi.append(hi)
    h2_hi = jnp.concatenate(h2_hi, axis=0)
    logits = jnp.concatenate(logits, axis=0)
    tiles = [slice(sub * TM_POST, (sub + 1) * TM_POST) for sub in range(POST_SUB)]
    r_i = lax.broadcasted_iota(jnp.int32, (TM_POST, TM_POST), 0)
    c_i = lax.broadcasted_iota(jnp.int32, (TM_POST, TM_POST), 1)
    before = jnp.where(c_i < r_i, 1.0, 0.0).astype(BF16)
    routed = []
    for rows in tiles:
        comb_t, flag_t = _route(logits[rows].T[0:N_EXPERTS, :], rb_ref[...])
        routed.append((_experts_to_lanes(comb_t), _experts_to_lanes(flag_t)))
    ranks = [_dot(before, flag.astype(BF16)) for _, flag in routed]
    for sub, rows in enumerate(tiles):
        comb, flag = routed[sub]
        _sort_tile(h2_hi[rows], comb, flag, ranks[sub],
                   h2s_ref.at[sub * LS_ROWS:(sub + 1) * LS_ROWS], pos_ref.at[rows], cnt_ref.at[sub])


def _sort_tile(h2_hi, comb, flag, rank, h2s_ref, pos_ref, cnt_ref):
    tm = TM_POST
    count = jnp.sum(flag, axis=0, keepdims=True)
    pieces = jnp.floor((count + (PIECE - 1)) * (1.0 / PIECE))
    start = (pltpu.roll(pieces, EXPERTS_PER_GROUP, 1) + pltpu.roll(pieces, 2 * EXPERTS_PER_GROUP, 1)
             + pltpu.roll(pieces, 3 * EXPERTS_PER_GROUP, 1)) * PIECE
    pos = jnp.sum(flag * (start + rank), axis=-1, keepdims=True)
    pos_ref[...] = pos
    cnt_ref[...] = count.astype(jnp.int32)
    pos_row = jnp.transpose(jnp.broadcast_to(pos, (tm, LANES)))[0:1, :].astype(jnp.int32)
    place = lax.broadcasted_iota(jnp.int32, (LS_ROWS, tm), 0) == pos_row
    place = jnp.where(place, 1.0, 0.0).astype(BF16)
    wide = jnp.concatenate([h2_hi] + _split_bf16(comb, COMB_TERMS), axis=1)
    h2s_ref[...] = _dot(place, wide).astype(BF16)


def _post_mixer(layer, attn_ctx, attn_lat, gm, x_ctx, x_lat, mod, oag, wout, g2, wr, rb):
    joint_x = x_ctx is x_lat
    tm = POST_SUB * TM_POST
    ls = POST_SUB * LS_ROWS
    row = lambda i: (i, 0)
    return pl.pallas_call(
        _post_kernel,
        out_shape=(
            jax.ShapeDtypeStruct((N_ROWS, D_MODEL), F32),
            jax.ShapeDtypeStruct((N_LS, XS_COLS), BF16),
            jax.ShapeDtypeStruct((N_ROWS, 1), F32),
            jax.ShapeDtypeStruct((N_POST_TILES, 1, LANES), jnp.int32),
        ),
        grid=(N_ROWS // tm,),
        in_specs=_ctx_lat_specs(tm, ATTN_HEADS * V_DIM) + [
            pl.BlockSpec((tm, GM_WIDTH), row),
        ] + _ctx_lat_specs(tm, D_MODEL, joint=joint_x) + [
            _mod_spec(layer, tm),
            _of_layer(layer, (1, ATTN_HEADS * V_DIM)),
            _of_layer(layer, (D_MODEL, D_MODEL)),
            _of_layer(layer, (1, D_MODEL)),
            _full((D_MODEL, 2 * LANES)),
            _full((N_EXPERTS, 1)),
        ],
        out_specs=(
            pl.BlockSpec((tm, D_MODEL), row),
            pl.BlockSpec((ls, XS_COLS), row),
            pl.BlockSpec((tm, 1), row),
            pl.BlockSpec((POST_SUB, 1, LANES), lambda i: (i, 0, 0)),
        ),
        compiler_params=pltpu.CompilerParams(
            dimension_semantics=("arbitrary",), vmem_limit_bytes=VMEM_LIMIT),
        name="post_mixer",
    )(attn_ctx, attn_lat, gm, x_ctx, x_lat, mod, oag, wout, g2, wr, rb)


def _plan_kernel(cnt_ref, tg_ref, tn_ref, src_ref, nu_ref, spare_ref):
    def n_pieces(i, g):
        return lax.shift_right_logical(cnt_ref[i, g * EXPERTS_PER_GROUP] + (PIECE - 1), PIECE_SHIFT)

    def spare_body(i, c):
        used = n_pieces(i, 0)
        for g in range(1, N_GROUPS):
            used = used + n_pieces(i, g)
        for u in range(SPARE_PIECES):
            spare_ref[i * SPARE_PIECES + u] = jnp.where(used + u < LS_PIECES, i * LS_PIECES + used + u, -1)
        return c

    lax.fori_loop(0, N_POST_TILES, spare_body, 0)

    def clear_src(j, c):
        for u in range(E_PIECES):
            src_ref[j * E_PIECES + u] = 0
        return c

    lax.fori_loop(0, SRC_SLOTS // E_PIECES, clear_src, 0)

    def clear_tile(j, c):
        tg_ref[j] = N_GROUPS - 1
        tn_ref[j] = 0
        return c

    lax.fori_loop(0, N_ETILES, clear_tile, 0)

    t = jnp.int32(0)
    for g in range(N_GROUPS):
        def tile_body(i, s, g=g):
            first = i * LS_PIECES
            for gp in range(g):
                first = first + n_pieces(i, gp)

            for p in range(TM_POST // PIECE):
                src_ref[s + p] = first + p
            return s + n_pieces(i, g)

        s0 = t * E_PIECES
        s1 = lax.fori_loop(0, N_POST_TILES, tile_body, s0)
        n = s1 - s0
        tiles = lax.shift_right_logical(n + (E_PIECES - 1), E_SHIFT)

        def mark_tile(u, c, g=g, n=n, t=t):
            tg_ref[t + u] = g
            tn_ref[t + u] = jnp.minimum(n - u * E_PIECES, E_PIECES)
            return c

        lax.fori_loop(0, tiles, mark_tile, 0)
        t = t + tiles
    nu_ref[0] = t


def _plan(counts):
    smem = pl.BlockSpec(memory_space=pltpu.SMEM)
    return pl.pallas_call(
        _plan_kernel,
        out_shape=(
            jax.ShapeDtypeStruct((N_ETILES,), jnp.int32),
            jax.ShapeDtypeStruct((N_ETILES,), jnp.int32),
            jax.ShapeDtypeStruct((SRC_SLOTS,), jnp.int32),
            jax.ShapeDtypeStruct((1,), jnp.int32),
            jax.ShapeDtypeStruct((N_POST_TILES * SPARE_PIECES,), jnp.int32),
        ),
        in_specs=[smem],
        out_specs=(smem, smem, smem, smem, smem),
        name="expert_plan",
    )(counts)


GATHER_X, SCATTER_Y, ZERO_Y = 0, 1, 2


def _moe_kernel(tg_ref, tn_ref, src_ref, nu_ref, spare_ref, h2s_hbm, wg32_ref, wu32_ref, wd32_ref,
                wsg32_ref, wsu32_ref, wsd32_ref, y_hbm, xbuf, ybuf, zbuf, wg_ref, wu_ref, wd_ref,
                wsg_ref, wsu_ref, wsd_ref, sem):
    j = pl.program_id(0)
    n_used = nu_ref[0]
    slot = lax.rem(j, 2)

    def for_spare_pieces(action):
        def body(u, c):
            @pl.when(spare_ref[u] >= 0)
            def _():
                rows = pl.ds(pl.multiple_of(spare_ref[u] * PIECE, PIECE), PIECE)
                action(pltpu.make_async_copy(zbuf, y_hbm.at[rows], sem.at[ZERO_Y, 0]))
            return c
        lax.fori_loop(0, N_POST_TILES * SPARE_PIECES, body, 0)

    @pl.when(j == 0)
    def _():
        zbuf[...] = jnp.zeros(zbuf.shape, zbuf.dtype)
        for_spare_pieces(lambda cp: cp.start())

    @pl.when(j == 0)
    def _():
        wsg_ref[...] = wsg32_ref[0].astype(BF16)
        wsu_ref[...] = wsu32_ref[0].astype(BF16)
        wsd_ref[...] = wsd32_ref[0].astype(BF16)

    @pl.when((j == 0) | (tg_ref[j] != tg_ref[jnp.maximum(j - 1, 0)]))
    def _():
        for k in range(EXPERTS_PER_GROUP):
            wg_ref[k] = wg32_ref[0, k].astype(BF16)
            wu_ref[k] = wu32_ref[0, k].astype(BF16)
            wd_ref[k] = wd32_ref[0, k].astype(BF16)

    def piece_rows(t, k):
        hbm_rows = pl.ds(pl.multiple_of(src_ref[t * E_PIECES + k] * PIECE, PIECE), PIECE)
        buf_rows = pl.ds(pl.multiple_of(k * PIECE, PIECE), PIECE)
        return hbm_rows, buf_rows

    def gather_copies(t, k, slot):
        hbm_rows, buf_rows = piece_rows(t, k)
        return (
            pltpu.make_async_copy(h2s_hbm.at[hbm_rows], xbuf.at[slot, buf_rows], sem.at[GATHER_X, slot]),
        )

    def scatter_copies(t, k, slot):
        hbm_rows, buf_rows = piece_rows(t, k)
        return (
            pltpu.make_async_copy(ybuf.at[slot, buf_rows], y_hbm.at[hbm_rows], sem.at[SCATTER_Y, slot]),
        )

    def for_pieces(t, slot, copies, action):
        def body(k, c):
            for cp in copies(t, k, slot):
                action(cp)
            return c

        @pl.when(tn_ref[t] == E_PIECES)
        def _():
            lax.fori_loop(0, E_PIECES, body, 0, unroll=8)

        @pl.when(tn_ref[t] != E_PIECES)
        def _():
            lax.fori_loop(0, tn_ref[t], body, 0)

    start = lambda cp: cp.start()
    wait = lambda cp: cp.wait()

    @pl.when(j == 0)
    def _():
        xbuf[...] = jnp.zeros(xbuf.shape, xbuf.dtype)
        for_pieces(0, 0, gather_copies, start)

    @pl.when(j + 1 < n_used)
    def _():
        for_pieces(j + 1, 1 - slot, gather_copies, start)

    @pl.when(j < n_used)
    def _():
        for_pieces(j, slot, gather_copies, wait)
        x = xbuf[slot, :, 0:D_MODEL]
        comb = sum(xbuf[slot, :, D_MODEL + t * LANES:D_MODEL + (t + 1) * LANES].astype(F32)
                   for t in range(COMB_TERMS))
        lane = lax.broadcasted_iota(jnp.int32, (TM_E, LANES), 1)
        first = tg_ref[j] * EXPERTS_PER_GROUP
        y = _dot((_silu(_dot(x, wsg_ref[...])) * _dot(x, wsu_ref[...])).astype(BF16), wsd_ref[...])
        for k in range(EXPERTS_PER_GROUP):
            wk = jnp.sum(jnp.where(lane == first + k, comb, 0.0), axis=-1, keepdims=True)
            act = _silu(_dot(x, wg_ref[k])) * _dot(x, wu_ref[k]) * wk
            y = y + _dot(act.astype(BF16), wd_ref[k])
        ybuf[slot] = y.astype(BF16)
        for_pieces(j, slot, scatter_copies, start)

    @pl.when((j >= 1) & (j < n_used))
    def _():
        for_pieces(j - 1, 1 - slot, scatter_copies, wait)

    @pl.when(j == n_used - 1)
    def _():
        for_pieces(j, slot, scatter_copies, wait)
        for_spare_pieces(lambda cp: cp.wait())


def _experts(plan, h2s, layer, w_gate, w_up, w_down, ws_gate, ws_up, ws_down):
    any_spec = pl.BlockSpec(memory_space=pl.ANY)
    group = lambda j, tg, tn, src, nu, spare: (layer, tg[j], 0, 0)
    whole = lambda j, tg, tn, src, nu, spare: (layer, 0, 0)
    g = EXPERTS_PER_GROUP
    return pl.pallas_call(
        _moe_kernel,
        out_shape=jax.ShapeDtypeStruct((N_LS, D_MODEL), BF16),
        grid_spec=pltpu.PrefetchScalarGridSpec(
            num_scalar_prefetch=len(plan),
            grid=(N_ETILES,),
            in_specs=[
                any_spec,
                pl.BlockSpec((1, g, D_MODEL, EXPERT_FF), group),
                pl.BlockSpec((1, g, D_MODEL, EXPERT_FF), group),
                pl.BlockSpec((1, g, EXPERT_FF, D_MODEL), group),
                pl.BlockSpec((1, D_MODEL, SHARED_FF), whole),
                pl.BlockSpec((1, D_MODEL, SHARED_FF), whole),
                pl.BlockSpec((1, SHARED_FF, D_MODEL), whole),
            ],
            out_specs=any_spec,
            scratch_shapes=[
                pltpu.VMEM((2, TM_E, XS_COLS), BF16),
                pltpu.VMEM((2, TM_E, D_MODEL), BF16),
                pltpu.VMEM((PIECE, D_MODEL), BF16),
                pltpu.VMEM((g, D_MODEL, EXPERT_FF), BF16),
                pltpu.VMEM((g, D_MODEL, EXPERT_FF), BF16),
                pltpu.VMEM((g, EXPERT_FF, D_MODEL), BF16),
                pltpu.VMEM((D_MODEL, SHARED_FF), BF16),
                pltpu.VMEM((D_MODEL, SHARED_FF), BF16),
                pltpu.VMEM((SHARED_FF, D_MODEL), BF16),
                pltpu.SemaphoreType.DMA((3, 2)),
            ],
        ),
        compiler_params=pltpu.CompilerParams(
            dimension_semantics=("arbitrary",), vmem_limit_bytes=VMEM_LIMIT),
        name="experts",
    )(*plan, h2s, w_gate, w_up, w_down, ws_gate, ws_up, ws_down)


def _unsort(y_ref, pos_ref, sub):
    pos = pos_ref[sub * TM_POST:(sub + 1) * TM_POST, :].astype(jnp.int32)
    pick = lax.broadcasted_iota(jnp.int32, (TM_POST, LS_ROWS), 1) == pos
    return _dot(jnp.where(pick, 1.0, 0.0).astype(BF16), y_ref[sub * LS_ROWS:(sub + 1) * LS_ROWS, :])


def _combine_kernel(y_ref, pos_ref, x1_ref, mod_ref, fg_ref, o_ref):
    gate2 = mod_ref[0][:, 5 * D_MODEL:6 * D_MODEL]
    for sub in range(COMBINE_SUB):
        rows = slice(sub * TM_POST, (sub + 1) * TM_POST)
        x2 = x1_ref[rows, :] + gate2 * _unsort(y_ref, pos_ref, sub)
        o_ref[rows, :] = _rms(x2) * fg_ref[...]


def _combine(y, pos, x1, mod, fg, first_tile, n_tiles):
    tm = COMBINE_SUB * TM_POST
    row = lambda i: (first_tile + i, 0)
    return pl.pallas_call(
        _combine_kernel,
        out_shape=jax.ShapeDtypeStruct((n_tiles * tm, D_MODEL), F32),
        grid=(n_tiles,),
        in_specs=[
            pl.BlockSpec((COMBINE_SUB * LS_ROWS, D_MODEL), row),
            pl.BlockSpec((tm, 1), row),
            pl.BlockSpec((tm, D_MODEL), row),
            _mod_spec(DEPTH - 1, tm, first_tile),
            _full((1, D_MODEL)),
        ],
        out_specs=pl.BlockSpec((tm, D_MODEL), lambda i: (i, 0)),
        compiler_params=pltpu.CompilerParams(
            dimension_semantics=("arbitrary",), vmem_limit_bytes=VMEM_LIMIT),
        name="combine",
    )(y, pos, x1, mod, fg)


def _rope_table():
    rows = DEC_SEQ // GRID_W
    row = jnp.repeat(jnp.arange(rows, dtype=F32), GRID_W)
    col = jnp.tile(jnp.arange(GRID_W, dtype=F32), rows)
    half = QK_ROPE // 2
    freqs = 1.0 / (ROPE_BASE ** (jnp.arange(0, half, 2, dtype=F32) / half))
    ang = jnp.concatenate([row[:, None] * freqs, col[:, None] * freqs], axis=-1)
    cos, sin = jnp.cos(ang), jnp.sin(ang)
    lat = jnp.concatenate([cos, cos, -sin, sin], axis=-1)
    ident = jnp.concatenate([jnp.ones((TM_PRE, QK_ROPE), F32), jnp.zeros((TM_PRE, QK_ROPE), F32)], axis=-1)
    return jnp.concatenate([ident, lat], axis=0)


_DEINT = np.concatenate([np.arange(0, QK_ROPE, 2), np.arange(1, QK_ROPE, 2)])
_SWAP = np.concatenate([np.arange(1, QK_ROPE, 2), np.arange(0, QK_ROPE, 2)])
_INTERLEAVE = np.argsort(_DEINT)


def _layout_w_in(w_in):
    cq_ckv = w_in[..., :Q_LORA + KV_LORA]
    kr = w_in[..., Q_LORA + KV_LORA:Q_LORA + KV_LORA + QK_ROPE]
    uv = w_in[..., Q_LORA + KV_LORA + QK_ROPE:]
    return jnp.concatenate([cq_ckv, kr[..., _DEINT], kr[..., _SWAP], uv], axis=-1).astype(BF16)


def _layout_w_qb(w_qb):
    w = (w_qb * (ATTN_SCALE * LOG2E)).reshape(DEPTH, Q_LORA, ATTN_HEADS, QK_NOPE + QK_ROPE)
    nope, rope = w[..., :QK_NOPE], w[..., QK_NOPE:]
    w = jnp.concatenate([nope, rope[..., _DEINT], rope[..., _SWAP]], axis=-1)
    return w.reshape(DEPTH, Q_LORA, ATTN_HEADS * HEAD_PAD).astype(BF16)


def _layout_w_kvb(w_kvb):
    w = w_kvb.reshape(DEPTH, KV_LORA, ATTN_HEADS, QK_NOPE + V_DIM)
    k = w[..., :QK_NOPE].reshape(DEPTH, KV_LORA, ATTN_HEADS * QK_NOPE)
    v = w[..., QK_NOPE:].reshape(DEPTH, KV_LORA, ATTN_HEADS * V_DIM)
    return jnp.concatenate([k, v], axis=-1).astype(BF16)


def kernel(x_prompt, x_sample, cache_ckv, cache_krope, c, c_ctx, norm1_g, w_ada, b_ada, w_in,
           q_norm_g, w_qb, kv_norm_g, w_kvb, gm_norm_g, w_spatial, b_spatial, onorm_attn_g,
           onorm_gm_g, w_out, norm2_g, w_router, router_bias, w_gate, w_up, w_down, ws_gate,
           ws_up, ws_down, final_norm_g):
    x_ctx = x_prompt.reshape(N_CTX, D_MODEL)
    x_lat = x_sample.reshape(N_LAT, D_MODEL)
    cond = jnp.concatenate([c_ctx[None, :], c, jnp.zeros((COND_PAD - N_COND, D_MODEL), F32)], axis=0)
    mod = _modulation(cond, w_ada, b_ada).reshape(DEPTH, COND_PAD, 1, 6 * D_MODEL)
    rope_tab = _rope_table()

    wkvb = _layout_w_kvb(w_kvb)
    cache_kr = jnp.pad(cache_krope[..., _DEINT], ((0, 0), (0, 0), (0, 0), (0, LANES - QK_ROPE)))
    kc, vct = _cache_kv(cache_ckv, cache_kr, wkvb)

    wr = jnp.pad(w_router, ((0, 0), (0, LANES - N_EXPERTS)))
    wr_hi = wr.astype(BF16)
    wr = jnp.concatenate([wr_hi, (wr - wr_hi.astype(F32)).astype(BF16)], axis=1)
    rb = router_bias.reshape(N_EXPERTS, 1)
    fg = final_norm_g.reshape(1, D_MODEL)

    row_stack = lambda g: g.reshape(DEPTH, 1, -1)
    bs = jnp.broadcast_to(jnp.swapaxes(b_spatial, 1, 2)[..., None],
                          (DEPTH, CHUNK, GM_GROUPS, GM_GROUP_DIM)).reshape(DEPTH, CHUNK, GM_WIDTH)
    pre_params = (row_stack(norm1_g), _layout_w_in(w_in), row_stack(q_norm_g), _layout_w_qb(w_qb),
                  row_stack(kv_norm_g), wkvb, row_stack(gm_norm_g), w_spatial.astype(BF16), bs,
                  row_stack(onorm_gm_g))
    post_params = (row_stack(onorm_attn_g), w_out.astype(BF16), row_stack(norm2_g), wr, rb)

    ckv_out, kr_out = [], []
    layer_input = (x_ctx, x_lat)
    for l in range(DEPTH):
        q, k, vt, gm, ckv_n, kr, *formed = _pre_mixer(l, layer_input, mod, rope_tab, *pre_params)
        ckv_out.append(ckv_n[:N_CTX].reshape(BATCH, SEQ, KV_LORA))
        kr_out.append(kr[:N_CTX][:, _INTERLEAVE].reshape(BATCH, SEQ, QK_ROPE))
        x_pair = (formed[0], formed[0]) if formed else layer_input
        x1, h2s, pos, counts = _post_mixer(
            l, _ctx_attention(q, k, vt), _lat_attention(q, k, vt, kc, vct, l), gm, *x_pair, mod,
            *post_params)
        plan = _plan(counts.reshape(N_POST_TILES, LANES))
        y = _experts(plan, h2s, l, w_gate, w_up, w_down, ws_gate, ws_up, ws_down)
        layer_input = (y, pos, x1)

    n_ctx_tiles = N_CTX // (COMBINE_SUB * TM_POST)
    n_lat_tiles = N_LAT // (COMBINE_SUB * TM_POST)
    x_ctx = _combine(y, pos, x1, mod, fg, 0, n_ctx_tiles)
    x_lat = _combine(y, pos, x1, mod, fg, n_ctx_tiles, n_lat_tiles)
    y_prompt = x_ctx.reshape(BATCH, SEQ, D_MODEL)
    y_sample = x_lat.reshape(DEC_BATCH, DEC_SEQ, D_MODEL)
    return y_prompt, y_sample, jnp.stack(ckv_out, axis=1), jnp.stack(kr_out, axis=1)
```

```python
import functools
import math

import jax
import jax.numpy as jnp
import numpy as np
from jax import lax
from jax.experimental import pallas as pl
from jax.experimental.pallas import tpu as pltpu

D_MODEL = 1024
BATCH = 16
SEQ = 256
DEPTH = 2
DEC_BATCH = 4
DEC_SEQ = 4096
PAST_LEN = 256
GRID_W = 64
ATTN_HEADS = 4
QK_NOPE = 128
QK_ROPE = 64
V_DIM = 128
V_EXT = V_DIM + 16
Q_LORA = 384
KV_LORA = 256
ATTN_SCALE = (QK_NOPE + QK_ROPE) ** -0.5
ROPE_BASE = 10000.0
CHUNK = 128
GM_WIDTH = 512
GM_GROUPS = 4
GM_GROUP_DIM = 128
N_EXPERTS = 16
N_GROUPS = 4
EXPERTS_PER_GROUP = 4
EXPERT_FF = 256
SHARED_FF = 256
EPS = 1e-6

N_CTX = BATCH * SEQ
N_LAT = DEC_BATCH * DEC_SEQ
N_ROWS = N_CTX + N_LAT
N_COND = 1 + DEC_BATCH
COND_PAD = 16
HEAD_PAD = 256
IN_COLS = Q_LORA + KV_LORA + 2 * QK_ROPE + 2 * GM_WIDTH
LANES = 128
LOG2E = 1.4426950408889634

TM_PRE = 512
TM_POST = 256
COMB_TERMS = 3
XS_COLS = D_MODEL + COMB_TERMS * LANES
COMBINE_SUB = 4
POST_SUB = 4
POST_PARTS = 4
PIECE = 16
LS_ROWS = TM_POST + N_GROUPS * PIECE
LS_PIECES = LS_ROWS // PIECE
N_POST_TILES = N_ROWS // TM_POST
N_LS = N_POST_TILES * LS_ROWS
TM_E = 512
E_PIECES = TM_E // PIECE
PIECE_SHIFT = PIECE.bit_length() - 1
E_SHIFT = E_PIECES.bit_length() - 1
N_ETILES = -(-(N_POST_TILES * (TM_POST // PIECE + N_GROUPS - 1)) // E_PIECES) + N_GROUPS
SRC_SLOTS = (N_ETILES + 1) * E_PIECES
TQ = 512
CTX_SUB = 4
SCORES_AHEAD = 4
HEADS_PER_STEP = 4
VMEM_LIMIT = 56 * 1024 * 1024

F32 = jnp.float32
BF16 = jnp.bfloat16


def _rms(x):
    return x * lax.rsqrt(jnp.mean(x * x, axis=-1, keepdims=True) + EPS)


def _gelu(x):
    return 0.5 * x * (1.0 + jnp.tanh(math.sqrt(2.0 / math.pi) * (x + 0.044715 * (x * x * x))))


def _silu(x):
    return x * (1.0 / (1.0 + jnp.exp(-x)))


def _dot(a, b):
    return jnp.dot(a, b, preferred_element_type=F32)


def _dot_nt(a, b):
    return lax.dot_general(a, b, (((1,), (1,)), ((), ())), preferred_element_type=F32)


def _cond_row(i, tm):
    n_ctx_tiles = N_CTX // tm
    per_batch = DEC_SEQ // tm
    return jnp.where(i < n_ctx_tiles, 0, 1 + (i - n_ctx_tiles) // per_batch)


def _rope_block(i, tm):
    n_ctx_tiles = N_CTX // tm
    per_batch = DEC_SEQ // tm
    return jnp.where(i < n_ctx_tiles, 0, 1 + (i - n_ctx_tiles) % per_batch)


def _ctx_lat_specs(tm, width, joint=False):
    n_ctx_tiles = N_CTX // tm
    lat_first = n_ctx_tiles if joint else 0
    return [pl.BlockSpec((tm, width), lambda i: (jnp.minimum(i, n_ctx_tiles - 1), 0)),
            pl.BlockSpec((tm, width), lambda i: (lat_first + jnp.maximum(i - n_ctx_tiles, 0), 0))]


def _ctx_or_lat(ctx_ref, lat_ref):
    n_ctx_tiles = N_CTX // ctx_ref.shape[0]
    return jnp.where(pl.program_id(0) < n_ctx_tiles, ctx_ref[...], lat_ref[...])


def _full(shape):
    n = len(shape)
    return pl.BlockSpec(shape, lambda *_: (0,) * n)


def _of_layer(layer, shape):
    n = len(shape)
    return pl.BlockSpec((None,) + tuple(shape), lambda *_: (layer,) + (0,) * n)


def _mod_spec(layer, tm, first_tile=0):
    return pl.BlockSpec((None, 1, 1, 6 * D_MODEL),
                        lambda i: (layer, _cond_row(first_tile + i, tm), 0, 0))


def _mod_kernel(cond_ref, w_ref, b_ref, o_ref):
    s_hi, s_lo = _split_bf16(_silu(cond_ref[...]), 2)
    w_hi, w_lo = _split_bf16(w_ref[0], 2)
    t = _dot(jnp.concatenate([s_hi, s_lo], axis=0), w_hi)
    o_ref[0] = (t[:COND_PAD] + t[COND_PAD:]) + _dot(s_hi, w_lo) + b_ref[0]


def _modulation(cond, w_ada, b_ada):
    tn = 1536
    return pl.pallas_call(
        _mod_kernel,
        out_shape=jax.ShapeDtypeStruct((DEPTH, COND_PAD, 6 * D_MODEL), F32),
        grid=(DEPTH, 6 * D_MODEL // tn),
        in_specs=[
            pl.BlockSpec((COND_PAD, D_MODEL), lambda l, j: (0, 0)),
            pl.BlockSpec((1, D_MODEL, tn), lambda l, j: (l, 0, j)),
            pl.BlockSpec((1, 1, tn), lambda l, j: (l, 0, j)),
        ],
        out_specs=pl.BlockSpec((1, COND_PAD, tn), lambda l, j: (l, 0, j)),
        compiler_params=pltpu.CompilerParams(
            dimension_semantics=("arbitrary", "arbitrary"), vmem_limit_bytes=VMEM_LIMIT),
        name="modulation",
    )(cond, w_ada, b_ada.reshape(DEPTH, 1, 6 * D_MODEL))


def _store_values_t(vt_ref, v):
    t = v.shape[0]
    v_t = v.T.astype(BF16)
    for hd in range(ATTN_HEADS):
        vt_ref[hd * V_EXT:hd * V_EXT + V_DIM, :] = v_t[hd * V_DIM:(hd + 1) * V_DIM, :]
        vt_ref[hd * V_EXT + V_DIM:(hd + 1) * V_EXT, :] = jnp.ones((V_EXT - V_DIM, t), BF16)


def _pre_kernel(*refs, after_experts):
    if after_experts:
        (y_ref, pos_ref, x1_ref, prev_mod_ref, mod_ref, rope_ref, g1_ref, win_ref, qg_ref, wqb_ref,
         kvg_ref, wkvb_ref, gmg_ref, ws_ref, bs_ref, ogm_ref,
         q_ref, k_ref, vt_ref, gm_ref, ckv_ref, kr_ref, x_ref) = refs
        tm = x1_ref.shape[0]
        gate2 = prev_mod_ref[0][:, 5 * D_MODEL:6 * D_MODEL]
        x = jnp.concatenate([x1_ref[sub * TM_POST:(sub + 1) * TM_POST, :] + gate2 * _unsort(y_ref, pos_ref, sub)
                             for sub in range(tm // TM_POST)], axis=0)
        x_ref[...] = x
    else:
        (xc_ref, xl_ref, mod_ref, rope_ref, g1_ref, win_ref, qg_ref, wqb_ref,
         kvg_ref, wkvb_ref, gmg_ref, ws_ref, bs_ref, ogm_ref,
         q_ref, k_ref, vt_ref, gm_ref, ckv_ref, kr_ref) = refs
        tm = xc_ref.shape[0]
        x = _ctx_or_lat(xc_ref, xl_ref)
    mod = mod_ref[0]
    shift1 = mod[:, 0:D_MODEL]
    scale1 = mod[:, D_MODEL:2 * D_MODEL]
    h = _rms(x) * g1_ref[...] * (1.0 + scale1) + shift1
    y = _dot(h.astype(BF16), win_ref[...])
    cq = y[:, 0:Q_LORA]
    ckv = y[:, Q_LORA:Q_LORA + KV_LORA]
    kr2 = y[:, 640:768]
    u = y[:, 768:768 + GM_WIDTH]
    vv = y[:, 768 + GM_WIDTH:768 + 2 * GM_WIDTH]

    rope = rope_ref[...]
    lane = lax.broadcasted_iota(jnp.int32, (tm, LANES), 1)

    kr_ref[...] = kr2[:, 0:QK_ROPE]
    t = kr2 * rope
    k_rot = jnp.where(lane < QK_ROPE, t + pltpu.roll(t, QK_ROPE, 1), 0.0).astype(BF16)

    ckv_n = _rms(ckv) * kvg_ref[...]
    ckv_ref[...] = ckv_n
    kv = _dot(ckv_n.astype(BF16), wkvb_ref[...])
    for hd in range(ATTN_HEADS):
        k_ref[:, hd * HEAD_PAD:hd * HEAD_PAD + QK_NOPE] = (
            kv[:, hd * QK_NOPE:(hd + 1) * QK_NOPE].astype(BF16))
        k_ref[:, hd * HEAD_PAD + QK_NOPE:(hd + 1) * HEAD_PAD] = k_rot
    _store_values_t(vt_ref, kv[:, ATTN_HEADS * QK_NOPE:])

    q = _dot((_rms(cq) * qg_ref[...]).astype(BF16), wqb_ref[...])
    for hd in range(ATTN_HEADS):
        q_ref[:, hd * HEAD_PAD:hd * HEAD_PAD + QK_NOPE] = (
            q[:, hd * HEAD_PAD:hd * HEAD_PAD + QK_NOPE].astype(BF16))
        t = q[:, hd * HEAD_PAD + QK_NOPE:(hd + 1) * HEAD_PAD] * rope
        q_ref[:, hd * HEAD_PAD + QK_NOPE:(hd + 1) * HEAD_PAD] = (
            t + pltpu.roll(t, QK_ROPE, 1)).astype(BF16)

    ug = _gelu(u)
    vg = _gelu(vv)
    cols = []
    for g in range(GM_GROUPS):
        sl = slice(g * GM_GROUP_DIM, (g + 1) * GM_GROUP_DIM)
        vn = (_rms(vg[:, sl]) * gmg_ref[:, sl]).astype(BF16)
        rows = []
        for c in range(tm // CHUNK):
            sv = _dot(ws_ref[g], vn[c * CHUNK:(c + 1) * CHUNK]) + bs_ref[:, sl]
            rows.append(ug[c * CHUNK:(c + 1) * CHUNK, sl] * sv)
        cols.append(jnp.concatenate(rows, axis=0))
    gm = jnp.concatenate(cols, axis=1)
    gm_ref[...] = (_rms(gm) * ogm_ref[...]).astype(BF16)


def _pre_mixer(layer, layer_input, mod, rope_tab, g1, win, qg, wqb, kvg, wkvb, gmg, ws, bs, ogm):
    tm = TM_PRE
    row = lambda i: (i, 0)
    mod_spec = _mod_spec(layer, tm)
    after_experts = len(layer_input) == 3
    if after_experts:
        layer_input = (*layer_input, mod)
        input_specs = [pl.BlockSpec((tm // TM_POST * LS_ROWS, D_MODEL), row),
                       pl.BlockSpec((tm, 1), row), pl.BlockSpec((tm, D_MODEL), row),
                       _mod_spec(layer - 1, tm)]
        extra_shape = (jax.ShapeDtypeStruct((N_ROWS, D_MODEL), F32),)
        extra_spec = (pl.BlockSpec((tm, D_MODEL), row),)
    else:
        input_specs = _ctx_lat_specs(tm, D_MODEL)
        extra_shape = extra_spec = ()
    return pl.pallas_call(
        functools.partial(_pre_kernel, after_experts=after_experts),
        out_shape=(
            jax.ShapeDtypeStruct((N_ROWS, ATTN_HEADS * HEAD_PAD), BF16),
            jax.ShapeDtypeStruct((N_ROWS, ATTN_HEADS * HEAD_PAD), BF16),
            jax.ShapeDtypeStruct((ATTN_HEADS * V_EXT, N_ROWS), BF16),
            jax.ShapeDtypeStruct((N_ROWS, GM_WIDTH), BF16),
            jax.ShapeDtypeStruct((N_ROWS, KV_LORA), F32),
            jax.ShapeDtypeStruct((N_ROWS, QK_ROPE), F32),
        ) + extra_shape,
        grid=(N_ROWS // tm,),
        in_specs=input_specs + [
            mod_spec,
            pl.BlockSpec((tm, LANES), lambda i: (_rope_block(i, tm), 0)),
            _of_layer(layer, (1, D_MODEL)),
            _of_layer(layer, (D_MODEL, IN_COLS)),
            _of_layer(layer, (1, Q_LORA)),
            _of_layer(layer, (Q_LORA, ATTN_HEADS * HEAD_PAD)),
            _of_layer(layer, (1, KV_LORA)),
            _of_layer(layer, (KV_LORA, ATTN_HEADS * (QK_NOPE + V_DIM))),
            _of_layer(layer, (1, GM_WIDTH)),
            _of_layer(layer, (GM_GROUPS, CHUNK, CHUNK)),
            _of_layer(layer, (CHUNK, GM_WIDTH)),
            _of_layer(layer, (1, GM_WIDTH)),
        ],
        out_specs=(
            pl.BlockSpec((tm, ATTN_HEADS * HEAD_PAD), row),
            pl.BlockSpec((tm, ATTN_HEADS * HEAD_PAD), row),
            pl.BlockSpec((ATTN_HEADS * V_EXT, tm), lambda i: (0, i)),
            pl.BlockSpec((tm, GM_WIDTH), row),
            pl.BlockSpec((tm, KV_LORA), row),
            pl.BlockSpec((tm, QK_ROPE), row),
        ) + extra_spec,
        compiler_params=pltpu.CompilerParams(
            dimension_semantics=("arbitrary",), vmem_limit_bytes=VMEM_LIMIT),
        name="pre_mixer",
    )(*layer_input, mod, rope_tab, g1, win, qg, wqb, kvg, wkvb, gmg, ws, bs, ogm)


def _cache_kv_kernel(ckv_ref, kr_ref, wkvb_ref, k_ref, vt_ref):
    kv = _dot(ckv_ref[0, 0].astype(BF16), wkvb_ref[0])
    kr = kr_ref[0, 0].astype(BF16)
    for hd in range(ATTN_HEADS):
        k_ref[0, 0, :, hd * HEAD_PAD:hd * HEAD_PAD + QK_NOPE] = (
            kv[:, hd * QK_NOPE:(hd + 1) * QK_NOPE].astype(BF16))
        k_ref[0, 0, :, hd * HEAD_PAD + QK_NOPE:(hd + 1) * HEAD_PAD] = kr
    _store_values_t(vt_ref.at[0, 0], kv[:, ATTN_HEADS * QK_NOPE:])


def _cache_kv(cache_ckv, cache_kr_pad, wkvb):
    blk = lambda w: pl.BlockSpec((1, 1, PAST_LEN, w), lambda l, b: (b, l, 0, 0))
    return pl.pallas_call(
        _cache_kv_kernel,
        out_shape=(
            jax.ShapeDtypeStruct((DEC_BATCH, DEPTH, PAST_LEN, ATTN_HEADS * HEAD_PAD), BF16),
            jax.ShapeDtypeStruct((DEC_BATCH, DEPTH, ATTN_HEADS * V_EXT, PAST_LEN), BF16),
        ),
        grid=(DEPTH, DEC_BATCH),
        in_specs=[
            blk(KV_LORA),
            blk(LANES),
            pl.BlockSpec((1, KV_LORA, ATTN_HEADS * (QK_NOPE + V_DIM)), lambda l, b: (l, 0, 0)),
        ],
        out_specs=(blk(ATTN_HEADS * HEAD_PAD),
                   pl.BlockSpec((1, 1, ATTN_HEADS * V_EXT, PAST_LEN), lambda l, b: (b, l, 0, 0))),
        compiler_params=pltpu.CompilerParams(
            dimension_semantics=("arbitrary", "arbitrary"), vmem_limit_bytes=VMEM_LIMIT),
        name="cache_kv",
    )(cache_ckv, cache_kr_pad, wkvb)


def _attn_scores(qh, key_blocks):
    return [_dot_nt(kb, qh) for kb in key_blocks]


def _attn_values(s, vt_blocks):
    s = [si.astype(BF16) for si in s]
    m = functools.reduce(jnp.maximum, [jnp.max(si, axis=0, keepdims=True) for si in s])
    o_t = sum(_dot(vt, jnp.exp2(si - m)) for vt, si in zip(vt_blocks, s))
    return (o_t[0:V_DIM, :] / o_t[V_DIM:V_DIM + 1, :]).T


def _ctx_attn_kernel(q_ref, k_ref, vt_ref, o_ref):
    chains = [(slice(sb * SEQ, (sb + 1) * SEQ), hd) for sb in range(CTX_SUB) for hd in range(ATTN_HEADS)]
    hs = lambda hd: slice(hd * HEAD_PAD, (hd + 1) * HEAD_PAD)
    scores = [_attn_scores(q_ref[rows, hs(hd)], [k_ref[rows, hs(hd)]]) for rows, hd in chains]
    for (rows, hd), s in zip(chains, scores):
        out = _attn_values(s, [vt_ref[hd * V_EXT:(hd + 1) * V_EXT, rows]])
        o_ref[rows, hd * V_DIM:(hd + 1) * V_DIM] = out.astype(o_ref.dtype)


def _ctx_attention(q, k, vt):
    blk = lambda w: pl.BlockSpec((CTX_SUB * SEQ, w), lambda b: (b, 0))
    return pl.pallas_call(
        _ctx_attn_kernel,
        out_shape=jax.ShapeDtypeStruct((N_CTX, ATTN_HEADS * V_DIM), BF16),
        grid=(BATCH // CTX_SUB,),
        in_specs=[blk(ATTN_HEADS * HEAD_PAD), blk(ATTN_HEADS * HEAD_PAD),
                  pl.BlockSpec((ATTN_HEADS * V_EXT, CTX_SUB * SEQ), lambda b: (0, b))],
        out_specs=blk(ATTN_HEADS * V_DIM),
        compiler_params=pltpu.CompilerParams(
            dimension_semantics=("arbitrary",), vmem_limit_bytes=VMEM_LIMIT),
        name="ctx_attention",
    )(q, k, vt)


def _lat_attn_kernel(q_ref, k_ref, vt_ref, kc_ref, vct_ref, o_ref):
    hs = [slice(hd * HEAD_PAD, (hd + 1) * HEAD_PAD) for hd in range(HEADS_PER_STEP)]
    vs = [slice(hd * V_DIM, (hd + 1) * V_DIM) for hd in range(HEADS_PER_STEP)]
    ve = [slice(hd * V_EXT, (hd + 1) * V_EXT) for hd in range(HEADS_PER_STEP)]

    def scores(hd):
        return _attn_scores(q_ref[:, hs[hd]], [kc_ref[0, 0, :, hs[hd]], k_ref[:, hs[hd]]])

    pending = [scores(hd) for hd in range(min(SCORES_AHEAD, HEADS_PER_STEP))]
    for hd in range(HEADS_PER_STEP):
        if hd + SCORES_AHEAD < HEADS_PER_STEP:
            pending.append(scores(hd + SCORES_AHEAD))
        o_ref[:, vs[hd]] = _attn_values(
            pending.pop(0), [vct_ref[0, 0, ve[hd], :], vt_ref[ve[hd], :]]).astype(o_ref.dtype)


def _lat_attention(q, k, vt, kc, vct, layer):
    nq = DEC_SEQ // TQ
    ctx_q_tiles = N_CTX // TQ
    ctx_kv_blocks = N_CTX // DEC_SEQ
    qk_w = HEADS_PER_STEP * HEAD_PAD
    v_w = HEADS_PER_STEP * V_DIM
    vt_w = HEADS_PER_STEP * V_EXT
    return pl.pallas_call(
        _lat_attn_kernel,
        out_shape=jax.ShapeDtypeStruct((N_LAT, ATTN_HEADS * V_DIM), BF16),
        grid=(DEC_BATCH, ATTN_HEADS // HEADS_PER_STEP, nq),
        in_specs=[
            pl.BlockSpec((TQ, qk_w), lambda b, h, i: (ctx_q_tiles + b * nq + i, h)),
            pl.BlockSpec((DEC_SEQ, qk_w), lambda b, h, i: (ctx_kv_blocks + b, h)),
            pl.BlockSpec((vt_w, DEC_SEQ), lambda b, h, i: (h, ctx_kv_blocks + b)),
            pl.BlockSpec((1, 1, PAST_LEN, qk_w), lambda b, h, i: (b, layer, 0, h)),
            pl.BlockSpec((1, 1, vt_w, PAST_LEN), lambda b, h, i: (b, layer, h, 0)),
        ],
        out_specs=pl.BlockSpec((TQ, v_w), lambda b, h, i: (b * nq + i, h)),
        compiler_params=pltpu.CompilerParams(
            dimension_semantics=("arbitrary", "arbitrary", "arbitrary"),
            vmem_limit_bytes=VMEM_LIMIT),
        name="lat_attention",
    )(q, k, vt, kc, vct)


def _group_peer(x, row, d, width, period):
    step = d * width
    ahead = pltpu.roll(x, N_EXPERTS - step, 0)
    wraps = (row & (period - 1)) + step >= period
    if period == N_EXPERTS:
        return ahead, wraps
    return jnp.where(wraps, pltpu.roll(x, period - step, 0), ahead), wraps


def _route(logits_t, bias):
    row = lax.broadcasted_iota(jnp.int32, logits_t.shape, 0)
    s = 1.0 / (1.0 + jnp.exp(-logits_t))
    sb = s + bias
    rank = jnp.zeros(sb.shape, jnp.int32)
    for d in range(1, EXPERTS_PER_GROUP):
        o, wraps = _group_peer(sb, row, d, 1, EXPERTS_PER_GROUP)
        beats = (o > sb) | ((o == sb) & wraps)
        rank = rank + beats.astype(jnp.int32)
    top2 = rank < 2
    t = jnp.where(top2, sb, 0.0)
    gscore = t
    for d in range(1, EXPERTS_PER_GROUP):
        gscore = gscore + _group_peer(t, row, d, 1, EXPERTS_PER_GROUP)[0]
    grank = jnp.zeros(sb.shape, jnp.int32)
    for d in range(1, N_GROUPS):
        o, wraps = _group_peer(gscore, row, d, EXPERTS_PER_GROUP, N_EXPERTS)
        beats = (o > gscore) | ((o == gscore) & wraps)
        grank = grank + beats.astype(jnp.int32)
    in_group = grank == 0
    w = jnp.where(top2 & in_group, s, 0.0)
    denom = jnp.sum(w, axis=0, keepdims=True)
    group_flag = jnp.where(in_group & ((row & (EXPERTS_PER_GROUP - 1)) == 0), 1.0, 0.0)
    return w / denom, group_flag


def _experts_to_lanes(x_t):
    pad = jnp.zeros((LANES - N_EXPERTS, x_t.shape[1]), x_t.dtype)
    return jnp.concatenate([x_t, pad], axis=0).T


def _split_bf16(x, terms):
    out = []
    for _ in range(terms - 1):
        t = x.astype(BF16)
        out.append(t)
        x = x - t.astype(F32)
    out.append(x.astype(BF16))
    return out


def _post_kernel(ac_ref, al_ref, gm_ref, xc_ref, xl_ref, mod_ref, oag_ref, wout_ref, g2_ref,
                 wr_ref, rb_ref, x1_ref, h2s_ref, y0_ref, pos_ref, cnt_ref):
    mod = mod_ref[0]
    gate1 = mod[:, 2 * D_MODEL:3 * D_MODEL]
    shift2 = mod[:, 3 * D_MODEL:4 * D_MODEL]
    scale2 = mod[:, 4 * D_MODEL:5 * D_MODEL]
    y0_ref[...] = jnp.zeros(y0_ref.shape, y0_ref.dtype)
    tm = xc_ref.shape[0]
    part = tm // POST_PARTS
    halves = [slice(p * part, (p + 1) * part) for p in range(POST_PARTS)]
    attn = _ctx_or_lat(ac_ref, al_ref)
    x_in = _ctx_or_lat(xc_ref, xl_ref)
    mixed = []
    for rows in halves:
        an = (_rms(attn[rows].astype(F32)) * oag_ref[...]).astype(BF16)
        mixed.append(_dot(jnp.concatenate([an, gm_ref[rows, :]], axis=1), wout_ref[...]))
    h2_hi, logits = [], []
    for rows, mix in zip(halves, mixed):
        x1 = x_in[rows] + gate1 * mix
        x1_ref[rows, :] = x1
        hi, lo = _split_bf16(_rms(x1) * g2_ref[...] * (1.0 + scale2) + shift2, 2)
        t = _dot(jnp.concatenate([hi, lo], axis=0), wr_ref[...])
        logits.append((t[:part, :LANES] + t[part:, :LANES]) + (t[:part, LANES:] + t[part:, LANES:]))
        h2_hi.append(hi)
    h2_hi = jnp.concatenate(h2_hi, axis=0)
    logits = jnp.concatenate(logits, axis=0)
    tiles = [slice(sub * TM_POST, (sub + 1) * TM_POST) for sub in range(POST_SUB)]
    r_i = lax.broadcasted_iota(jnp.int32, (TM_POST, TM_POST), 0)
    c_i = lax.broadcasted_iota(jnp.int32, (TM_POST, TM_POST), 1)
    before = jnp.where(c_i < r_i, 1.0, 0.0).astype(BF16)
    routed = []
    for rows in tiles:
        comb_t, flag_t = _route(logits[rows].T[0:N_EXPERTS, :], rb_ref[...])
        routed.append((_experts_to_lanes(comb_t), _experts_to_lanes(flag_t)))
    ranks = [_dot(before, flag.astype(BF16)) for _, flag in routed]
    for sub, rows in enumerate(tiles):
        comb, flag = routed[sub]
        _sort_tile(h2_hi[rows], comb, flag, ranks[sub],
                   h2s_ref.at[sub * LS_ROWS:(sub + 1) * LS_ROWS], pos_ref.at[rows], cnt_ref.at[sub])


def _sort_tile(h2_hi, comb, flag, rank, h2s_ref, pos_ref, cnt_ref):
    tm = TM_POST
    count = jnp.sum(flag, axis=0, keepdims=True)
    pieces = jnp.floor((count + (PIECE - 1)) * (1.0 / PIECE))
    start = (pltpu.roll(pieces, EXPERTS_PER_GROUP, 1) + pltpu.roll(pieces, 2 * EXPERTS_PER_GROUP, 1)
             + pltpu.roll(pieces, 3 * EXPERTS_PER_GROUP, 1)) * PIECE
    pos = jnp.sum(flag * (start + rank), axis=-1, keepdims=True)
    pos_ref[...] = pos
    cnt_ref[...] = count.astype(jnp.int32)
    pos_row = jnp.transpose(jnp.broadcast_to(pos, (tm, LANES)))[0:1, :].astype(jnp.int32)
    place = lax.broadcasted_iota(jnp.int32, (LS_ROWS, tm), 0) == pos_row
    place = jnp.where(place, 1.0, 0.0).astype(BF16)
    wide = jnp.concatenate([h2_hi] + _split_bf16(comb, COMB_TERMS), axis=1)
    h2s_ref[...] = _dot(place, wide).astype(BF16)


def _post_mixer(layer, attn_ctx, attn_lat, gm, x_ctx, x_lat, mod, oag, wout, g2, wr, rb):
    joint_x = x_ctx is x_lat
    tm = POST_SUB * TM_POST
    ls = POST_SUB * LS_ROWS
    row = lambda i: (i, 0)
    return pl.pallas_call(
        _post_kernel,
        out_shape=(
            jax.ShapeDtypeStruct((N_ROWS, D_MODEL), F32),
            jax.ShapeDtypeStruct((N_LS, XS_COLS), BF16),
            jax.ShapeDtypeStruct((N_LS, D_MODEL), BF16),
            jax.ShapeDtypeStruct((N_ROWS, 1), F32),
            jax.ShapeDtypeStruct((N_POST_TILES, 1, LANES), jnp.int32),
        ),
        grid=(N_ROWS // tm,),
        in_specs=_ctx_lat_specs(tm, ATTN_HEADS * V_DIM) + [
            pl.BlockSpec((tm, GM_WIDTH), row),
        ] + _ctx_lat_specs(tm, D_MODEL, joint=joint_x) + [
            _mod_spec(layer, tm),
            _of_layer(layer, (1, ATTN_HEADS * V_DIM)),
            _of_layer(layer, (D_MODEL, D_MODEL)),
            _of_layer(layer, (1, D_MODEL)),
            _full((D_MODEL, 2 * LANES)),
            _full((N_EXPERTS, 1)),
        ],
        out_specs=(
            pl.BlockSpec((tm, D_MODEL), row),
            pl.BlockSpec((ls, XS_COLS), row),
            pl.BlockSpec((ls, D_MODEL), row),
            pl.BlockSpec((tm, 1), row),
            pl.BlockSpec((POST_SUB, 1, LANES), lambda i: (i, 0, 0)),
        ),
        compiler_params=pltpu.CompilerParams(
            dimension_semantics=("arbitrary",), vmem_limit_bytes=VMEM_LIMIT),
        name="post_mixer",
    )(attn_ctx, attn_lat, gm, x_ctx, x_lat, mod, oag, wout, g2, wr, rb)


def _plan_kernel(cnt_ref, tg_ref, tn_ref, src_ref, nu_ref):
    def clear_src(j, c):
        for u in range(E_PIECES):
            src_ref[j * E_PIECES + u] = 0
        return c

    lax.fori_loop(0, SRC_SLOTS // E_PIECES, clear_src, 0)

    def clear_tile(j, c):
        tg_ref[j] = N_GROUPS - 1
        tn_ref[j] = 0
        return c

    lax.fori_loop(0, N_ETILES, clear_tile, 0)

    def n_pieces(i, g):
        return lax.shift_right_logical(cnt_ref[i, g * EXPERTS_PER_GROUP] + (PIECE - 1), PIECE_SHIFT)

    t = jnp.int32(0)
    for g in range(N_GROUPS):
        def tile_body(i, s, g=g):
            first = i * LS_PIECES
            for gp in range(g):
                first = first + n_pieces(i, gp)

            for p in range(TM_POST // PIECE):
                src_ref[s + p] = first + p
            return s + n_pieces(i, g)

        s0 = t * E_PIECES
        s1 = lax.fori_loop(0, N_POST_TILES, tile_body, s0)
        n = s1 - s0
        tiles = lax.shift_right_logical(n + (E_PIECES - 1), E_SHIFT)

        def mark_tile(u, c, g=g, n=n, t=t):
            tg_ref[t + u] = g
            tn_ref[t + u] = jnp.minimum(n - u * E_PIECES, E_PIECES)
            return c

        lax.fori_loop(0, tiles, mark_tile, 0)
        t = t + tiles
    nu_ref[0] = t


def _plan(counts):
    smem = pl.BlockSpec(memory_space=pltpu.SMEM)
    return pl.pallas_call(
        _plan_kernel,
        out_shape=(
            jax.ShapeDtypeStruct((N_ETILES,), jnp.int32),
            jax.ShapeDtypeStruct((N_ETILES,), jnp.int32),
            jax.ShapeDtypeStruct((SRC_SLOTS,), jnp.int32),
            jax.ShapeDtypeStruct((1,), jnp.int32),
        ),
        in_specs=[smem],
        out_specs=(smem, smem, smem, smem),
        name="expert_plan",
    )(counts)


GATHER_X, SCATTER_Y = 0, 1


def _moe_kernel(tg_ref, tn_ref, src_ref, nu_ref, h2s_hbm, y0_hbm, wg32_ref, wu32_ref, wd32_ref,
                wsg32_ref, wsu32_ref, wsd32_ref, y_hbm, xbuf, ybuf, wg_ref, wu_ref, wd_ref,
                wsg_ref, wsu_ref, wsd_ref, sem):
    del y0_hbm
    j = pl.program_id(0)
    n_used = nu_ref[0]
    slot = lax.rem(j, 2)

    @pl.when(j == 0)
    def _():
        wsg_ref[...] = wsg32_ref[0].astype(BF16)
        wsu_ref[...] = wsu32_ref[0].astype(BF16)
        wsd_ref[...] = wsd32_ref[0].astype(BF16)

    @pl.when((j == 0) | (tg_ref[j] != tg_ref[jnp.maximum(j - 1, 0)]))
    def _():
        for k in range(EXPERTS_PER_GROUP):
            wg_ref[k] = wg32_ref[0, k].astype(BF16)
            wu_ref[k] = wu32_ref[0, k].astype(BF16)
            wd_ref[k] = wd32_ref[0, k].astype(BF16)

    def piece_rows(t, k):
        hbm_rows = pl.ds(pl.multiple_of(src_ref[t * E_PIECES + k] * PIECE, PIECE), PIECE)
        buf_rows = pl.ds(pl.multiple_of(k * PIECE, PIECE), PIECE)
        return hbm_rows, buf_rows

    def gather_copies(t, k, slot):
        hbm_rows, buf_rows = piece_rows(t, k)
        return (
            pltpu.make_async_copy(h2s_hbm.at[hbm_rows], xbuf.at[slot, buf_rows], sem.at[GATHER_X, slot]),
        )

    def scatter_copies(t, k, slot):
        hbm_rows, buf_rows = piece_rows(t, k)
        return (
            pltpu.make_async_copy(ybuf.at[slot, buf_rows], y_hbm.at[hbm_rows], sem.at[SCATTER_Y, slot]),
        )

    def for_pieces(t, slot, copies, action):
        def body(k, c):
            for cp in copies(t, k, slot):
                action(cp)
            return c

        @pl.when(tn_ref[t] == E_PIECES)
        def _():
            lax.fori_loop(0, E_PIECES, body, 0, unroll=8)

        @pl.when(tn_ref[t] != E_PIECES)
        def _():
            lax.fori_loop(0, tn_ref[t], body, 0)

    start = lambda cp: cp.start()
    wait = lambda cp: cp.wait()

    @pl.when(j == 0)
    def _():
        xbuf[...] = jnp.zeros(xbuf.shape, xbuf.dtype)
        for_pieces(0, 0, gather_copies, start)

    @pl.when(j + 1 < n_used)
    def _():
        for_pieces(j + 1, 1 - slot, gather_copies, start)

    @pl.when(j < n_used)
    def _():
        for_pieces(j, slot, gather_copies, wait)
        x = xbuf[slot, :, 0:D_MODEL]
        comb = sum(xbuf[slot, :, D_MODEL + t * LANES:D_MODEL + (t + 1) * LANES].astype(F32)
                   for t in range(COMB_TERMS))
        lane = lax.broadcasted_iota(jnp.int32, (TM_E, LANES), 1)
        first = tg_ref[j] * EXPERTS_PER_GROUP
        gates = [_dot(x, wsg_ref[...])] + [_dot(x, wg_ref[k]) for k in range(EXPERTS_PER_GROUP)]
        ups = [_dot(x, wsu_ref[...])] + [_dot(x, wu_ref[k]) for k in range(EXPERTS_PER_GROUP)]
        acts = [(_silu(gates[0]) * ups[0]).astype(BF16)]
        for k in range(EXPERTS_PER_GROUP):
            wk = jnp.sum(jnp.where(lane == first + k, comb, 0.0), axis=-1, keepdims=True)
            acts.append((_silu(gates[k + 1]) * ups[k + 1] * wk).astype(BF16))
        downs = [wsd_ref[...]] + [wd_ref[k] for k in range(EXPERTS_PER_GROUP)]
        y = sum(_dot(a, w) for a, w in zip(acts, downs))
        ybuf[slot] = y.astype(BF16)
        for_pieces(j, slot, scatter_copies, start)

    @pl.when((j >= 1) & (j < n_used))
    def _():
        for_pieces(j - 1, 1 - slot, scatter_copies, wait)

    @pl.when(j == n_used - 1)
    def _():
        for_pieces(j, slot, scatter_copies, wait)


def _experts(plan, h2s, y0, layer, w_gate, w_up, w_down, ws_gate, ws_up, ws_down):
    tg, tn, src, nu = plan
    any_spec = pl.BlockSpec(memory_space=pl.ANY)
    group = lambda j, tg, tn, src, nu: (layer, tg[j], 0, 0)
    whole = lambda j, tg, tn, src, nu: (layer, 0, 0)
    g = EXPERTS_PER_GROUP
    return pl.pallas_call(
        _moe_kernel,
        out_shape=jax.ShapeDtypeStruct((N_LS, D_MODEL), BF16),
        grid_spec=pltpu.PrefetchScalarGridSpec(
            num_scalar_prefetch=4,
            grid=(N_ETILES,),
            in_specs=[
                any_spec, any_spec,
                pl.BlockSpec((1, g, D_MODEL, EXPERT_FF), group),
                pl.BlockSpec((1, g, D_MODEL, EXPERT_FF), group),
                pl.BlockSpec((1, g, EXPERT_FF, D_MODEL), group),
                pl.BlockSpec((1, D_MODEL, SHARED_FF), whole),
                pl.BlockSpec((1, D_MODEL, SHARED_FF), whole),
                pl.BlockSpec((1, SHARED_FF, D_MODEL), whole),
            ],
            out_specs=any_spec,
            scratch_shapes=[
                pltpu.VMEM((2, TM_E, XS_COLS), BF16),
                pltpu.VMEM((2, TM_E, D_MODEL), BF16),
                pltpu.VMEM((g, D_MODEL, EXPERT_FF), BF16),
                pltpu.VMEM((g, D_MODEL, EXPERT_FF), BF16),
                pltpu.VMEM((g, EXPERT_FF, D_MODEL), BF16),
                pltpu.VMEM((D_MODEL, SHARED_FF), BF16),
                pltpu.VMEM((D_MODEL, SHARED_FF), BF16),
                pltpu.VMEM((SHARED_FF, D_MODEL), BF16),
                pltpu.SemaphoreType.DMA((2, 2)),
            ],
        ),
        input_output_aliases={5: 0},
        compiler_params=pltpu.CompilerParams(
            dimension_semantics=("arbitrary",), vmem_limit_bytes=VMEM_LIMIT),
        name="experts",
    )(tg, tn, src, nu, h2s, y0, w_gate, w_up, w_down, ws_gate, ws_up, ws_down)


def _unsort(y_ref, pos_ref, sub):
    pos = pos_ref[sub * TM_POST:(sub + 1) * TM_POST, :].astype(jnp.int32)
    pick = lax.broadcasted_iota(jnp.int32, (TM_POST, LS_ROWS), 1) == pos
    return _dot(jnp.where(pick, 1.0, 0.0).astype(BF16), y_ref[sub * LS_ROWS:(sub + 1) * LS_ROWS, :])


def _combine_kernel(y_ref, pos_ref, x1_ref, mod_ref, fg_ref, o_ref):
    gate2 = mod_ref[0][:, 5 * D_MODEL:6 * D_MODEL]
    for sub in range(COMBINE_SUB):
        rows = slice(sub * TM_POST, (sub + 1) * TM_POST)
        x2 = x1_ref[rows, :] + gate2 * _unsort(y_ref, pos_ref, sub)
        o_ref[rows, :] = _rms(x2) * fg_ref[...]


def _combine(y, pos, x1, mod, fg, first_tile, n_tiles):
    tm = COMBINE_SUB * TM_POST
    row = lambda i: (first_tile + i, 0)
    return pl.pallas_call(
        _combine_kernel,
        out_shape=jax.ShapeDtypeStruct((n_tiles * tm, D_MODEL), F32),
        grid=(n_tiles,),
        in_specs=[
            pl.BlockSpec((COMBINE_SUB * LS_ROWS, D_MODEL), row),
            pl.BlockSpec((tm, 1), row),
            pl.BlockSpec((tm, D_MODEL), row),
            _mod_spec(DEPTH - 1, tm, first_tile),
            _full((1, D_MODEL)),
        ],
        out_specs=pl.BlockSpec((tm, D_MODEL), lambda i: (i, 0)),
        compiler_params=pltpu.CompilerParams(
            dimension_semantics=("arbitrary",), vmem_limit_bytes=VMEM_LIMIT),
        name="combine",
    )(y, pos, x1, mod, fg)


def _rope_table():
    rows = DEC_SEQ // GRID_W
    row = jnp.repeat(jnp.arange(rows, dtype=F32), GRID_W)
    col = jnp.tile(jnp.arange(GRID_W, dtype=F32), rows)
    half = QK_ROPE // 2
    freqs = 1.0 / (ROPE_BASE ** (jnp.arange(0, half, 2, dtype=F32) / half))
    ang = jnp.concatenate([row[:, None] * freqs, col[:, None] * freqs], axis=-1)
    cos, sin = jnp.cos(ang), jnp.sin(ang)
    lat = jnp.concatenate([cos, cos, -sin, sin], axis=-1)
    ident = jnp.concatenate([jnp.ones((TM_PRE, QK_ROPE), F32), jnp.zeros((TM_PRE, QK_ROPE), F32)], axis=-1)
    return jnp.concatenate([ident, lat], axis=0)


_DEINT = np.concatenate([np.arange(0, QK_ROPE, 2), np.arange(1, QK_ROPE, 2)])
_SWAP = np.concatenate([np.arange(1, QK_ROPE, 2), np.arange(0, QK_ROPE, 2)])
_INTERLEAVE = np.argsort(_DEINT)


def _layout_w_in(w_in):
    cq_ckv = w_in[..., :Q_LORA + KV_LORA]
    kr = w_in[..., Q_LORA + KV_LORA:Q_LORA + KV_LORA + QK_ROPE]
    uv = w_in[..., Q_LORA + KV_LORA + QK_ROPE:]
    return jnp.concatenate([cq_ckv, kr[..., _DEINT], kr[..., _SWAP], uv], axis=-1).astype(BF16)


def _layout_w_qb(w_qb):
    w = (w_qb * (ATTN_SCALE * LOG2E)).reshape(DEPTH, Q_LORA, ATTN_HEADS, QK_NOPE + QK_ROPE)
    nope, rope = w[..., :QK_NOPE], w[..., QK_NOPE:]
    w = jnp.concatenate([nope, rope[..., _DEINT], rope[..., _SWAP]], axis=-1)
    return w.reshape(DEPTH, Q_LORA, ATTN_HEADS * HEAD_PAD).astype(BF16)


def _layout_w_kvb(w_kvb):
    w = w_kvb.reshape(DEPTH, KV_LORA, ATTN_HEADS, QK_NOPE + V_DIM)
    k = w[..., :QK_NOPE].reshape(DEPTH, KV_LORA, ATTN_HEADS * QK_NOPE)
    v = w[..., QK_NOPE:].reshape(DEPTH, KV_LORA, ATTN_HEADS * V_DIM)
    return jnp.concatenate([k, v], axis=-1).astype(BF16)


def kernel(x_prompt, x_sample, cache_ckv, cache_krope, c, c_ctx, norm1_g, w_ada, b_ada, w_in,
           q_norm_g, w_qb, kv_norm_g, w_kvb, gm_norm_g, w_spatial, b_spatial, onorm_attn_g,
           onorm_gm_g, w_out, norm2_g, w_router, router_bias, w_gate, w_up, w_down, ws_gate,
           ws_up, ws_down, final_norm_g):
    x_ctx = x_prompt.reshape(N_CTX, D_MODEL)
    x_lat = x_sample.reshape(N_LAT, D_MODEL)
    cond = jnp.concatenate([c_ctx[None, :], c, jnp.zeros((COND_PAD - N_COND, D_MODEL), F32)], axis=0)
    mod = _modulation(cond, w_ada, b_ada).reshape(DEPTH, COND_PAD, 1, 6 * D_MODEL)
    rope_tab = _rope_table()

    wkvb = _layout_w_kvb(w_kvb)
    cache_kr = jnp.pad(cache_krope[..., _DEINT], ((0, 0), (0, 0), (0, 0), (0, LANES - QK_ROPE)))
    kc, vct = _cache_kv(cache_ckv, cache_kr, wkvb)

    wr = jnp.pad(w_router, ((0, 0), (0, LANES - N_EXPERTS)))
    wr_hi = wr.astype(BF16)
    wr = jnp.concatenate([wr_hi, (wr - wr_hi.astype(F32)).astype(BF16)], axis=1)
    rb = router_bias.reshape(N_EXPERTS, 1)
    fg = final_norm_g.reshape(1, D_MODEL)

    row_stack = lambda g: g.reshape(DEPTH, 1, -1)
    bs = jnp.broadcast_to(jnp.swapaxes(b_spatial, 1, 2)[..., None],
                          (DEPTH, CHUNK, GM_GROUPS, GM_GROUP_DIM)).reshape(DEPTH, CHUNK, GM_WIDTH)
    pre_params = (row_stack(norm1_g), _layout_w_in(w_in), row_stack(q_norm_g), _layout_w_qb(w_qb),
                  row_stack(kv_norm_g), wkvb, row_stack(gm_norm_g), w_spatial.astype(BF16), bs,
                  row_stack(onorm_gm_g))
    post_params = (row_stack(onorm_attn_g), w_out.astype(BF16), row_stack(norm2_g), wr, rb)

    ckv_out, kr_out = [], []
    layer_input = (x_ctx, x_lat)
    for l in range(DEPTH):
        q, k, vt, gm, ckv_n, kr, *formed = _pre_mixer(l, layer_input, mod, rope_tab, *pre_params)
        ckv_out.append(ckv_n[:N_CTX].reshape(BATCH, SEQ, KV_LORA))
        kr_out.append(kr[:N_CTX][:, _INTERLEAVE].reshape(BATCH, SEQ, QK_ROPE))
        x_pair = (formed[0], formed[0]) if formed else layer_input
        x1, h2s, y0, pos, counts = _post_mixer(
            l, _ctx_attention(q, k, vt), _lat_attention(q, k, vt, kc, vct, l), gm, *x_pair, mod,
            *post_params)
        plan = _plan(counts.reshape(N_POST_TILES, LANES))
        y = _experts(plan, h2s, y0, l, w_gate, w_up, w_down, ws_gate, ws_up, ws_down)
        layer_input = (y, pos, x1)

    n_ctx_tiles = N_CTX // (COMBINE_SUB * TM_POST)
    n_lat_tiles = N_LAT // (COMBINE_SUB * TM_POST)
    x_ctx = _combine(y, pos, x1, mod, fg, 0, n_ctx_tiles)
    x_lat = _combine(y, pos, x1, mod, fg, n_ctx_tiles, n_lat_tiles)
    y_prompt = x_ctx.reshape(BATCH, SEQ, D_MODEL)
    y_sample = x_lat.reshape(DEC_BATCH, DEC_SEQ, D_MODEL)
    return y_prompt, y_sample, jnp.stack(ckv_out, axis=1), jnp.stack(kr_out, axis=1)
```

```python
import functools
import math

import jax
import jax.numpy as jnp
import numpy as np
from jax import lax
from jax.experimental import pallas as pl
from jax.experimental.pallas import tpu as pltpu

D_MODEL = 1024
BATCH = 16
SEQ = 256
DEPTH = 2
DEC_BATCH = 4
DEC_SEQ = 4096
PAST_LEN = 256
GRID_W = 64
ATTN_HEADS = 4
QK_NOPE = 128
QK_ROPE = 64
V_DIM = 128
V_EXT = V_DIM + 16
Q_LORA = 384
KV_LORA = 256
ATTN_SCALE = (QK_NOPE + QK_ROPE) ** -0.5
ROPE_BASE = 10000.0
CHUNK = 128
GM_WIDTH = 512
GM_GROUPS = 4
GM_GROUP_DIM = 128
N_EXPERTS = 16
N_GROUPS = 4
EXPERTS_PER_GROUP = 4
EXPERT_FF = 256
SHARED_FF = 256
EPS = 1e-6

N_CTX = BATCH * SEQ
N_LAT = DEC_BATCH * DEC_SEQ
N_ROWS = N_CTX + N_LAT
N_COND = 1 + DEC_BATCH
COND_PAD = 16
HEAD_PAD = 256
IN_COLS = Q_LORA + KV_LORA + 2 * QK_ROPE + 2 * GM_WIDTH
LANES = 128
LOG2E = 1.4426950408889634

TM_PRE = 512
TM_POST = 256
COMB_TERMS = 3
XS_COLS = D_MODEL + COMB_TERMS * LANES
COMBINE_SUB = 4
POST_SUB = 4
POST_PARTS = 4
PIECE = 16
LS_ROWS = TM_POST + N_GROUPS * PIECE
LS_PIECES = LS_ROWS // PIECE
N_POST_TILES = N_ROWS // TM_POST
N_LS = N_POST_TILES * LS_ROWS
TM_E = 512
E_PIECES = TM_E // PIECE
PIECE_SHIFT = PIECE.bit_length() - 1
E_SHIFT = E_PIECES.bit_length() - 1
N_ETILES = -(-(N_POST_TILES * (TM_POST // PIECE + N_GROUPS - 1)) // E_PIECES) + N_GROUPS
SRC_SLOTS = (N_ETILES + 1) * E_PIECES
TQ = 512
CTX_SUB = 4
SCORES_AHEAD = 4
HEADS_PER_STEP = 4
VMEM_LIMIT = 56 * 1024 * 1024

F32 = jnp.float32
BF16 = jnp.bfloat16


def _rms(x):
    return x * lax.rsqrt(jnp.mean(x * x, axis=-1, keepdims=True) + EPS)


def _gelu(x):
    return 0.5 * x * (1.0 + jnp.tanh(math.sqrt(2.0 / math.pi) * (x + 0.044715 * (x * x * x))))


def _silu(x):
    return x * (1.0 / (1.0 + jnp.exp(-x)))


def _dot(a, b):
    return jnp.dot(a, b, preferred_element_type=F32)


def _dot_nt(a, b):
    return lax.dot_general(a, b, (((1,), (1,)), ((), ())), preferred_element_type=F32)


def _cond_row(i, tm):
    n_ctx_tiles = N_CTX // tm
    per_batch = DEC_SEQ // tm
    return jnp.where(i < n_ctx_tiles, 0, 1 + (i - n_ctx_tiles) // per_batch)


def _rope_block(i, tm):
    n_ctx_tiles = N_CTX // tm
    per_batch = DEC_SEQ // tm
    return jnp.where(i < n_ctx_tiles, 0, 1 + (i - n_ctx_tiles) % per_batch)


def _ctx_lat_specs(tm, width, joint=False):
    n_ctx_tiles = N_CTX // tm
    lat_first = n_ctx_tiles if joint else 0
    return [pl.BlockSpec((tm, width), lambda i: (jnp.minimum(i, n_ctx_tiles - 1), 0)),
            pl.BlockSpec((tm, width), lambda i: (lat_first + jnp.maximum(i - n_ctx_tiles, 0), 0))]


def _ctx_or_lat(ctx_ref, lat_ref):
    n_ctx_tiles = N_CTX // ctx_ref.shape[0]
    return jnp.where(pl.program_id(0) < n_ctx_tiles, ctx_ref[...], lat_ref[...])


def _full(shape):
    n = len(shape)
    return pl.BlockSpec(shape, lambda *_: (0,) * n)


def _of_layer(layer, shape):
    n = len(shape)
    return pl.BlockSpec((None,) + tuple(shape), lambda *_: (layer,) + (0,) * n)


def _mod_spec(layer, tm, first_tile=0):
    return pl.BlockSpec((None, 1, 1, 6 * D_MODEL),
                        lambda i: (layer, _cond_row(first_tile + i, tm), 0, 0))


def _mod_kernel(cond_ref, w_ref, b_ref, o_ref):
    s_hi, s_lo = _split_bf16(_silu(cond_ref[...]), 2)
    w_hi, w_lo = _split_bf16(w_ref[0], 2)
    t = _dot(jnp.concatenate([s_hi, s_lo], axis=0), w_hi)
    o_ref[0] = (t[:COND_PAD] + t[COND_PAD:]) + _dot(s_hi, w_lo) + b_ref[0]


def _modulation(cond, w_ada, b_ada):
    tn = 1536
    return pl.pallas_call(
        _mod_kernel,
        out_shape=jax.ShapeDtypeStruct((DEPTH, COND_PAD, 6 * D_MODEL), F32),
        grid=(DEPTH, 6 * D_MODEL // tn),
        in_specs=[
            pl.BlockSpec((COND_PAD, D_MODEL), lambda l, j: (0, 0)),
            pl.BlockSpec((1, D_MODEL, tn), lambda l, j: (l, 0, j)),
            pl.BlockSpec((1, 1, tn), lambda l, j: (l, 0, j)),
        ],
        out_specs=pl.BlockSpec((1, COND_PAD, tn), lambda l, j: (l, 0, j)),
        compiler_params=pltpu.CompilerParams(
            dimension_semantics=("arbitrary", "arbitrary"), vmem_limit_bytes=VMEM_LIMIT),
        name="modulation",
    )(cond, w_ada, b_ada.reshape(DEPTH, 1, 6 * D_MODEL))


def _store_values_t(vt_ref, v):
    t = v.shape[0]
    v_t = v.T.astype(BF16)
    for hd in range(ATTN_HEADS):
        vt_ref[hd * V_EXT:hd * V_EXT + V_DIM, :] = v_t[hd * V_DIM:(hd + 1) * V_DIM, :]
        vt_ref[hd * V_EXT + V_DIM:(hd + 1) * V_EXT, :] = jnp.ones((V_EXT - V_DIM, t), BF16)


def _pre_kernel(*refs, after_experts):
    if after_experts:
        (y_ref, pos_ref, x1_ref, prev_mod_ref, mod_ref, rope_ref, g1_ref, win_ref, qg_ref, wqb_ref,
         kvg_ref, wkvb_ref, gmg_ref, ws_ref, bs_ref, ogm_ref,
         q_ref, k_ref, vt_ref, gm_ref, ckv_ref, kr_ref, x_ref) = refs
        tm = x1_ref.shape[0]
        gate2 = prev_mod_ref[0][:, 5 * D_MODEL:6 * D_MODEL]
        x = jnp.concatenate([x1_ref[sub * TM_POST:(sub + 1) * TM_POST, :] + gate2 * _unsort(y_ref, pos_ref, sub)
                             for sub in range(tm // TM_POST)], axis=0)
        x_ref[...] = x
    else:
        (xc_ref, xl_ref, mod_ref, rope_ref, g1_ref, win_ref, qg_ref, wqb_ref,
         kvg_ref, wkvb_ref, gmg_ref, ws_ref, bs_ref, ogm_ref,
         q_ref, k_ref, vt_ref, gm_ref, ckv_ref, kr_ref) = refs
        tm = xc_ref.shape[0]
        x = _ctx_or_lat(xc_ref, xl_ref)
    mod = mod_ref[0]
    shift1 = mod[:, 0:D_MODEL]
    scale1 = mod[:, D_MODEL:2 * D_MODEL]
    h = _rms(x) * g1_ref[...] * (1.0 + scale1) + shift1
    y = _dot(h.astype(BF16), win_ref[...])
    cq = y[:, 0:Q_LORA]
    ckv = y[:, Q_LORA:Q_LORA + KV_LORA]
    kr2 = y[:, 640:768]
    u = y[:, 768:768 + GM_WIDTH]
    vv = y[:, 768 + GM_WIDTH:768 + 2 * GM_WIDTH]

    rope = rope_ref[...]
    lane = lax.broadcasted_iota(jnp.int32, (tm, LANES), 1)

    kr_ref[...] = kr2[:, 0:QK_ROPE]
    t = kr2 * rope
    k_rot = jnp.where(lane < QK_ROPE, t + pltpu.roll(t, QK_ROPE, 1), 0.0).astype(BF16)

    ckv_n = _rms(ckv) * kvg_ref[...]
    ckv_ref[...] = ckv_n
    kv = _dot(ckv_n.astype(BF16), wkvb_ref[...])
    for hd in range(ATTN_HEADS):
        k_ref[:, hd * HEAD_PAD:hd * HEAD_PAD + QK_NOPE] = (
            kv[:, hd * QK_NOPE:(hd + 1) * QK_NOPE].astype(BF16))
        k_ref[:, hd * HEAD_PAD + QK_NOPE:(hd + 1) * HEAD_PAD] = k_rot
    _store_values_t(vt_ref, kv[:, ATTN_HEADS * QK_NOPE:])

    q = _dot((_rms(cq) * qg_ref[...]).astype(BF16), wqb_ref[...])
    for hd in range(ATTN_HEADS):
        q_ref[:, hd * HEAD_PAD:hd * HEAD_PAD + QK_NOPE] = (
            q[:, hd * HEAD_PAD:hd * HEAD_PAD + QK_NOPE].astype(BF16))
        t = q[:, hd * HEAD_PAD + QK_NOPE:(hd + 1) * HEAD_PAD] * rope
        q_ref[:, hd * HEAD_PAD + QK_NOPE:(hd + 1) * HEAD_PAD] = (
            t + pltpu.roll(t, QK_ROPE, 1)).astype(BF16)

    ug = _gelu(u)
    vg = _gelu(vv)
    cols = []
    for g in range(GM_GROUPS):
        sl = slice(g * GM_GROUP_DIM, (g + 1) * GM_GROUP_DIM)
        vn = (_rms(vg[:, sl]) * gmg_ref[:, sl]).astype(BF16)
        rows = []
        for c in range(tm // CHUNK):
            sv = _dot(ws_ref[g], vn[c * CHUNK:(c + 1) * CHUNK]) + bs_ref[:, sl]
            rows.append(ug[c * CHUNK:(c + 1) * CHUNK, sl] * sv)
        cols.append(jnp.concatenate(rows, axis=0))
    gm = jnp.concatenate(cols, axis=1)
    gm_ref[...] = (_rms(gm) * ogm_ref[...]).astype(BF16)


def _pre_mixer(layer, layer_input, mod, rope_tab, g1, win, qg, wqb, kvg, wkvb, gmg, ws, bs, ogm):
    tm = TM_PRE
    row = lambda i: (i, 0)
    mod_spec = _mod_spec(layer, tm)
    after_experts = len(layer_input) == 3
    if after_experts:
        layer_input = (*layer_input, mod)
        input_specs = [pl.BlockSpec((tm // TM_POST * LS_ROWS, D_MODEL), row),
                       pl.BlockSpec((tm, 1), row), pl.BlockSpec((tm, D_MODEL), row),
                       _mod_spec(layer - 1, tm)]
        extra_shape = (jax.ShapeDtypeStruct((N_ROWS, D_MODEL), F32),)
        extra_spec = (pl.BlockSpec((tm, D_MODEL), row),)
    else:
        input_specs = _ctx_lat_specs(tm, D_MODEL)
        extra_shape = extra_spec = ()
    return pl.pallas_call(
        functools.partial(_pre_kernel, after_experts=after_experts),
        out_shape=(
            jax.ShapeDtypeStruct((N_ROWS, ATTN_HEADS * HEAD_PAD), BF16),
            jax.ShapeDtypeStruct((N_ROWS, ATTN_HEADS * HEAD_PAD), BF16),
            jax.ShapeDtypeStruct((ATTN_HEADS * V_EXT, N_ROWS), BF16),
            jax.ShapeDtypeStruct((N_ROWS, GM_WIDTH), BF16),
            jax.ShapeDtypeStruct((N_ROWS, KV_LORA), F32),
            jax.ShapeDtypeStruct((N_ROWS, QK_ROPE), F32),
        ) + extra_shape,
        grid=(N_ROWS // tm,),
        in_specs=input_specs + [
            mod_spec,
            pl.BlockSpec((tm, LANES), lambda i: (_rope_block(i, tm), 0)),
            _of_layer(layer, (1, D_MODEL)),
            _of_layer(layer, (D_MODEL, IN_COLS)),
            _of_layer(layer, (1, Q_LORA)),
            _of_layer(layer, (Q_LORA, ATTN_HEADS * HEAD_PAD)),
            _of_layer(layer, (1, KV_LORA)),
            _of_layer(layer, (KV_LORA, ATTN_HEADS * (QK_NOPE + V_DIM))),
            _of_layer(layer, (1, GM_WIDTH)),
            _of_layer(layer, (GM_GROUPS, CHUNK, CHUNK)),
            _of_layer(layer, (CHUNK, GM_WIDTH)),
            _of_layer(layer, (1, GM_WIDTH)),
        ],
        out_specs=(
            pl.BlockSpec((tm, ATTN_HEADS * HEAD_PAD), row),
            pl.BlockSpec((tm, ATTN_HEADS * HEAD_PAD), row),
            pl.BlockSpec((ATTN_HEADS * V_EXT, tm), lambda i: (0, i)),
            pl.BlockSpec((tm, GM_WIDTH), row),
            pl.BlockSpec((tm, KV_LORA), row),
            pl.BlockSpec((tm, QK_ROPE), row),
        ) + extra_spec,
        compiler_params=pltpu.CompilerParams(
            dimension_semantics=("arbitrary",), vmem_limit_bytes=VMEM_LIMIT),
        name="pre_mixer",
    )(*layer_input, mod, rope_tab, g1, win, qg, wqb, kvg, wkvb, gmg, ws, bs, ogm)


def _cache_kv_kernel(ckv_ref, kr_ref, wkvb_ref, k_ref, vt_ref):
    kv = _dot(ckv_ref[0, 0].astype(BF16), wkvb_ref[0])
    kr = kr_ref[0, 0].astype(BF16)
    for hd in range(ATTN_HEADS):
        k_ref[0, 0, :, hd * HEAD_PAD:hd * HEAD_PAD + QK_NOPE] = (
            kv[:, hd * QK_NOPE:(hd + 1) * QK_NOPE].astype(BF16))
        k_ref[0, 0, :, hd * HEAD_PAD + QK_NOPE:(hd + 1) * HEAD_PAD] = kr
    _store_values_t(vt_ref.at[0, 0], kv[:, ATTN_HEADS * QK_NOPE:])


def _cache_kv(cache_ckv, cache_kr_pad, wkvb):
    blk = lambda w: pl.BlockSpec((1, 1, PAST_LEN, w), lambda l, b: (b, l, 0, 0))
    return pl.pallas_call(
        _cache_kv_kernel,
        out_shape=(
            jax.ShapeDtypeStruct((DEC_BATCH, DEPTH, PAST_LEN, ATTN_HEADS * HEAD_PAD), BF16),
            jax.ShapeDtypeStruct((DEC_BATCH, DEPTH, ATTN_HEADS * V_EXT, PAST_LEN), BF16),
        ),
        grid=(DEPTH, DEC_BATCH),
        in_specs=[
            blk(KV_LORA),
            blk(LANES),
            pl.BlockSpec((1, KV_LORA, ATTN_HEADS * (QK_NOPE + V_DIM)), lambda l, b: (l, 0, 0)),
        ],
        out_specs=(blk(ATTN_HEADS * HEAD_PAD),
                   pl.BlockSpec((1, 1, ATTN_HEADS * V_EXT, PAST_LEN), lambda l, b: (b, l, 0, 0))),
        compiler_params=pltpu.CompilerParams(
            dimension_semantics=("arbitrary", "arbitrary"), vmem_limit_bytes=VMEM_LIMIT),
        name="cache_kv",
    )(cache_ckv, cache_kr_pad, wkvb)


def _attn_scores(qh, key_blocks):
    return [_dot_nt(kb, qh) for kb in key_blocks]


def _attn_values(s, vt_blocks):
    s = [si.astype(BF16) for si in s]
    m = functools.reduce(jnp.maximum, [jnp.max(si, axis=0, keepdims=True) for si in s])
    o_t = sum(_dot(vt, jnp.exp2(si - m)) for vt, si in zip(vt_blocks, s))
    return (o_t[0:V_DIM, :] / o_t[V_DIM:V_DIM + 1, :]).T


def _ctx_attn_kernel(q_ref, k_ref, vt_ref, o_ref):
    chains = [(slice(sb * SEQ, (sb + 1) * SEQ), hd) for sb in range(CTX_SUB) for hd in range(ATTN_HEADS)]
    hs = lambda hd: slice(hd * HEAD_PAD, (hd + 1) * HEAD_PAD)
    scores = [_attn_scores(q_ref[rows, hs(hd)], [k_ref[rows, hs(hd)]]) for rows, hd in chains]
    for (rows, hd), s in zip(chains, scores):
        out = _attn_values(s, [vt_ref[hd * V_EXT:(hd + 1) * V_EXT, rows]])
        o_ref[rows, hd * V_DIM:(hd + 1) * V_DIM] = out.astype(o_ref.dtype)


def _ctx_attention(q, k, vt):
    blk = lambda w: pl.BlockSpec((CTX_SUB * SEQ, w), lambda b: (b, 0))
    return pl.pallas_call(
        _ctx_attn_kernel,
        out_shape=jax.ShapeDtypeStruct((N_CTX, ATTN_HEADS * V_DIM), BF16),
        grid=(BATCH // CTX_SUB,),
        in_specs=[blk(ATTN_HEADS * HEAD_PAD), blk(ATTN_HEADS * HEAD_PAD),
                  pl.BlockSpec((ATTN_HEADS * V_EXT, CTX_SUB * SEQ), lambda b: (0, b))],
        out_specs=blk(ATTN_HEADS * V_DIM),
        compiler_params=pltpu.CompilerParams(
            dimension_semantics=("arbitrary",), vmem_limit_bytes=VMEM_LIMIT),
        name="ctx_attention",
    )(q, k, vt)


def _lat_attn_kernel(q_ref, k_ref, vt_ref, kc_ref, vct_ref, o_ref):
    hs = [slice(hd * HEAD_PAD, (hd + 1) * HEAD_PAD) for hd in range(HEADS_PER_STEP)]
    vs = [slice(hd * V_DIM, (hd + 1) * V_DIM) for hd in range(HEADS_PER_STEP)]
    ve = [slice(hd * V_EXT, (hd + 1) * V_EXT) for hd in range(HEADS_PER_STEP)]

    def scores(hd):
        return _attn_scores(q_ref[:, hs[hd]], [kc_ref[0, 0, :, hs[hd]], k_ref[:, hs[hd]]])

    pending = [scores(hd) for hd in range(min(SCORES_AHEAD, HEADS_PER_STEP))]
    for hd in range(HEADS_PER_STEP):
        if hd + SCORES_AHEAD < HEADS_PER_STEP:
            pending.append(scores(hd + SCORES_AHEAD))
        o_ref[:, vs[hd]] = _attn_values(
            pending.pop(0), [vct_ref[0, 0, ve[hd], :], vt_ref[ve[hd], :]]).astype(o_ref.dtype)


def _lat_attention(q, k, vt, kc, vct, layer):
    nq = DEC_SEQ // TQ
    ctx_q_tiles = N_CTX // TQ
    ctx_kv_blocks = N_CTX // DEC_SEQ
    qk_w = HEADS_PER_STEP * HEAD_PAD
    v_w = HEADS_PER_STEP * V_DIM
    vt_w = HEADS_PER_STEP * V_EXT
    return pl.pallas_call(
        _lat_attn_kernel,
        out_shape=jax.ShapeDtypeStruct((N_LAT, ATTN_HEADS * V_DIM), BF16),
        grid=(DEC_BATCH, ATTN_HEADS // HEADS_PER_STEP, nq),
        in_specs=[
            pl.BlockSpec((TQ, qk_w), lambda b, h, i: (ctx_q_tiles + b * nq + i, h)),
            pl.BlockSpec((DEC_SEQ, qk_w), lambda b, h, i: (ctx_kv_blocks + b, h)),
            pl.BlockSpec((vt_w, DEC_SEQ), lambda b, h, i: (h, ctx_kv_blocks + b)),
            pl.BlockSpec((1, 1, PAST_LEN, qk_w), lambda b, h, i: (b, layer, 0, h)),
            pl.BlockSpec((1, 1, vt_w, PAST_LEN), lambda b, h, i: (b, layer, h, 0)),
        ],
        out_specs=pl.BlockSpec((TQ, v_w), lambda b, h, i: (b * nq + i, h)),
        compiler_params=pltpu.CompilerParams(
            dimension_semantics=("arbitrary", "arbitrary", "arbitrary"),
            vmem_limit_bytes=VMEM_LIMIT),
        name="lat_attention",
    )(q, k, vt, kc, vct)


def _group_peer(x, row, d, width, period):
    step = d * width
    ahead = pltpu.roll(x, N_EXPERTS - step, 0)
    wraps = (row & (period - 1)) + step >= period
    if period == N_EXPERTS:
        return ahead, wraps
    return jnp.where(wraps, pltpu.roll(x, period - step, 0), ahead), wraps


def _route(logits_t, bias):
    row = lax.broadcasted_iota(jnp.int32, logits_t.shape, 0)
    s = 1.0 / (1.0 + jnp.exp(-logits_t))
    sb = s + bias
    rank = jnp.zeros(sb.shape, jnp.int32)
    for d in range(1, EXPERTS_PER_GROUP):
        o, wraps = _group_peer(sb, row, d, 1, EXPERTS_PER_GROUP)
        beats = (o > sb) | ((o == sb) & wraps)
        rank = rank + beats.astype(jnp.int32)
    top2 = rank < 2
    t = jnp.where(top2, sb, 0.0)
    gscore = t
    for d in range(1, EXPERTS_PER_GROUP):
        gscore = gscore + _group_peer(t, row, d, 1, EXPERTS_PER_GROUP)[0]
    grank = jnp.zeros(sb.shape, jnp.int32)
    for d in range(1, N_GROUPS):
        o, wraps = _group_peer(gscore, row, d, EXPERTS_PER_GROUP, N_EXPERTS)
        beats = (o > gscore) | ((o == gscore) & wraps)
        grank = grank + beats.astype(jnp.int32)
    in_group = grank == 0
    w = jnp.where(top2 & in_group, s, 0.0)
    denom = jnp.sum(w, axis=0, keepdims=True)
    group_flag = jnp.where(in_group & ((row & (EXPERTS_PER_GROUP - 1)) == 0), 1.0, 0.0)
    return w / denom, group_flag


def _experts_to_lanes(x_t):
    pad = jnp.zeros((LANES - N_EXPERTS, x_t.shape[1]), x_t.dtype)
    return jnp.concatenate([x_t, pad], axis=0).T


def _split_bf16(x, terms):
    out = []
    for _ in range(terms - 1):
        t = x.astype(BF16)
        out.append(t)
        x = x - t.astype(F32)
    out.append(x.astype(BF16))
    return out


def _post_kernel(ac_ref, al_ref, gm_ref, xc_ref, xl_ref, mod_ref, oag_ref, wout_ref, g2_ref,
                 wr_ref, rb_ref, x1_ref, h2s_ref, y0_ref, pos_ref, cnt_ref):
    mod = mod_ref[0]
    gate1 = mod[:, 2 * D_MODEL:3 * D_MODEL]
    shift2 = mod[:, 3 * D_MODEL:4 * D_MODEL]
    scale2 = mod[:, 4 * D_MODEL:5 * D_MODEL]
    y0_ref[...] = jnp.zeros(y0_ref.shape, y0_ref.dtype)
    tm = xc_ref.shape[0]
    part = tm // POST_PARTS
    halves = [slice(p * part, (p + 1) * part) for p in range(POST_PARTS)]
    attn = _ctx_or_lat(ac_ref, al_ref)
    x_in = _ctx_or_lat(xc_ref, xl_ref)
    mixed = []
    for rows in halves:
        an = (_rms(attn[rows].astype(F32)) * oag_ref[...]).astype(BF16)
        mixed.append(_dot(jnp.concatenate([an, gm_ref[rows, :]], axis=1), wout_ref[...]))
    h2_hi, logits = [], []
    for rows, mix in zip(halves, mixed):
        x1 = x_in[rows] + gate1 * mix
        x1_ref[rows, :] = x1
        hi, lo = _split_bf16(_rms(x1) * g2_ref[...] * (1.0 + scale2) + shift2, 2)
        t = _dot(jnp.concatenate([hi, lo], axis=0), wr_ref[...])
        logits.append((t[:part, :LANES] + t[part:, :LANES]) + (t[:part, LANES:] + t[part:, LANES:]))
        h2_hi.append(hi)
    h2_hi = jnp.concatenate(h2_hi, axis=0)
    logits = jnp.concatenate(logits, axis=0)
    tiles = [slice(sub * TM_POST, (sub + 1) * TM_POST) for sub in range(POST_SUB)]
    r_i = lax.broadcasted_iota(jnp.int32, (TM_POST, TM_POST), 0)
    c_i = lax.broadcasted_iota(jnp.int32, (TM_POST, TM_POST), 1)
    before = jnp.where(c_i < r_i, 1.0, 0.0).astype(BF16)
    routed = []
    for rows in tiles:
        comb_t, flag_t = _route(logits[rows].T[0:N_EXPERTS, :], rb_ref[...])
        routed.append((_experts_to_lanes(comb_t), _experts_to_lanes(flag_t)))
    ranks = [_dot(before, flag.astype(BF16)) for _, flag in routed]
    for sub, rows in enumerate(tiles):
        comb, flag = routed[sub]
        _sort_tile(h2_hi[rows], comb, flag, ranks[sub],
                   h2s_ref.at[sub * LS_ROWS:(sub + 1) * LS_ROWS], pos_ref.at[rows], cnt_ref.at[sub])


def _sort_tile(h2_hi, comb, flag, rank, h2s_ref, pos_ref, cnt_ref):
    tm = TM_POST
    count = jnp.sum(flag, axis=0, keepdims=True)
    pieces = jnp.floor((count + (PIECE - 1)) * (1.0 / PIECE))
    start = (pltpu.roll(pieces, EXPERTS_PER_GROUP, 1) + pltpu.roll(pieces, 2 * EXPERTS_PER_GROUP, 1)
             + pltpu.roll(pieces, 3 * EXPERTS_PER_GROUP, 1)) * PIECE
    pos = jnp.sum(flag * (start + rank), axis=-1, keepdims=True)
    pos_ref[...] = pos
    cnt_ref[...] = count.astype(jnp.int32)
    pos_row = jnp.transpose(jnp.broadcast_to(pos, (tm, LANES)))[0:1, :].astype(jnp.int32)
    place = lax.broadcasted_iota(jnp.int32, (LS_ROWS, tm), 0) == pos_row
    place = jnp.where(place, 1.0, 0.0).astype(BF16)
    wide = jnp.concatenate([h2_hi] + _split_bf16(comb, COMB_TERMS), axis=1)
    h2s_ref[...] = _dot(place, wide).astype(BF16)


def _post_mixer(layer, attn_ctx, attn_lat, gm, x_ctx, x_lat, mod, oag, wout, g2, wr, rb):
    joint_x = x_ctx is x_lat
    tm = POST_SUB * TM_POST
    ls = POST_SUB * LS_ROWS
    row = lambda i: (i, 0)
    return pl.pallas_call(
        _post_kernel,
        out_shape=(
            jax.ShapeDtypeStruct((N_ROWS, D_MODEL), F32),
            jax.ShapeDtypeStruct((N_LS, XS_COLS), BF16),
            jax.ShapeDtypeStruct((N_LS, D_MODEL), BF16),
            jax.ShapeDtypeStruct((N_ROWS, 1), F32),
            jax.ShapeDtypeStruct((N_POST_TILES, 1, LANES), jnp.int32),
        ),
        grid=(N_ROWS // tm,),
        in_specs=_ctx_lat_specs(tm, ATTN_HEADS * V_DIM) + [
            pl.BlockSpec((tm, GM_WIDTH), row),
        ] + _ctx_lat_specs(tm, D_MODEL, joint=joint_x) + [
            _mod_spec(layer, tm),
            _of_layer(layer, (1, ATTN_HEADS * V_DIM)),
            _of_layer(layer, (D_MODEL, D_MODEL)),
            _of_layer(layer, (1, D_MODEL)),
            _full((D_MODEL, 2 * LANES)),
            _full((N_EXPERTS, 1)),
        ],
        out_specs=(
            pl.BlockSpec((tm, D_MODEL), row),
            pl.BlockSpec((ls, XS_COLS), row),
            pl.BlockSpec((ls, D_MODEL), row),
            pl.BlockSpec((tm, 1), row),
            pl.BlockSpec((POST_SUB, 1, LANES), lambda i: (i, 0, 0)),
        ),
        compiler_params=pltpu.CompilerParams(
            dimension_semantics=("arbitrary",), vmem_limit_bytes=VMEM_LIMIT),
        name="post_mixer",
    )(attn_ctx, attn_lat, gm, x_ctx, x_lat, mod, oag, wout, g2, wr, rb)


def _plan_kernel(cnt_ref, tg_ref, tn_ref, src_ref, nu_ref):
    def clear_src(j, c):
        for u in range(E_PIECES):
            src_ref[j * E_PIECES + u] = 0
        return c

    lax.fori_loop(0, SRC_SLOTS // E_PIECES, clear_src, 0)

    def clear_tile(j, c):
        tg_ref[j] = N_GROUPS - 1
        tn_ref[j] = 0
        return c

    lax.fori_loop(0, N_ETILES, clear_tile, 0)

    def n_pieces(i, g):
        return lax.shift_right_logical(cnt_ref[i, g * EXPERTS_PER_GROUP] + (PIECE - 1), PIECE_SHIFT)

    t = jnp.int32(0)
    for g in range(N_GROUPS):
        def tile_body(i, s, g=g):
            first = i * LS_PIECES
            for gp in range(g):
                first = first + n_pieces(i, gp)

            for p in range(TM_POST // PIECE):
                src_ref[s + p] = first + p
            return s + n_pieces(i, g)

        s0 = t * E_PIECES
        s1 = lax.fori_loop(0, N_POST_TILES, tile_body, s0)
        n = s1 - s0
        tiles = lax.shift_right_logical(n + (E_PIECES - 1), E_SHIFT)

        def mark_tile(u, c, g=g, n=n, t=t):
            tg_ref[t + u] = g
            tn_ref[t + u] = jnp.minimum(n - u * E_PIECES, E_PIECES)
            return c

        lax.fori_loop(0, tiles, mark_tile, 0)
        t = t + tiles
    nu_ref[0] = t


def _plan(counts):
    smem = pl.BlockSpec(memory_space=pltpu.SMEM)
    return pl.pallas_call(
        _plan_kernel,
        out_shape=(
            jax.ShapeDtypeStruct((N_ETILES,), jnp.int32),
            jax.ShapeDtypeStruct((N_ETILES,), jnp.int32),
            jax.ShapeDtypeStruct((SRC_SLOTS,), jnp.int32),
            jax.ShapeDtypeStruct((1,), jnp.int32),
        ),
        in_specs=[smem],
        out_specs=(smem, smem, smem, smem),
        name="expert_plan",
    )(counts)


GATHER_X, SCATTER_Y = 0, 1


def _moe_kernel(tg_ref, tn_ref, src_ref, nu_ref, h2s_hbm, y0_hbm, wg32_ref, wu32_ref, wd32_ref,
                wsg32_ref, wsu32_ref, wsd32_ref, y_hbm, xbuf, ybuf, wg_ref, wu_ref, wd_ref,
                wsg_ref, wsu_ref, wsd_ref, sem):
    del y0_hbm
    j = pl.program_id(0)
    n_used = nu_ref[0]
    slot = lax.rem(j, 2)

    @pl.when(j == 0)
    def _():
        wsg_ref[...] = wsg32_ref[0].astype(BF16)
        wsu_ref[...] = wsu32_ref[0].astype(BF16)
        wsd_ref[...] = wsd32_ref[0].astype(BF16)

    @pl.when((j == 0) | (tg_ref[j] != tg_ref[jnp.maximum(j - 1, 0)]))
    def _():
        for k in range(EXPERTS_PER_GROUP):
            wg_ref[k] = wg32_ref[0, k].astype(BF16)
            wu_ref[k] = wu32_ref[0, k].astype(BF16)
            wd_ref[k] = wd32_ref[0, k].astype(BF16)

    def piece_rows(t, k):
        hbm_rows = pl.ds(pl.multiple_of(src_ref[t * E_PIECES + k] * PIECE, PIECE), PIECE)
        buf_rows = pl.ds(pl.multiple_of(k * PIECE, PIECE), PIECE)
        return hbm_rows, buf_rows

    def gather_copies(t, k, slot):
        hbm_rows, buf_rows = piece_rows(t, k)
        return (
            pltpu.make_async_copy(h2s_hbm.at[hbm_rows], xbuf.at[slot, buf_rows], sem.at[GATHER_X, slot]),
        )

    def scatter_copies(t, k, slot):
        hbm_rows, buf_rows = piece_rows(t, k)
        return (
            pltpu.make_async_copy(ybuf.at[slot, buf_rows], y_hbm.at[hbm_rows], sem.at[SCATTER_Y, slot]),
        )

    def for_pieces(t, slot, copies, action):
        def body(k, c):
            for cp in copies(t, k, slot):
                action(cp)
            return c

        @pl.when(tn_ref[t] == E_PIECES)
        def _():
            lax.fori_loop(0, E_PIECES, body, 0, unroll=8)

        @pl.when(tn_ref[t] != E_PIECES)
        def _():
            lax.fori_loop(0, tn_ref[t], body, 0)

    start = lambda cp: cp.start()
    start_on_second_queue = lambda cp: cp.start(priority=1)
    wait = lambda cp: cp.wait()

    @pl.when(j == 0)
    def _():
        xbuf[...] = jnp.zeros(xbuf.shape, xbuf.dtype)
        for_pieces(0, 0, gather_copies, start)

    @pl.when(j + 1 < n_used)
    def _():
        for_pieces(j + 1, 1 - slot, gather_copies, start)

    @pl.when(j < n_used)
    def _():
        for_pieces(j, slot, gather_copies, wait)
        x = xbuf[slot, :, 0:D_MODEL]
        comb = sum(xbuf[slot, :, D_MODEL + t * LANES:D_MODEL + (t + 1) * LANES].astype(F32)
                   for t in range(COMB_TERMS))
        lane = lax.broadcasted_iota(jnp.int32, (TM_E, LANES), 1)
        first = tg_ref[j] * EXPERTS_PER_GROUP
        gates = [_dot(x, wsg_ref[...])] + [_dot(x, wg_ref[k]) for k in range(EXPERTS_PER_GROUP)]
        ups = [_dot(x, wsu_ref[...])] + [_dot(x, wu_ref[k]) for k in range(EXPERTS_PER_GROUP)]
        acts = [(_silu(gates[0]) * ups[0]).astype(BF16)]
        for k in range(EXPERTS_PER_GROUP):
            wk = jnp.sum(jnp.where(lane == first + k, comb, 0.0), axis=-1, keepdims=True)
            acts.append((_silu(gates[k + 1]) * ups[k + 1] * wk).astype(BF16))
        downs = [wsd_ref[...]] + [wd_ref[k] for k in range(EXPERTS_PER_GROUP)]
        y = sum(_dot(a, w) for a, w in zip(acts, downs))
        ybuf[slot] = y.astype(BF16)
        for_pieces(j, slot, scatter_copies, start_on_second_queue)

    @pl.when((j >= 1) & (j < n_used))
    def _():
        for_pieces(j - 1, 1 - slot, scatter_copies, wait)

    @pl.when(j == n_used - 1)
    def _():
        for_pieces(j, slot, scatter_copies, wait)


def _experts(plan, h2s, y0, layer, w_gate, w_up, w_down, ws_gate, ws_up, ws_down):
    tg, tn, src, nu = plan
    any_spec = pl.BlockSpec(memory_space=pl.ANY)
    group = lambda j, tg, tn, src, nu: (layer, tg[j], 0, 0)
    whole = lambda j, tg, tn, src, nu: (layer, 0, 0)
    g = EXPERTS_PER_GROUP
    return pl.pallas_call(
        _moe_kernel,
        out_shape=jax.ShapeDtypeStruct((N_LS, D_MODEL), BF16),
        grid_spec=pltpu.PrefetchScalarGridSpec(
            num_scalar_prefetch=4,
            grid=(N_ETILES,),
            in_specs=[
                any_spec, any_spec,
                pl.BlockSpec((1, g, D_MODEL, EXPERT_FF), group),
                pl.BlockSpec((1, g, D_MODEL, EXPERT_FF), group),
                pl.BlockSpec((1, g, EXPERT_FF, D_MODEL), group),
                pl.BlockSpec((1, D_MODEL, SHARED_FF), whole),
                pl.BlockSpec((1, D_MODEL, SHARED_FF), whole),
                pl.BlockSpec((1, SHARED_FF, D_MODEL), whole),
            ],
            out_specs=any_spec,
            scratch_shapes=[
                pltpu.VMEM((2, TM_E, XS_COLS), BF16),
                pltpu.VMEM((2, TM_E, D_MODEL), BF16),
                pltpu.VMEM((g, D_MODEL, EXPERT_FF), BF16),
                pltpu.VMEM((g, D_MODEL, EXPERT_FF), BF16),
                pltpu.VMEM((g, EXPERT_FF, D_MODEL), BF16),
                pltpu.VMEM((D_MODEL, SHARED_FF), BF16),
                pltpu.VMEM((D_MODEL, SHARED_FF), BF16),
                pltpu.VMEM((SHARED_FF, D_MODEL), BF16),
                pltpu.SemaphoreType.DMA((2, 2)),
            ],
        ),
        input_output_aliases={5: 0},
        compiler_params=pltpu.CompilerParams(
            dimension_semantics=("arbitrary",), vmem_limit_bytes=VMEM_LIMIT),
        name="experts",
    )(tg, tn, src, nu, h2s, y0, w_gate, w_up, w_down, ws_gate, ws_up, ws_down)


def _unsort(y_ref, pos_ref, sub):
    pos = pos_ref[sub * TM_POST:(sub + 1) * TM_POST, :].astype(jnp.int32)
    pick = lax.broadcasted_iota(jnp.int32, (TM_POST, LS_ROWS), 1) == pos
    return _dot(jnp.where(pick, 1.0, 0.0).astype(BF16), y_ref[sub * LS_ROWS:(sub + 1) * LS_ROWS, :])


def _combine_kernel(y_ref, pos_ref, x1_ref, mod_ref, fg_ref, o_ref):
    gate2 = mod_ref[0][:, 5 * D_MODEL:6 * D_MODEL]
    for sub in range(COMBINE_SUB):
        rows = slice(sub * TM_POST, (sub + 1) * TM_POST)
        x2 = x1_ref[rows, :] + gate2 * _unsort(y_ref, pos_ref, sub)
        o_ref[rows, :] = _rms(x2) * fg_ref[...]


def _combine(y, pos, x1, mod, fg, first_tile, n_tiles):
    tm = COMBINE_SUB * TM_POST
    row = lambda i: (first_tile + i, 0)
    return pl.pallas_call(
        _combine_kernel,
        out_shape=jax.ShapeDtypeStruct((n_tiles * tm, D_MODEL), F32),
        grid=(n_tiles,),
        in_specs=[
            pl.BlockSpec((COMBINE_SUB * LS_ROWS, D_MODEL), row),
            pl.BlockSpec((tm, 1), row),
            pl.BlockSpec((tm, D_MODEL), row),
            _mod_spec(DEPTH - 1, tm, first_tile),
            _full((1, D_MODEL)),
        ],
        out_specs=pl.BlockSpec((tm, D_MODEL), lambda i: (i, 0)),
        compiler_params=pltpu.CompilerParams(
            dimension_semantics=("arbitrary",), vmem_limit_bytes=VMEM_LIMIT),
        name="combine",
    )(y, pos, x1, mod, fg)


def _rope_table():
    rows = DEC_SEQ // GRID_W
    row = jnp.repeat(jnp.arange(rows, dtype=F32), GRID_W)
    col = jnp.tile(jnp.arange(GRID_W, dtype=F32), rows)
    half = QK_ROPE // 2
    freqs = 1.0 / (ROPE_BASE ** (jnp.arange(0, half, 2, dtype=F32) / half))
    ang = jnp.concatenate([row[:, None] * freqs, col[:, None] * freqs], axis=-1)
    cos, sin = jnp.cos(ang), jnp.sin(ang)
    lat = jnp.concatenate([cos, cos, -sin, sin], axis=-1)
    ident = jnp.concatenate([jnp.ones((TM_PRE, QK_ROPE), F32), jnp.zeros((TM_PRE, QK_ROPE), F32)], axis=-1)
    return jnp.concatenate([ident, lat], axis=0)


_DEINT = np.concatenate([np.arange(0, QK_ROPE, 2), np.arange(1, QK_ROPE, 2)])
_SWAP = np.concatenate([np.arange(1, QK_ROPE, 2), np.arange(0, QK_ROPE, 2)])
_INTERLEAVE = np.argsort(_DEINT)


def _layout_w_in(w_in):
    cq_ckv = w_in[..., :Q_LORA + KV_LORA]
    kr = w_in[..., Q_LORA + KV_LORA:Q_LORA + KV_LORA + QK_ROPE]
    uv = w_in[..., Q_LORA + KV_LORA + QK_ROPE:]
    return jnp.concatenate([cq_ckv, kr[..., _DEINT], kr[..., _SWAP], uv], axis=-1).astype(BF16)


def _layout_w_qb(w_qb):
    w = (w_qb * (ATTN_SCALE * LOG2E)).reshape(DEPTH, Q_LORA, ATTN_HEADS, QK_NOPE + QK_ROPE)
    nope, rope = w[..., :QK_NOPE], w[..., QK_NOPE:]
    w = jnp.concatenate([nope, rope[..., _DEINT], rope[..., _SWAP]], axis=-1)
    return w.reshape(DEPTH, Q_LORA, ATTN_HEADS * HEAD_PAD).astype(BF16)


def _layout_w_kvb(w_kvb):
    w = w_kvb.reshape(DEPTH, KV_LORA, ATTN_HEADS, QK_NOPE + V_DIM)
    k = w[..., :QK_NOPE].reshape(DEPTH, KV_LORA, ATTN_HEADS * QK_NOPE)
    v = w[..., QK_NOPE:].reshape(DEPTH, KV_LORA, ATTN_HEADS * V_DIM)
    return jnp.concatenate([k, v], axis=-1).astype(BF16)


def kernel(x_prompt, x_sample, cache_ckv, cache_krope, c, c_ctx, norm1_g, w_ada, b_ada, w_in,
           q_norm_g, w_qb, kv_norm_g, w_kvb, gm_norm_g, w_spatial, b_spatial, onorm_attn_g,
           onorm_gm_g, w_out, norm2_g, w_router, router_bias, w_gate, w_up, w_down, ws_gate,
           ws_up, ws_down, final_norm_g):
    x_ctx = x_prompt.reshape(N_CTX, D_MODEL)
    x_lat = x_sample.reshape(N_LAT, D_MODEL)
    cond = jnp.concatenate([c_ctx[None, :], c, jnp.zeros((COND_PAD - N_COND, D_MODEL), F32)], axis=0)
    mod = _modulation(cond, w_ada, b_ada).reshape(DEPTH, COND_PAD, 1, 6 * D_MODEL)
    rope_tab = _rope_table()

    wkvb = _layout_w_kvb(w_kvb)
    cache_kr = jnp.pad(cache_krope[..., _DEINT], ((0, 0), (0, 0), (0, 0), (0, LANES - QK_ROPE)))
    kc, vct = _cache_kv(cache_ckv, cache_kr, wkvb)

    wr = jnp.pad(w_router, ((0, 0), (0, LANES - N_EXPERTS)))
    wr_hi = wr.astype(BF16)
    wr = jnp.concatenate([wr_hi, (wr - wr_hi.astype(F32)).astype(BF16)], axis=1)
    rb = router_bias.reshape(N_EXPERTS, 1)
    fg = final_norm_g.reshape(1, D_MODEL)

    row_stack = lambda g: g.reshape(DEPTH, 1, -1)
    bs = jnp.broadcast_to(jnp.swapaxes(b_spatial, 1, 2)[..., None],
                          (DEPTH, CHUNK, GM_GROUPS, GM_GROUP_DIM)).reshape(DEPTH, CHUNK, GM_WIDTH)
    pre_params = (row_stack(norm1_g), _layout_w_in(w_in), row_stack(q_norm_g), _layout_w_qb(w_qb),
                  row_stack(kv_norm_g), wkvb, row_stack(gm_norm_g), w_spatial.astype(BF16), bs,
                  row_stack(onorm_gm_g))
    post_params = (row_stack(onorm_attn_g), w_out.astype(BF16), row_stack(norm2_g), wr, rb)

    ckv_out, kr_out = [], []
    layer_input = (x_ctx, x_lat)
    for l in range(DEPTH):
        q, k, vt, gm, ckv_n, kr, *formed = _pre_mixer(l, layer_input, mod, rope_tab, *pre_params)
        ckv_out.append(ckv_n[:N_CTX].reshape(BATCH, SEQ, KV_LORA))
        kr_out.append(kr[:N_CTX][:, _INTERLEAVE].reshape(BATCH, SEQ, QK_ROPE))
        x_pair = (formed[0], formed[0]) if formed else layer_input
        x1, h2s, y0, pos, counts = _post_mixer(
            l, _ctx_attention(q, k, vt), _lat_attention(q, k, vt, kc, vct, l), gm, *x_pair, mod,
            *post_params)
        plan = _plan(counts.reshape(N_POST_TILES, LANES))
        y = _experts(plan, h2s, y0, l, w_gate, w_up, w_down, ws_gate, ws_up, ws_down)
        layer_input = (y, pos, x1)

    n_ctx_tiles = N_CTX // (COMBINE_SUB * TM_POST)
    n_lat_tiles = N_LAT // (COMBINE_SUB * TM_POST)
    x_ctx = _combine(y, pos, x1, mod, fg, 0, n_ctx_tiles)
    x_lat = _combine(y, pos, x1, mod, fg, n_ctx_tiles, n_lat_tiles)
    y_prompt = x_ctx.reshape(BATCH, SEQ, D_MODEL)
    y_sample = x_lat.reshape(DEC_BATCH, DEC_SEQ, D_MODEL)
    return y_prompt, y_sample, jnp.stack(ckv_out, axis=1), jnp.stack(kr_out, axis=1)
```
